```python
import jax, jax.numpy as jnp
from jax import lax
import numpy as np

D_MODEL = 2048
BATCH = 2
SEQ = 16384
DEPTH = 2

GRID_W = 64
CTX_LEN = 256
ROPE_BASE = 10000.0
NORM_EPS = 1e-6
NEG_INF = -1e30

RW_HEADS = 16
RW_HEAD_DIM = 64
RW_WIDTH = RW_HEADS * RW_HEAD_DIM
RW_DECAY_LORA = 96
RW_AAA_LORA = 96
RW_MV_LORA = 64
RW_GATE_LORA = 64
RW_LNX_EPS = 64e-5

WA_HEADS = 16
WA_KV_HEADS = 4
WA_GROUP = WA_HEADS // WA_KV_HEADS
WA_HEAD_DIM = 64
WA_WIDTH = WA_HEADS * WA_HEAD_DIM
WINDOW = 128
BLOCK = 128
WA_SCALE = WA_HEAD_DIM ** -0.5

MLA_HEADS = 8
MLA_NOPE = 128
MLA_ROPE = 64
MLA_V = 128
MLA_Q_LORA = 512
MLA_KV_LORA = 512
MLA_WIDTH = MLA_HEADS * MLA_V
MLA_SCALE = (MLA_NOPE + MLA_ROPE) ** -0.5

N_BRANCH = 3
BRANCH_WIDTH = 1024

D_FF = 5632
CONV_W = 3

RW_SPLITS = (RW_WIDTH, RW_WIDTH, RW_WIDTH, RW_DECAY_LORA, RW_DECAY_LORA, RW_AAA_LORA, RW_AAA_LORA, RW_GATE_LORA)
RW_COLS = 3 * RW_WIDTH + 2 * RW_DECAY_LORA + 2 * RW_AAA_LORA + RW_GATE_LORA
WA_COLS = WA_HEADS * WA_HEAD_DIM + 2 * WA_KV_HEADS * WA_HEAD_DIM
MLA_COLS = MLA_Q_LORA + MLA_KV_LORA + MLA_ROPE
GATE_COLS = N_BRANCH * D_MODEL
IN_SPLITS = (RW_COLS, WA_COLS, MLA_COLS, GATE_COLS)
IN_COLS = RW_COLS + WA_COLS + MLA_COLS + GATE_COLS

kernel_name = "hybrid_rwkv7_swa_mla_dit_block"


def split_last(z, sizes):
    out, off = [], 0
    for s in sizes:
        out.append(z[..., off:off + s])
        off += s
    return out


def rmsnorm(z, g):
    zf = z.astype(jnp.float32)
    zf = zf * lax.rsqrt(jnp.mean(zf * zf, axis=-1, keepdims=True) + NORM_EPS)
    return (zf * g.astype(jnp.float32)).astype(z.dtype)


def modulate(n, shift, scale):
    return n * (1 + scale) + shift


def axial_rope_table(rows, dim):
    nf = dim // 4
    inv = ROPE_BASE ** (-jnp.arange(nf, dtype=jnp.float32) / nf)
    row = jnp.repeat(jnp.arange(rows, dtype=jnp.float32), GRID_W)
    col = jnp.tile(jnp.arange(GRID_W, dtype=jnp.float32), rows)
    ang = jnp.concatenate([row[:, None] * inv, col[:, None] * inv], axis=-1)
    return jnp.cos(ang), jnp.sin(ang)


def apply_rope(z, cos, sin):
    half = z.shape[-1] // 2
    zf = z.astype(jnp.float32)
    z1, z2 = zf[..., :half], zf[..., half:]
    cs, sn = cos[None, :, None, :], sin[None, :, None, :]
    return jnp.concatenate([z1 * cs - z2 * sn, z1 * sn + z2 * cs], axis=-1).astype(z.dtype)


def token_lerp(z, mu):
    zp = jnp.pad(z, ((0, 0), (1, 1), (0, 0)))
    return z + mu * (0.5 * (zp[:, :-2] + zp[:, 2:]) - z)


def depthwise_conv(z, w, b):
    half = CONV_W // 2
    T = z.shape[1]
    zp = jnp.pad(z, ((0, 0), (half, half), (0, 0)))
    out = b
    for j in range(CONV_W):
        out = out + zp[:, j:j + T] * w[j]
    return out


def wkv7_scan(r, w, k, v, a, b):
    Bn, S, H, N = r.shape

    def step(state, xs):
        r_t, w_t, k_t, v_t, a_t, b_t = xs
        sa = jnp.einsum('bhvk,bhk->bhv', state, a_t)
        state = state * w_t[:, :, None, :] + sa[..., None] * b_t[:, :, None, :] + v_t[..., None] * k_t[:, :, None, :]
        return state, jnp.einsum('bhvk,bhk->bhv', state, r_t)

    xs = tuple(jnp.moveaxis(t.astype(jnp.float32), 1, 0) for t in (r, w, k, v, a, b))
    _, y = lax.scan(step, jnp.zeros((Bn, H, N, N), jnp.float32), xs)
    return jnp.moveaxis(y, 0, 1)


def rwkv7_branch(zc, zl, v_first, vres, mu, w0, w2, a0, a2, g2, k_k, k_a, r_k, lnx_g, lnx_b):
    L = zc.shape[1]
    z = jnp.concatenate([token_lerp(zc, mu), token_lerp(zl, mu)], axis=1)
    Bn, S = z.shape[:2]
    r, k, v, wdf, wdb, adf, adb, gd = split_last(z, RW_SPLITS)
    if vres is None:
        v_first = v
    else:
        v0, v1, v2 = vres
        v = v + (v_first - v) * jax.nn.sigmoid(v0 + (v @ v1) @ v2)
    g = jax.nn.sigmoid(gd) @ g2

    def heads(t):
        return t.reshape(Bn, S, RW_HEADS, RW_HEAD_DIM)

    kk = heads((k * k_k).astype(jnp.float32))
    kk = kk / jnp.maximum(jnp.linalg.norm(kk, axis=-1, keepdims=True), 1e-12)
    rh, vh = heads(r), heads(v)

    def fwd(t):
        return t

    def bwd(t):
        return jnp.concatenate([t[:, :L][:, ::-1], t[:, L:][:, ::-1]], axis=1)

    def direction(wd, ad, d, order):
        wlog = -jax.nn.softplus(-(w0[d] + jnp.tanh(wd) @ w2[d]).astype(jnp.float32)) - 0.5
        decay = jnp.exp(-jnp.exp(wlog))
        a = jax.nn.sigmoid((a0[d] + ad @ a2[d]).astype(jnp.float32))
        kd = heads(k * (1 + (a - 1) * k_a))
        ys = wkv7_scan(order(rh), order(heads(decay)), order(kd), order(vh), order(-kk), order(kk * heads(a)))
        return order(ys), kd

    y_f, k_f = direction(wdf, adf, 0, fwd)
    y_b, k_b = direction(wdb, adb, 1, bwd)
    y = y_f + y_b
    mean = jnp.mean(y, axis=-1, keepdims=True)
    var = jnp.mean(jnp.square(y - mean), axis=-1, keepdims=True)
    y = ((y - mean) * lax.rsqrt(var + RW_LNX_EPS)).reshape(Bn, S, RW_WIDTH) * lnx_g + lnx_b
    bonus = (jnp.sum(rh * (k_f + k_b) * r_k, axis=-1, keepdims=True) * vh).reshape(Bn, S, RW_WIDTH)
    out = ((y + bonus) * g).astype(zl.dtype)
    return out[:, :L], out[:, L:], v_first


def window_sink_attention(q, k, v, kc, vc, sink):
    Bn, T = q.shape[:2]
    Lc = kc.shape[1]
    nb = T // BLOCK
    span = BLOCK + 2 * WINDOW
    kp = jnp.pad(k, ((0, 0), (WINDOW, WINDOW), (0, 0), (0, 0)))
    vp = jnp.pad(v, ((0, 0), (WINDOW, WINDOW), (0, 0), (0, 0)))
    qb = jnp.moveaxis(q.reshape(Bn, nb, BLOCK, WA_KV_HEADS, WA_GROUP, WA_HEAD_DIM), 1, 0)
    rel = jnp.arange(span)[None, :] - WINDOW - jnp.arange(BLOCK)[:, None]
    sink_f = sink.astype(jnp.float32)[None, :, :, None, None]

    def one_block(args):
        i, qi = args
        start = i * BLOCK
        ki = lax.dynamic_slice_in_dim(kp, start, span, axis=1)
        vi = lax.dynamic_slice_in_dim(vp, start, span, axis=1)
        kpos = start - WINDOW + jnp.arange(span)
        valid = (jnp.abs(rel) <= WINDOW) & ((kpos >= 0) & (kpos < T))[None, :]
        s_loc = jnp.einsum('bqhgd,bkhd->bhgqk', qi, ki).astype(jnp.float32) * WA_SCALE
        s_loc = jnp.where(valid, s_loc, NEG_INF)
        s_ctx = jnp.einsum('bqhgd,blhd->bhgql', qi, kc).astype(jnp.float32) * WA_SCALE
        s_sink = jnp.broadcast_to(sink_f, s_loc.shape[:-1] + (1,))
        p = jax.nn.softmax(jnp.concatenate([s_loc, s_ctx, s_sink], axis=-1), axis=-1).astype(v.dtype)
        return (jnp.einsum('bhgqk,bkhd->bqhgd', p[..., :span], vi)
                + jnp.einsum('bhgql,blhd->bqhgd', p[..., span:span + Lc], vc))

    o = lax.map(one_block, (jnp.arange(nb), qb))
    return jnp.moveaxis(o, 0, 1).reshape(Bn, T, WA_WIDTH)


def context_sink_attention(q, k, v, sink):
    s = jnp.einsum('bqhgd,blhd->bhgql', q, k).astype(jnp.float32) * WA_SCALE
    s_sink = jnp.broadcast_to(sink.astype(jnp.float32)[None, :, :, None, None], s.shape[:-1] + (1,))
    p = jax.nn.softmax(jnp.concatenate([s, s_sink], axis=-1), axis=-1)[..., :-1].astype(v.dtype)
    o = jnp.einsum('bhgql,blhd->bqhgd', p, v)
    return o.reshape(q.shape[0], q.shape[1], WA_WIDTH)


def windowed_gqa_branch(zc, zl, sink, cos, sin, need_ctx):
    sink = sink.reshape(WA_KV_HEADS, WA_GROUP)

    def heads(z, rope):
        Bn, T = z.shape[:2]
        q, k, v = split_last(z, (WA_HEADS * WA_HEAD_DIM, WA_KV_HEADS * WA_HEAD_DIM, WA_KV_HEADS * WA_HEAD_DIM))
        q = q.reshape(Bn, T, WA_HEADS, WA_HEAD_DIM)
        k = k.reshape(Bn, T, WA_KV_HEADS, WA_HEAD_DIM)
        v = v.reshape(Bn, T, WA_KV_HEADS, WA_HEAD_DIM)
        if rope:
            q, k = apply_rope(q, cos, sin), apply_rope(k, cos, sin)
        return q.reshape(Bn, T, WA_KV_HEADS, WA_GROUP, WA_HEAD_DIM), k, v

    qc, kc, vc = heads(zc, False)
    ql, kl, vl = heads(zl, True)
    out_l = window_sink_attention(ql, kl, vl, kc, vc, sink)
    out_c = context_sink_attention(qc, kc, vc, sink) if need_ctx else None
    return out_c, out_l


def dense_attention(q, k, v, scale):
    Bn, T, H, dq = q.shape
    nb = T // BLOCK
    qb = jnp.moveaxis(q.reshape(Bn, nb, BLOCK, H, dq), 1, 0)

    def one_block(qi):
        s = jnp.einsum('bqhd,bshd->bhqs', qi, k).astype(jnp.float32) * scale
        p = jax.nn.softmax(s, axis=-1).astype(v.dtype)
        return jnp.einsum('bhqs,bshd->bqhd', p, v)

    o = lax.map(one_block, qb)
    return jnp.moveaxis(o, 0, 1).reshape(Bn, T, H * v.shape[-1])


def mla_branch(zc, zl, qnorm_g, kvnorm_g, w_uq, w_ukv, cos, sin, need_ctx):
    def project(z, rope):
        Bn, T = z.shape[:2]
        cq, ckv, kr = split_last(z, (MLA_Q_LORA, MLA_KV_LORA, MLA_ROPE))
        q = (rmsnorm(cq, qnorm_g) @ w_uq).reshape(Bn, T, MLA_HEADS, MLA_NOPE + MLA_ROPE)
        kv = (rmsnorm(ckv, kvnorm_g) @ w_ukv).reshape(Bn, T, MLA_HEADS, MLA_NOPE + MLA_V)
        q_nope, q_rope = q[..., :MLA_NOPE], q[..., MLA_NOPE:]
        k_nope, v = kv[..., :MLA_NOPE], kv[..., MLA_NOPE:]
        kr = kr[:, :, None, :]
        if rope:
            q_rope, kr = apply_rope(q_rope, cos, sin), apply_rope(kr, cos, sin)
        q = jnp.concatenate([q_nope, q_rope], axis=-1)
        k = jnp.concatenate([k_nope, jnp.broadcast_to(kr, k_nope.shape[:-1] + (MLA_ROPE,))], axis=-1)
        return q, k, v

    qc, kc, vc = project(zc, False)
    ql, kl, vl = project(zl, True)
    out_l = dense_attention(ql, jnp.concatenate([kc, kl], axis=1), jnp.concatenate([vc, vl], axis=1), MLA_SCALE)
    out_c = dense_attention(qc, kc, vc, MLA_SCALE) if need_ctx else None
    return out_c, out_l


def merge_branches(gate_cols, outs, w_branch_l, w_out_l):
    gates = jax.nn.sigmoid(gate_cols)
    y = gates[..., :D_MODEL] * (outs[0] @ w_branch_l[0])
    for i in range(1, N_BRANCH):
        y = y + gates[..., i * D_MODEL:(i + 1) * D_MODEL] * (outs[i] @ w_branch_l[i])
    return y @ w_out_l


def conv_glu(h, w_in, conv_w, conv_b, w_out):
    gt, u = jnp.split(h @ w_in, 2, axis=-1)
    return (jax.nn.gelu(depthwise_conv(gt, conv_w, conv_b), approximate=True) * u) @ w_out


def setup_inputs(seed: int = 0) -> dict:
    key = jax.random.key(seed)
    ks = iter(jax.random.split(key, 48))

    def nrm(shape, s):
        return jax.random.normal(next(ks), shape, jnp.float32) * s

    def unif(shape, lo, hi):
        return jax.random.uniform(next(ks), shape, jnp.float32, lo, hi)

    L, D = DEPTH, D_MODEL
    return {
        "x": nrm((BATCH, SEQ, D), 1.0),
        "c": nrm((BATCH, D), 1.0),
        "ctx": nrm((BATCH, CTX_LEN, D), 1.0),
        "c_ctx": nrm((D,), 1.0),
        "ada_w": nrm((L, D, 6 * D), 0.5 * D ** -0.5),
        "ada_b": nrm((L, 6 * D), 0.02),
        "norm1_g": 1.0 + nrm((L, D), 0.05),
        "w_in": nrm((L, D, IN_COLS), D ** -0.5),
        "rw_mu": unif((L, RW_COLS), 0.0, 1.0),
        "rw_w0": unif((L, 2, RW_WIDTH), -6.5, -1.5),
        "rw_w2": nrm((L, 2, RW_DECAY_LORA, RW_WIDTH), 0.5 * RW_DECAY_LORA ** -0.5),
        "rw_a0": nrm((L, 2, RW_WIDTH), 0.1),
        "rw_a2": nrm((L, 2, RW_AAA_LORA, RW_WIDTH), 0.5 * RW_AAA_LORA ** -0.5),
        "rw_g2": nrm((L, RW_GATE_LORA, RW_WIDTH), RW_GATE_LORA ** -0.5),
        "rw_kk": 0.85 + nrm((L, RW_WIDTH), 0.05),
        "rw_ka": 1.0 + nrm((L, RW_WIDTH), 0.05),
        "rw_rk": nrm((L, RW_HEADS, RW_HEAD_DIM), 0.1),
        "rw_lnx_g": 1.0 + nrm((L, RW_WIDTH), 0.05),
        "rw_lnx_b": nrm((L, RW_WIDTH), 0.01),
        "rw_v0": 1.0 + nrm((L - 1, RW_WIDTH), 0.1),
        "rw_v1": nrm((L - 1, RW_WIDTH, RW_MV_LORA), RW_WIDTH ** -0.5),
        "rw_v2": nrm((L - 1, RW_MV_LORA, RW_WIDTH), 0.5 * RW_MV_LORA ** -0.5),
        "wa_sink": nrm((L, WA_HEADS), 0.5),
        "mla_qnorm_g": 1.0 + nrm((L, MLA_Q_LORA), 0.05),
        "mla_kvnorm_g": 1.0 + nrm((L, MLA_KV_LORA), 0.05),
        "mla_w_uq": nrm((L, MLA_Q_LORA, MLA_HEADS * (MLA_NOPE + MLA_ROPE)), MLA_Q_LORA ** -0.5),
        "mla_w_ukv": nrm((L, MLA_KV_LORA, MLA_HEADS * (MLA_NOPE + MLA_V)), MLA_KV_LORA ** -0.5),
        "w_branch": nrm((L, N_BRANCH, BRANCH_WIDTH, D), BRANCH_WIDTH ** -0.5),
        "w_out": nrm((L, D, D), D ** -0.5),
        "norm2_g": 1.0 + nrm((L, D), 0.05),
        "ffn_w_in": nrm((L, D, 2 * D_FF), D ** -0.5),
        "ffn_conv_w": nrm((L, CONV_W, D_FF), 0.5),
        "ffn_conv_b": nrm((L, D_FF), 0.02),
        "ffn_w_out": nrm((L, D_FF, D), D_FF ** -0.5),
        "final_norm_g": 1.0 + nrm((D,), 0.05),
    }


def reference(x, c, ctx, c_ctx, ada_w, ada_b, norm1_g, w_in, rw_mu, rw_w0, rw_w2, rw_a0, rw_a2, rw_g2,
              rw_kk, rw_ka, rw_rk, rw_lnx_g, rw_lnx_b, rw_v0, rw_v1, rw_v2, wa_sink, mla_qnorm_g, mla_kvnorm_g,
              mla_w_uq, mla_w_ukv, w_branch, w_out, norm2_g, ffn_w_in, ffn_conv_w, ffn_conv_b, ffn_w_out,
              final_norm_g):
    rows = x.shape[1] // GRID_W
    cos_wa, sin_wa = axial_rope_table(rows, WA_HEAD_DIM)
    cos_mla, sin_mla = axial_rope_table(rows, MLA_ROPE)
    xc, xl = ctx, x
    v_first = None
    for l in range(DEPTH):
        need_ctx = l < DEPTH - 1
        mod_l = jnp.split((jax.nn.silu(c) @ ada_w[l] + ada_b[l])[:, None, :], 6, axis=-1)
        mod_c = jnp.split((jax.nn.silu(c_ctx) @ ada_w[l] + ada_b[l])[None, None, :], 6, axis=-1)

        hl = modulate(rmsnorm(xl, norm1_g[l]), mod_l[0], mod_l[1])
        hc = modulate(rmsnorm(xc, norm1_g[l]), mod_c[0], mod_c[1])
        rw_l, wa_l, ml_l, gt_l = split_last(hl @ w_in[l], IN_SPLITS)
        rw_c, wa_c, ml_c, gt_c = split_last(hc @ w_in[l], IN_SPLITS)
        vres = None if l == 0 else (rw_v0[l - 1], rw_v1[l - 1], rw_v2[l - 1])
        oa_c, oa_l, v_first = rwkv7_branch(rw_c, rw_l, v_first, vres, rw_mu[l], rw_w0[l], rw_w2[l], rw_a0[l],
                                           rw_a2[l], rw_g2[l], rw_kk[l], rw_ka[l], rw_rk[l], rw_lnx_g[l], rw_lnx_b[l])
        ob_c, ob_l = windowed_gqa_branch(wa_c, wa_l, wa_sink[l], cos_wa, sin_wa, need_ctx)
        oc_c, oc_l = mla_branch(ml_c, ml_l, mla_qnorm_g[l], mla_kvnorm_g[l], mla_w_uq[l], mla_w_ukv[l],
                                cos_mla, sin_mla, need_ctx)
        xl = xl + mod_l[2] * merge_branches(gt_l, (oa_l, ob_l, oc_l), w_branch[l], w_out[l])

        hl = modulate(rmsnorm(xl, norm2_g[l]), mod_l[3], mod_l[4])
        xl = xl + mod_l[5] * conv_glu(hl, ffn_w_in[l], ffn_conv_w[l], ffn_conv_b[l], ffn_w_out[l])

        if need_ctx:
            xc = xc + mod_c[2] * merge_branches(gt_c, (oa_c, ob_c, oc_c), w_branch[l], w_out[l])
            hc = modulate(rmsnorm(xc, norm2_g[l]), mod_c[3], mod_c[4])
            xc = xc + mod_c[5] * conv_glu(hc, ffn_w_in[l], ffn_conv_w[l], ffn_conv_b[l], ffn_w_out[l])
    return rmsnorm(xl, final_norm_g)
```

```python
import functools

import jax
import jax.numpy as jnp
import numpy as np
from jax import lax
from jax.experimental import pallas as pl
from jax.experimental.pallas import tpu as pltpu

F32 = jnp.float32
BF16 = jnp.bfloat16
HIGHEST = lax.Precision.HIGHEST

NORM_EPS = 1e-6
NEG_INF = -1e30
GRID_W = 64
ROPE_BASE = 10000.0

RW_HEADS = 16
RW_HEAD_DIM = 64
RW_WIDTH = RW_HEADS * RW_HEAD_DIM
RW_DECAY_LORA = 96
RW_AAA_LORA = 96
RW_GATE_LORA = 64
RW_LNX_EPS = 64e-5
RW_CHUNK = 64
LORA_PAD = 128

WA_HEADS = 16
WA_KV_HEADS = 4
WA_GROUP = WA_HEADS // WA_KV_HEADS
WA_HEAD_DIM = 64
WA_WIDTH = WA_HEADS * WA_HEAD_DIM
WA_KV_WIDTH = WA_KV_HEADS * WA_HEAD_DIM
WINDOW = 128
WA_SCALE = WA_HEAD_DIM ** -0.5

MLA_HEADS = 8
MLA_NOPE = 128
MLA_ROPE = 64
MLA_V = 128
MLA_Q_LORA = 512
MLA_KV_LORA = 512
MLA_QK_PAD = 256
MLA_SCALE = (MLA_NOPE + MLA_ROPE) ** -0.5

CONV_W = 3
VMEM_LIMIT_BYTES = 56 * 1024 * 1024


def _cparams(*sem):
    return pltpu.CompilerParams(dimension_semantics=sem, vmem_limit_bytes=VMEM_LIMIT_BYTES)


def _dot(a, b, precision=None):
    return jnp.dot(a, b, preferred_element_type=F32, precision=precision)


def _dot_nt(a, b, precision=None):
    return lax.dot_general(a, b, (((1,), (1,)), ((), ())), preferred_element_type=F32, precision=precision)


def _dot_tn(a, b, precision=None):
    return lax.dot_general(a, b, (((0,), (0,)), ((), ())), preferred_element_type=F32, precision=precision)


def _pick_tile(n, candidates):
    for c in candidates:
        if n % c == 0:
            return c
    raise ValueError(f"no tile in {candidates} divides {n}")


def _pad_cols(w, n):
    return jnp.pad(w, [(0, 0)] * (w.ndim - 1) + [(0, n - w.shape[-1])])


def _round_up(n, m):
    return (n + m - 1) // m * m


class Geom:
    def __init__(self, B, T, L):
        assert T & (T - 1) == 0 and L & (L - 1) == 0, "sequence lengths must be powers of two"
        assert T % L == 0 and L % RW_CHUNK == 0 and T % GRID_W == 0
        self.B, self.T, self.L = B, T, L
        self.BT = B * T
        self.R = B * T + B * L

    def group_of_tile(self, i, tm):
        return jnp.minimum((i * tm) // self.T, self.B)


def _seq_edge_masks(geom, row0, tm):
    r = row0 + lax.broadcasted_iota(jnp.int32, (tm, 1), 0)
    is_lat = r < geom.BT
    pos = jnp.where(is_lat, r & (geom.T - 1), (r - geom.BT) & (geom.L - 1))
    last = jnp.where(is_lat, geom.T - 1, geom.L - 1)
    return pos == 0, pos == last


def _shifted_rows(x, prev8, next8, first, last):
    tm = x.shape[0]
    rid = lax.broadcasted_iota(jnp.int32, (tm, 1), 0)
    up = jnp.where(rid == 0, prev8[7:8, :], pltpu.roll(x, 1, axis=0))
    dn = jnp.where(rid == tm - 1, next8[0:1, :], pltpu.roll(x, tm - 1, axis=0))
    return jnp.where(first, 0.0, up), jnp.where(last, 0.0, dn)


def _halo_specs(tm, R, width, col_of):
    nb8 = tm // 8
    prev = pl.BlockSpec((8, width), lambda i, *a: (jnp.maximum(i * nb8 - 1, 0), col_of(i, *a)))
    nxt = pl.BlockSpec((8, width), lambda i, *a: (jnp.minimum((i + 1) * nb8, R // 8 - 1), col_of(i, *a)))
    return prev, nxt


def _ada_kernel(c_ref, w_ref, b_ref, o_ref):
    c = c_ref[...]
    o_ref[...] = _dot(c * jax.nn.sigmoid(c), w_ref[...], HIGHEST) + b_ref[...]


def ada_modulation(cvec, w, b):
    G, D = cvec.shape
    N = w.shape[1]
    tn = _pick_tile(N, (1024, 512, 256, 128))
    return pl.pallas_call(
        _ada_kernel,
        grid=(N // tn,),
        in_specs=[pl.BlockSpec((G, D), lambda j: (0, 0)),
                  pl.BlockSpec((D, tn), lambda j: (0, j)),
                  pl.BlockSpec((1, tn), lambda j: (0, j))],
        out_specs=pl.BlockSpec((G, tn), lambda j: (0, j)),
        out_shape=jax.ShapeDtypeStruct((G, N), F32),
        compiler_params=_cparams("arbitrary"),
    )(cvec, w, b.reshape(1, N))


def _nmm_kernel(x_ref, g_ref, sh_ref, sc_ref, w_ref, o_ref, h_ref, *, act):
    @pl.when(pl.program_id(1) == 0)
    def _():
        x = x_ref[...]
        n = x * lax.rsqrt(jnp.mean(x * x, axis=-1, keepdims=True) + NORM_EPS) * g_ref[...]
        h_ref[...] = (n * (1.0 + sc_ref[0]) + sh_ref[0]).astype(BF16)

    acc = _dot(h_ref[...], w_ref[...])
    if act == "sigmoid":
        acc = jax.nn.sigmoid(acc)
    o_ref[...] = acc.astype(o_ref.dtype)


def norm_mod_matmul(geom, x, g, shift, scale, w, *, tm, tn, act=None, out_dtype=F32, x_col_block=0):
    R = x.shape[0]
    K, N = w.shape
    assert R % tm == 0 and N % tn == 0 and geom.T % tm == 0
    grp = lambda i, j: (geom.group_of_tile(i, tm), 0, 0)
    return pl.pallas_call(
        functools.partial(_nmm_kernel, act=act),
        grid=(R // tm, N // tn),
        in_specs=[pl.BlockSpec((tm, K), lambda i, j: (i, x_col_block)),
                  pl.BlockSpec((1, K), lambda i, j: (0, 0)),
                  pl.BlockSpec((1, 1, K), grp),
                  pl.BlockSpec((1, 1, K), grp),
                  pl.BlockSpec((K, tn), lambda i, j: (0, j))],
        out_specs=pl.BlockSpec((tm, tn), lambda i, j: (i, j)),
        out_shape=jax.ShapeDtypeStruct((R, N), out_dtype),
        scratch_shapes=[pltpu.VMEM((tm, K), BF16)],
        compiler_params=_cparams("parallel", "arbitrary"),
    )(x, g.reshape(1, K), shift, scale, w)


def _mm_resid_kernel(y_ref, w_ref, r_ref, gate_ref, o_ref):
    o_ref[...] = r_ref[...] + gate_ref[0] * _dot(y_ref[...], w_ref[...])


def matmul_gated_residual(geom, y, w, resid, gate, *, tm, tn):
    R, K = y.shape
    N = w.shape[1]
    assert R % tm == 0 and N % tn == 0 and geom.T % tm == 0
    return pl.pallas_call(
        _mm_resid_kernel,
        grid=(R // tm, N // tn),
        in_specs=[pl.BlockSpec((tm, K), lambda i, j: (i, 0)),
                  pl.BlockSpec((K, tn), lambda i, j: (0, j)),
                  pl.BlockSpec((tm, tn), lambda i, j: (i, j)),
                  pl.BlockSpec((1, 1, tn), lambda i, j: (geom.group_of_tile(i, tm), 0, j))],
        out_specs=pl.BlockSpec((tm, tn), lambda i, j: (i, j)),
        out_shape=jax.ShapeDtypeStruct((R, N), F32),
        compiler_params=_cparams("parallel", "arbitrary"),
    )(y, w, resid, gate)


def _merge_kernel(oa_ref, ob_ref, oc_ref, ga_ref, gb_ref, gc_ref, w_ref, o_ref):
    y = ga_ref[...] * _dot(oa_ref[...], w_ref[0])
    y = y + gb_ref[...] * _dot(ob_ref[...], w_ref[1])
    y = y + gc_ref[...] * _dot(oc_ref[...], w_ref[2])
    o_ref[...] = y.astype(o_ref.dtype)


def merge_branches(oa, ob, oc, gates, wb, *, tm, tn):
    R, K = oa.shape
    D = wb.shape[2]
    nj = D // tn
    bspec = pl.BlockSpec((tm, K), lambda i, j: (i, 0))
    gspec = lambda k: pl.BlockSpec((tm, tn), lambda i, j: (i, k * nj + j))
    return pl.pallas_call(
        _merge_kernel,
        grid=(R // tm, nj),
        in_specs=[bspec, bspec, bspec, gspec(0), gspec(1), gspec(2),
                  pl.BlockSpec((3, K, tn), lambda i, j: (0, 0, j))],
        out_specs=pl.BlockSpec((tm, tn), lambda i, j: (i, j)),
        out_shape=jax.ShapeDtypeStruct((R, D), BF16),
        compiler_params=_cparams("parallel", "arbitrary"),
    )(oa, ob, oc, gates, gates, gates, wb)


def _convglu_kernel(gt_ref, gp_ref, gn_ref, u_ref, cw_ref, cb_ref, o_ref, *, geom, tm):
    first, last = _seq_edge_masks(geom, pl.program_id(0) * tm, tm)
    gt = gt_ref[...]
    up, dn = _shifted_rows(gt, gp_ref[...], gn_ref[...], first, last)
    cw = cw_ref[...]
    conv = cb_ref[...] + up * cw[0:1, :]
    conv = conv + gt * cw[1:2, :]
    conv = conv + dn * cw[2:3, :]
    o_ref[...] = (jax.nn.gelu(conv, approximate=True) * u_ref[...]).astype(o_ref.dtype)


def conv_glu_middle(geom, h, conv_w, conv_b, *, tm, tf):
    R, F2 = h.shape
    F = F2 // 2
    nj = F // tf
    prev, nxt = _halo_specs(tm, R, tf, lambda i, j: j)
    return pl.pallas_call(
        functools.partial(_convglu_kernel, geom=geom, tm=tm),
        grid=(R // tm, nj),
        in_specs=[pl.BlockSpec((tm, tf), lambda i, j: (i, j)), prev, nxt,
                  pl.BlockSpec((tm, tf), lambda i, j: (i, nj + j)),
                  pl.BlockSpec((CONV_W, tf), lambda i, j: (0, j)),
                  pl.BlockSpec((1, tf), lambda i, j: (0, j))],
        out_specs=pl.BlockSpec((tm, tf), lambda i, j: (i, j)),
        out_shape=jax.ShapeDtypeStruct((R, F), BF16),
        compiler_params=_cparams("parallel", "arbitrary"),
    )(h, h, h, h, conv_w, conv_b.reshape(1, F))


def _rmsnorm_kernel(x_ref, g_ref, o_ref):
    x = x_ref[...]
    o_ref[...] = x * lax.rsqrt(jnp.mean(x * x, axis=-1, keepdims=True) + NORM_EPS) * g_ref[...]


def final_rmsnorm(x, g, rows, *, tm):
    D = x.shape[1]
    return pl.pallas_call(
        _rmsnorm_kernel,
        grid=(rows // tm,),
        in_specs=[pl.BlockSpec((tm, D), lambda i: (i, 0)), pl.BlockSpec((1, D), lambda i: (0, 0))],
        out_specs=pl.BlockSpec((tm, D), lambda i: (i, 0)),
        out_shape=jax.ShapeDtypeStruct((rows, D), F32),
        compiler_params=_cparams("parallel"),
    )(x, g.reshape(1, D))


def _rot_half64(z):
    n = z.shape[-1]
    lane = lax.broadcasted_iota(jnp.int32, z.shape, z.ndim - 1)
    return jnp.where((lane & 63) < 32, pltpu.roll(z, n - 32, axis=z.ndim - 1), pltpu.roll(z, 32, axis=z.ndim - 1))


def _rope_tables(geom, dim):
    nf = dim // 4
    inv = ROPE_BASE ** (-jnp.arange(nf, dtype=F32) / nf)
    rows = geom.T // GRID_W
    row = jnp.repeat(jnp.arange(rows, dtype=F32), GRID_W)
    col = jnp.tile(jnp.arange(GRID_W, dtype=F32), rows)
    ang = jnp.concatenate([row[:, None] * inv, col[:, None] * inv], axis=-1)
    cos, sin = jnp.cos(ang), jnp.sin(ang)
    cos_t = jnp.concatenate([cos, cos], axis=-1)
    sin_t = jnp.concatenate([-sin, sin], axis=-1)
    nctx = geom.B * geom.L
    cos_f = jnp.concatenate([jnp.tile(cos_t, (geom.B, 1)), jnp.ones((nctx, dim), F32)], axis=0)
    sin_f = jnp.concatenate([jnp.tile(sin_t, (geom.B, 1)), jnp.zeros((nctx, dim), F32)], axis=0)
    return cos_f, sin_f


def _wa_prep_kernel(z_ref, cos_ref, sin_ref, q_ref, k_ref, v_ref):
    cos = cos_ref[...]
    sin = sin_ref[...]
    for c in range(WA_WIDTH // 128):
        z = z_ref[:, c * 128:(c + 1) * 128]
        q_ref[:, c * 128:(c + 1) * 128] = ((z * cos + _rot_half64(z) * sin) * WA_SCALE).astype(BF16)
    for c in range(WA_KV_WIDTH // 128):
        z = z_ref[:, WA_WIDTH + c * 128:WA_WIDTH + (c + 1) * 128]
        k_ref[:, c * 128:(c + 1) * 128] = (z * cos + _rot_half64(z) * sin).astype(BF16)
    v_ref[...] = z_ref[:, WA_WIDTH + WA_KV_WIDTH:].astype(BF16)


def wa_prep(z, cos, sin, *, tm):
    R = z.shape[0]
    row = lambda w: pl.BlockSpec((tm, w), lambda i: (i, 0))
    return pl.pallas_call(
        _wa_prep_kernel,
        grid=(R // tm,),
        in_specs=[row(WA_WIDTH + 2 * WA_KV_WIDTH), row(128), row(128)],
        out_specs=[row(WA_WIDTH), row(WA_KV_WIDTH), row(WA_KV_WIDTH)],
        out_shape=[jax.ShapeDtypeStruct((R, WA_WIDTH), BF16),
                   jax.ShapeDtypeStruct((R, WA_KV_WIDTH), BF16),
                   jax.ShapeDtypeStruct((R, WA_KV_WIDTH), BF16)],
        compiler_params=_cparams("parallel"),
    )(z, cos, sin)


def _wa_attn_kernel(*refs, local, nqb, tq):
    if local:
        sink_ref, q_ref, kp_ref, kc_ref, kn_ref, vp_ref, vc_ref, vn_ref, kx_ref, vx_ref, o_ref = refs
    else:
        sink_ref, q_ref, kx_ref, vx_ref, o_ref = refs
    i = pl.program_id(1)
    nk_ctx = kx_ref.shape[0]
    rows = WA_GROUP * tq
    qpos = lax.broadcasted_iota(jnp.int32, (rows, 1), 0) & (tq - 1)
    head_in_group = lax.broadcasted_iota(jnp.int32, (rows, 1), 0) >> int(np.log2(tq))
    if local:
        off_prev = jnp.where(i > 0, 0, tq)
        off_next = jnp.where(i < nqb - 1, 0, tq)
        j = lax.broadcasted_iota(jnp.int32, (1, 3 * tq + nk_ctx), 1)
        valid = ((j >= tq) & (j < 2 * tq)) | (j >= 3 * tq)
        valid = valid | ((j < tq) & (j >= qpos + off_prev))
        valid = valid | ((j >= 2 * tq) & (j < 3 * tq) & ((j - 2 * tq) <= qpos - off_next))
    for g in range(WA_KV_HEADS):
        ks = slice(g * WA_HEAD_DIM, (g + 1) * WA_HEAD_DIM)
        qg = jnp.concatenate(
            [q_ref[:, (g * WA_GROUP + a) * WA_HEAD_DIM:(g * WA_GROUP + a + 1) * WA_HEAD_DIM] for a in range(WA_GROUP)],
            axis=0)
        sink = jnp.zeros((rows, 1), F32)
        for a in range(WA_GROUP):
            sink = jnp.where(head_in_group == a, sink_ref[g * WA_GROUP + a], sink)
        if local:
            kcat = jnp.concatenate([kp_ref[:, ks], kc_ref[:, ks], kn_ref[:, ks], kx_ref[:, ks]], axis=0)
            vcat = jnp.concatenate([vp_ref[:, ks], vc_ref[:, ks], vn_ref[:, ks], vx_ref[:, ks]], axis=0)
        else:
            kcat, vcat = kx_ref[:, ks], vx_ref[:, ks]
        s = _dot_nt(qg, kcat)
        if local:
            s = jnp.where(valid, s, NEG_INF)
        m = jnp.maximum(jnp.max(s, axis=-1, keepdims=True), sink)
        e = jnp.exp(s - m)
        denom = jnp.sum(e, axis=-1, keepdims=True) + jnp.exp(sink - m)
        o = _dot((e / denom).astype(BF16), vcat)
        for a in range(WA_GROUP):
            h = g * WA_GROUP + a
            o_ref[:, h * WA_HEAD_DIM:(h + 1) * WA_HEAD_DIM] = o[a * tq:(a + 1) * tq].astype(o_ref.dtype)


def wa_attention(geom, q, k, v, sink, *, local):
    B, T, L = geom.B, geom.T, geom.L
    sink_spec = pl.BlockSpec(memory_space=pltpu.SMEM)
    ctx_spec = pl.BlockSpec((L, WA_KV_WIDTH), lambda b, i: (geom.BT // L + b, 0))
    if local:
        tq = WINDOW
        nqb = T // tq
        kv = lambda f: pl.BlockSpec((tq, WA_KV_WIDTH), lambda b, i: (b * nqb + f(i), 0))
        prev = lambda i: jnp.maximum(i - 1, 0)
        cur = lambda i: i
        nxt = lambda i: jnp.minimum(i + 1, nqb - 1)
        in_specs = [sink_spec, pl.BlockSpec((tq, WA_WIDTH), lambda b, i: (b * nqb + i, 0)),
                    kv(prev), kv(cur), kv(nxt), kv(prev), kv(cur), kv(nxt), ctx_spec, ctx_spec]
        args = (sink, q, k, k, k, v, v, v, k, v)
        out_rows, out_spec = geom.BT, pl.BlockSpec((tq, WA_WIDTH), lambda b, i: (b * nqb + i, 0))
    else:
        tq, nqb = L, 1
        in_specs = [sink_spec, pl.BlockSpec((tq, WA_WIDTH), lambda b, i: (geom.BT // L + b, 0)), ctx_spec, ctx_spec]
        args = (sink, q, k, v)
        out_rows, out_spec = B * L, pl.BlockSpec((tq, WA_WIDTH), lambda b, i: (b, 0))
    return pl.pallas_call(
        functools.partial(_wa_attn_kernel, local=local, nqb=nqb, tq=tq),
        grid=(B, nqb),
        in_specs=in_specs,
        out_specs=out_spec,
        out_shape=jax.ShapeDtypeStruct((out_rows, WA_WIDTH), BF16),
        compiler_params=_cparams("parallel", "arbitrary"),
    )(*args)


def _mla_prep_kernel(z_ref, qg_ref, kvg_ref, wq_ref, wkv_ref, cos_ref, sin_ref, q_ref, k_ref, v_ref):
    def norm(x, g):
        return (x * lax.rsqrt(jnp.mean(x * x, axis=-1, keepdims=True) + NORM_EPS) * g).astype(BF16)

    cos = cos_ref[...]
    sin = sin_ref[...]
    q = _dot(norm(z_ref[:, :MLA_Q_LORA], qg_ref[...]), wq_ref[...])
    kv = _dot(norm(z_ref[:, MLA_Q_LORA:MLA_Q_LORA + MLA_KV_LORA], kvg_ref[...]), wkv_ref[...])
    kr = z_ref[:, MLA_Q_LORA + MLA_KV_LORA:MLA_Q_LORA + MLA_KV_LORA + 128]
    kr = (kr * cos[:, 128:] + _rot_half64(kr) * sin[:, 128:]).astype(BF16)
    for h in range(MLA_HEADS):
        qh = q[:, h * MLA_QK_PAD:(h + 1) * MLA_QK_PAD]
        q_ref[h] = ((qh * cos + _rot_half64(qh) * sin) * MLA_SCALE).astype(BF16)
        k_ref[h, :, :MLA_NOPE] = kv[:, h * 256:h * 256 + MLA_NOPE].astype(BF16)
        k_ref[h, :, MLA_NOPE:] = kr
        v_ref[h] = kv[:, h * 256 + MLA_NOPE:(h + 1) * 256].astype(BF16)


def mla_prep(z, qnorm_g, kvnorm_g, wq, wkv, cos, sin, *, tm):
    R, Z = z.shape
    full = lambda a: pl.BlockSpec(a.shape, lambda i: (0,) * a.ndim)
    qg, kvg = qnorm_g.reshape(1, -1), kvnorm_g.reshape(1, -1)
    hd = lambda w: pl.BlockSpec((MLA_HEADS, tm, w), lambda i: (0, i, 0))
    return pl.pallas_call(
        _mla_prep_kernel,
        grid=(R // tm,),
        in_specs=[pl.BlockSpec((tm, Z), lambda i: (i, 0)), full(qg), full(kvg), full(wq), full(wkv),
                  pl.BlockSpec((tm, MLA_QK_PAD), lambda i: (i, 0)), pl.BlockSpec((tm, MLA_QK_PAD), lambda i: (i, 0))],
        out_specs=[hd(MLA_QK_PAD), hd(MLA_QK_PAD), hd(MLA_V)],
        out_shape=[jax.ShapeDtypeStruct((MLA_HEADS, R, MLA_QK_PAD), BF16),
                   jax.ShapeDtypeStruct((MLA_HEADS, R, MLA_QK_PAD), BF16),
                   jax.ShapeDtypeStruct((MLA_HEADS, R, MLA_V), BF16)],
        compiler_params=_cparams("parallel"),
    )(z, qg, kvg, wq, wkv, cos, sin)


def _mla_flash_kernel(*refs, with_latent):
    if with_latent:
        q_ref, kx_ref, vx_ref, k_ref, v_ref, o_ref, m_ref, l_ref, acc_ref = refs
    else:
        q_ref, kx_ref, vx_ref, o_ref, m_ref, l_ref, acc_ref = refs
    ki = pl.program_id(3)
    q = q_ref[0]

    @pl.when(ki == 0)
    def _():
        s = _dot_nt(q, kx_ref[0])
        m = jnp.max(s, axis=-1, keepdims=True)
        e = jnp.exp(s - m)
        m_ref[...] = m
        l_ref[...] = jnp.sum(e, axis=-1, keepdims=True)
        acc_ref[...] = _dot(e.astype(BF16), vx_ref[0])

    if with_latent:
        s = _dot_nt(q, k_ref[0])
        m_old = m_ref[...]
        m_new = jnp.maximum(m_old, jnp.max(s, axis=-1, keepdims=True))
        alpha = jnp.exp(m_old - m_new)
        e = jnp.exp(s - m_new)
        m_ref[...] = m_new
        l_ref[...] = alpha * l_ref[...] + jnp.sum(e, axis=-1, keepdims=True)
        acc_ref[...] = alpha * acc_ref[...] + _dot(e.astype(BF16), v_ref[0])

    @pl.when(ki == pl.num_programs(3) - 1)
    def _():
        o_ref[...] = (acc_ref[...] / l_ref[...]).astype(o_ref.dtype)


def mla_attention(geom, q, k, v, *, with_latent, tq, tk):
    B, T, L = geom.B, geom.T, geom.L
    cblk = geom.BT // L
    ctx_k = pl.BlockSpec((1, L, MLA_QK_PAD), lambda b, h, qi, ki: (h, cblk + b, 0))
    ctx_v = pl.BlockSpec((1, L, MLA_V), lambda b, h, qi, ki: (h, cblk + b, 0))
    if with_latent:
        nq, nk = T // tq, T // tk
        in_specs = [pl.BlockSpec((1, tq, MLA_QK_PAD), lambda b, h, qi, ki: (h, b * nq + qi, 0)), ctx_k, ctx_v,
                    pl.BlockSpec((1, tk, MLA_QK_PAD), lambda b, h, qi, ki: (h, b * nk + ki, 0)),
                    pl.BlockSpec((1, tk, MLA_V), lambda b, h, qi, ki: (h, b * nk + ki, 0))]
        args = (q, k, v, k, v)
        out_rows, out_spec = geom.BT, pl.BlockSpec((tq, MLA_V), lambda b, h, qi, ki: (b * nq + qi, h))
    else:
        tq, nq, nk = L, 1, 1
        in_specs = [pl.BlockSpec((1, tq, MLA_QK_PAD), lambda b, h, qi, ki: (h, cblk + b, 0)), ctx_k, ctx_v]
        args = (q, k, v)
        out_rows, out_spec = B * L, pl.BlockSpec((tq, MLA_V), lambda b, h, qi, ki: (b, h))
    return pl.pallas_call(
        functools.partial(_mla_flash_kernel, with_latent=with_latent),
        grid=(B, MLA_HEADS, nq, nk),
        in_specs=in_specs,
        out_specs=out_spec,
        out_shape=jax.ShapeDtypeStruct((out_rows, MLA_HEADS * MLA_V), BF16),
        scratch_shapes=[pltpu.VMEM((tq, 1), F32), pltpu.VMEM((tq, 1), F32), pltpu.VMEM((tq, MLA_V), F32)],
        compiler_params=_cparams("parallel", "parallel", "parallel", "arbitrary"),
    )(*args)


RW_Z_R, RW_Z_K, RW_Z_V = 0, RW_WIDTH, 2 * RW_WIDTH
RW_Z_LORA = 3 * RW_WIDTH
RW_Z_COLS = 3 * RW_WIDTH + 5 * LORA_PAD


def _head_sum(x, ones):
    return jnp.concatenate(
        [_dot(x[:, c * 128:(c + 1) * 128], ones, HIGHEST) for c in range(x.shape[1] // 128)], axis=-1)


def _rw_prep_kernel(*refs, geom, tm, has_vres):
    if has_vres:
        (z_ref, zp_ref, zn_ref, mu_ref, w0_ref, w2_ref, a0_ref, a2_ref, g2_ref, kk_ref, ka_ref, rk_ref, ones_ref,
         vf_ref, v0_ref, v1_ref, v2_ref,
         r_o, v_o, kk_o, lwf_o, kf_o, bf_o, lwb_o, kb_o, bb_o, g_o, bonus_o) = refs
    else:
        (z_ref, zp_ref, zn_ref, mu_ref, w0_ref, w2_ref, a0_ref, a2_ref, g2_ref, kk_ref, ka_ref, rk_ref, ones_ref,
         r_o, v_o, kk_o, lwf_o, kf_o, bf_o, lwb_o, kb_o, bb_o, g_o, bonus_o) = refs
    first, last = _seq_edge_masks(geom, pl.program_id(0) * tm, tm)
    z = z_ref[...]
    up, dn = _shifted_rows(z, zp_ref[...], zn_ref[...], first, last)
    z = z + mu_ref[...] * (0.5 * (up + dn) - z)
    r = z[:, RW_Z_R:RW_Z_R + RW_WIDTH]
    k = z[:, RW_Z_K:RW_Z_K + RW_WIDTH]
    v = z[:, RW_Z_V:RW_Z_V + RW_WIDTH]
    lora = lambda n: z[:, RW_Z_LORA + n * LORA_PAD:RW_Z_LORA + (n + 1) * LORA_PAD]
    ones = ones_ref[...]
    if has_vres:
        mix = jax.nn.sigmoid(v0_ref[...] + _dot(_dot(v, v1_ref[...], HIGHEST), v2_ref[...], HIGHEST))
        v = v + (vf_ref[...] - v) * mix
    g_o[...] = _dot(jax.nn.sigmoid(lora(4)), g2_ref[...], HIGHEST)
    kk = k * kk_ref[...]
    kk = kk / jnp.maximum(jnp.sqrt(_head_sum(kk * kk, ones)), 1e-12)
    ksum = None
    for d, (lw_o, k_o, b_o) in enumerate(((lwf_o, kf_o, bf_o), (lwb_o, kb_o, bb_o))):
        x = -(w0_ref[d:d + 1, :] + _dot(jnp.tanh(lora(d)), w2_ref[d], HIGHEST))
        softplus = jnp.maximum(x, 0.0) + jnp.log1p(jnp.exp(-jnp.abs(x)))
        lw_o[...] = -jnp.exp(-softplus - 0.5)
        a = jax.nn.sigmoid(a0_ref[d:d + 1, :] + _dot(lora(2 + d), a2_ref[d], HIGHEST))
        kd = k * (1.0 + (a - 1.0) * ka_ref[...])
        k_o[...] = kd
        b_o[...] = kk * a
        ksum = kd if ksum is None else ksum + kd
    r_o[...] = r
    v_o[...] = v
    kk_o[...] = kk
    bonus_o[...] = _head_sum(r * ksum * rk_ref[...], ones) * v


def rw_prep(geom, z, p, v_first, vres, *, tm):
    R = z.shape[0]
    has_vres = vres is not None
    full = lambda a: pl.BlockSpec(a.shape, lambda i: (0,) * a.ndim)
    row = pl.BlockSpec((tm, RW_WIDTH), lambda i: (i, 0))
    prev, nxt = _halo_specs(tm, R, RW_Z_COLS, lambda i: 0)
    params = [p["mu"], p["w0"], p["w2"], p["a0"], p["a2"], p["g2"], p["kk"], p["ka"], p["rk"], p["ones"]]
    in_specs = [pl.BlockSpec((tm, RW_Z_COLS), lambda i: (i, 0)), prev, nxt] + [full(a) for a in params]
    args = [z, z, z] + params
    if has_vres:
        in_specs += [row] + [full(a) for a in vres]
        args += [v_first] + list(vres)
    return pl.pallas_call(
        functools.partial(_rw_prep_kernel, geom=geom, tm=tm, has_vres=has_vres),
        grid=(R // tm,),
        in_specs=in_specs,
        out_specs=[row] * 11,
        out_shape=[jax.ShapeDtypeStruct((R, RW_WIDTH), F32)] * 11,
        compiler_params=_cparams("parallel"),
    )(*args)


def _rw_chunk_kernel(r_ref, v_ref, kk_ref, lwf_ref, kf_ref, bf_ref, lwb_ref, kb_ref, bb_ref,
                     rrf_o, ylf_o, mf_o, nf_o, rrb_o, ylb_o, mb_o, nb_o):
    C = RW_CHUNK
    N = RW_HEAD_DIM
    ri = lax.broadcasted_iota(jnp.int32, (C, C), 0)
    ci = lax.broadcasted_iota(jnp.int32, (C, C), 1)
    eye = ri == ci
    r, v, kk = r_ref[...], v_ref[...], kk_ref[...]
    for lw_ref, k_ref, b_ref, rr_o, yl_o, m_o, n_o, before, tot_row in (
            (lwf_ref, kf_ref, bf_ref, rrf_o, ylf_o, mf_o, nf_o, ci < ri, C - 1),
            (lwb_ref, kb_ref, bb_ref, rrb_o, ylb_o, mb_o, nb_o, ci > ri, 0)):
        lw, kd, bd = lw_ref[...], k_ref[...], b_ref[...]
        incl = (before | eye).astype(F32)
        c = _dot(incl, lw, HIGHEST)
        c_tot = c[tot_row:tot_row + 1, :]
        at = -kk * jnp.exp(c - lw)
        rt = r * jnp.exp(c)
        e_neg = jnp.exp(-c)
        bt, kt = bd * e_neg, kd * e_neg
        e_rest = jnp.exp(c_tot - c)
        bc, kc = bd * e_rest, kd * e_rest
        e_tot = jnp.exp(c_tot)
        for h in range(RW_HEADS):
            hs = slice(h * N, (h + 1) * N)
            vh = v[:, hs]
            p = _dot_nt(jnp.concatenate([at[:, hs], rt[:, hs]], axis=0),
                        jnp.concatenate([bt[:, hs], kt[:, hs]], axis=0), HIGHEST)
            a_ab = jnp.where(before, p[:C, :C], 0.0)
            a_ak = jnp.where(before, p[:C, C:], 0.0)
            a_rb = jnp.where(before | eye, p[C:, :C], 0.0)
            a_rk = jnp.where(before | eye, p[C:, C:], 0.0)
            tinv = jnp.where(eye, 1.0, a_ab)
            pw = a_ab
            for _ in range(int(np.log2(C)) - 1):
                pw = _dot(pw, pw, HIGHEST)
                tinv = tinv + _dot(tinv, pw, HIGHEST)
            tw = _dot(tinv, jnp.concatenate([at[:, hs], _dot(a_ak, vh, HIGHEST)], axis=1), HIGHEST)
            ry = _dot(a_rb, tw, HIGHEST)
            rr_o[:, hs] = rt[:, hs] + ry[:, :N]
            yl_o[:, hs] = ry[:, N:] + _dot(a_rk, vh, HIGHEST)
            mn = _dot_tn(bc[:, hs], tw, HIGHEST)
            m_o[:, hs] = mn[:, :N] + jnp.where(eye, e_tot[:, hs], 0.0)
            n_o[:, hs] = mn[:, N:] + _dot_tn(kc[:, hs], vh, HIGHEST)


def rw_chunk(arrs):
    R = arrs[0].shape[0]
    blk = pl.BlockSpec((RW_CHUNK, RW_WIDTH), lambda i: (i, 0))
    return pl.pallas_call(
        _rw_chunk_kernel,
        grid=(R // RW_CHUNK,),
        in_specs=[blk] * 9,
        out_specs=[blk] * 8,
        out_shape=[jax.ShapeDtypeStruct((R, RW_WIDTH), F32)] * 8,
        compiler_params=_cparams("parallel"),
    )(*arrs)


def _rw_scan_kernel(rrf_ref, ylf_ref, mf_ref, nf_ref, rrb_ref, ylb_ref, mb_ref, nb_ref, yf_o, yb_o, h_ref, *, nchunk):
    C = RW_CHUNK
    N = RW_HEAD_DIM

    @pl.when(pl.program_id(1) == 0)
    def _():
        h_ref[...] = jnp.zeros_like(h_ref)

    for d, (rr_ref, yl_ref, m_ref, n_ref, y_o) in enumerate(
            ((rrf_ref, ylf_ref, mf_ref, nf_ref, yf_o), (rrb_ref, ylb_ref, mb_ref, nb_ref, yb_o))):
        order = range(nchunk) if d == 0 else range(nchunk - 1, -1, -1)
        for h in range(RW_HEADS):
            hs = slice(h * N, (h + 1) * N)
            state = h_ref[d, :, hs]
            for c in order:
                rows = slice(c * C, (c + 1) * C)
                y_o[rows, hs] = _dot(rr_ref[rows, hs], state, HIGHEST) + yl_ref[rows, hs]
                state = _dot(m_ref[rows, hs], state, HIGHEST) + n_ref[rows, hs]
            h_ref[d, :, hs] = state


def rw_scan(geom, chunk_out):
    B, T, L = geom.B, geom.T, geom.L
    blk = L
    nlat = T // blk
    cblk = geom.BT // blk
    fwd = pl.BlockSpec((blk, RW_WIDTH), lambda b, s: (jnp.where(s == 0, cblk + b, b * nlat + s - 1), 0))
    bwd = pl.BlockSpec((blk, RW_WIDTH), lambda b, s: (jnp.where(s == 0, cblk + b, b * nlat + nlat - s), 0))
    return pl.pallas_call(
        functools.partial(_rw_scan_kernel, nchunk=blk // RW_CHUNK),
        grid=(B, nlat + 1),
        in_specs=[fwd] * 4 + [bwd] * 4,
        out_specs=[fwd, bwd],
        out_shape=[jax.ShapeDtypeStruct((geom.R, RW_WIDTH), F32)] * 2,
        scratch_shapes=[pltpu.VMEM((2, RW_HEAD_DIM, RW_WIDTH), F32)],
        compiler_params=_cparams("parallel", "arbitrary"),
    )(*chunk_out)


def _rw_post_kernel(yf_ref, yb_ref, bonus_ref, g_ref, lng_ref, lnb_ref, ones_ref, o_ref):
    ones = ones_ref[...]
    y = yf_ref[...] + yb_ref[...]
    mean = _head_sum(y, ones) * (1.0 / RW_HEAD_DIM)
    yc = y - mean
    var = _head_sum(yc * yc, ones) * (1.0 / RW_HEAD_DIM)
    y = yc * lax.rsqrt(var + RW_LNX_EPS) * lng_ref[...] + lnb_ref[...]
    o_ref[...] = ((y + bonus_ref[...]) * g_ref[...]).astype(o_ref.dtype)


def rw_post(yf, yb, bonus, g, lnx_g, lnx_b, ones, *, tm):
    R = yf.shape[0]
    row = pl.BlockSpec((tm, RW_WIDTH), lambda i: (i, 0))
    full = lambda a: pl.BlockSpec(a.shape, lambda i: (0,) * a.ndim)
    return pl.pallas_call(
        _rw_post_kernel,
        grid=(R // tm,),
        in_specs=[row] * 4 + [full(lnx_g), full(lnx_b), full(ones)],
        out_specs=row,
        out_shape=jax.ShapeDtypeStruct((R, RW_WIDTH), BF16),
        compiler_params=_cparams("parallel"),
    )(yf, yb, bonus, g, lnx_g, lnx_b, ones)


def _rw_in_cols(w):
    parts = [w[..., :3 * RW_WIDTH]]
    off = 3 * RW_WIDTH
    for n in (RW_DECAY_LORA, RW_DECAY_LORA, RW_AAA_LORA, RW_AAA_LORA, RW_GATE_LORA):
        parts.append(_pad_cols(w[..., off:off + n], LORA_PAD))
        off += n
    return jnp.concatenate(parts, axis=-1)


def _pad_rows(w, n):
    return jnp.pad(w, [(0, 0)] * (w.ndim - 2) + [(0, n - w.shape[-2]), (0, 0)])


def _mla_wq_cols(w):
    w = w.reshape(w.shape[0], MLA_HEADS, MLA_NOPE + MLA_ROPE)
    return _pad_cols(w, MLA_QK_PAD).reshape(w.shape[0], MLA_HEADS * MLA_QK_PAD)


def kernel(x, c, ctx, c_ctx, ada_w, ada_b, norm1_g, w_in, rw_mu, rw_w0, rw_w2, rw_a0, rw_a2, rw_g2, rw_kk, rw_ka,
           rw_rk, rw_lnx_g, rw_lnx_b, rw_v0, rw_v1, rw_v2, wa_sink, mla_qnorm_g, mla_kvnorm_g, mla_w_uq, mla_w_ukv,
           w_branch, w_out, norm2_g, ffn_w_in, ffn_conv_w, ffn_conv_b, ffn_w_out, final_norm_g):
    B, T, D = x.shape
    L = ctx.shape[1]
    depth = w_in.shape[0]
    F = ffn_w_out.shape[1]
    geom = Geom(B, T, L)
    tm = _pick_tile(T, (512, 256, 128))
    assert (B * L) % tm == 0

    rw_cols = 3 * RW_WIDTH + 2 * RW_DECAY_LORA + 2 * RW_AAA_LORA + RW_GATE_LORA
    wa_cols = WA_WIDTH + 2 * WA_KV_WIDTH
    mla_cols = MLA_Q_LORA + MLA_KV_LORA + MLA_ROPE
    mla_cols_pad = MLA_Q_LORA + MLA_KV_LORA + 128

    cos_wa, sin_wa = _rope_tables(geom, WA_HEAD_DIM)
    cos_wa, sin_wa = jnp.tile(cos_wa, (1, 2)), jnp.tile(sin_wa, (1, 2))
    cos_m, sin_m = _rope_tables(geom, MLA_ROPE)
    one, zero = jnp.ones((geom.R, MLA_NOPE), F32), jnp.zeros((geom.R, MLA_NOPE), F32)
    cos_mla = jnp.concatenate([one, cos_m, one[:, :64]], axis=-1)
    sin_mla = jnp.concatenate([zero, sin_m, zero[:, :64]], axis=-1)
    lane = np.arange(128)
    ones_blk = jnp.asarray((lane[:, None] // RW_HEAD_DIM) == (lane[None, :] // RW_HEAD_DIM), F32)

    xs = jnp.concatenate([x.reshape(B * T, D), ctx.reshape(B * L, D)], axis=0)
    cvec = jnp.concatenate([c, c_ctx[None, :], jnp.zeros((8 - (B + 1) % 8, D), F32)], axis=0)
    v_first = None
    for l in range(depth):
        need_ctx = l < depth - 1
        mod = ada_modulation(cvec, ada_w[l], ada_b[l])
        mod = [mod[:, k * D:(k + 1) * D].reshape(-1, 1, D) for k in range(6)]

        w = w_in[l]
        w_rw = _pad_cols(_rw_in_cols(w[:, :rw_cols]), _round_up(RW_Z_COLS, 768)).astype(BF16)
        w_wa = w[:, rw_cols:rw_cols + wa_cols].astype(BF16)
        w_mla = _pad_cols(w[:, rw_cols + wa_cols:rw_cols + wa_cols + mla_cols], mla_cols_pad).astype(BF16)
        w_gate = w[:, rw_cols + wa_cols + mla_cols:].astype(BF16)
        nmm = functools.partial(norm_mod_matmul, geom, xs, norm1_g[l], mod[0], mod[1], tm=tm)
        z_rw = nmm(w_rw, tn=768)
        z_wa = nmm(w_wa, tn=512)
        z_mla = nmm(w_mla, tn=mla_cols_pad)
        gates = nmm(w_gate, tn=_pick_tile(3 * D, (1024, 768, 512, 256, 128)), act="sigmoid")

        rw_p = dict(
            mu=_rw_in_cols(rw_mu[l][None, :]), w0=rw_w0[l], w2=_pad_rows(rw_w2[l], LORA_PAD), a0=rw_a0[l],
            a2=_pad_rows(rw_a2[l], LORA_PAD), g2=_pad_rows(rw_g2[l], LORA_PAD), kk=rw_kk[l][None, :],
            ka=rw_ka[l][None, :], rk=rw_rk[l].reshape(1, RW_WIDTH), ones=ones_blk)
        vres = None if l == 0 else (rw_v0[l - 1][None, :], rw_v1[l - 1], rw_v2[l - 1])
        r, v, kk, lwf, kf, bf, lwb, kb, bb, g, bonus = rw_prep(geom, z_rw, rw_p, v_first, vres, tm=min(tm, 256))
        if l == 0:
            v_first = v
        yf, yb = rw_scan(geom, rw_chunk((r, v, kk, lwf, kf, bf, lwb, kb, bb)))
        o_a = rw_post(yf, yb, bonus, g, rw_lnx_g[l][None, :], rw_lnx_b[l][None, :], ones_blk, tm=tm)

        q_wa, k_wa, v_wa = wa_prep(z_wa, cos_wa, sin_wa, tm=tm)
        ob_l = wa_attention(geom, q_wa, k_wa, v_wa, wa_sink[l], local=True)
        parts = [ob_l]
        if need_ctx:
            parts.append(wa_attention(geom, q_wa, k_wa, v_wa, wa_sink[l], local=False))
        else:
            parts.append(jnp.zeros((B * L, WA_WIDTH), BF16))
        o_b = jnp.concatenate(parts, axis=0)

        wq = _mla_wq_cols(mla_w_uq[l]).astype(BF16)
        q_m, k_m, v_m = mla_prep(z_mla, mla_qnorm_g[l], mla_kvnorm_g[l], wq, mla_w_ukv[l].astype(BF16),
                                 cos_mla, sin_mla, tm=tm)
        tq = _pick_tile(T, (512, 256, 128))
        tk = _pick_tile(T, (1024, 512, 256, 128))
        parts = [mla_attention(geom, q_m, k_m, v_m, with_latent=True, tq=tq, tk=tk)]
        if need_ctx:
            parts.append(mla_attention(geom, q_m, k_m, v_m, with_latent=False, tq=L, tk=L))
        else:
            parts.append(jnp.zeros((B * L, MLA_HEADS * MLA_V), BF16))
        o_c = jnp.concatenate(parts, axis=0)

        tn_d = _pick_tile(D, (1024, 512, 256, 128))
        y = merge_branches(o_a, o_b, o_c, gates, w_branch[l].astype(BF16), tm=tm, tn=tn_d)
        xs = matmul_gated_residual(geom, y, w_out[l].astype(BF16), xs, mod[2], tm=tm, tn=tn_d)

        tf = _pick_tile(F, (512, 256, 128))
        hmid = norm_mod_matmul(geom, xs, norm2_g[l], mod[3], mod[4], ffn_w_in[l].astype(BF16), tm=tm, tn=tf)
        hmid = conv_glu_middle(geom, hmid, ffn_conv_w[l], ffn_conv_b[l], tm=tm, tf=tf)
        xs = matmul_gated_residual(geom, hmid, ffn_w_out[l].astype(BF16), xs, mod[5], tm=tm, tn=tn_d)

    out = final_rmsnorm(xs, final_norm_g, B * T, tm=tm)
    return out.reshape(B, T, D)
```

```python
import functools

import jax
import jax.numpy as jnp
import numpy as np
from jax import lax
from jax.experimental import pallas as pl
from jax.experimental.pallas import tpu as pltpu

F32 = jnp.float32
BF16 = jnp.bfloat16
HIGHEST = lax.Precision.HIGHEST

NORM_EPS = 1e-6
NEG_INF = -1e30
GRID_W = 64
ROPE_BASE = 10000.0

RW_HEADS = 16
RW_HEAD_DIM = 64
RW_WIDTH = RW_HEADS * RW_HEAD_DIM
RW_DECAY_LORA = 96
RW_AAA_LORA = 96
RW_GATE_LORA = 64
RW_LNX_EPS = 64e-5
RW_CHUNK = 64
RW_PAIR = 2 * RW_HEAD_DIM
RW_PASSES_LOCAL = 1
RW_PASSES_STATE = 3
LORA_PAD = 128

WA_HEADS = 16
WA_KV_HEADS = 4
WA_GROUP = WA_HEADS // WA_KV_HEADS
WA_HEAD_DIM = 64
WA_WIDTH = WA_HEADS * WA_HEAD_DIM
WA_KV_WIDTH = WA_KV_HEADS * WA_HEAD_DIM
WINDOW = 128
WA_SCALE = WA_HEAD_DIM ** -0.5

MLA_HEADS = 8
MLA_NOPE = 128
MLA_ROPE = 64
MLA_V = 128
MLA_Q_LORA = 512
MLA_KV_LORA = 512
MLA_QK_PAD = 256
MLA_SCALE = (MLA_NOPE + MLA_ROPE) ** -0.5

CONV_W = 3
VMEM_LIMIT_BYTES = 56 * 1024 * 1024


def _cparams(*sem):
    return pltpu.CompilerParams(dimension_semantics=sem, vmem_limit_bytes=VMEM_LIMIT_BYTES)


def _dot(a, b, precision=None):
    return jnp.dot(a, b, preferred_element_type=F32, precision=precision)


def _dot_nt(a, b, precision=None):
    return lax.dot_general(a, b, (((1,), (1,)), ((), ())), preferred_element_type=F32, precision=precision)


def _dot_tn(a, b, precision=None):
    return lax.dot_general(a, b, (((0,), (0,)), ((), ())), preferred_element_type=F32, precision=precision)


def _pick_tile(n, candidates):
    for c in candidates:
        if n % c == 0:
            return c
    raise ValueError(f"no tile in {candidates} divides {n}")


def _pad_cols(w, n):
    return jnp.pad(w, [(0, 0)] * (w.ndim - 1) + [(0, n - w.shape[-1])])


def _round_up(n, m):
    return (n + m - 1) // m * m


class Geom:
    def __init__(self, B, T, L):
        assert T & (T - 1) == 0 and L & (L - 1) == 0, "sequence lengths must be powers of two"
        assert T % L == 0 and L % RW_CHUNK == 0 and T % GRID_W == 0
        self.B, self.T, self.L = B, T, L
        self.BT = B * T
        self.R = B * T + B * L

    def group_of_tile(self, i, tm):
        return jnp.minimum((i * tm) // self.T, self.B)


def _seq_edge_masks(geom, row0, tm):
    r = row0 + lax.broadcasted_iota(jnp.int32, (tm, 1), 0)
    is_lat = r < geom.BT
    pos = jnp.where(is_lat, r & (geom.T - 1), (r - geom.BT) & (geom.L - 1))
    last = jnp.where(is_lat, geom.T - 1, geom.L - 1)
    return pos == 0, pos == last


def _shifted_rows(x, prev8, next8, first, last):
    tm = x.shape[0]
    rid = lax.broadcasted_iota(jnp.int32, (tm, 1), 0)
    up = jnp.where(rid == 0, prev8[7:8, :], pltpu.roll(x, 1, axis=0))
    dn = jnp.where(rid == tm - 1, next8[0:1, :], pltpu.roll(x, tm - 1, axis=0))
    return jnp.where(first, 0.0, up), jnp.where(last, 0.0, dn)


def _halo_specs(tm, R, width, col_of):
    nb8 = tm // 8
    prev = pl.BlockSpec((8, width), lambda i, *a: (jnp.maximum(i * nb8 - 1, 0), col_of(i, *a)))
    nxt = pl.BlockSpec((8, width), lambda i, *a: (jnp.minimum((i + 1) * nb8, R // 8 - 1), col_of(i, *a)))
    return prev, nxt


def _ada_kernel(c_ref, w_ref, b_ref, o_ref):
    c = c_ref[...]
    o_ref[...] = _dot(c * jax.nn.sigmoid(c), w_ref[...], HIGHEST) + b_ref[...]


def ada_modulation(cvec, w, b):
    G, D = cvec.shape
    N = w.shape[1]
    tn = _pick_tile(N, (1024, 512, 256, 128))
    return pl.pallas_call(
        _ada_kernel,
        grid=(N // tn,),
        in_specs=[pl.BlockSpec((G, D), lambda j: (0, 0)),
                  pl.BlockSpec((D, tn), lambda j: (0, j)),
                  pl.BlockSpec((1, tn), lambda j: (0, j))],
        out_specs=pl.BlockSpec((G, tn), lambda j: (0, j)),
        out_shape=jax.ShapeDtypeStruct((G, N), F32),
        compiler_params=_cparams("arbitrary"),
    )(cvec, w, b.reshape(1, N))


def _nmm_kernel(x_ref, g_ref, sh_ref, sc_ref, w_ref, o_ref, h_ref, *, act):
    @pl.when(pl.program_id(1) == 0)
    def _():
        x = x_ref[...]
        n = x * lax.rsqrt(jnp.mean(x * x, axis=-1, keepdims=True) + NORM_EPS) * g_ref[...]
        h_ref[...] = (n * (1.0 + sc_ref[0]) + sh_ref[0]).astype(BF16)

    acc = _dot(h_ref[...], w_ref[...])
    if act == "sigmoid":
        acc = jax.nn.sigmoid(acc)
    o_ref[...] = acc.astype(o_ref.dtype)


def norm_mod_matmul(geom, x, g, shift, scale, w, *, tm, tn, act=None, out_dtype=F32, x_col_block=0):
    R = x.shape[0]
    K, N = w.shape
    assert R % tm == 0 and N % tn == 0 and geom.T % tm == 0
    grp = lambda i, j: (geom.group_of_tile(i, tm), 0, 0)
    return pl.pallas_call(
        functools.partial(_nmm_kernel, act=act),
        grid=(R // tm, N // tn),
        in_specs=[pl.BlockSpec((tm, K), lambda i, j: (i, x_col_block)),
                  pl.BlockSpec((1, K), lambda i, j: (0, 0)),
                  pl.BlockSpec((1, 1, K), grp),
                  pl.BlockSpec((1, 1, K), grp),
                  pl.BlockSpec((K, tn), lambda i, j: (0, j))],
        out_specs=pl.BlockSpec((tm, tn), lambda i, j: (i, j)),
        out_shape=jax.ShapeDtypeStruct((R, N), out_dtype),
        scratch_shapes=[pltpu.VMEM((tm, K), BF16)],
        compiler_params=_cparams("parallel", "arbitrary"),
    )(x, g.reshape(1, K), shift, scale, w)


def _mm_resid_kernel(y_ref, w_ref, r_ref, gate_ref, o_ref):
    o_ref[...] = r_ref[...] + gate_ref[0] * _dot(y_ref[...], w_ref[...])


def matmul_gated_residual(geom, y, w, resid, gate, *, tm, tn):
    R, K = y.shape
    N = w.shape[1]
    assert R % tm == 0 and N % tn == 0 and geom.T % tm == 0
    return pl.pallas_call(
        _mm_resid_kernel,
        grid=(R // tm, N // tn),
        in_specs=[pl.BlockSpec((tm, K), lambda i, j: (i, 0)),
                  pl.BlockSpec((K, tn), lambda i, j: (0, j)),
                  pl.BlockSpec((tm, tn), lambda i, j: (i, j)),
                  pl.BlockSpec((1, 1, tn), lambda i, j: (geom.group_of_tile(i, tm), 0, j))],
        out_specs=pl.BlockSpec((tm, tn), lambda i, j: (i, j)),
        out_shape=jax.ShapeDtypeStruct((R, N), F32),
        compiler_params=_cparams("parallel", "arbitrary"),
    )(y, w, resid, gate)


def _merge_kernel(oa_ref, ob_ref, oc_ref, ga_ref, gb_ref, gc_ref, w_ref, o_ref):
    y = ga_ref[...] * _dot(oa_ref[...], w_ref[0])
    y = y + gb_ref[...] * _dot(ob_ref[...], w_ref[1])
    y = y + gc_ref[...] * _dot(oc_ref[...], w_ref[2])
    o_ref[...] = y.astype(o_ref.dtype)


def merge_branches(oa, ob, oc, gates, wb, *, tm, tn):
    R, K = oa.shape
    D = wb.shape[2]
    nj = D // tn
    bspec = pl.BlockSpec((tm, K), lambda i, j: (i, 0))
    gspec = lambda k: pl.BlockSpec((tm, tn), lambda i, j: (i, k * nj + j))
    return pl.pallas_call(
        _merge_kernel,
        grid=(R // tm, nj),
        in_specs=[bspec, bspec, bspec, gspec(0), gspec(1), gspec(2),
                  pl.BlockSpec((3, K, tn), lambda i, j: (0, 0, j))],
        out_specs=pl.BlockSpec((tm, tn), lambda i, j: (i, j)),
        out_shape=jax.ShapeDtypeStruct((R, D), BF16),
        compiler_params=_cparams("parallel", "arbitrary"),
    )(oa, ob, oc, gates, gates, gates, wb)


def _convglu_kernel(gt_ref, gp_ref, gn_ref, u_ref, cw_ref, cb_ref, o_ref, *, geom, tm):
    first, last = _seq_edge_masks(geom, pl.program_id(0) * tm, tm)
    gt = gt_ref[...]
    up, dn = _shifted_rows(gt, gp_ref[...], gn_ref[...], first, last)
    cw = cw_ref[...]
    conv = cb_ref[...] + up * cw[0:1, :]
    conv = conv + gt * cw[1:2, :]
    conv = conv + dn * cw[2:3, :]
    o_ref[...] = (jax.nn.gelu(conv, approximate=True) * u_ref[...]).astype(o_ref.dtype)


def conv_glu_middle(geom, h, conv_w, conv_b, *, tm, tf):
    R, F2 = h.shape
    F = F2 // 2
    nj = F // tf
    prev, nxt = _halo_specs(tm, R, tf, lambda i, j: j)
    return pl.pallas_call(
        functools.partial(_convglu_kernel, geom=geom, tm=tm),
        grid=(R // tm, nj),
        in_specs=[pl.BlockSpec((tm, tf), lambda i, j: (i, j)), prev, nxt,
                  pl.BlockSpec((tm, tf), lambda i, j: (i, nj + j)),
                  pl.BlockSpec((CONV_W, tf), lambda i, j: (0, j)),
                  pl.BlockSpec((1, tf), lambda i, j: (0, j))],
        out_specs=pl.BlockSpec((tm, tf), lambda i, j: (i, j)),
        out_shape=jax.ShapeDtypeStruct((R, F), BF16),
        compiler_params=_cparams("parallel", "arbitrary"),
    )(h, h, h, h, conv_w, conv_b.reshape(1, F))


def _rmsnorm_kernel(x_ref, g_ref, o_ref):
    x = x_ref[...]
    o_ref[...] = x * lax.rsqrt(jnp.mean(x * x, axis=-1, keepdims=True) + NORM_EPS) * g_ref[...]


def final_rmsnorm(x, g, rows, *, tm):
    D = x.shape[1]
    return pl.pallas_call(
        _rmsnorm_kernel,
        grid=(rows // tm,),
        in_specs=[pl.BlockSpec((tm, D), lambda i: (i, 0)), pl.BlockSpec((1, D), lambda i: (0, 0))],
        out_specs=pl.BlockSpec((tm, D), lambda i: (i, 0)),
        out_shape=jax.ShapeDtypeStruct((rows, D), F32),
        compiler_params=_cparams("parallel"),
    )(x, g.reshape(1, D))


def _rot_half64(z):
    n = z.shape[-1]
    lane = lax.broadcasted_iota(jnp.int32, z.shape, z.ndim - 1)
    return jnp.where((lane & 63) < 32, pltpu.roll(z, n - 32, axis=z.ndim - 1), pltpu.roll(z, 32, axis=z.ndim - 1))


def _rope_tables(geom, dim):
    nf = dim // 4
    inv = ROPE_BASE ** (-jnp.arange(nf, dtype=F32) / nf)
    rows = geom.T // GRID_W
    row = jnp.repeat(jnp.arange(rows, dtype=F32), GRID_W)
    col = jnp.tile(jnp.arange(GRID_W, dtype=F32), rows)
    ang = jnp.concatenate([row[:, None] * inv, col[:, None] * inv], axis=-1)
    cos, sin = jnp.cos(ang), jnp.sin(ang)
    cos_t = jnp.concatenate([cos, cos], axis=-1)
    sin_t = jnp.concatenate([-sin, sin], axis=-1)
    nctx = geom.B * geom.L
    cos_f = jnp.concatenate([jnp.tile(cos_t, (geom.B, 1)), jnp.ones((nctx, dim), F32)], axis=0)
    sin_f = jnp.concatenate([jnp.tile(sin_t, (geom.B, 1)), jnp.zeros((nctx, dim), F32)], axis=0)
    return cos_f, sin_f


def _wa_prep_kernel(z_ref, cos_ref, sin_ref, q_ref, k_ref, v_ref):
    cos = cos_ref[...]
    sin = sin_ref[...]
    for c in range(WA_WIDTH // 128):
        z = z_ref[:, c * 128:(c + 1) * 128]
        q_ref[:, c * 128:(c + 1) * 128] = ((z * cos + _rot_half64(z) * sin) * WA_SCALE).astype(BF16)
    for c in range(WA_KV_WIDTH // 128):
        z = z_ref[:, WA_WIDTH + c * 128:WA_WIDTH + (c + 1) * 128]
        k_ref[:, c * 128:(c + 1) * 128] = (z * cos + _rot_half64(z) * sin).astype(BF16)
    v_ref[...] = z_ref[:, WA_WIDTH + WA_KV_WIDTH:].astype(BF16)


def wa_prep(z, cos, sin, *, tm):
    R = z.shape[0]
    row = lambda w: pl.BlockSpec((tm, w), lambda i: (i, 0))
    return pl.pallas_call(
        _wa_prep_kernel,
        grid=(R // tm,),
        in_specs=[row(WA_WIDTH + 2 * WA_KV_WIDTH), row(128), row(128)],
        out_specs=[row(WA_WIDTH), row(WA_KV_WIDTH), row(WA_KV_WIDTH)],
        out_shape=[jax.ShapeDtypeStruct((R, WA_WIDTH), BF16),
                   jax.ShapeDtypeStruct((R, WA_KV_WIDTH), BF16),
                   jax.ShapeDtypeStruct((R, WA_KV_WIDTH), BF16)],
        compiler_params=_cparams("parallel"),
    )(z, cos, sin)


def _wa_attn_kernel(*refs, local, nqb, tq):
    if local:
        sink_ref, q_ref, kp_ref, kc_ref, kn_ref, vp_ref, vc_ref, vn_ref, kx_ref, vx_ref, o_ref = refs
    else:
        sink_ref, q_ref, kx_ref, vx_ref, o_ref = refs
    i = pl.program_id(1)
    nk_ctx = kx_ref.shape[0]
    rows = WA_GROUP * tq
    qpos = lax.broadcasted_iota(jnp.int32, (rows, 1), 0) & (tq - 1)
    head_in_group = lax.broadcasted_iota(jnp.int32, (rows, 1), 0) >> int(np.log2(tq))
    if local:
        off_prev = jnp.where(i > 0, 0, tq)
        off_next = jnp.where(i < nqb - 1, 0, tq)
        j = lax.broadcasted_iota(jnp.int32, (1, 3 * tq + nk_ctx), 1)
        valid = ((j >= tq) & (j < 2 * tq)) | (j >= 3 * tq)
        valid = valid | ((j < tq) & (j >= qpos + off_prev))
        valid = valid | ((j >= 2 * tq) & (j < 3 * tq) & ((j - 2 * tq) <= qpos - off_next))
    for g in range(WA_KV_HEADS):
        ks = slice(g * WA_HEAD_DIM, (g + 1) * WA_HEAD_DIM)
        qg = jnp.concatenate(
            [q_ref[:, (g * WA_GROUP + a) * WA_HEAD_DIM:(g * WA_GROUP + a + 1) * WA_HEAD_DIM] for a in range(WA_GROUP)],
            axis=0)
        sink = jnp.zeros((rows, 1), F32)
        for a in range(WA_GROUP):
            sink = jnp.where(head_in_group == a, sink_ref[g * WA_GROUP + a], sink)
        if local:
            kcat = jnp.concatenate([kp_ref[:, ks], kc_ref[:, ks], kn_ref[:, ks], kx_ref[:, ks]], axis=0)
            vcat = jnp.concatenate([vp_ref[:, ks], vc_ref[:, ks], vn_ref[:, ks], vx_ref[:, ks]], axis=0)
        else:
            kcat, vcat = kx_ref[:, ks], vx_ref[:, ks]
        s = _dot_nt(qg, kcat)
        if local:
            s = jnp.where(valid, s, NEG_INF)
        m = jnp.maximum(jnp.max(s, axis=-1, keepdims=True), sink)
        e = jnp.exp(s - m)
        denom = jnp.sum(e, axis=-1, keepdims=True) + jnp.exp(sink - m)
        o = _dot((e / denom).astype(BF16), vcat)
        for a in range(WA_GROUP):
            h = g * WA_GROUP + a
            o_ref[:, h * WA_HEAD_DIM:(h + 1) * WA_HEAD_DIM] = o[a * tq:(a + 1) * tq].astype(o_ref.dtype)


def wa_attention(geom, q, k, v, sink, *, local):
    B, T, L = geom.B, geom.T, geom.L
    sink_spec = pl.BlockSpec(memory_space=pltpu.SMEM)
    ctx_spec = pl.BlockSpec((L, WA_KV_WIDTH), lambda b, i: (geom.BT // L + b, 0))
    if local:
        tq = WINDOW
        nqb = T // tq
        kv = lambda f: pl.BlockSpec((tq, WA_KV_WIDTH), lambda b, i: (b * nqb + f(i), 0))
        prev = lambda i: jnp.maximum(i - 1, 0)
        cur = lambda i: i
        nxt = lambda i: jnp.minimum(i + 1, nqb - 1)
        in_specs = [sink_spec, pl.BlockSpec((tq, WA_WIDTH), lambda b, i: (b * nqb + i, 0)),
                    kv(prev), kv(cur), kv(nxt), kv(prev), kv(cur), kv(nxt), ctx_spec, ctx_spec]
        args = (sink, q, k, k, k, v, v, v, k, v)
        out_rows, out_spec = geom.BT, pl.BlockSpec((tq, WA_WIDTH), lambda b, i: (b * nqb + i, 0))
    else:
        tq, nqb = L, 1
        in_specs = [sink_spec, pl.BlockSpec((tq, WA_WIDTH), lambda b, i: (geom.BT // L + b, 0)), ctx_spec, ctx_spec]
        args = (sink, q, k, v)
        out_rows, out_spec = B * L, pl.BlockSpec((tq, WA_WIDTH), lambda b, i: (b, 0))
    return pl.pallas_call(
        functools.partial(_wa_attn_kernel, local=local, nqb=nqb, tq=tq),
        grid=(B, nqb),
        in_specs=in_specs,
        out_specs=out_spec,
        out_shape=jax.ShapeDtypeStruct((out_rows, WA_WIDTH), BF16),
        compiler_params=_cparams("parallel", "arbitrary"),
    )(*args)


def _mla_prep_kernel(z_ref, qg_ref, kvg_ref, wq_ref, wkv_ref, cos_ref, sin_ref, q_ref, k_ref, v_ref):
    def norm(x, g):
        return (x * lax.rsqrt(jnp.mean(x * x, axis=-1, keepdims=True) + NORM_EPS) * g).astype(BF16)

    cos = cos_ref[...]
    sin = sin_ref[...]
    q = _dot(norm(z_ref[:, :MLA_Q_LORA], qg_ref[...]), wq_ref[...])
    kv = _dot(norm(z_ref[:, MLA_Q_LORA:MLA_Q_LORA + MLA_KV_LORA], kvg_ref[...]), wkv_ref[...])
    kr = z_ref[:, MLA_Q_LORA + MLA_KV_LORA:MLA_Q_LORA + MLA_KV_LORA + 128]
    kr = (kr * cos[:, 128:] + _rot_half64(kr) * sin[:, 128:]).astype(BF16)
    for h in range(MLA_HEADS):
        qh = q[:, h * MLA_QK_PAD:(h + 1) * MLA_QK_PAD]
        q_ref[h] = ((qh * cos + _rot_half64(qh) * sin) * MLA_SCALE).astype(BF16)
        k_ref[h, :, :MLA_NOPE] = kv[:, h * 256:h * 256 + MLA_NOPE].astype(BF16)
        k_ref[h, :, MLA_NOPE:] = kr
        v_ref[h] = kv[:, h * 256 + MLA_NOPE:(h + 1) * 256].astype(BF16)


def mla_prep(z, qnorm_g, kvnorm_g, wq, wkv, cos, sin, *, tm):
    R, Z = z.shape
    full = lambda a: pl.BlockSpec(a.shape, lambda i: (0,) * a.ndim)
    qg, kvg = qnorm_g.reshape(1, -1), kvnorm_g.reshape(1, -1)
    hd = lambda w: pl.BlockSpec((MLA_HEADS, tm, w), lambda i: (0, i, 0))
    return pl.pallas_call(
        _mla_prep_kernel,
        grid=(R // tm,),
        in_specs=[pl.BlockSpec((tm, Z), lambda i: (i, 0)), full(qg), full(kvg), full(wq), full(wkv),
                  pl.BlockSpec((tm, MLA_QK_PAD), lambda i: (i, 0)), pl.BlockSpec((tm, MLA_QK_PAD), lambda i: (i, 0))],
        out_specs=[hd(MLA_QK_PAD), hd(MLA_QK_PAD), hd(MLA_V)],
        out_shape=[jax.ShapeDtypeStruct((MLA_HEADS, R, MLA_QK_PAD), BF16),
                   jax.ShapeDtypeStruct((MLA_HEADS, R, MLA_QK_PAD), BF16),
                   jax.ShapeDtypeStruct((MLA_HEADS, R, MLA_V), BF16)],
        compiler_params=_cparams("parallel"),
    )(z, qg, kvg, wq, wkv, cos, sin)


def _mla_flash_kernel(*refs, with_latent):
    if with_latent:
        q_ref, kx_ref, vx_ref, k_ref, v_ref, o_ref, m_ref, l_ref, acc_ref = refs
    else:
        q_ref, kx_ref, vx_ref, o_ref, m_ref, l_ref, acc_ref = refs
    ki = pl.program_id(3)
    q = q_ref[0]

    @pl.when(ki == 0)
    def _():
        s = _dot_nt(q, kx_ref[0])
        m = jnp.max(s, axis=-1, keepdims=True)
        e = jnp.exp(s - m)
        m_ref[...] = m
        l_ref[...] = jnp.sum(e, axis=-1, keepdims=True)
        acc_ref[...] = _dot(e.astype(BF16), vx_ref[0])

    if with_latent:
        s = _dot_nt(q, k_ref[0])
        m_old = m_ref[...]
        m_new = jnp.maximum(m_old, jnp.max(s, axis=-1, keepdims=True))
        alpha = jnp.exp(m_old - m_new)
        e = jnp.exp(s - m_new)
        m_ref[...] = m_new
        l_ref[...] = alpha * l_ref[...] + jnp.sum(e, axis=-1, keepdims=True)
        acc_ref[...] = alpha * acc_ref[...] + _dot(e.astype(BF16), v_ref[0])

    @pl.when(ki == pl.num_programs(3) - 1)
    def _():
        o_ref[...] = (acc_ref[...] / l_ref[...]).astype(o_ref.dtype)


def mla_attention(geom, q, k, v, *, with_latent, tq, tk):
    B, T, L = geom.B, geom.T, geom.L
    cblk = geom.BT // L
    ctx_k = pl.BlockSpec((1, L, MLA_QK_PAD), lambda b, h, qi, ki: (h, cblk + b, 0))
    ctx_v = pl.BlockSpec((1, L, MLA_V), lambda b, h, qi, ki: (h, cblk + b, 0))
    if with_latent:
        nq, nk = T // tq, T // tk
        in_specs = [pl.BlockSpec((1, tq, MLA_QK_PAD), lambda b, h, qi, ki: (h, b * nq + qi, 0)), ctx_k, ctx_v,
                    pl.BlockSpec((1, tk, MLA_QK_PAD), lambda b, h, qi, ki: (h, b * nk + ki, 0)),
                    pl.BlockSpec((1, tk, MLA_V), lambda b, h, qi, ki: (h, b * nk + ki, 0))]
        args = (q, k, v, k, v)
        out_rows, out_spec = geom.BT, pl.BlockSpec((tq, MLA_V), lambda b, h, qi, ki: (b * nq + qi, h))
    else:
        tq, nq, nk = L, 1, 1
        in_specs = [pl.BlockSpec((1, tq, MLA_QK_PAD), lambda b, h, qi, ki: (h, cblk + b, 0)), ctx_k, ctx_v]
        args = (q, k, v)
        out_rows, out_spec = B * L, pl.BlockSpec((tq, MLA_V), lambda b, h, qi, ki: (b, h))
    return pl.pallas_call(
        functools.partial(_mla_flash_kernel, with_latent=with_latent),
        grid=(B, MLA_HEADS, nq, nk),
        in_specs=in_specs,
        out_specs=out_spec,
        out_shape=jax.ShapeDtypeStruct((out_rows, MLA_HEADS * MLA_V), BF16),
        scratch_shapes=[pltpu.VMEM((tq, 1), F32), pltpu.VMEM((tq, 1), F32), pltpu.VMEM((tq, MLA_V), F32)],
        compiler_params=_cparams("parallel", "parallel", "parallel", "arbitrary"),
    )(*args)


RW_Z_R, RW_Z_K, RW_Z_V = 0, RW_WIDTH, 2 * RW_WIDTH
RW_Z_LORA = 3 * RW_WIDTH
RW_Z_COLS = 3 * RW_WIDTH + 5 * LORA_PAD


def _head_sum(x, ones):
    return jnp.concatenate(
        [_dot(x[:, c * 128:(c + 1) * 128], ones, HIGHEST) for c in range(x.shape[1] // 128)], axis=-1)


def _rw_prep_kernel(*refs, geom, tm, has_vres):
    if has_vres:
        (z_ref, zp_ref, zn_ref, mu_ref, w0_ref, w2_ref, a0_ref, a2_ref, g2_ref, kk_ref, ka_ref, rk_ref, ones_ref,
         vf_ref, v0_ref, v1_ref, v2_ref,
         r_o, v_o, kk_o, lwf_o, kf_o, bf_o, lwb_o, kb_o, bb_o, g_o, bonus_o) = refs
    else:
        (z_ref, zp_ref, zn_ref, mu_ref, w0_ref, w2_ref, a0_ref, a2_ref, g2_ref, kk_ref, ka_ref, rk_ref, ones_ref,
         r_o, v_o, kk_o, lwf_o, kf_o, bf_o, lwb_o, kb_o, bb_o, g_o, bonus_o) = refs
    first, last = _seq_edge_masks(geom, pl.program_id(0) * tm, tm)
    z = z_ref[...]
    up, dn = _shifted_rows(z, zp_ref[...], zn_ref[...], first, last)
    z = z + mu_ref[...] * (0.5 * (up + dn) - z)
    r = z[:, RW_Z_R:RW_Z_R + RW_WIDTH]
    k = z[:, RW_Z_K:RW_Z_K + RW_WIDTH]
    v = z[:, RW_Z_V:RW_Z_V + RW_WIDTH]
    lora = lambda n: z[:, RW_Z_LORA + n * LORA_PAD:RW_Z_LORA + (n + 1) * LORA_PAD]
    ones = ones_ref[...]
    if has_vres:
        mix = jax.nn.sigmoid(v0_ref[...] + _dot(_dot(v, v1_ref[...], HIGHEST), v2_ref[...], HIGHEST))
        v = v + (vf_ref[...] - v) * mix
    g_o[...] = _dot(jax.nn.sigmoid(lora(4)), g2_ref[...], HIGHEST)
    kk = k * kk_ref[...]
    kk = kk / jnp.maximum(jnp.sqrt(_head_sum(kk * kk, ones)), 1e-12)
    ksum = None
    for d, (lw_o, k_o, b_o) in enumerate(((lwf_o, kf_o, bf_o), (lwb_o, kb_o, bb_o))):
        x = -(w0_ref[d:d + 1, :] + _dot(jnp.tanh(lora(d)), w2_ref[d], HIGHEST))
        softplus = jnp.maximum(x, 0.0) + jnp.log1p(jnp.exp(-jnp.abs(x)))
        lw_o[...] = -jnp.exp(-softplus - 0.5)
        a = jax.nn.sigmoid(a0_ref[d:d + 1, :] + _dot(lora(2 + d), a2_ref[d], HIGHEST))
        kd = k * (1.0 + (a - 1.0) * ka_ref[...])
        k_o[...] = kd
        b_o[...] = kk * a
        ksum = kd if ksum is None else ksum + kd
    r_o[...] = r
    v_o[...] = v
    kk_o[...] = kk
    bonus_o[...] = _head_sum(r * ksum * rk_ref[...], ones) * v


def rw_prep(geom, z, p, v_first, vres, *, tm):
    R = z.shape[0]
    has_vres = vres is not None
    full = lambda a: pl.BlockSpec(a.shape, lambda i: (0,) * a.ndim)
    row = pl.BlockSpec((tm, RW_WIDTH), lambda i: (i, 0))
    prev, nxt = _halo_specs(tm, R, RW_Z_COLS, lambda i: 0)
    params = [p["mu"], p["w0"], p["w2"], p["a0"], p["a2"], p["g2"], p["kk"], p["ka"], p["rk"], p["ones"]]
    in_specs = [pl.BlockSpec((tm, RW_Z_COLS), lambda i: (i, 0)), prev, nxt] + [full(a) for a in params]
    args = [z, z, z] + params
    if has_vres:
        in_specs += [row] + [full(a) for a in vres]
        args += [v_first] + list(vres)
    return pl.pallas_call(
        functools.partial(_rw_prep_kernel, geom=geom, tm=tm, has_vres=has_vres),
        grid=(R // tm,),
        in_specs=in_specs,
        out_specs=[row] * 11,
        out_shape=[jax.ShapeDtypeStruct((R, RW_WIDTH), F32)] * 11,
        compiler_params=_cparams("parallel"),
    )(*args)


def _pieces(x, passes):
    hi = x.astype(BF16)
    if passes == 1:
        return (hi,)
    return hi, (x - hi.astype(F32)).astype(BF16)


def _mmx(a, b, dot):
    out = dot(a[0], b[0])
    if len(a) > 1:
        out = out + dot(a[1], b[0])
    if len(b) > 1:
        out = out + dot(a[0], b[1])
    return out


def _stack_pair(first_head, x):
    return jnp.concatenate([jnp.where(first_head, x, 0.0), jnp.where(first_head, 0.0, x)], axis=0)


def _fold_pair(x):
    half = x.shape[0] // 2
    return x[:half] + x[half:]


def _rw_chunk_kernel(r_ref, v_ref, kk_ref, lwf_ref, kf_ref, bf_ref, lwb_ref, kb_ref, bb_ref,
                     rrf_o, ylf_o, mf_o, nf_o, ef_o, rrb_o, ylb_o, mb_o, nb_o, eb_o):
    C = RW_CHUNK
    PW = RW_PAIR
    ri = lax.broadcasted_iota(jnp.int32, (PW, PW), 0)
    ci = lax.broadcasted_iota(jnp.int32, (PW, PW), 1)
    eye = ri == ci
    ri, ci = ri & (C - 1), ci & (C - 1)
    first_head = lax.broadcasted_iota(jnp.int32, (1, PW), 1) < RW_HEAD_DIM
    stack = functools.partial(_stack_pair, first_head)
    r, v, kk = r_ref[...], v_ref[...], kk_ref[...]
    jobs = []
    for lw_ref, k_ref, b_ref, outs, before, tot_row in (
            (lwf_ref, kf_ref, bf_ref, (rrf_o, ylf_o, mf_o, nf_o, ef_o), ci < ri, C - 1),
            (lwb_ref, kb_ref, bb_ref, (rrb_o, ylb_o, mb_o, nb_o, eb_o), ci > ri, 0)):
        lw, kd, bd = lw_ref[...], k_ref[...], b_ref[...]
        incl = (before | eye)[:C, :C].astype(BF16)
        c, rest = None, lw
        for _ in range(3):
            piece = rest.astype(BF16)
            rest = rest - piece.astype(F32)
            part = _dot(incl, piece)
            c = part if c is None else c + part
        c_tot = c[tot_row:tot_row + 1, :]
        at = -kk * jnp.exp(c - lw)
        rt = r * jnp.exp(c)
        e_neg = jnp.exp(-c)
        bt, kt = bd * e_neg, kd * e_neg
        e_rest = jnp.exp(c_tot - c)
        bc, kc = bd * e_rest, kd * e_rest
        outs[4][...] = jnp.broadcast_to(jnp.exp(c_tot), (8, RW_WIDTH))
        for p in range(RW_WIDTH // PW):
            ps = slice(p * PW, (p + 1) * PW)
            jobs.append(dict(ps=ps, outs=outs, before=before, incl=before | eye, rt_pair=rt[:, ps],
                             at=stack(at[:, ps]), rt=stack(rt[:, ps]), bt=stack(bt[:, ps]), kt=stack(kt[:, ps]),
                             bc=stack(bc[:, ps]), kc=stack(kc[:, ps]), v=stack(v[:, ps])))
    for j in jobs:
        p = _mmx(_pieces(jnp.concatenate([j["at"], j["rt"]], axis=0), RW_PASSES_LOCAL),
                 _pieces(jnp.concatenate([j["bt"], j["kt"]], axis=0), RW_PASSES_LOCAL), _dot_nt)
        j["a_ab"] = jnp.where(j["before"], p[:PW, :PW], 0.0)
        j["a_ak"] = jnp.where(j["before"], p[:PW, PW:], 0.0)
        j["a_rb"] = jnp.where(j["incl"], p[PW:, :PW], 0.0)
        j["a_rk"] = jnp.where(j["incl"], p[PW:, PW:], 0.0)
        j["vp"] = _pieces(j["v"], RW_PASSES_LOCAL)
    for j in jobs:
        j["w1"] = _mmx(_pieces(j["a_ak"], RW_PASSES_LOCAL), j["vp"], _dot)
        j["tinv"] = jnp.where(eye, 1.0, j["a_ab"])
        j["pw"] = j["a_ab"]
    for _ in range(int(np.log2(C)) - 1):
        for j in jobs:
            pw = _pieces(j["pw"], RW_PASSES_LOCAL)
            j["pw"] = _mmx(pw, pw, _dot)
        for j in jobs:
            j["tinv"] = j["tinv"] + _mmx(_pieces(j["tinv"], RW_PASSES_LOCAL), _pieces(j["pw"], RW_PASSES_LOCAL), _dot)
    for j in jobs:
        tw = _mmx(_pieces(j["tinv"], RW_PASSES_LOCAL),
                  _pieces(jnp.concatenate([j["at"], j["w1"]], axis=1), RW_PASSES_LOCAL), _dot)
        j["tw"] = tw
        j["twp"] = _pieces(tw, RW_PASSES_LOCAL)
    for j in jobs:
        rr_o, yl_o, m_o, n_o, _ = j["outs"]
        ps = j["ps"]
        ry = _mmx(_pieces(j["a_rb"], RW_PASSES_LOCAL), j["twp"], _dot)
        yk = _mmx(_pieces(j["a_rk"], RW_PASSES_LOCAL), j["vp"], _dot)
        rr_o[:, ps] = j["rt_pair"] + _fold_pair(ry[:, :PW])
        yl_o[:, ps] = _fold_pair(ry[:, PW:] + yk)
        bcp = _pieces(j["bc"], RW_PASSES_STATE)
        m_o[:, ps] = _fold_pair(_mmx(bcp, _pieces(j["tw"][:, :PW], RW_PASSES_STATE), _dot_tn))
        uv = jnp.concatenate([j["tw"][:, PW:], j["v"]], axis=0)
        bk = jnp.concatenate([j["bc"], j["kc"]], axis=0)
        n_o[:, ps] = _fold_pair(_mmx(_pieces(uv, RW_PASSES_STATE), _pieces(bk, RW_PASSES_STATE), _dot_tn))


def rw_chunk(arrs):
    R = arrs[0].shape[0]
    blk = pl.BlockSpec((RW_CHUNK, RW_WIDTH), lambda i: (i, 0))
    eblk = pl.BlockSpec((8, RW_WIDTH), lambda i: (i, 0))
    tok = jax.ShapeDtypeStruct((R, RW_WIDTH), F32)
    dec = jax.ShapeDtypeStruct((R // RW_CHUNK * 8, RW_WIDTH), F32)
    return pl.pallas_call(
        _rw_chunk_kernel,
        grid=(R // RW_CHUNK,),
        in_specs=[blk] * 9,
        out_specs=[blk] * 4 + [eblk] + [blk] * 4 + [eblk],
        out_shape=[tok] * 4 + [dec] + [tok] * 4 + [dec],
        compiler_params=_cparams("parallel"),
    )(*arrs)


def _rw_scan_kernel(rrf_ref, ylf_ref, mf_ref, nf_ref, ef_ref, rrb_ref, ylb_ref, mb_ref, nb_ref, eb_ref,
                    yf_o, yb_o, s_ref, *, nchunk):
    C = RW_CHUNK
    PW = RW_PAIR
    first_head = lax.broadcasted_iota(jnp.int32, (1, PW), 1) < RW_HEAD_DIM
    stack = functools.partial(_stack_pair, first_head)

    @pl.when(pl.program_id(1) == 0)
    def _():
        s_ref[...] = jnp.zeros_like(s_ref)

    dirs = ((rrf_ref, ylf_ref, mf_ref, nf_ref, ef_ref, yf_o), (rrb_ref, ylb_ref, mb_ref, nb_ref, eb_ref, yb_o))
    pairs = [slice(p * PW, (p + 1) * PW) for p in range(RW_WIDTH // PW)]
    state = [[s_ref[d, :, ps] for ps in pairs] for d in range(2)]
    for step in range(nchunk):
        for d, (rr_ref, yl_ref, m_ref, n_ref, e_ref, y_o) in enumerate(dirs):
            c = step if d == 0 else nchunk - 1 - step
            rows = slice(c * C, (c + 1) * C)
            for p, ps in enumerate(pairs):
                s = state[d][p]
                sp = _pieces(stack(s), RW_PASSES_STATE)
                y = _mmx(_pieces(stack(rr_ref[rows, ps]), RW_PASSES_STATE), sp, _dot_nt)
                y_o[rows, ps] = _fold_pair(y) + yl_ref[rows, ps]
                sm = _mmx(sp, _pieces(stack(m_ref[rows, ps]), RW_PASSES_STATE), _dot_nt)
                state[d][p] = s * e_ref[c * 8:c * 8 + 1, ps] + _fold_pair(sm) + n_ref[rows, ps]
    for d in range(2):
        for p, ps in enumerate(pairs):
            s_ref[d, :, ps] = state[d][p]


def rw_scan(geom, chunk_out):
    B, T, L = geom.B, geom.T, geom.L
    blk = L
    nchunk = blk // RW_CHUNK
    nlat = T // blk
    cblk = geom.BT // blk
    fwd_i = lambda b, s: (jnp.where(s == 0, cblk + b, b * nlat + s - 1), 0)
    bwd_i = lambda b, s: (jnp.where(s == 0, cblk + b, b * nlat + nlat - s), 0)
    tok = lambda f: pl.BlockSpec((blk, RW_WIDTH), f)
    dec = lambda f: pl.BlockSpec((8 * nchunk, RW_WIDTH), f)
    return pl.pallas_call(
        functools.partial(_rw_scan_kernel, nchunk=nchunk),
        grid=(B, nlat + 1),
        in_specs=[tok(fwd_i)] * 4 + [dec(fwd_i)] + [tok(bwd_i)] * 4 + [dec(bwd_i)],
        out_specs=[tok(fwd_i), tok(bwd_i)],
        out_shape=[jax.ShapeDtypeStruct((geom.R, RW_WIDTH), F32)] * 2,
        scratch_shapes=[pltpu.VMEM((2, RW_HEAD_DIM, RW_WIDTH), F32)],
        compiler_params=_cparams("parallel", "arbitrary"),
    )(*chunk_out)


def _rw_post_kernel(yf_ref, yb_ref, bonus_ref, g_ref, lng_ref, lnb_ref, ones_ref, o_ref):
    ones = ones_ref[...]
    y = yf_ref[...] + yb_ref[...]
    mean = _head_sum(y, ones) * (1.0 / RW_HEAD_DIM)
    yc = y - mean
    var = _head_sum(yc * yc, ones) * (1.0 / RW_HEAD_DIM)
    y = yc * lax.rsqrt(var + RW_LNX_EPS) * lng_ref[...] + lnb_ref[...]
    o_ref[...] = ((y + bonus_ref[...]) * g_ref[...]).astype(o_ref.dtype)


def rw_post(yf, yb, bonus, g, lnx_g, lnx_b, ones, *, tm):
    R = yf.shape[0]
    row = pl.BlockSpec((tm, RW_WIDTH), lambda i: (i, 0))
    full = lambda a: pl.BlockSpec(a.shape, lambda i: (0,) * a.ndim)
    return pl.pallas_call(
        _rw_post_kernel,
        grid=(R // tm,),
        in_specs=[row] * 4 + [full(lnx_g), full(lnx_b), full(ones)],
        out_specs=row,
        out_shape=jax.ShapeDtypeStruct((R, RW_WIDTH), BF16),
        compiler_params=_cparams("parallel"),
    )(yf, yb, bonus, g, lnx_g, lnx_b, ones)


def _rw_in_cols(w):
    parts = [w[..., :3 * RW_WIDTH]]
    off = 3 * RW_WIDTH
    for n in (RW_DECAY_LORA, RW_DECAY_LORA, RW_AAA_LORA, RW_AAA_LORA, RW_GATE_LORA):
        parts.append(_pad_cols(w[..., off:off + n], LORA_PAD))
        off += n
    return jnp.concatenate(parts, axis=-1)


def _pad_rows(w, n):
    return jnp.pad(w, [(0, 0)] * (w.ndim - 2) + [(0, n - w.shape[-2]), (0, 0)])


def _mla_wq_cols(w):
    w = w.reshape(w.shape[0], MLA_HEADS, MLA_NOPE + MLA_ROPE)
    return _pad_cols(w, MLA_QK_PAD).reshape(w.shape[0], MLA_HEADS * MLA_QK_PAD)


def kernel(x, c, ctx, c_ctx, ada_w, ada_b, norm1_g, w_in, rw_mu, rw_w0, rw_w2, rw_a0, rw_a2, rw_g2, rw_kk, rw_ka,
           rw_rk, rw_lnx_g, rw_lnx_b, rw_v0, rw_v1, rw_v2, wa_sink, mla_qnorm_g, mla_kvnorm_g, mla_w_uq, mla_w_ukv,
           w_branch, w_out, norm2_g, ffn_w_in, ffn_conv_w, ffn_conv_b, ffn_w_out, final_norm_g):
    B, T, D = x.shape
    L = ctx.shape[1]
    depth = w_in.shape[0]
    F = ffn_w_out.shape[1]
    geom = Geom(B, T, L)
    tm = _pick_tile(T, (512, 256, 128))
    assert (B * L) % tm == 0

    rw_cols = 3 * RW_WIDTH + 2 * RW_DECAY_LORA + 2 * RW_AAA_LORA + RW_GATE_LORA
    wa_cols = WA_WIDTH + 2 * WA_KV_WIDTH
    mla_cols = MLA_Q_LORA + MLA_KV_LORA + MLA_ROPE
    mla_cols_pad = MLA_Q_LORA + MLA_KV_LORA + 128

    cos_wa, sin_wa = _rope_tables(geom, WA_HEAD_DIM)
    cos_wa, sin_wa = jnp.tile(cos_wa, (1, 2)), jnp.tile(sin_wa, (1, 2))
    cos_m, sin_m = _rope_tables(geom, MLA_ROPE)
    one, zero = jnp.ones((geom.R, MLA_NOPE), F32), jnp.zeros((geom.R, MLA_NOPE), F32)
    cos_mla = jnp.concatenate([one, cos_m, one[:, :64]], axis=-1)
    sin_mla = jnp.concatenate([zero, sin_m, zero[:, :64]], axis=-1)
    lane = np.arange(128)
    ones_blk = jnp.asarray((lane[:, None] // RW_HEAD_DIM) == (lane[None, :] // RW_HEAD_DIM), F32)

    xs = jnp.concatenate([x.reshape(B * T, D), ctx.reshape(B * L, D)], axis=0)
    cvec = jnp.concatenate([c, c_ctx[None, :], jnp.zeros((8 - (B + 1) % 8, D), F32)], axis=0)
    v_first = None
    for l in range(depth):
        need_ctx = l < depth - 1
        mod = ada_modulation(cvec, ada_w[l], ada_b[l])
        mod = [mod[:, k * D:(k + 1) * D].reshape(-1, 1, D) for k in range(6)]

        w = w_in[l]
        w_rw = _pad_cols(_rw_in_cols(w[:, :rw_cols]), _round_up(RW_Z_COLS, 768)).astype(BF16)
        w_wa = w[:, rw_cols:rw_cols + wa_cols].astype(BF16)
        w_mla = _pad_cols(w[:, rw_cols + wa_cols:rw_cols + wa_cols + mla_cols], mla_cols_pad).astype(BF16)
        w_gate = w[:, rw_cols + wa_cols + mla_cols:].astype(BF16)
        nmm = functools.partial(norm_mod_matmul, geom, xs, norm1_g[l], mod[0], mod[1], tm=tm)
        z_rw = nmm(w_rw, tn=768)
        z_wa = nmm(w_wa, tn=512)
        z_mla = nmm(w_mla, tn=mla_cols_pad)
        gates = nmm(w_gate, tn=_pick_tile(3 * D, (1024, 768, 512, 256, 128)), act="sigmoid")

        rw_p = dict(
            mu=_rw_in_cols(rw_mu[l][None, :]), w0=rw_w0[l], w2=_pad_rows(rw_w2[l], LORA_PAD), a0=rw_a0[l],
            a2=_pad_rows(rw_a2[l], LORA_PAD), g2=_pad_rows(rw_g2[l], LORA_PAD), kk=rw_kk[l][None, :],
            ka=rw_ka[l][None, :], rk=rw_rk[l].reshape(1, RW_WIDTH), ones=ones_blk)
        vres = None if l == 0 else (rw_v0[l - 1][None, :], rw_v1[l - 1], rw_v2[l - 1])
        r, v, kk, lwf, kf, bf, lwb, kb, bb, g, bonus = rw_prep(geom, z_rw, rw_p, v_first, vres, tm=min(tm, 256))
        if l == 0:
            v_first = v
        yf, yb = rw_scan(geom, rw_chunk((r, v, kk, lwf, kf, bf, lwb, kb, bb)))
        o_a = rw_post(yf, yb, bonus, g, rw_lnx_g[l][None, :], rw_lnx_b[l][None, :], ones_blk, tm=tm)

        q_wa, k_wa, v_wa = wa_prep(z_wa, cos_wa, sin_wa, tm=tm)
        ob_l = wa_attention(geom, q_wa, k_wa, v_wa, wa_sink[l], local=True)
        parts = [ob_l]
        if need_ctx:
            parts.append(wa_attention(geom, q_wa, k_wa, v_wa, wa_sink[l], local=False))
        else:
            parts.append(jnp.zeros((B * L, WA_WIDTH), BF16))
        o_b = jnp.concatenate(parts, axis=0)

        wq = _mla_wq_cols(mla_w_uq[l]).astype(BF16)
        q_m, k_m, v_m = mla_prep(z_mla, mla_qnorm_g[l], mla_kvnorm_g[l], wq, mla_w_ukv[l].astype(BF16),
                                 cos_mla, sin_mla, tm=tm)
        tq = _pick_tile(T, (512, 256, 128))
        tk = _pick_tile(T, (1024, 512, 256, 128))
        parts = [mla_attention(geom, q_m, k_m, v_m, with_latent=True, tq=tq, tk=tk)]
        if need_ctx:
            parts.append(mla_attention(geom, q_m, k_m, v_m, with_latent=False, tq=L, tk=L))
        else:
            parts.append(jnp.zeros((B * L, MLA_HEADS * MLA_V), BF16))
        o_c = jnp.concatenate(parts, axis=0)

        tn_d = _pick_tile(D, (1024, 512, 256, 128))
        y = merge_branches(o_a, o_b, o_c, gates, w_branch[l].astype(BF16), tm=tm, tn=tn_d)
        xs = matmul_gated_residual(geom, y, w_out[l].astype(BF16), xs, mod[2], tm=tm, tn=tn_d)

        tf = _pick_tile(F, (512, 256, 128))
        hmid = norm_mod_matmul(geom, xs, norm2_g[l], mod[3], mod[4], ffn_w_in[l].astype(BF16), tm=tm, tn=tf)
        hmid = conv_glu_middle(geom, hmid, ffn_conv_w[l], ffn_conv_b[l], tm=tm, tf=tf)
        xs = matmul_gated_residual(geom, hmid, ffn_w_out[l].astype(BF16), xs, mod[5], tm=tm, tn=tn_d)

    out = final_rmsnorm(xs, final_norm_g, B * T, tm=tm)
    return out.reshape(B, T, D)
```

```python
import functools

import jax
import jax.numpy as jnp
import numpy as np
from jax import lax
from jax.experimental import pallas as pl
from jax.experimental.pallas import tpu as pltpu

F32 = jnp.float32
BF16 = jnp.bfloat16
HIGHEST = lax.Precision.HIGHEST

NORM_EPS = 1e-6
NEG_INF = -1e30
GRID_W = 64
ROPE_BASE = 10000.0

RW_HEADS = 16
RW_HEAD_DIM = 64
RW_WIDTH = RW_HEADS * RW_HEAD_DIM
RW_DECAY_LORA = 96
RW_AAA_LORA = 96
RW_GATE_LORA = 64
RW_LNX_EPS = 64e-5
RW_CHUNK = 64
RW_PAIR = 2 * RW_HEAD_DIM
RW_PASSES_LOCAL = 1
RW_PASSES_STATE = 3
LORA_PAD = 128

WA_HEADS = 16
WA_KV_HEADS = 4
WA_GROUP = WA_HEADS // WA_KV_HEADS
WA_HEAD_DIM = 64
WA_WIDTH = WA_HEADS * WA_HEAD_DIM
WA_KV_WIDTH = WA_KV_HEADS * WA_HEAD_DIM
WINDOW = 128
WA_SCALE = WA_HEAD_DIM ** -0.5

MLA_HEADS = 8
MLA_NOPE = 128
MLA_ROPE = 64
MLA_V = 128
MLA_Q_LORA = 512
MLA_KV_LORA = 512
MLA_QK_PAD = 256
MLA_SCALE = (MLA_NOPE + MLA_ROPE) ** -0.5
MLA_SUB_KEYS = 512
LOG2_E = 1.4426950408889634

CONV_W = 3
VMEM_LIMIT_BYTES = 56 * 1024 * 1024


def _cparams(*sem):
    return pltpu.CompilerParams(dimension_semantics=sem, vmem_limit_bytes=VMEM_LIMIT_BYTES)


def _dot(a, b, precision=None):
    return jnp.dot(a, b, preferred_element_type=F32, precision=precision)


def _dot_nt(a, b, precision=None):
    return lax.dot_general(a, b, (((1,), (1,)), ((), ())), preferred_element_type=F32, precision=precision)


def _dot_tn(a, b, precision=None):
    return lax.dot_general(a, b, (((0,), (0,)), ((), ())), preferred_element_type=F32, precision=precision)


def _pick_tile(n, candidates):
    for c in candidates:
        if n % c == 0:
            return c
    raise ValueError(f"no tile in {candidates} divides {n}")


def _pad_cols(w, n):
    return jnp.pad(w, [(0, 0)] * (w.ndim - 1) + [(0, n - w.shape[-1])])


def _round_up(n, m):
    return (n + m - 1) // m * m


class Geom:
    def __init__(self, B, T, L):
        assert T & (T - 1) == 0 and L & (L - 1) == 0, "sequence lengths must be powers of two"
        assert T % L == 0 and L % RW_CHUNK == 0 and T % GRID_W == 0
        self.B, self.T, self.L = B, T, L
        self.BT = B * T
        self.R = B * T + B * L

    def group_of_tile(self, i, tm):
        return jnp.minimum((i * tm) // self.T, self.B)


def _seq_edge_masks(geom, row0, tm):
    r = row0 + lax.broadcasted_iota(jnp.int32, (tm, 1), 0)
    is_lat = r < geom.BT
    pos = jnp.where(is_lat, r & (geom.T - 1), (r - geom.BT) & (geom.L - 1))
    last = jnp.where(is_lat, geom.T - 1, geom.L - 1)
    return pos == 0, pos == last


def _shifted_rows(x, prev8, next8, first, last):
    tm = x.shape[0]
    rid = lax.broadcasted_iota(jnp.int32, (tm, 1), 0)
    up = jnp.where(rid == 0, prev8[7:8, :], pltpu.roll(x, 1, axis=0))
    dn = jnp.where(rid == tm - 1, next8[0:1, :], pltpu.roll(x, tm - 1, axis=0))
    return jnp.where(first, 0.0, up), jnp.where(last, 0.0, dn)


def _halo_specs(tm, R, width, col_of):
    nb8 = tm // 8
    prev = pl.BlockSpec((8, width), lambda i, *a: (jnp.maximum(i * nb8 - 1, 0), col_of(i, *a)))
    nxt = pl.BlockSpec((8, width), lambda i, *a: (jnp.minimum((i + 1) * nb8, R // 8 - 1), col_of(i, *a)))
    return prev, nxt


def _ada_kernel(c_ref, w_ref, b_ref, o_ref):
    c = c_ref[...]
    o_ref[...] = _dot(c * jax.nn.sigmoid(c), w_ref[...], HIGHEST) + b_ref[...]


def ada_modulation(cvec, w, b):
    G, D = cvec.shape
    N = w.shape[1]
    tn = _pick_tile(N, (1024, 512, 256, 128))
    return pl.pallas_call(
        _ada_kernel,
        grid=(N // tn,),
        in_specs=[pl.BlockSpec((G, D), lambda j: (0, 0)),
                  pl.BlockSpec((D, tn), lambda j: (0, j)),
                  pl.BlockSpec((1, tn), lambda j: (0, j))],
        out_specs=pl.BlockSpec((G, tn), lambda j: (0, j)),
        out_shape=jax.ShapeDtypeStruct((G, N), F32),
        compiler_params=_cparams("arbitrary"),
    )(cvec, w, b.reshape(1, N))


def _nmm_kernel(x_ref, g_ref, sh_ref, sc_ref, w_ref, o_ref, h_ref, *, act):
    @pl.when(pl.program_id(1) == 0)
    def _():
        x = x_ref[...]
        n = x * lax.rsqrt(jnp.mean(x * x, axis=-1, keepdims=True) + NORM_EPS) * g_ref[...]
        h_ref[...] = (n * (1.0 + sc_ref[0]) + sh_ref[0]).astype(BF16)

    acc = _dot(h_ref[...], w_ref[...])
    if act == "sigmoid":
        acc = jax.nn.sigmoid(acc)
    o_ref[...] = acc.astype(o_ref.dtype)


def norm_mod_matmul(geom, x, g, shift, scale, w, *, tm, tn, act=None, out_dtype=F32, x_col_block=0):
    R = x.shape[0]
    K, N = w.shape
    assert R % tm == 0 and N % tn == 0 and geom.T % tm == 0
    grp = lambda i, j: (geom.group_of_tile(i, tm), 0, 0)
    return pl.pallas_call(
        functools.partial(_nmm_kernel, act=act),
        grid=(R // tm, N // tn),
        in_specs=[pl.BlockSpec((tm, K), lambda i, j: (i, x_col_block)),
                  pl.BlockSpec((1, K), lambda i, j: (0, 0)),
                  pl.BlockSpec((1, 1, K), grp),
                  pl.BlockSpec((1, 1, K), grp),
                  pl.BlockSpec((K, tn), lambda i, j: (0, j))],
        out_specs=pl.BlockSpec((tm, tn), lambda i, j: (i, j)),
        out_shape=jax.ShapeDtypeStruct((R, N), out_dtype),
        scratch_shapes=[pltpu.VMEM((tm, K), BF16)],
        compiler_params=_cparams("parallel", "arbitrary"),
    )(x, g.reshape(1, K), shift, scale, w)


def _mm_resid_kernel(y_ref, w_ref, r_ref, gate_ref, o_ref):
    o_ref[...] = r_ref[...] + gate_ref[0] * _dot(y_ref[...], w_ref[...])


def matmul_gated_residual(geom, y, w, resid, gate, *, tm, tn):
    R, K = y.shape
    N = w.shape[1]
    assert R % tm == 0 and N % tn == 0 and geom.T % tm == 0
    return pl.pallas_call(
        _mm_resid_kernel,
        grid=(R // tm, N // tn),
        in_specs=[pl.BlockSpec((tm, K), lambda i, j: (i, 0)),
                  pl.BlockSpec((K, tn), lambda i, j: (0, j)),
                  pl.BlockSpec((tm, tn), lambda i, j: (i, j)),
                  pl.BlockSpec((1, 1, tn), lambda i, j: (geom.group_of_tile(i, tm), 0, j))],
        out_specs=pl.BlockSpec((tm, tn), lambda i, j: (i, j)),
        out_shape=jax.ShapeDtypeStruct((R, N), F32),
        compiler_params=_cparams("parallel", "arbitrary"),
    )(y, w, resid, gate)


def _merge_kernel(oa_ref, ob_ref, oc_ref, ga_ref, gb_ref, gc_ref, w_ref, o_ref):
    y = ga_ref[...] * _dot(oa_ref[...], w_ref[0])
    y = y + gb_ref[...] * _dot(ob_ref[...], w_ref[1])
    y = y + gc_ref[...] * _dot(oc_ref[...], w_ref[2])
    o_ref[...] = y.astype(o_ref.dtype)


def merge_branches(oa, ob, oc, gates, wb, *, tm, tn):
    R, K = oa.shape
    D = wb.shape[2]
    nj = D // tn
    bspec = pl.BlockSpec((tm, K), lambda i, j: (i, 0))
    gspec = lambda k: pl.BlockSpec((tm, tn), lambda i, j: (i, k * nj + j))
    return pl.pallas_call(
        _merge_kernel,
        grid=(R // tm, nj),
        in_specs=[bspec, bspec, bspec, gspec(0), gspec(1), gspec(2),
                  pl.BlockSpec((3, K, tn), lambda i, j: (0, 0, j))],
        out_specs=pl.BlockSpec((tm, tn), lambda i, j: (i, j)),
        out_shape=jax.ShapeDtypeStruct((R, D), BF16),
        compiler_params=_cparams("parallel", "arbitrary"),
    )(oa, ob, oc, gates, gates, gates, wb)


def _convglu_kernel(gt_ref, gp_ref, gn_ref, u_ref, cw_ref, cb_ref, o_ref, *, geom, tm):
    first, last = _seq_edge_masks(geom, pl.program_id(0) * tm, tm)
    gt = gt_ref[...]
    up, dn = _shifted_rows(gt, gp_ref[...], gn_ref[...], first, last)
    cw = cw_ref[...]
    conv = cb_ref[...] + up * cw[0:1, :]
    conv = conv + gt * cw[1:2, :]
    conv = conv + dn * cw[2:3, :]
    o_ref[...] = (jax.nn.gelu(conv, approximate=True) * u_ref[...]).astype(o_ref.dtype)


def conv_glu_middle(geom, h, conv_w, conv_b, *, tm, tf):
    R, F2 = h.shape
    F = F2 // 2
    nj = F // tf
    prev, nxt = _halo_specs(tm, R, tf, lambda i, j: j)
    return pl.pallas_call(
        functools.partial(_convglu_kernel, geom=geom, tm=tm),
        grid=(R // tm, nj),
        in_specs=[pl.BlockSpec((tm, tf), lambda i, j: (i, j)), prev, nxt,
                  pl.BlockSpec((tm, tf), lambda i, j: (i, nj + j)),
                  pl.BlockSpec((CONV_W, tf), lambda i, j: (0, j)),
                  pl.BlockSpec((1, tf), lambda i, j: (0, j))],
        out_specs=pl.BlockSpec((tm, tf), lambda i, j: (i, j)),
        out_shape=jax.ShapeDtypeStruct((R, F), BF16),
        compiler_params=_cparams("parallel", "arbitrary"),
    )(h, h, h, h, conv_w, conv_b.reshape(1, F))


def _rmsnorm_kernel(x_ref, g_ref, o_ref):
    x = x_ref[...]
    o_ref[...] = x * lax.rsqrt(jnp.mean(x * x, axis=-1, keepdims=True) + NORM_EPS) * g_ref[...]


def final_rmsnorm(x, g, rows, *, tm):
    D = x.shape[1]
    return pl.pallas_call(
        _rmsnorm_kernel,
        grid=(rows // tm,),
        in_specs=[pl.BlockSpec((tm, D), lambda i: (i, 0)), pl.BlockSpec((1, D), lambda i: (0, 0))],
        out_specs=pl.BlockSpec((tm, D), lambda i: (i, 0)),
        out_shape=jax.ShapeDtypeStruct((rows, D), F32),
        compiler_params=_cparams("parallel"),
    )(x, g.reshape(1, D))


def _rot_half64(z):
    n = z.shape[-1]
    lane = lax.broadcasted_iota(jnp.int32, z.shape, z.ndim - 1)
    return jnp.where((lane & 63) < 32, pltpu.roll(z, n - 32, axis=z.ndim - 1), pltpu.roll(z, 32, axis=z.ndim - 1))


def _rope_tables(geom, dim):
    nf = dim // 4
    inv = ROPE_BASE ** (-jnp.arange(nf, dtype=F32) / nf)
    rows = geom.T // GRID_W
    row = jnp.repeat(jnp.arange(rows, dtype=F32), GRID_W)
    col = jnp.tile(jnp.arange(GRID_W, dtype=F32), rows)
    ang = jnp.concatenate([row[:, None] * inv, col[:, None] * inv], axis=-1)
    cos, sin = jnp.cos(ang), jnp.sin(ang)
    cos_t = jnp.concatenate([cos, cos], axis=-1)
    sin_t = jnp.concatenate([-sin, sin], axis=-1)
    nctx = geom.B * geom.L
    cos_f = jnp.concatenate([jnp.tile(cos_t, (geom.B, 1)), jnp.ones((nctx, dim), F32)], axis=0)
    sin_f = jnp.concatenate([jnp.tile(sin_t, (geom.B, 1)), jnp.zeros((nctx, dim), F32)], axis=0)
    return cos_f, sin_f


def _wa_prep_kernel(z_ref, cos_ref, sin_ref, q_ref, k_ref, v_ref):
    cos = cos_ref[...]
    sin = sin_ref[...]
    for c in range(WA_WIDTH // 128):
        z = z_ref[:, c * 128:(c + 1) * 128]
        q_ref[:, c * 128:(c + 1) * 128] = ((z * cos + _rot_half64(z) * sin) * WA_SCALE).astype(BF16)
    for c in range(WA_KV_WIDTH // 128):
        z = z_ref[:, WA_WIDTH + c * 128:WA_WIDTH + (c + 1) * 128]
        k_ref[:, c * 128:(c + 1) * 128] = (z * cos + _rot_half64(z) * sin).astype(BF16)
    v_ref[...] = z_ref[:, WA_WIDTH + WA_KV_WIDTH:].astype(BF16)


def wa_prep(z, cos, sin, *, tm):
    R = z.shape[0]
    row = lambda w: pl.BlockSpec((tm, w), lambda i: (i, 0))
    return pl.pallas_call(
        _wa_prep_kernel,
        grid=(R // tm,),
        in_specs=[row(WA_WIDTH + 2 * WA_KV_WIDTH), row(128), row(128)],
        out_specs=[row(WA_WIDTH), row(WA_KV_WIDTH), row(WA_KV_WIDTH)],
        out_shape=[jax.ShapeDtypeStruct((R, WA_WIDTH), BF16),
                   jax.ShapeDtypeStruct((R, WA_KV_WIDTH), BF16),
                   jax.ShapeDtypeStruct((R, WA_KV_WIDTH), BF16)],
        compiler_params=_cparams("parallel"),
    )(z, cos, sin)


def _wa_attn_kernel(*refs, local, nqb, tq):
    if local:
        sink_ref, q_ref, kp_ref, kc_ref, kn_ref, vp_ref, vc_ref, vn_ref, kx_ref, vx_ref, o_ref = refs
    else:
        sink_ref, q_ref, kx_ref, vx_ref, o_ref = refs
    i = pl.program_id(1)
    nk_ctx = kx_ref.shape[0]
    rows = WA_GROUP * tq
    qpos = lax.broadcasted_iota(jnp.int32, (rows, 1), 0) & (tq - 1)
    head_in_group = lax.broadcasted_iota(jnp.int32, (rows, 1), 0) >> int(np.log2(tq))
    if local:
        off_prev = jnp.where(i > 0, 0, tq)
        off_next = jnp.where(i < nqb - 1, 0, tq)
        j = lax.broadcasted_iota(jnp.int32, (1, 3 * tq + nk_ctx), 1)
        valid = ((j >= tq) & (j < 2 * tq)) | (j >= 3 * tq)
        valid = valid | ((j < tq) & (j >= qpos + off_prev))
        valid = valid | ((j >= 2 * tq) & (j < 3 * tq) & ((j - 2 * tq) <= qpos - off_next))
    for g in range(WA_KV_HEADS):
        ks = slice(g * WA_HEAD_DIM, (g + 1) * WA_HEAD_DIM)
        qg = jnp.concatenate(
            [q_ref[:, (g * WA_GROUP + a) * WA_HEAD_DIM:(g * WA_GROUP + a + 1) * WA_HEAD_DIM] for a in range(WA_GROUP)],
            axis=0)
        sink = jnp.zeros((rows, 1), F32)
        for a in range(WA_GROUP):
            sink = jnp.where(head_in_group == a, sink_ref[g * WA_GROUP + a], sink)
        if local:
            kcat = jnp.concatenate([kp_ref[:, ks], kc_ref[:, ks], kn_ref[:, ks], kx_ref[:, ks]], axis=0)
            vcat = jnp.concatenate([vp_ref[:, ks], vc_ref[:, ks], vn_ref[:, ks], vx_ref[:, ks]], axis=0)
        else:
            kcat, vcat = kx_ref[:, ks], vx_ref[:, ks]
        s = _dot_nt(qg, kcat)
        if local:
            s = jnp.where(valid, s, NEG_INF)
        m = jnp.maximum(jnp.max(s, axis=-1, keepdims=True), sink)
        e = jnp.exp(s - m)
        denom = jnp.sum(e, axis=-1, keepdims=True) + jnp.exp(sink - m)
        o = _dot((e / denom).astype(BF16), vcat)
        for a in range(WA_GROUP):
            h = g * WA_GROUP + a
            o_ref[:, h * WA_HEAD_DIM:(h + 1) * WA_HEAD_DIM] = o[a * tq:(a + 1) * tq].astype(o_ref.dtype)


def wa_attention(geom, q, k, v, sink, *, local):
    B, T, L = geom.B, geom.T, geom.L
    sink_spec = pl.BlockSpec(memory_space=pltpu.SMEM)
    ctx_spec = pl.BlockSpec((L, WA_KV_WIDTH), lambda b, i: (geom.BT // L + b, 0))
    if local:
        tq = WINDOW
        nqb = T // tq
        kv = lambda f: pl.BlockSpec((tq, WA_KV_WIDTH), lambda b, i: (b * nqb + f(i), 0))
        prev = lambda i: jnp.maximum(i - 1, 0)
        cur = lambda i: i
        nxt = lambda i: jnp.minimum(i + 1, nqb - 1)
        in_specs = [sink_spec, pl.BlockSpec((tq, WA_WIDTH), lambda b, i: (b * nqb + i, 0)),
                    kv(prev), kv(cur), kv(nxt), kv(prev), kv(cur), kv(nxt), ctx_spec, ctx_spec]
        args = (sink, q, k, k, k, v, v, v, k, v)
        out_rows, out_spec = geom.BT, pl.BlockSpec((tq, WA_WIDTH), lambda b, i: (b * nqb + i, 0))
    else:
        tq, nqb = L, 1
        in_specs = [sink_spec, pl.BlockSpec((tq, WA_WIDTH), lambda b, i: (geom.BT // L + b, 0)), ctx_spec, ctx_spec]
        args = (sink, q, k, v)
        out_rows, out_spec = B * L, pl.BlockSpec((tq, WA_WIDTH), lambda b, i: (b, 0))
    return pl.pallas_call(
        functools.partial(_wa_attn_kernel, local=local, nqb=nqb, tq=tq),
        grid=(B, nqb),
        in_specs=in_specs,
        out_specs=out_spec,
        out_shape=jax.ShapeDtypeStruct((out_rows, WA_WIDTH), BF16),
        compiler_params=_cparams("parallel", "arbitrary"),
    )(*args)


def _mla_prep_kernel(z_ref, qg_ref, kvg_ref, wq_ref, wkv_ref, cos_ref, sin_ref, q_ref, k_ref, v_ref):
    def norm(x, g):
        return (x * lax.rsqrt(jnp.mean(x * x, axis=-1, keepdims=True) + NORM_EPS) * g).astype(BF16)

    cos = cos_ref[...]
    sin = sin_ref[...]
    q = _dot(norm(z_ref[:, :MLA_Q_LORA], qg_ref[...]), wq_ref[...])
    kv = _dot(norm(z_ref[:, MLA_Q_LORA:MLA_Q_LORA + MLA_KV_LORA], kvg_ref[...]), wkv_ref[...])
    kr = z_ref[:, MLA_Q_LORA + MLA_KV_LORA:MLA_Q_LORA + MLA_KV_LORA + 128]
    kr = (kr * cos[:, 128:] + _rot_half64(kr) * sin[:, 128:]).astype(BF16)
    for h in range(MLA_HEADS):
        qh = q[:, h * MLA_QK_PAD:(h + 1) * MLA_QK_PAD]
        q_ref[h] = ((qh * cos + _rot_half64(qh) * sin) * (MLA_SCALE * LOG2_E)).astype(BF16)
        k_ref[h, :, :MLA_NOPE] = kv[:, h * 256:h * 256 + MLA_NOPE].astype(BF16)
        k_ref[h, :, MLA_NOPE:] = kr
        v_ref[h] = kv[:, h * 256 + MLA_NOPE:(h + 1) * 256].astype(BF16)


def mla_prep(z, qnorm_g, kvnorm_g, wq, wkv, cos, sin, *, tm):
    R, Z = z.shape
    full = lambda a: pl.BlockSpec(a.shape, lambda i: (0,) * a.ndim)
    qg, kvg = qnorm_g.reshape(1, -1), kvnorm_g.reshape(1, -1)
    hd = lambda w: pl.BlockSpec((MLA_HEADS, tm, w), lambda i: (0, i, 0))
    return pl.pallas_call(
        _mla_prep_kernel,
        grid=(R // tm,),
        in_specs=[pl.BlockSpec((tm, Z), lambda i: (i, 0)), full(qg), full(kvg), full(wq), full(wkv),
                  pl.BlockSpec((tm, MLA_QK_PAD), lambda i: (i, 0)), pl.BlockSpec((tm, MLA_QK_PAD), lambda i: (i, 0))],
        out_specs=[hd(MLA_QK_PAD), hd(MLA_QK_PAD), hd(MLA_V)],
        out_shape=[jax.ShapeDtypeStruct((MLA_HEADS, R, MLA_QK_PAD), BF16),
                   jax.ShapeDtypeStruct((MLA_HEADS, R, MLA_QK_PAD), BF16),
                   jax.ShapeDtypeStruct((MLA_HEADS, R, MLA_V), BF16)],
        compiler_params=_cparams("parallel"),
    )(z, qg, kvg, wq, wkv, cos, sin)


def _mla_flash_kernel(*refs, with_latent, sub):
    if with_latent:
        q_ref, kx_ref, vx_ref, k_ref, v_ref, o_ref, m_ref, l_ref, acc_ref = refs
    else:
        q_ref, kx_ref, vx_ref, o_ref, m_ref, l_ref, acc_ref = refs
    ki = pl.program_id(3)
    q = q_ref[0]

    def update(s, v, m_old, l_old, acc_old):
        cols = [s[:, c * 128:(c + 1) * 128] for c in range(s.shape[1] // 128)]
        mx = functools.reduce(jnp.maximum, cols)
        m_new = jnp.maximum(m_old, jnp.max(mx, axis=-1, keepdims=True))
        alpha = jnp.exp2(m_old - m_new)
        e = [jnp.exp2(c - m_new) for c in cols]
        p = jnp.concatenate([x.astype(BF16) for x in e], axis=-1)
        return m_new, alpha * l_old + functools.reduce(jnp.add, e), alpha * acc_old + _dot(p, v)

    @pl.when(ki == 0)
    def _():
        tq = q.shape[0]
        m, l, acc = update(_dot_nt(q, kx_ref[0]), vx_ref[0], jnp.full((tq, 128), NEG_INF, F32),
                           jnp.zeros((tq, 128), F32), jnp.zeros((tq, MLA_V), F32))
        m_ref[...], l_ref[...], acc_ref[...] = m, l, acc

    if with_latent:
        nsub = k_ref.shape[1] // sub
        m, l, acc = m_ref[...], l_ref[...], acc_ref[...]
        s_next = _dot_nt(q, k_ref[0, 0:sub, :])
        for j in range(nsub):
            s = s_next
            if j + 1 < nsub:
                s_next = _dot_nt(q, k_ref[0, (j + 1) * sub:(j + 2) * sub, :])
            m, l, acc = update(s, v_ref[0, j * sub:(j + 1) * sub, :], m, l, acc)
        m_ref[...], l_ref[...], acc_ref[...] = m, l, acc

    @pl.when(ki == pl.num_programs(3) - 1)
    def _():
        o_ref[...] = (acc_ref[...] / jnp.sum(l_ref[...], axis=-1, keepdims=True)).astype(o_ref.dtype)


def mla_attention(geom, q, k, v, *, with_latent, tq, tk):
    B, T, L = geom.B, geom.T, geom.L
    cblk = geom.BT // L
    ctx_k = pl.BlockSpec((1, L, MLA_QK_PAD), lambda b, h, qi, ki: (h, cblk + b, 0))
    ctx_v = pl.BlockSpec((1, L, MLA_V), lambda b, h, qi, ki: (h, cblk + b, 0))
    if with_latent:
        nq, nk = T // tq, T // tk
        in_specs = [pl.BlockSpec((1, tq, MLA_QK_PAD), lambda b, h, qi, ki: (h, b * nq + qi, 0)), ctx_k, ctx_v,
                    pl.BlockSpec((1, tk, MLA_QK_PAD), lambda b, h, qi, ki: (h, b * nk + ki, 0)),
                    pl.BlockSpec((1, tk, MLA_V), lambda b, h, qi, ki: (h, b * nk + ki, 0))]
        args = (q, k, v, k, v)
        out_rows, out_spec = geom.BT, pl.BlockSpec((tq, MLA_V), lambda b, h, qi, ki: (b * nq + qi, h))
    else:
        tq, nq, nk = L, 1, 1
        in_specs = [pl.BlockSpec((1, tq, MLA_QK_PAD), lambda b, h, qi, ki: (h, cblk + b, 0)), ctx_k, ctx_v]
        args = (q, k, v)
        out_rows, out_spec = B * L, pl.BlockSpec((tq, MLA_V), lambda b, h, qi, ki: (b, h))
    return pl.pallas_call(
        functools.partial(_mla_flash_kernel, with_latent=with_latent, sub=min(MLA_SUB_KEYS, tk)),
        grid=(B, MLA_HEADS, nq, nk),
        in_specs=in_specs,
        out_specs=out_spec,
        out_shape=jax.ShapeDtypeStruct((out_rows, MLA_HEADS * MLA_V), BF16),
        scratch_shapes=[pltpu.VMEM((tq, 128), F32), pltpu.VMEM((tq, 128), F32), pltpu.VMEM((tq, MLA_V), F32)],
        compiler_params=_cparams("parallel", "parallel", "parallel", "arbitrary"),
    )(*args)


RW_Z_R, RW_Z_K, RW_Z_V = 0, RW_WIDTH, 2 * RW_WIDTH
RW_Z_LORA = 3 * RW_WIDTH
RW_Z_COLS = 3 * RW_WIDTH + 5 * LORA_PAD


def _head_sum(x, ones):
    return jnp.concatenate(
        [_dot(x[:, c * 128:(c + 1) * 128], ones, HIGHEST) for c in range(x.shape[1] // 128)], axis=-1)


def _rw_prep_kernel(*refs, geom, tm, has_vres):
    if has_vres:
        (z_ref, zp_ref, zn_ref, mu_ref, w0_ref, w2_ref, a0_ref, a2_ref, g2_ref, kk_ref, ka_ref, rk_ref, ones_ref,
         vf_ref, v0_ref, v1_ref, v2_ref,
         r_o, v_o, kk_o, lwf_o, kf_o, bf_o, lwb_o, kb_o, bb_o, g_o, bonus_o) = refs
    else:
        (z_ref, zp_ref, zn_ref, mu_ref, w0_ref, w2_ref, a0_ref, a2_ref, g2_ref, kk_ref, ka_ref, rk_ref, ones_ref,
         r_o, v_o, kk_o, lwf_o, kf_o, bf_o, lwb_o, kb_o, bb_o, g_o, bonus_o) = refs
    first, last = _seq_edge_masks(geom, pl.program_id(0) * tm, tm)
    z = z_ref[...]
    up, dn = _shifted_rows(z, zp_ref[...], zn_ref[...], first, last)
    z = z + mu_ref[...] * (0.5 * (up + dn) - z)
    r = z[:, RW_Z_R:RW_Z_R + RW_WIDTH]
    k = z[:, RW_Z_K:RW_Z_K + RW_WIDTH]
    v = z[:, RW_Z_V:RW_Z_V + RW_WIDTH]
    lora = lambda n: z[:, RW_Z_LORA + n * LORA_PAD:RW_Z_LORA + (n + 1) * LORA_PAD]
    ones = ones_ref[...]
    if has_vres:
        mix = jax.nn.sigmoid(v0_ref[...] + _dot(_dot(v, v1_ref[...], HIGHEST), v2_ref[...], HIGHEST))
        v = v + (vf_ref[...] - v) * mix
    g_o[...] = _dot(jax.nn.sigmoid(lora(4)), g2_ref[...], HIGHEST)
    kk = k * kk_ref[...]
    kk = kk / jnp.maximum(jnp.sqrt(_head_sum(kk * kk, ones)), 1e-12)
    ksum = None
    for d, (lw_o, k_o, b_o) in enumerate(((lwf_o, kf_o, bf_o), (lwb_o, kb_o, bb_o))):
        x = -(w0_ref[d:d + 1, :] + _dot(jnp.tanh(lora(d)), w2_ref[d], HIGHEST))
        softplus = jnp.maximum(x, 0.0) + jnp.log1p(jnp.exp(-jnp.abs(x)))
        lw_o[...] = -jnp.exp(-softplus - 0.5)
        a = jax.nn.sigmoid(a0_ref[d:d + 1, :] + _dot(lora(2 + d), a2_ref[d], HIGHEST))
        kd = k * (1.0 + (a - 1.0) * ka_ref[...])
        k_o[...] = kd
        b_o[...] = kk * a
        ksum = kd if ksum is None else ksum + kd
    r_o[...] = r
    v_o[...] = v
    kk_o[...] = kk
    bonus_o[...] = _head_sum(r * ksum * rk_ref[...], ones) * v


def rw_prep(geom, z, p, v_first, vres, *, tm):
    R = z.shape[0]
    has_vres = vres is not None
    full = lambda a: pl.BlockSpec(a.shape, lambda i: (0,) * a.ndim)
    row = pl.BlockSpec((tm, RW_WIDTH), lambda i: (i, 0))
    prev, nxt = _halo_specs(tm, R, RW_Z_COLS, lambda i: 0)
    params = [p["mu"], p["w0"], p["w2"], p["a0"], p["a2"], p["g2"], p["kk"], p["ka"], p["rk"], p["ones"]]
    in_specs = [pl.BlockSpec((tm, RW_Z_COLS), lambda i: (i, 0)), prev, nxt] + [full(a) for a in params]
    args = [z, z, z] + params
    if has_vres:
        in_specs += [row] + [full(a) for a in vres]
        args += [v_first] + list(vres)
    return pl.pallas_call(
        functools.partial(_rw_prep_kernel, geom=geom, tm=tm, has_vres=has_vres),
        grid=(R // tm,),
        in_specs=in_specs,
        out_specs=[row] * 11,
        out_shape=[jax.ShapeDtypeStruct((R, RW_WIDTH), F32)] * 11,
        compiler_params=_cparams("parallel"),
    )(*args)


def _pieces(x, passes):
    hi = x.astype(BF16)
    if passes == 1:
        return (hi,)
    return hi, (x - hi.astype(F32)).astype(BF16)


def _mmx(a, b, dot):
    out = dot(a[0], b[0])
    if len(a) > 1:
        out = out + dot(a[1], b[0])
    if len(b) > 1:
        out = out + dot(a[0], b[1])
    return out


def _stack_pair(first_head, x):
    return jnp.concatenate([jnp.where(first_head, x, 0.0), jnp.where(first_head, 0.0, x)], axis=0)


def _fold_pair(x):
    half = x.shape[0] // 2
    return x[:half] + x[half:]


def _rw_chunk_kernel(r_ref, v_ref, kk_ref, lwf_ref, kf_ref, bf_ref, lwb_ref, kb_ref, bb_ref,
                     rrf_o, ylf_o, mf_o, nf_o, ef_o, rrb_o, ylb_o, mb_o, nb_o, eb_o):
    C = RW_CHUNK
    PW = RW_PAIR
    ri = lax.broadcasted_iota(jnp.int32, (PW, PW), 0)
    ci = lax.broadcasted_iota(jnp.int32, (PW, PW), 1)
    eye = ri == ci
    ri, ci = ri & (C - 1), ci & (C - 1)
    first_head = lax.broadcasted_iota(jnp.int32, (1, PW), 1) < RW_HEAD_DIM
    stack = functools.partial(_stack_pair, first_head)
    r, v, kk = r_ref[...], v_ref[...], kk_ref[...]
    jobs = []
    for lw_ref, k_ref, b_ref, outs, before, tot_row in (
            (lwf_ref, kf_ref, bf_ref, (rrf_o, ylf_o, mf_o, nf_o, ef_o), ci < ri, C - 1),
            (lwb_ref, kb_ref, bb_ref, (rrb_o, ylb_o, mb_o, nb_o, eb_o), ci > ri, 0)):
        lw, kd, bd = lw_ref[...], k_ref[...], b_ref[...]
        incl = (before | eye)[:C, :C].astype(BF16)
        c, rest = None, lw
        for _ in range(3):
            piece = rest.astype(BF16)
            rest = rest - piece.astype(F32)
            part = _dot(incl, piece)
            c = part if c is None else c + part
        c_tot = c[tot_row:tot_row + 1, :]
        at = -kk * jnp.exp(c - lw)
        rt = r * jnp.exp(c)
        e_neg = jnp.exp(-c)
        bt, kt = bd * e_neg, kd * e_neg
        e_rest = jnp.exp(c_tot - c)
        bc, kc = bd * e_rest, kd * e_rest
        outs[4][...] = jnp.broadcast_to(jnp.exp(c_tot), (8, RW_WIDTH))
        for p in range(RW_WIDTH // PW):
            ps = slice(p * PW, (p + 1) * PW)
            jobs.append(dict(ps=ps, outs=outs, before=before, incl=before | eye, rt_pair=rt[:, ps],
                             at=stack(at[:, ps]), rt=stack(rt[:, ps]), bt=stack(bt[:, ps]), kt=stack(kt[:, ps]),
                             bc=stack(bc[:, ps]), kc=stack(kc[:, ps]), v=stack(v[:, ps])))
    for j in jobs:
        p = _mmx(_pieces(jnp.concatenate([j["at"], j["rt"]], axis=0), RW_PASSES_LOCAL),
                 _pieces(jnp.concatenate([j["bt"], j["kt"]], axis=0), RW_PASSES_LOCAL), _dot_nt)
        j["a_ab"] = jnp.where(j["before"], p[:PW, :PW], 0.0)
        j["a_ak"] = jnp.where(j["before"], p[:PW, PW:], 0.0)
        j["a_rb"] = jnp.where(j["incl"], p[PW:, :PW], 0.0)
        j["a_rk"] = jnp.where(j["incl"], p[PW:, PW:], 0.0)
        j["vp"] = _pieces(j["v"], RW_PASSES_LOCAL)
    for j in jobs:
        j["w1"] = _mmx(_pieces(j["a_ak"], RW_PASSES_LOCAL), j["vp"], _dot)
        j["tinv"] = jnp.where(eye, 1.0, j["a_ab"])
        j["pw"] = j["a_ab"]
    for _ in range(int(np.log2(C)) - 1):
        for j in jobs:
            pw = _pieces(j["pw"], RW_PASSES_LOCAL)
            j["pw"] = _mmx(pw, pw, _dot)
        for j in jobs:
            j["tinv"] = j["tinv"] + _mmx(_pieces(j["tinv"], RW_PASSES_LOCAL), _pieces(j["pw"], RW_PASSES_LOCAL), _dot)
    for j in jobs:
        tw = _mmx(_pieces(j["tinv"], RW_PASSES_LOCAL),
                  _pieces(jnp.concatenate([j["at"], j["w1"]], axis=1), RW_PASSES_LOCAL), _dot)
        j["tw"] = tw
        j["twp"] = _pieces(tw, RW_PASSES_LOCAL)
    for j in jobs:
        rr_o, yl_o, m_o, n_o, _ = j["outs"]
        ps = j["ps"]
        ry = _mmx(_pieces(j["a_rb"], RW_PASSES_LOCAL), j["twp"], _dot)
        yk = _mmx(_pieces(j["a_rk"], RW_PASSES_LOCAL), j["vp"], _dot)
        rr_o[:, ps] = j["rt_pair"] + _fold_pair(ry[:, :PW])
        yl_o[:, ps] = _fold_pair(ry[:, PW:] + yk)
        bcp = _pieces(j["bc"], RW_PASSES_STATE)
        m_o[:, ps] = _fold_pair(_mmx(bcp, _pieces(j["tw"][:, :PW], RW_PASSES_STATE), _dot_tn))
        uv = jnp.concatenate([j["tw"][:, PW:], j["v"]], axis=0)
        bk = jnp.concatenate([j["bc"], j["kc"]], axis=0)
        n_o[:, ps] = _fold_pair(_mmx(_pieces(uv, RW_PASSES_STATE), _pieces(bk, RW_PASSES_STATE), _dot_tn))


def rw_chunk(arrs):
    R = arrs[0].shape[0]
    blk = pl.BlockSpec((RW_CHUNK, RW_WIDTH), lambda i: (i, 0))
    eblk = pl.BlockSpec((8, RW_WIDTH), lambda i: (i, 0))
    tok = jax.ShapeDtypeStruct((R, RW_WIDTH), F32)
    dec = jax.ShapeDtypeStruct((R // RW_CHUNK * 8, RW_WIDTH), F32)
    return pl.pallas_call(
        _rw_chunk_kernel,
        grid=(R // RW_CHUNK,),
        in_specs=[blk] * 9,
        out_specs=[blk] * 4 + [eblk] + [blk] * 4 + [eblk],
        out_shape=[tok] * 4 + [dec] + [tok] * 4 + [dec],
        compiler_params=_cparams("parallel"),
    )(*arrs)


def _rw_scan_kernel(rrf_ref, ylf_ref, mf_ref, nf_ref, ef_ref, rrb_ref, ylb_ref, mb_ref, nb_ref, eb_ref,
                    yf_o, yb_o, s_ref, *, nchunk):
    C = RW_CHUNK
    PW = RW_PAIR
    first_head = lax.broadcasted_iota(jnp.int32, (1, PW), 1) < RW_HEAD_DIM
    stack = functools.partial(_stack_pair, first_head)

    @pl.when(pl.program_id(1) == 0)
    def _():
        s_ref[...] = jnp.zeros_like(s_ref)

    dirs = ((rrf_ref, ylf_ref, mf_ref, nf_ref, ef_ref, yf_o), (rrb_ref, ylb_ref, mb_ref, nb_ref, eb_ref, yb_o))
    pairs = [slice(p * PW, (p + 1) * PW) for p in range(RW_WIDTH // PW)]
    state = [[s_ref[d, :, ps] for ps in pairs] for d in range(2)]
    for step in range(nchunk):
        for d, (rr_ref, yl_ref, m_ref, n_ref, e_ref, y_o) in enumerate(dirs):
            c = step if d == 0 else nchunk - 1 - step
            rows = slice(c * C, (c + 1) * C)
            for p, ps in enumerate(pairs):
                s = state[d][p]
                sp = _pieces(stack(s), RW_PASSES_STATE)
                y = _mmx(_pieces(stack(rr_ref[rows, ps]), RW_PASSES_STATE), sp, _dot_nt)
                y_o[rows, ps] = _fold_pair(y) + yl_ref[rows, ps]
                sm = _mmx(sp, _pieces(stack(m_ref[rows, ps]), RW_PASSES_STATE), _dot_nt)
                state[d][p] = s * e_ref[c * 8:c * 8 + 1, ps] + _fold_pair(sm) + n_ref[rows, ps]
    for d in range(2):
        for p, ps in enumerate(pairs):
            s_ref[d, :, ps] = state[d][p]


def rw_scan(geom, chunk_out):
    B, T, L = geom.B, geom.T, geom.L
    blk = L
    nchunk = blk // RW_CHUNK
    nlat = T // blk
    cblk = geom.BT // blk
    fwd_i = lambda b, s: (jnp.where(s == 0, cblk + b, b * nlat + s - 1), 0)
    bwd_i = lambda b, s: (jnp.where(s == 0, cblk + b, b * nlat + nlat - s), 0)
    tok = lambda f: pl.BlockSpec((blk, RW_WIDTH), f)
    dec = lambda f: pl.BlockSpec((8 * nchunk, RW_WIDTH), f)
    return pl.pallas_call(
        functools.partial(_rw_scan_kernel, nchunk=nchunk),
        grid=(B, nlat + 1),
        in_specs=[tok(fwd_i)] * 4 + [dec(fwd_i)] + [tok(bwd_i)] * 4 + [dec(bwd_i)],
        out_specs=[tok(fwd_i), tok(bwd_i)],
        out_shape=[jax.ShapeDtypeStruct((geom.R, RW_WIDTH), F32)] * 2,
        scratch_shapes=[pltpu.VMEM((2, RW_HEAD_DIM, RW_WIDTH), F32)],
        compiler_params=_cparams("parallel", "arbitrary"),
    )(*chunk_out)


def _rw_post_kernel(yf_ref, yb_ref, bonus_ref, g_ref, lng_ref, lnb_ref, ones_ref, o_ref):
    ones = ones_ref[...]
    y = yf_ref[...] + yb_ref[...]
    mean = _head_sum(y, ones) * (1.0 / RW_HEAD_DIM)
    yc = y - mean
    var = _head_sum(yc * yc, ones) * (1.0 / RW_HEAD_DIM)
    y = yc * lax.rsqrt(var + RW_LNX_EPS) * lng_ref[...] + lnb_ref[...]
    o_ref[...] = ((y + bonus_ref[...]) * g_ref[...]).astype(o_ref.dtype)


def rw_post(yf, yb, bonus, g, lnx_g, lnx_b, ones, *, tm):
    R = yf.shape[0]
    row = pl.BlockSpec((tm, RW_WIDTH), lambda i: (i, 0))
    full = lambda a: pl.BlockSpec(a.shape, lambda i: (0,) * a.ndim)
    return pl.pallas_call(
        _rw_post_kernel,
        grid=(R // tm,),
        in_specs=[row] * 4 + [full(lnx_g), full(lnx_b), full(ones)],
        out_specs=row,
        out_shape=jax.ShapeDtypeStruct((R, RW_WIDTH), BF16),
        compiler_params=_cparams("parallel"),
    )(yf, yb, bonus, g, lnx_g, lnx_b, ones)


def _rw_in_cols(w):
    parts = [w[..., :3 * RW_WIDTH]]
    off = 3 * RW_WIDTH
    for n in (RW_DECAY_LORA, RW_DECAY_LORA, RW_AAA_LORA, RW_AAA_LORA, RW_GATE_LORA):
        parts.append(_pad_cols(w[..., off:off + n], LORA_PAD))
        off += n
    return jnp.concatenate(parts, axis=-1)


def _pad_rows(w, n):
    return jnp.pad(w, [(0, 0)] * (w.ndim - 2) + [(0, n - w.shape[-2]), (0, 0)])


def _mla_wq_cols(w):
    w = w.reshape(w.shape[0], MLA_HEADS, MLA_NOPE + MLA_ROPE)
    return _pad_cols(w, MLA_QK_PAD).reshape(w.shape[0], MLA_HEADS * MLA_QK_PAD)


def kernel(x, c, ctx, c_ctx, ada_w, ada_b, norm1_g, w_in, rw_mu, rw_w0, rw_w2, rw_a0, rw_a2, rw_g2, rw_kk, rw_ka,
           rw_rk, rw_lnx_g, rw_lnx_b, rw_v0, rw_v1, rw_v2, wa_sink, mla_qnorm_g, mla_kvnorm_g, mla_w_uq, mla_w_ukv,
           w_branch, w_out, norm2_g, ffn_w_in, ffn_conv_w, ffn_conv_b, ffn_w_out, final_norm_g):
    B, T, D = x.shape
    L = ctx.shape[1]
    depth = w_in.shape[0]
    F = ffn_w_out.shape[1]
    geom = Geom(B, T, L)
    tm = _pick_tile(T, (512, 256, 128))
    assert (B * L) % tm == 0

    rw_cols = 3 * RW_WIDTH + 2 * RW_DECAY_LORA + 2 * RW_AAA_LORA + RW_GATE_LORA
    wa_cols = WA_WIDTH + 2 * WA_KV_WIDTH
    mla_cols = MLA_Q_LORA + MLA_KV_LORA + MLA_ROPE
    mla_cols_pad = MLA_Q_LORA + MLA_KV_LORA + 128

    cos_wa, sin_wa = _rope_tables(geom, WA_HEAD_DIM)
    cos_wa, sin_wa = jnp.tile(cos_wa, (1, 2)), jnp.tile(sin_wa, (1, 2))
    cos_m, sin_m = _rope_tables(geom, MLA_ROPE)
    one, zero = jnp.ones((geom.R, MLA_NOPE), F32), jnp.zeros((geom.R, MLA_NOPE), F32)
    cos_mla = jnp.concatenate([one, cos_m, one[:, :64]], axis=-1)
    sin_mla = jnp.concatenate([zero, sin_m, zero[:, :64]], axis=-1)
    lane = np.arange(128)
    ones_blk = jnp.asarray((lane[:, None] // RW_HEAD_DIM) == (lane[None, :] // RW_HEAD_DIM), F32)

    xs = jnp.concatenate([x.reshape(B * T, D), ctx.reshape(B * L, D)], axis=0)
    cvec = jnp.concatenate([c, c_ctx[None, :], jnp.zeros((8 - (B + 1) % 8, D), F32)], axis=0)
    v_first = None
    for l in range(depth):
        need_ctx = l < depth - 1
        mod = ada_modulation(cvec, ada_w[l], ada_b[l])
        mod = [mod[:, k * D:(k + 1) * D].reshape(-1, 1, D) for k in range(6)]

        w = w_in[l]
        w_rw = _pad_cols(_rw_in_cols(w[:, :rw_cols]), _round_up(RW_Z_COLS, 768)).astype(BF16)
        w_wa = w[:, rw_cols:rw_cols + wa_cols].astype(BF16)
        w_mla = _pad_cols(w[:, rw_cols + wa_cols:rw_cols + wa_cols + mla_cols], mla_cols_pad).astype(BF16)
        w_gate = w[:, rw_cols + wa_cols + mla_cols:].astype(BF16)
        nmm = functools.partial(norm_mod_matmul, geom, xs, norm1_g[l], mod[0], mod[1], tm=tm)
        z_rw = nmm(w_rw, tn=768)
        z_wa = nmm(w_wa, tn=512)
        z_mla = nmm(w_mla, tn=mla_cols_pad)
        gates = nmm(w_gate, tn=_pick_tile(3 * D, (1024, 768, 512, 256, 128)), act="sigmoid")

        rw_p = dict(
            mu=_rw_in_cols(rw_mu[l][None, :]), w0=rw_w0[l], w2=_pad_rows(rw_w2[l], LORA_PAD), a0=rw_a0[l],
            a2=_pad_rows(rw_a2[l], LORA_PAD), g2=_pad_rows(rw_g2[l], LORA_PAD), kk=rw_kk[l][None, :],
            ka=rw_ka[l][None, :], rk=rw_rk[l].reshape(1, RW_WIDTH), ones=ones_blk)
        vres = None if l == 0 else (rw_v0[l - 1][None, :], rw_v1[l - 1], rw_v2[l - 1])
        r, v, kk, lwf, kf, bf, lwb, kb, bb, g, bonus = rw_prep(geom, z_rw, rw_p, v_first, vres, tm=min(tm, 256))
        if l == 0:
            v_first = v
        yf, yb = rw_scan(geom, rw_chunk((r, v, kk, lwf, kf, bf, lwb, kb, bb)))
        o_a = rw_post(yf, yb, bonus, g, rw_lnx_g[l][None, :], rw_lnx_b[l][None, :], ones_blk, tm=tm)

        q_wa, k_wa, v_wa = wa_prep(z_wa, cos_wa, sin_wa, tm=tm)
        ob_l = wa_attention(geom, q_wa, k_wa, v_wa, wa_sink[l], local=True)
        parts = [ob_l]
        if need_ctx:
            parts.append(wa_attention(geom, q_wa, k_wa, v_wa, wa_sink[l], local=False))
        else:
            parts.append(jnp.zeros((B * L, WA_WIDTH), BF16))
        o_b = jnp.concatenate(parts, axis=0)

        wq = _mla_wq_cols(mla_w_uq[l]).astype(BF16)
        q_m, k_m, v_m = mla_prep(z_mla, mla_qnorm_g[l], mla_kvnorm_g[l], wq, mla_w_ukv[l].astype(BF16),
                                 cos_mla, sin_mla, tm=tm)
        tq = _pick_tile(T, (1024, 512, 256, 128))
        tk = _pick_tile(T, (2048, 1024, 512, 256, 128))
        parts = [mla_attention(geom, q_m, k_m, v_m, with_latent=True, tq=tq, tk=tk)]
        if need_ctx:
            parts.append(mla_attention(geom, q_m, k_m, v_m, with_latent=False, tq=L, tk=L))
        else:
            parts.append(jnp.zeros((B * L, MLA_HEADS * MLA_V), BF16))
        o_c = jnp.concatenate(parts, axis=0)

        tn_d = _pick_tile(D, (1024, 512, 256, 128))
        y = merge_branches(o_a, o_b, o_c, gates, w_branch[l].astype(BF16), tm=tm, tn=tn_d)
        xs = matmul_gated_residual(geom, y, w_out[l].astype(BF16), xs, mod[2], tm=tm, tn=tn_d)

        tf = _pick_tile(F, (512, 256, 128))
        hmid = norm_mod_matmul(geom, xs, norm2_g[l], mod[3], mod[4], ffn_w_in[l].astype(BF16), tm=tm, tn=tf)
        hmid = conv_glu_middle(geom, hmid, ffn_conv_w[l], ffn_conv_b[l], tm=tm, tf=tf)
        xs = matmul_gated_residual(geom, hmid, ffn_w_out[l].astype(BF16), xs, mod[5], tm=tm, tn=tn_d)

    out = final_rmsnorm(xs, final_norm_g, B * T, tm=tm)
    return out.reshape(B, T, D)
```

```python
import functools

import jax
import jax.numpy as jnp
import numpy as np
from jax import lax
from jax.experimental import pallas as pl
from jax.experimental.pallas import tpu as pltpu

F32 = jnp.float32
BF16 = jnp.bfloat16
HIGHEST = lax.Precision.HIGHEST

NORM_EPS = 1e-6
NEG_INF = -1e30
GRID_W = 64
ROPE_BASE = 10000.0

RW_HEADS = 16
RW_HEAD_DIM = 64
RW_WIDTH = RW_HEADS * RW_HEAD_DIM
RW_DECAY_LORA = 96
RW_AAA_LORA = 96
RW_GATE_LORA = 64
RW_LNX_EPS = 64e-5
RW_CHUNK = 64
RW_PAIR = 2 * RW_HEAD_DIM
RW_PASSES_LOCAL = 1
RW_PASSES_STATE = 3
LORA_PAD = 128

WA_HEADS = 16
WA_KV_HEADS = 4
WA_GROUP = WA_HEADS // WA_KV_HEADS
WA_HEAD_DIM = 64
WA_WIDTH = WA_HEADS * WA_HEAD_DIM
WA_KV_WIDTH = WA_KV_HEADS * WA_HEAD_DIM
WINDOW = 128
WA_SCALE = WA_HEAD_DIM ** -0.5

MLA_HEADS = 8
MLA_NOPE = 128
MLA_ROPE = 64
MLA_V = 128
MLA_Q_LORA = 512
MLA_KV_LORA = 512
MLA_QK_PAD = 256
MLA_SCALE = (MLA_NOPE + MLA_ROPE) ** -0.5
MLA_SUB_KEYS = 512
LOG2_E = 1.4426950408889634

CONV_W = 3
VMEM_LIMIT_BYTES = 56 * 1024 * 1024


def _cparams(*sem):
    return pltpu.CompilerParams(dimension_semantics=sem, vmem_limit_bytes=VMEM_LIMIT_BYTES)


def _dot(a, b, precision=None):
    return jnp.dot(a, b, preferred_element_type=F32, precision=precision)


def _dot_nt(a, b, precision=None):
    return lax.dot_general(a, b, (((1,), (1,)), ((), ())), preferred_element_type=F32, precision=precision)


def _dot_tn(a, b, precision=None):
    return lax.dot_general(a, b, (((0,), (0,)), ((), ())), preferred_element_type=F32, precision=precision)


def _pick_tile(n, candidates):
    for c in candidates:
        if n % c == 0:
            return c
    raise ValueError(f"no tile in {candidates} divides {n}")


def _pad_cols(w, n):
    return jnp.pad(w, [(0, 0)] * (w.ndim - 1) + [(0, n - w.shape[-1])])


def _round_up(n, m):
    return (n + m - 1) // m * m


class Geom:
    def __init__(self, B, T, L):
        assert T & (T - 1) == 0 and L & (L - 1) == 0, "sequence lengths must be powers of two"
        assert T % L == 0 and L % RW_CHUNK == 0 and T % GRID_W == 0
        self.B, self.T, self.L = B, T, L
        self.BT = B * T
        self.R = B * T + B * L


def _select_row_group(geom, row0, tm, tab_ref):
    r = row0 + lax.broadcasted_iota(jnp.int32, (tm, 1), 0)
    out = tab_ref[geom.B]
    for b in range(geom.B):
        out = jnp.where((r >= b * geom.T) & (r < (b + 1) * geom.T), tab_ref[b], out)
    return out


def _seq_edge_masks(geom, row0, tm):
    r = row0 + lax.broadcasted_iota(jnp.int32, (tm, 1), 0)
    is_lat = r < geom.BT
    pos = jnp.where(is_lat, r & (geom.T - 1), (r - geom.BT) & (geom.L - 1))
    last = jnp.where(is_lat, geom.T - 1, geom.L - 1)
    return pos == 0, pos == last


def _shifted_rows(x, prev8, next8, first, last):
    tm = x.shape[0]
    rid = lax.broadcasted_iota(jnp.int32, (tm, 1), 0)
    up = jnp.where(rid == 0, prev8[7:8, :], pltpu.roll(x, 1, axis=0))
    dn = jnp.where(rid == tm - 1, next8[0:1, :], pltpu.roll(x, tm - 1, axis=0))
    return jnp.where(first, 0.0, up), jnp.where(last, 0.0, dn)


def _halo_specs(tm, R, width, col_of):
    nb8 = tm // 8
    prev = pl.BlockSpec((8, width), lambda i, *a: (jnp.maximum(i * nb8 - 1, 0), col_of(i, *a)))
    nxt = pl.BlockSpec((8, width), lambda i, *a: (jnp.minimum((i + 1) * nb8, R // 8 - 1), col_of(i, *a)))
    return prev, nxt


def _ada_kernel(c_ref, w_ref, b_ref, o_ref):
    c = c_ref[...]
    o_ref[...] = _dot(c * jax.nn.sigmoid(c), w_ref[...], HIGHEST) + b_ref[...]


def ada_modulation(cvec, w, b):
    G, D = cvec.shape
    N = w.shape[1]
    tn = _pick_tile(N, (1024, 512, 256, 128))
    return pl.pallas_call(
        _ada_kernel,
        grid=(N // tn,),
        in_specs=[pl.BlockSpec((G, D), lambda j: (0, 0)),
                  pl.BlockSpec((D, tn), lambda j: (0, j)),
                  pl.BlockSpec((1, tn), lambda j: (0, j))],
        out_specs=pl.BlockSpec((G, tn), lambda j: (0, j)),
        out_shape=jax.ShapeDtypeStruct((G, N), F32),
        compiler_params=_cparams("arbitrary"),
    )(cvec, w, b.reshape(1, N))


def _norm_modulate(x, g, sc, sh):
    n = x * lax.rsqrt(jnp.mean(x * x, axis=-1, keepdims=True) + NORM_EPS) * g
    return (n * (1.0 + sc) + sh).astype(BF16)


PROLOGUE_ROWS = 256


def _norm_modulate_rows(geom, row0, x_ref, g_ref, sc_ref, sh_ref, h_ref, h_off):
    tm = x_ref.shape[0]
    step = PROLOGUE_ROWS if tm % PROLOGUE_ROWS == 0 else tm
    g = g_ref[...]
    for r in range(0, tm, step):
        sc = _select_row_group(geom, row0 + r, step, sc_ref)
        sh = _select_row_group(geom, row0 + r, step, sh_ref)
        h_ref[h_off + r:h_off + r + step] = _norm_modulate(x_ref[r:r + step], g, sc, sh)


def _nmm_kernel(x_ref, g_ref, sh_ref, sc_ref, w_ref, o_ref, h_ref, *, geom, tm, act):
    @pl.when(pl.program_id(1) == 0)
    def _():
        _norm_modulate_rows(geom, pl.program_id(0) * tm, x_ref, g_ref, sc_ref, sh_ref, h_ref, 0)

    acc = _dot(h_ref[...], w_ref[...])
    if act == "sigmoid":
        acc = jax.nn.sigmoid(acc)
    o_ref[...] = acc.astype(o_ref.dtype)


def norm_mod_matmul(geom, x, g, shift, scale, w, *, tm, tn, act=None, out_dtype=F32):
    R, K = x.shape
    N = w.shape[1]
    assert R % tm == 0 and N % tn == 0
    full = lambda a: pl.BlockSpec(a.shape, lambda i, j: (0,) * a.ndim)
    return pl.pallas_call(
        functools.partial(_nmm_kernel, geom=geom, tm=tm, act=act),
        grid=(R // tm, N // tn),
        in_specs=[pl.BlockSpec((tm, K), lambda i, j: (i, 0), pipeline_mode=pl.Buffered(1)),
                  pl.BlockSpec((1, K), lambda i, j: (0, 0)),
                  full(shift), full(scale),
                  pl.BlockSpec((K, tn), lambda i, j: (0, j))],
        out_specs=pl.BlockSpec((tm, tn), lambda i, j: (i, j)),
        out_shape=jax.ShapeDtypeStruct((R, N), out_dtype),
        scratch_shapes=[pltpu.VMEM((tm, K), BF16)],
        compiler_params=_cparams("parallel", "arbitrary"),
    )(x, g.reshape(1, K), shift, scale, w)


def _mm_resid_kernel(y_ref, w_ref, r_ref, gate_ref, o_ref, *, geom, tm):
    gate = _select_row_group(geom, pl.program_id(0) * tm, tm, gate_ref)
    o_ref[...] = r_ref[...] + gate * _dot(y_ref[...], w_ref[...])


def matmul_gated_residual(geom, y, w, resid, gate, *, tm, tn):
    R, K = y.shape
    N = w.shape[1]
    G = gate.shape[0]
    assert R % tm == 0 and N % tn == 0
    return pl.pallas_call(
        functools.partial(_mm_resid_kernel, geom=geom, tm=tm),
        grid=(R // tm, N // tn),
        in_specs=[pl.BlockSpec((tm, K), lambda i, j: (i, 0)),
                  pl.BlockSpec((K, tn), lambda i, j: (0, j)),
                  pl.BlockSpec((tm, tn), lambda i, j: (i, j)),
                  pl.BlockSpec((G, 1, tn), lambda i, j: (0, 0, j))],
        out_specs=pl.BlockSpec((tm, tn), lambda i, j: (i, j)),
        out_shape=jax.ShapeDtypeStruct((R, N), F32),
        compiler_params=_cparams("parallel", "arbitrary"),
    )(y, w, resid, gate)


FFN_HALO = 16


def _ffn_in_kernel(x_ref, xp_ref, xn_ref, g_ref, sh_ref, sc_ref, wg_ref, wu_ref, cw_ref, cb_ref, o_ref, h_ref,
                   *, geom, tm):
    row0 = pl.program_id(0) * tm
    H = FFN_HALO

    @pl.when(pl.program_id(1) == 0)
    def _():
        _norm_modulate_rows(geom, row0, xp_ref, g_ref, sc_ref, sh_ref, h_ref, 0)
        _norm_modulate_rows(geom, row0, x_ref, g_ref, sc_ref, sh_ref, h_ref, H)
        _norm_modulate_rows(geom, row0 + tm - H, xn_ref, g_ref, sc_ref, sh_ref, h_ref, H + tm)

    first, last = _seq_edge_masks(geom, row0, tm)
    gt = _dot(h_ref[...], wg_ref[...])
    u = _dot(h_ref[H:H + tm], wu_ref[...])
    up = jnp.where(first, 0.0, pltpu.roll(gt, 1, axis=0)[H:H + tm])
    dn = jnp.where(last, 0.0, pltpu.roll(gt, tm + 2 * H - 1, axis=0)[H:H + tm])
    cw = cw_ref[...]
    conv = cb_ref[...] + up * cw[0:1, :]
    conv = conv + gt[H:H + tm] * cw[1:2, :]
    conv = conv + dn * cw[2:3, :]
    o_ref[...] = (jax.nn.gelu(conv, approximate=True) * u).astype(o_ref.dtype)


def ffn_in_conv_glu(geom, x, g, shift, scale, w_in, conv_w, conv_b, *, tm, tf):
    R, K = x.shape
    F = w_in.shape[1] // 2
    nj = F // tf
    nbh = tm // FFN_HALO
    assert R % tm == 0 and F % tf == 0 and tm % FFN_HALO == 0
    full = lambda a: pl.BlockSpec(a.shape, lambda i, j: (0,) * a.ndim)
    return pl.pallas_call(
        functools.partial(_ffn_in_kernel, geom=geom, tm=tm),
        grid=(R // tm, nj),
        in_specs=[pl.BlockSpec((tm, K), lambda i, j: (i, 0), pipeline_mode=pl.Buffered(1)),
                  pl.BlockSpec((FFN_HALO, K), lambda i, j: (jnp.maximum(i * nbh - 1, 0), 0)),
                  pl.BlockSpec((FFN_HALO, K), lambda i, j: (jnp.minimum((i + 1) * nbh, R // FFN_HALO - 1), 0)),
                  pl.BlockSpec((1, K), lambda i, j: (0, 0)),
                  full(shift), full(scale),
                  pl.BlockSpec((K, tf), lambda i, j: (0, j)),
                  pl.BlockSpec((K, tf), lambda i, j: (0, nj + j)),
                  pl.BlockSpec((CONV_W, tf), lambda i, j: (0, j)),
                  pl.BlockSpec((1, tf), lambda i, j: (0, j))],
        out_specs=pl.BlockSpec((tm, tf), lambda i, j: (i, j)),
        out_shape=jax.ShapeDtypeStruct((R, F), BF16),
        scratch_shapes=[pltpu.VMEM((tm + 2 * FFN_HALO, K), BF16)],
        compiler_params=_cparams("parallel", "arbitrary"),
    )(x, x, x, g.reshape(1, K), shift, scale, w_in, w_in, conv_w, conv_b.reshape(1, F))


def _merge_kernel(oa_ref, ob_ref, oc_ref, ga_ref, gb_ref, gc_ref, w_ref, o_ref):
    y = ga_ref[...] * _dot(oa_ref[...], w_ref[0])
    y = y + gb_ref[...] * _dot(ob_ref[...], w_ref[1])
    y = y + gc_ref[...] * _dot(oc_ref[...], w_ref[2])
    o_ref[...] = y.astype(o_ref.dtype)


def merge_branches(oa, ob, oc, gates, wb, *, tm, tn):
    R, K = oa.shape
    D = wb.shape[2]
    nj = D // tn
    bspec = pl.BlockSpec((tm, K), lambda i, j: (i, 0))
    gspec = lambda k: pl.BlockSpec((tm, tn), lambda i, j: (i, k * nj + j))
    return pl.pallas_call(
        _merge_kernel,
        grid=(R // tm, nj),
        in_specs=[bspec, bspec, bspec, gspec(0), gspec(1), gspec(2),
                  pl.BlockSpec((3, K, tn), lambda i, j: (0, 0, j))],
        out_specs=pl.BlockSpec((tm, tn), lambda i, j: (i, j)),
        out_shape=jax.ShapeDtypeStruct((R, D), BF16),
        compiler_params=_cparams("parallel", "arbitrary"),
    )(oa, ob, oc, gates, gates, gates, wb)


def _rmsnorm_kernel(x_ref, g_ref, o_ref):
    x = x_ref[...]
    o_ref[...] = x * lax.rsqrt(jnp.mean(x * x, axis=-1, keepdims=True) + NORM_EPS) * g_ref[...]


def final_rmsnorm(x, g, rows, *, tm):
    D = x.shape[1]
    return pl.pallas_call(
        _rmsnorm_kernel,
        grid=(rows // tm,),
        in_specs=[pl.BlockSpec((tm, D), lambda i: (i, 0)), pl.BlockSpec((1, D), lambda i: (0, 0))],
        out_specs=pl.BlockSpec((tm, D), lambda i: (i, 0)),
        out_shape=jax.ShapeDtypeStruct((rows, D), F32),
        compiler_params=_cparams("parallel"),
    )(x, g.reshape(1, D))


def _rot_half64(z):
    n = z.shape[-1]
    lane = lax.broadcasted_iota(jnp.int32, z.shape, z.ndim - 1)
    return jnp.where((lane & 63) < 32, pltpu.roll(z, n - 32, axis=z.ndim - 1), pltpu.roll(z, 32, axis=z.ndim - 1))


def _rope_tables(geom, dim):
    nf = dim // 4
    inv = ROPE_BASE ** (-jnp.arange(nf, dtype=F32) / nf)
    rows = geom.T // GRID_W
    row = jnp.repeat(jnp.arange(rows, dtype=F32), GRID_W)
    col = jnp.tile(jnp.arange(GRID_W, dtype=F32), rows)
    ang = jnp.concatenate([row[:, None] * inv, col[:, None] * inv], axis=-1)
    cos, sin = jnp.cos(ang), jnp.sin(ang)
    cos_t = jnp.concatenate([cos, cos], axis=-1)
    sin_t = jnp.concatenate([-sin, sin], axis=-1)
    nctx = geom.B * geom.L
    cos_f = jnp.concatenate([jnp.tile(cos_t, (geom.B, 1)), jnp.ones((nctx, dim), F32)], axis=0)
    sin_f = jnp.concatenate([jnp.tile(sin_t, (geom.B, 1)), jnp.zeros((nctx, dim), F32)], axis=0)
    return cos_f, sin_f


def _wa_prep_kernel(z_ref, cos_ref, sin_ref, q_ref, k_ref, v_ref):
    cos = cos_ref[...]
    sin = sin_ref[...]
    for c in range(WA_WIDTH // 128):
        z = z_ref[:, c * 128:(c + 1) * 128]
        q_ref[:, c * 128:(c + 1) * 128] = ((z * cos + _rot_half64(z) * sin) * WA_SCALE).astype(BF16)
    for c in range(WA_KV_WIDTH // 128):
        z = z_ref[:, WA_WIDTH + c * 128:WA_WIDTH + (c + 1) * 128]
        k_ref[:, c * 128:(c + 1) * 128] = (z * cos + _rot_half64(z) * sin).astype(BF16)
    v_ref[...] = z_ref[:, WA_WIDTH + WA_KV_WIDTH:].astype(BF16)


def wa_prep(z, cos, sin, *, tm):
    R = z.shape[0]
    row = lambda w: pl.BlockSpec((tm, w), lambda i: (i, 0))
    return pl.pallas_call(
        _wa_prep_kernel,
        grid=(R // tm,),
        in_specs=[row(WA_WIDTH + 2 * WA_KV_WIDTH), row(128), row(128)],
        out_specs=[row(WA_WIDTH), row(WA_KV_WIDTH), row(WA_KV_WIDTH)],
        out_shape=[jax.ShapeDtypeStruct((R, WA_WIDTH), BF16),
                   jax.ShapeDtypeStruct((R, WA_KV_WIDTH), BF16),
                   jax.ShapeDtypeStruct((R, WA_KV_WIDTH), BF16)],
        compiler_params=_cparams("parallel"),
    )(z, cos, sin)


def _wa_attn_kernel(*refs, local, nqb, tq):
    if local:
        sink_ref, q_ref, kp_ref, kc_ref, kn_ref, vp_ref, vc_ref, vn_ref, kx_ref, vx_ref, o_ref = refs
    else:
        sink_ref, q_ref, kx_ref, vx_ref, o_ref = refs
    i = pl.program_id(1)
    nk_ctx = kx_ref.shape[0]
    rows = WA_GROUP * tq
    qpos = lax.broadcasted_iota(jnp.int32, (rows, 1), 0) & (tq - 1)
    head_in_group = lax.broadcasted_iota(jnp.int32, (rows, 1), 0) >> int(np.log2(tq))
    if local:
        off_prev = jnp.where(i > 0, 0, tq)
        off_next = jnp.where(i < nqb - 1, 0, tq)
        j = lax.broadcasted_iota(jnp.int32, (1, 3 * tq + nk_ctx), 1)
        valid = ((j >= tq) & (j < 2 * tq)) | (j >= 3 * tq)
        valid = valid | ((j < tq) & (j >= qpos + off_prev))
        valid = valid | ((j >= 2 * tq) & (j < 3 * tq) & ((j - 2 * tq) <= qpos - off_next))
    for g in range(WA_KV_HEADS):
        ks = slice(g * WA_HEAD_DIM, (g + 1) * WA_HEAD_DIM)
        qg = jnp.concatenate(
            [q_ref[:, (g * WA_GROUP + a) * WA_HEAD_DIM:(g * WA_GROUP + a + 1) * WA_HEAD_DIM] for a in range(WA_GROUP)],
            axis=0)
        sink = jnp.zeros((rows, 1), F32)
        for a in range(WA_GROUP):
            sink = jnp.where(head_in_group == a, sink_ref[g * WA_GROUP + a], sink)
        if local:
            kcat = jnp.concatenate([kp_ref[:, ks], kc_ref[:, ks], kn_ref[:, ks], kx_ref[:, ks]], axis=0)
            vcat = jnp.concatenate([vp_ref[:, ks], vc_ref[:, ks], vn_ref[:, ks], vx_ref[:, ks]], axis=0)
        else:
            kcat, vcat = kx_ref[:, ks], vx_ref[:, ks]
        s = _dot_nt(qg, kcat)
        if local:
            s = jnp.where(valid, s, NEG_INF)
        m = jnp.maximum(jnp.max(s, axis=-1, keepdims=True), sink)
        e = jnp.exp(s - m)
        denom = jnp.sum(e, axis=-1, keepdims=True) + jnp.exp(sink - m)
        o = _dot((e / denom).astype(BF16), vcat)
        for a in range(WA_GROUP):
            h = g * WA_GROUP + a
            o_ref[:, h * WA_HEAD_DIM:(h + 1) * WA_HEAD_DIM] = o[a * tq:(a + 1) * tq].astype(o_ref.dtype)


def wa_attention(geom, q, k, v, sink, *, local):
    B, T, L = geom.B, geom.T, geom.L
    sink_spec = pl.BlockSpec(memory_space=pltpu.SMEM)
    ctx_spec = pl.BlockSpec((L, WA_KV_WIDTH), lambda b, i: (geom.BT // L + b, 0))
    if local:
        tq = WINDOW
        nqb = T // tq
        kv = lambda f: pl.BlockSpec((tq, WA_KV_WIDTH), lambda b, i: (b * nqb + f(i), 0))
        prev = lambda i: jnp.maximum(i - 1, 0)
        cur = lambda i: i
        nxt = lambda i: jnp.minimum(i + 1, nqb - 1)
        in_specs = [sink_spec, pl.BlockSpec((tq, WA_WIDTH), lambda b, i: (b * nqb + i, 0)),
                    kv(prev), kv(cur), kv(nxt), kv(prev), kv(cur), kv(nxt), ctx_spec, ctx_spec]
        args = (sink, q, k, k, k, v, v, v, k, v)
        out_rows, out_spec = geom.BT, pl.BlockSpec((tq, WA_WIDTH), lambda b, i: (b * nqb + i, 0))
    else:
        tq, nqb = L, 1
        in_specs = [sink_spec, pl.BlockSpec((tq, WA_WIDTH), lambda b, i: (geom.BT // L + b, 0)), ctx_spec, ctx_spec]
        args = (sink, q, k, v)
        out_rows, out_spec = B * L, pl.BlockSpec((tq, WA_WIDTH), lambda b, i: (b, 0))
    return pl.pallas_call(
        functools.partial(_wa_attn_kernel, local=local, nqb=nqb, tq=tq),
        grid=(B, nqb),
        in_specs=in_specs,
        out_specs=out_spec,
        out_shape=jax.ShapeDtypeStruct((out_rows, WA_WIDTH), BF16),
        compiler_params=_cparams("parallel", "arbitrary"),
    )(*args)


def _mla_prep_kernel(z_ref, qg_ref, kvg_ref, wq_ref, wkv_ref, cos_ref, sin_ref, q_ref, k_ref, v_ref):
    def norm(x, g):
        return (x * lax.rsqrt(jnp.mean(x * x, axis=-1, keepdims=True) + NORM_EPS) * g).astype(BF16)

    cos = cos_ref[...]
    sin = sin_ref[...]
    q = _dot(norm(z_ref[:, :MLA_Q_LORA], qg_ref[...]), wq_ref[...])
    kv = _dot(norm(z_ref[:, MLA_Q_LORA:MLA_Q_LORA + MLA_KV_LORA], kvg_ref[...]), wkv_ref[...])
    kr = z_ref[:, MLA_Q_LORA + MLA_KV_LORA:MLA_Q_LORA + MLA_KV_LORA + 128]
    kr = (kr * cos[:, 128:] + _rot_half64(kr) * sin[:, 128:]).astype(BF16)
    for h in range(MLA_HEADS):
        qh = q[:, h * MLA_QK_PAD:(h + 1) * MLA_QK_PAD]
        q_ref[h] = ((qh * cos + _rot_half64(qh) * sin) * (MLA_SCALE * LOG2_E)).astype(BF16)
        k_ref[h, :, :MLA_NOPE] = kv[:, h * 256:h * 256 + MLA_NOPE].astype(BF16)
        k_ref[h, :, MLA_NOPE:] = kr
        v_ref[h] = kv[:, h * 256 + MLA_NOPE:(h + 1) * 256].astype(BF16)


def mla_prep(z, qnorm_g, kvnorm_g, wq, wkv, cos, sin, *, tm):
    R, Z = z.shape
    full = lambda a: pl.BlockSpec(a.shape, lambda i: (0,) * a.ndim)
    qg, kvg = qnorm_g.reshape(1, -1), kvnorm_g.reshape(1, -1)
    hd = lambda w: pl.BlockSpec((MLA_HEADS, tm, w), lambda i: (0, i, 0))
    return pl.pallas_call(
        _mla_prep_kernel,
        grid=(R // tm,),
        in_specs=[pl.BlockSpec((tm, Z), lambda i: (i, 0)), full(qg), full(kvg), full(wq), full(wkv),
                  pl.BlockSpec((tm, MLA_QK_PAD), lambda i: (i, 0)), pl.BlockSpec((tm, MLA_QK_PAD), lambda i: (i, 0))],
        out_specs=[hd(MLA_QK_PAD), hd(MLA_QK_PAD), hd(MLA_V)],
        out_shape=[jax.ShapeDtypeStruct((MLA_HEADS, R, MLA_QK_PAD), BF16),
                   jax.ShapeDtypeStruct((MLA_HEADS, R, MLA_QK_PAD), BF16),
                   jax.ShapeDtypeStruct((MLA_HEADS, R, MLA_V), BF16)],
        compiler_params=_cparams("parallel"),
    )(z, qg, kvg, wq, wkv, cos, sin)


def _mla_flash_kernel(*refs, with_latent, sub):
    if with_latent:
        q_ref, kx_ref, vx_ref, k_ref, v_ref, o_ref, m_ref, l_ref, acc_ref = refs
    else:
        q_ref, kx_ref, vx_ref, o_ref, m_ref, l_ref, acc_ref = refs
    ki = pl.program_id(3)
    q = q_ref[0]

    def update(s, v, m_old, l_old, acc_old):
        cols = [s[:, c * 128:(c + 1) * 128] for c in range(s.shape[1] // 128)]
        mx = functools.reduce(jnp.maximum, cols)
        m_new = jnp.maximum(m_old, jnp.max(mx, axis=-1, keepdims=True))
        alpha = jnp.exp2(m_old - m_new)
        e = [jnp.exp2(c - m_new) for c in cols]
        p = jnp.concatenate([x.astype(BF16) for x in e], axis=-1)
        return m_new, alpha * l_old + functools.reduce(jnp.add, e), alpha * acc_old + _dot(p, v)

    @pl.when(ki == 0)
    def _():
        tq = q.shape[0]
        m, l, acc = update(_dot_nt(q, kx_ref[0]), vx_ref[0], jnp.full((tq, 128), NEG_INF, F32),
                           jnp.zeros((tq, 128), F32), jnp.zeros((tq, MLA_V), F32))
        m_ref[...], l_ref[...], acc_ref[...] = m, l, acc

    if with_latent:
        nsub = k_ref.shape[1] // sub
        m, l, acc = m_ref[...], l_ref[...], acc_ref[...]
        s_next = _dot_nt(q, k_ref[0, 0:sub, :])
        for j in range(nsub):
            s = s_next
            if j + 1 < nsub:
                s_next = _dot_nt(q, k_ref[0, (j + 1) * sub:(j + 2) * sub, :])
            m, l, acc = update(s, v_ref[0, j * sub:(j + 1) * sub, :], m, l, acc)
        m_ref[...], l_ref[...], acc_ref[...] = m, l, acc

    @pl.when(ki == pl.num_programs(3) - 1)
    def _():
        o_ref[...] = (acc_ref[...] / jnp.sum(l_ref[...], axis=-1, keepdims=True)).astype(o_ref.dtype)


def mla_attention(geom, q, k, v, *, with_latent, tq, tk):
    B, T, L = geom.B, geom.T, geom.L
    cblk = geom.BT // L
    ctx_k = pl.BlockSpec((1, L, MLA_QK_PAD), lambda b, h, qi, ki: (h, cblk + b, 0))
    ctx_v = pl.BlockSpec((1, L, MLA_V), lambda b, h, qi, ki: (h, cblk + b, 0))
    if with_latent:
        nq, nk = T // tq, T // tk
        in_specs = [pl.BlockSpec((1, tq, MLA_QK_PAD), lambda b, h, qi, ki: (h, b * nq + qi, 0)), ctx_k, ctx_v,
                    pl.BlockSpec((1, tk, MLA_QK_PAD), lambda b, h, qi, ki: (h, b * nk + ki, 0)),
                    pl.BlockSpec((1, tk, MLA_V), lambda b, h, qi, ki: (h, b * nk + ki, 0))]
        args = (q, k, v, k, v)
        out_rows, out_spec = geom.BT, pl.BlockSpec((tq, MLA_V), lambda b, h, qi, ki: (b * nq + qi, h))
    else:
        tq, nq, nk = L, 1, 1
        in_specs = [pl.BlockSpec((1, tq, MLA_QK_PAD), lambda b, h, qi, ki: (h, cblk + b, 0)), ctx_k, ctx_v]
        args = (q, k, v)
        out_rows, out_spec = B * L, pl.BlockSpec((tq, MLA_V), lambda b, h, qi, ki: (b, h))
    return pl.pallas_call(
        functools.partial(_mla_flash_kernel, with_latent=with_latent, sub=min(MLA_SUB_KEYS, tk)),
        grid=(B, MLA_HEADS, nq, nk),
        in_specs=in_specs,
        out_specs=out_spec,
        out_shape=jax.ShapeDtypeStruct((out_rows, MLA_HEADS * MLA_V), BF16),
        scratch_shapes=[pltpu.VMEM((tq, 128), F32), pltpu.VMEM((tq, 128), F32), pltpu.VMEM((tq, MLA_V), F32)],
        compiler_params=_cparams("parallel", "parallel", "parallel", "arbitrary"),
    )(*args)


RW_Z_R, RW_Z_K, RW_Z_V = 0, RW_WIDTH, 2 * RW_WIDTH
RW_Z_LORA = 3 * RW_WIDTH
RW_Z_COLS = 3 * RW_WIDTH + 5 * LORA_PAD


def _head_sum(x, ones):
    return jnp.concatenate(
        [_dot(x[:, c * 128:(c + 1) * 128], ones, HIGHEST) for c in range(x.shape[1] // 128)], axis=-1)


def _rw_prep_kernel(*refs, geom, tm, has_vres):
    if has_vres:
        (z_ref, zp_ref, zn_ref, mu_ref, w0_ref, w2_ref, a0_ref, a2_ref, g2_ref, kk_ref, ka_ref, rk_ref, ones_ref,
         vf_ref, v0_ref, v1_ref, v2_ref,
         r_o, v_o, kk_o, lwf_o, kf_o, bf_o, lwb_o, kb_o, bb_o, g_o, bonus_o) = refs
    else:
        (z_ref, zp_ref, zn_ref, mu_ref, w0_ref, w2_ref, a0_ref, a2_ref, g2_ref, kk_ref, ka_ref, rk_ref, ones_ref,
         r_o, v_o, kk_o, lwf_o, kf_o, bf_o, lwb_o, kb_o, bb_o, g_o, bonus_o) = refs
    first, last = _seq_edge_masks(geom, pl.program_id(0) * tm, tm)
    z = z_ref[...]
    up, dn = _shifted_rows(z, zp_ref[...], zn_ref[...], first, last)
    z = z + mu_ref[...] * (0.5 * (up + dn) - z)
    r = z[:, RW_Z_R:RW_Z_R + RW_WIDTH]
    k = z[:, RW_Z_K:RW_Z_K + RW_WIDTH]
    v = z[:, RW_Z_V:RW_Z_V + RW_WIDTH]
    lora = lambda n: z[:, RW_Z_LORA + n * LORA_PAD:RW_Z_LORA + (n + 1) * LORA_PAD]
    ones = ones_ref[...]
    if has_vres:
        mix = jax.nn.sigmoid(v0_ref[...] + _dot(_dot(v, v1_ref[...], HIGHEST), v2_ref[...], HIGHEST))
        v = v + (vf_ref[...] - v) * mix
    g_o[...] = _dot(jax.nn.sigmoid(lora(4)), g2_ref[...], HIGHEST)
    kk = k * kk_ref[...]
    kk = kk / jnp.maximum(jnp.sqrt(_head_sum(kk * kk, ones)), 1e-12)
    ksum = None
    for d, (lw_o, k_o, b_o) in enumerate(((lwf_o, kf_o, bf_o), (lwb_o, kb_o, bb_o))):
        x = -(w0_ref[d:d + 1, :] + _dot(jnp.tanh(lora(d)), w2_ref[d], HIGHEST))
        softplus = jnp.maximum(x, 0.0) + jnp.log1p(jnp.exp(-jnp.abs(x)))
        lw_o[...] = -jnp.exp(-softplus - 0.5)
        a = jax.nn.sigmoid(a0_ref[d:d + 1, :] + _dot(lora(2 + d), a2_ref[d], HIGHEST))
        kd = k * (1.0 + (a - 1.0) * ka_ref[...])
        k_o[...] = kd
        b_o[...] = kk * a
        ksum = kd if ksum is None else ksum + kd
    r_o[...] = r
    v_o[...] = v
    kk_o[...] = kk
    bonus_o[...] = _head_sum(r * ksum * rk_ref[...], ones) * v


def rw_prep(geom, z, p, v_first, vres, *, tm):
    R = z.shape[0]
    has_vres = vres is not None
    full = lambda a: pl.BlockSpec(a.shape, lambda i: (0,) * a.ndim)
    row = pl.BlockSpec((tm, RW_WIDTH), lambda i: (i, 0))
    prev, nxt = _halo_specs(tm, R, RW_Z_COLS, lambda i: 0)
    params = [p["mu"], p["w0"], p["w2"], p["a0"], p["a2"], p["g2"], p["kk"], p["ka"], p["rk"], p["ones"]]
    in_specs = [pl.BlockSpec((tm, RW_Z_COLS), lambda i: (i, 0)), prev, nxt] + [full(a) for a in params]
    args = [z, z, z] + params
    if has_vres:
        in_specs += [row] + [full(a) for a in vres]
        args += [v_first] + list(vres)
    return pl.pallas_call(
        functools.partial(_rw_prep_kernel, geom=geom, tm=tm, has_vres=has_vres),
        grid=(R // tm,),
        in_specs=in_specs,
        out_specs=[row] * 11,
        out_shape=[jax.ShapeDtypeStruct((R, RW_WIDTH), F32)] * 11,
        compiler_params=_cparams("parallel"),
    )(*args)


def _pieces(x, passes):
    hi = x.astype(BF16)
    if passes == 1:
        return (hi,)
    return hi, (x - hi.astype(F32)).astype(BF16)


def _mmx(a, b, dot):
    out = dot(a[0], b[0])
    if len(a) > 1:
        out = out + dot(a[1], b[0])
    if len(b) > 1:
        out = out + dot(a[0], b[1])
    return out


def _stack_pair(first_head, x):
    return jnp.concatenate([jnp.where(first_head, x, 0.0), jnp.where(first_head, 0.0, x)], axis=0)


def _fold_pair(x):
    half = x.shape[0] // 2
    return x[:half] + x[half:]


def _rw_chunk_kernel(r_ref, v_ref, kk_ref, lwf_ref, kf_ref, bf_ref, lwb_ref, kb_ref, bb_ref,
                     rrf_o, ylf_o, mf_o, nf_o, ef_o, rrb_o, ylb_o, mb_o, nb_o, eb_o):
    C = RW_CHUNK
    PW = RW_PAIR
    ri = lax.broadcasted_iota(jnp.int32, (PW, PW), 0)
    ci = lax.broadcasted_iota(jnp.int32, (PW, PW), 1)
    eye = ri == ci
    ri, ci = ri & (C - 1), ci & (C - 1)
    first_head = lax.broadcasted_iota(jnp.int32, (1, PW), 1) < RW_HEAD_DIM
    stack = functools.partial(_stack_pair, first_head)
    r, v, kk = r_ref[...], v_ref[...], kk_ref[...]
    jobs = []
    for lw_ref, k_ref, b_ref, outs, before, tot_row in (
            (lwf_ref, kf_ref, bf_ref, (rrf_o, ylf_o, mf_o, nf_o, ef_o), ci < ri, C - 1),
            (lwb_ref, kb_ref, bb_ref, (rrb_o, ylb_o, mb_o, nb_o, eb_o), ci > ri, 0)):
        lw, kd, bd = lw_ref[...], k_ref[...], b_ref[...]
        incl = (before | eye)[:C, :C].astype(BF16)
        c, rest = None, lw
        for _ in range(3):
            piece = rest.astype(BF16)
            rest = rest - piece.astype(F32)
            part = _dot(incl, piece)
            c = part if c is None else c + part
        c_tot = c[tot_row:tot_row + 1, :]
        at = -kk * jnp.exp(c - lw)
        rt = r * jnp.exp(c)
        e_neg = jnp.exp(-c)
        bt, kt = bd * e_neg, kd * e_neg
        e_rest = jnp.exp(c_tot - c)
        bc, kc = bd * e_rest, kd * e_rest
        outs[4][...] = jnp.broadcast_to(jnp.exp(c_tot), (8, RW_WIDTH))
        for p in range(RW_WIDTH // PW):
            ps = slice(p * PW, (p + 1) * PW)
            jobs.append(dict(ps=ps, outs=outs, before=before, incl=before | eye, rt_pair=rt[:, ps],
                             at=stack(at[:, ps]), rt=stack(rt[:, ps]), bt=stack(bt[:, ps]), kt=stack(kt[:, ps]),
                             bc=stack(bc[:, ps]), kc=stack(kc[:, ps]), v=stack(v[:, ps])))
    for j in jobs:
        p = _mmx(_pieces(jnp.concatenate([j["at"], j["rt"]], axis=0), RW_PASSES_LOCAL),
                 _pieces(jnp.concatenate([j["bt"], j["kt"]], axis=0), RW_PASSES_LOCAL), _dot_nt)
        j["a_ab"] = jnp.where(j["before"], p[:PW, :PW], 0.0)
        j["a_ak"] = jnp.where(j["before"], p[:PW, PW:], 0.0)
        j["a_rb"] = jnp.where(j["incl"], p[PW:, :PW], 0.0)
        j["a_rk"] = jnp.where(j["incl"], p[PW:, PW:], 0.0)
        j["vp"] = _pieces(j["v"], RW_PASSES_LOCAL)
    for j in jobs:
        j["w1"] = _mmx(_pieces(j["a_ak"], RW_PASSES_LOCAL), j["vp"], _dot)
        j["tinv"] = jnp.where(eye, 1.0, j["a_ab"])
        j["pw"] = j["a_ab"]
    for _ in range(int(np.log2(C)) - 1):
        for j in jobs:
            pw = _pieces(j["pw"], RW_PASSES_LOCAL)
            j["pw"] = _mmx(pw, pw, _dot)
        for j in jobs:
            j["tinv"] = j["tinv"] + _mmx(_pieces(j["tinv"], RW_PASSES_LOCAL), _pieces(j["pw"], RW_PASSES_LOCAL), _dot)
    for j in jobs:
        tw = _mmx(_pieces(j["tinv"], RW_PASSES_LOCAL),
                  _pieces(jnp.concatenate([j["at"], j["w1"]], axis=1), RW_PASSES_LOCAL), _dot)
        j["tw"] = tw
        j["twp"] = _pieces(tw, RW_PASSES_LOCAL)
    for j in jobs:
        rr_o, yl_o, m_o, n_o, _ = j["outs"]
        ps = j["ps"]
        ry = _mmx(_pieces(j["a_rb"], RW_PASSES_LOCAL), j["twp"], _dot)
        yk = _mmx(_pieces(j["a_rk"], RW_PASSES_LOCAL), j["vp"], _dot)
        rr_o[:, ps] = j["rt_pair"] + _fold_pair(ry[:, :PW])
        yl_o[:, ps] = _fold_pair(ry[:, PW:] + yk)
        bcp = _pieces(j["bc"], RW_PASSES_STATE)
        m_o[:, ps] = _fold_pair(_mmx(bcp, _pieces(j["tw"][:, :PW], RW_PASSES_STATE), _dot_tn))
        uv = jnp.concatenate([j["tw"][:, PW:], j["v"]], axis=0)
        bk = jnp.concatenate([j["bc"], j["kc"]], axis=0)
        n_o[:, ps] = _fold_pair(_mmx(_pieces(uv, RW_PASSES_STATE), _pieces(bk, RW_PASSES_STATE), _dot_tn))


def rw_chunk(arrs):
    R = arrs[0].shape[0]
    blk = pl.BlockSpec((RW_CHUNK, RW_WIDTH), lambda i: (i, 0))
    eblk = pl.BlockSpec((8, RW_WIDTH), lambda i: (i, 0))
    tok = jax.ShapeDtypeStruct((R, RW_WIDTH), F32)
    dec = jax.ShapeDtypeStruct((R // RW_CHUNK * 8, RW_WIDTH), F32)
    return pl.pallas_call(
        _rw_chunk_kernel,
        grid=(R // RW_CHUNK,),
        in_specs=[blk] * 9,
        out_specs=[blk] * 4 + [eblk] + [blk] * 4 + [eblk],
        out_shape=[tok] * 4 + [dec] + [tok] * 4 + [dec],
        compiler_params=_cparams("parallel"),
    )(*arrs)


def _rw_scan_kernel(rrf_ref, ylf_ref, mf_ref, nf_ref, ef_ref, rrb_ref, ylb_ref, mb_ref, nb_ref, eb_ref,
                    yf_o, yb_o, s_ref, *, nchunk):
    C = RW_CHUNK
    PW = RW_PAIR
    first_head = lax.broadcasted_iota(jnp.int32, (1, PW), 1) < RW_HEAD_DIM
    stack = functools.partial(_stack_pair, first_head)

    @pl.when(pl.program_id(1) == 0)
    def _():
        s_ref[...] = jnp.zeros_like(s_ref)

    dirs = ((rrf_ref, ylf_ref, mf_ref, nf_ref, ef_ref, yf_o), (rrb_ref, ylb_ref, mb_ref, nb_ref, eb_ref, yb_o))
    pairs = [slice(p * PW, (p + 1) * PW) for p in range(RW_WIDTH // PW)]
    state = [[s_ref[d, :, ps] for ps in pairs] for d in range(2)]
    for step in range(nchunk):
        for d, (rr_ref, yl_ref, m_ref, n_ref, e_ref, y_o) in enumerate(dirs):
            c = step if d == 0 else nchunk - 1 - step
            rows = slice(c * C, (c + 1) * C)
            for p, ps in enumerate(pairs):
                s = state[d][p]
                sp = _pieces(stack(s), RW_PASSES_STATE)
                y = _mmx(_pieces(stack(rr_ref[rows, ps]), RW_PASSES_STATE), sp, _dot_nt)
                y_o[rows, ps] = _fold_pair(y) + yl_ref[rows, ps]
                sm = _mmx(sp, _pieces(stack(m_ref[rows, ps]), RW_PASSES_STATE), _dot_nt)
                state[d][p] = s * e_ref[c * 8:c * 8 + 1, ps] + _fold_pair(sm) + n_ref[rows, ps]
    for d in range(2):
        for p, ps in enumerate(pairs):
            s_ref[d, :, ps] = state[d][p]


def rw_scan(geom, chunk_out):
    B, T, L = geom.B, geom.T, geom.L
    blk = L
    nchunk = blk // RW_CHUNK
    nlat = T // blk
    cblk = geom.BT // blk
    fwd_i = lambda b, s: (jnp.where(s == 0, cblk + b, b * nlat + s - 1), 0)
    bwd_i = lambda b, s: (jnp.where(s == 0, cblk + b, b * nlat + nlat - s), 0)
    tok = lambda f: pl.BlockSpec((blk, RW_WIDTH), f)
    dec = lambda f: pl.BlockSpec((8 * nchunk, RW_WIDTH), f)
    return pl.pallas_call(
        functools.partial(_rw_scan_kernel, nchunk=nchunk),
        grid=(B, nlat + 1),
        in_specs=[tok(fwd_i)] * 4 + [dec(fwd_i)] + [tok(bwd_i)] * 4 + [dec(bwd_i)],
        out_specs=[tok(fwd_i), tok(bwd_i)],
        out_shape=[jax.ShapeDtypeStruct((geom.R, RW_WIDTH), F32)] * 2,
        scratch_shapes=[pltpu.VMEM((2, RW_HEAD_DIM, RW_WIDTH), F32)],
        compiler_params=_cparams("parallel", "arbitrary"),
    )(*chunk_out)


def _rw_post_kernel(yf_ref, yb_ref, bonus_ref, g_ref, lng_ref, lnb_ref, ones_ref, o_ref):
    ones = ones_ref[...]
    y = yf_ref[...] + yb_ref[...]
    mean = _head_sum(y, ones) * (1.0 / RW_HEAD_DIM)
    yc = y - mean
    var = _head_sum(yc * yc, ones) * (1.0 / RW_HEAD_DIM)
    y = yc * lax.rsqrt(var + RW_LNX_EPS) * lng_ref[...] + lnb_ref[...]
    o_ref[...] = ((y + bonus_ref[...]) * g_ref[...]).astype(o_ref.dtype)


def rw_post(yf, yb, bonus, g, lnx_g, lnx_b, ones, *, tm):
    R = yf.shape[0]
    row = pl.BlockSpec((tm, RW_WIDTH), lambda i: (i, 0))
    full = lambda a: pl.BlockSpec(a.shape, lambda i: (0,) * a.ndim)
    return pl.pallas_call(
        _rw_post_kernel,
        grid=(R // tm,),
        in_specs=[row] * 4 + [full(lnx_g), full(lnx_b), full(ones)],
        out_specs=row,
        out_shape=jax.ShapeDtypeStruct((R, RW_WIDTH), BF16),
        compiler_params=_cparams("parallel"),
    )(yf, yb, bonus, g, lnx_g, lnx_b, ones)


def _rw_in_cols(w):
    parts = [w[..., :3 * RW_WIDTH]]
    off = 3 * RW_WIDTH
    for n in (RW_DECAY_LORA, RW_DECAY_LORA, RW_AAA_LORA, RW_AAA_LORA, RW_GATE_LORA):
        parts.append(_pad_cols(w[..., off:off + n], LORA_PAD))
        off += n
    return jnp.concatenate(parts, axis=-1)


def _pad_rows(w, n):
    return jnp.pad(w, [(0, 0)] * (w.ndim - 2) + [(0, n - w.shape[-2]), (0, 0)])


def _mla_wq_cols(w):
    w = w.reshape(w.shape[0], MLA_HEADS, MLA_NOPE + MLA_ROPE)
    return _pad_cols(w, MLA_QK_PAD).reshape(w.shape[0], MLA_HEADS * MLA_QK_PAD)


def kernel(x, c, ctx, c_ctx, ada_w, ada_b, norm1_g, w_in, rw_mu, rw_w0, rw_w2, rw_a0, rw_a2, rw_g2, rw_kk, rw_ka,
           rw_rk, rw_lnx_g, rw_lnx_b, rw_v0, rw_v1, rw_v2, wa_sink, mla_qnorm_g, mla_kvnorm_g, mla_w_uq, mla_w_ukv,
           w_branch, w_out, norm2_g, ffn_w_in, ffn_conv_w, ffn_conv_b, ffn_w_out, final_norm_g):
    B, T, D = x.shape
    L = ctx.shape[1]
    depth = w_in.shape[0]
    F = ffn_w_out.shape[1]
    geom = Geom(B, T, L)
    tm = _pick_tile(T, (512, 256, 128))
    assert (B * L) % tm == 0
    tmm = _pick_tile(geom.R, (1280, 1024, 640, 512, 256, 128))

    rw_cols = 3 * RW_WIDTH + 2 * RW_DECAY_LORA + 2 * RW_AAA_LORA + RW_GATE_LORA
    wa_cols = WA_WIDTH + 2 * WA_KV_WIDTH
    mla_cols = MLA_Q_LORA + MLA_KV_LORA + MLA_ROPE
    mla_cols_pad = MLA_Q_LORA + MLA_KV_LORA + 128

    cos_wa, sin_wa = _rope_tables(geom, WA_HEAD_DIM)
    cos_wa, sin_wa = jnp.tile(cos_wa, (1, 2)), jnp.tile(sin_wa, (1, 2))
    cos_m, sin_m = _rope_tables(geom, MLA_ROPE)
    one, zero = jnp.ones((geom.R, MLA_NOPE), F32), jnp.zeros((geom.R, MLA_NOPE), F32)
    cos_mla = jnp.concatenate([one, cos_m, one[:, :64]], axis=-1)
    sin_mla = jnp.concatenate([zero, sin_m, zero[:, :64]], axis=-1)
    lane = np.arange(128)
    ones_blk = jnp.asarray((lane[:, None] // RW_HEAD_DIM) == (lane[None, :] // RW_HEAD_DIM), F32)

    xs = jnp.concatenate([x.reshape(B * T, D), ctx.reshape(B * L, D)], axis=0)
    cvec = jnp.concatenate([c, c_ctx[None, :], jnp.zeros((8 - (B + 1) % 8, D), F32)], axis=0)
    v_first = None
    for l in range(depth):
        need_ctx = l < depth - 1
        mod = ada_modulation(cvec, ada_w[l], ada_b[l])
        mod = [mod[:, k * D:(k + 1) * D].reshape(-1, 1, D) for k in range(6)]

        w = w_in[l]
        w_rw = _pad_cols(_rw_in_cols(w[:, :rw_cols]), _round_up(RW_Z_COLS, 768)).astype(BF16)
        w_wa = w[:, rw_cols:rw_cols + wa_cols].astype(BF16)
        w_mla = _pad_cols(w[:, rw_cols + wa_cols:rw_cols + wa_cols + mla_cols], mla_cols_pad).astype(BF16)
        w_gate = w[:, rw_cols + wa_cols + mla_cols:].astype(BF16)
        nmm = functools.partial(norm_mod_matmul, geom, xs, norm1_g[l], mod[0], mod[1], tm=tmm)
        z_rw = nmm(w_rw, tn=768)
        z_wa = nmm(w_wa, tn=_pick_tile(wa_cols, (768, 512)))
        z_mla = nmm(w_mla, tn=mla_cols_pad)
        gates = nmm(w_gate, tn=_pick_tile(3 * D, (1024, 768, 512, 256, 128)), act="sigmoid", out_dtype=BF16)

        rw_p = dict(
            mu=_rw_in_cols(rw_mu[l][None, :]), w0=rw_w0[l], w2=_pad_rows(rw_w2[l], LORA_PAD), a0=rw_a0[l],
            a2=_pad_rows(rw_a2[l], LORA_PAD), g2=_pad_rows(rw_g2[l], LORA_PAD), kk=rw_kk[l][None, :],
            ka=rw_ka[l][None, :], rk=rw_rk[l].reshape(1, RW_WIDTH), ones=ones_blk)
        vres = None if l == 0 else (rw_v0[l - 1][None, :], rw_v1[l - 1], rw_v2[l - 1])
        r, v, kk, lwf, kf, bf, lwb, kb, bb, g, bonus = rw_prep(geom, z_rw, rw_p, v_first, vres, tm=min(tm, 256))
        if l == 0:
            v_first = v
        yf, yb = rw_scan(geom, rw_chunk((r, v, kk, lwf, kf, bf, lwb, kb, bb)))
        o_a = rw_post(yf, yb, bonus, g, rw_lnx_g[l][None, :], rw_lnx_b[l][None, :], ones_blk, tm=tm)

        q_wa, k_wa, v_wa = wa_prep(z_wa, cos_wa, sin_wa, tm=tm)
        ob_l = wa_attention(geom, q_wa, k_wa, v_wa, wa_sink[l], local=True)
        parts = [ob_l]
        if need_ctx:
            parts.append(wa_attention(geom, q_wa, k_wa, v_wa, wa_sink[l], local=False))
        else:
            parts.append(jnp.zeros((B * L, WA_WIDTH), BF16))
        o_b = jnp.concatenate(parts, axis=0)

        wq = _mla_wq_cols(mla_w_uq[l]).astype(BF16)
        q_m, k_m, v_m = mla_prep(z_mla, mla_qnorm_g[l], mla_kvnorm_g[l], wq, mla_w_ukv[l].astype(BF16),
                                 cos_mla, sin_mla, tm=tm)
        tq = _pick_tile(T, (1024, 512, 256, 128))
        tk = _pick_tile(T, (2048, 1024, 512, 256, 128))
        parts = [mla_attention(geom, q_m, k_m, v_m, with_latent=True, tq=tq, tk=tk)]
        if need_ctx:
            parts.append(mla_attention(geom, q_m, k_m, v_m, with_latent=False, tq=L, tk=L))
        else:
            parts.append(jnp.zeros((B * L, MLA_HEADS * MLA_V), BF16))
        o_c = jnp.concatenate(parts, axis=0)

        tn_d = _pick_tile(D, (1024, 512, 256, 128))
        y = merge_branches(o_a, o_b, o_c, gates, w_branch[l].astype(BF16), tm=tmm, tn=tn_d)
        xs = matmul_gated_residual(geom, y, w_out[l].astype(BF16), xs, mod[2], tm=tmm, tn=tn_d)

        tf = _pick_tile(F, (512, 256, 128))
        hmid = ffn_in_conv_glu(geom, xs, norm2_g[l], mod[3], mod[4], ffn_w_in[l].astype(BF16), ffn_conv_w[l],
                               ffn_conv_b[l], tm=tmm, tf=tf)
        xs = matmul_gated_residual(geom, hmid, ffn_w_out[l].astype(BF16), xs, mod[5], tm=tmm,
                                   tn=_pick_tile(D, (512, 256, 128)))

    out = final_rmsnorm(xs, final_norm_g, B * T, tm=tm)
    return out.reshape(B, T, D)
```

```python
import functools

import jax
import jax.numpy as jnp
import numpy as np
from jax import lax
from jax.experimental import pallas as pl
from jax.experimental.pallas import tpu as pltpu

F32 = jnp.float32
BF16 = jnp.bfloat16
HIGHEST = lax.Precision.HIGHEST

NORM_EPS = 1e-6
NEG_INF = -1e30
GRID_W = 64
ROPE_BASE = 10000.0

RW_HEADS = 16
RW_HEAD_DIM = 64
RW_WIDTH = RW_HEADS * RW_HEAD_DIM
RW_DECAY_LORA = 96
RW_AAA_LORA = 96
RW_GATE_LORA = 64
RW_LNX_EPS = 64e-5
RW_CHUNK = 64
RW_PAIR = 2 * RW_HEAD_DIM
RW_PASSES_LOCAL = 1
RW_PASSES_STATE = 3
LORA_PAD = 128

WA_HEADS = 16
WA_KV_HEADS = 4
WA_GROUP = WA_HEADS // WA_KV_HEADS
WA_HEAD_DIM = 64
WA_WIDTH = WA_HEADS * WA_HEAD_DIM
WA_KV_WIDTH = WA_KV_HEADS * WA_HEAD_DIM
WINDOW = 128
WA_SCALE = WA_HEAD_DIM ** -0.5

MLA_HEADS = 8
MLA_NOPE = 128
MLA_ROPE = 64
MLA_V = 128
MLA_Q_LORA = 512
MLA_KV_LORA = 512
MLA_QK_PAD = 256
MLA_V_PAD = 256
MLA_SCALE = (MLA_NOPE + MLA_ROPE) ** -0.5
MLA_SUB_KEYS = 512
LOG2_E = 1.4426950408889634

CONV_W = 3
VMEM_LIMIT_BYTES = 56 * 1024 * 1024


def _cparams(*sem):
    return pltpu.CompilerParams(dimension_semantics=sem, vmem_limit_bytes=VMEM_LIMIT_BYTES)


def _dot(a, b, precision=None):
    return jnp.dot(a, b, preferred_element_type=F32, precision=precision)


def _dot_nt(a, b, precision=None):
    return lax.dot_general(a, b, (((1,), (1,)), ((), ())), preferred_element_type=F32, precision=precision)


def _dot_tn(a, b, precision=None):
    return lax.dot_general(a, b, (((0,), (0,)), ((), ())), preferred_element_type=F32, precision=precision)


def _pick_tile(n, candidates):
    for c in candidates:
        if n % c == 0:
            return c
    raise ValueError(f"no tile in {candidates} divides {n}")


def _pad_cols(w, n):
    return jnp.pad(w, [(0, 0)] * (w.ndim - 1) + [(0, n - w.shape[-1])])


def _round_up(n, m):
    return (n + m - 1) // m * m


class Geom:
    def __init__(self, B, T, L):
        assert T & (T - 1) == 0 and L & (L - 1) == 0, "sequence lengths must be powers of two"
        assert T % L == 0 and L % RW_CHUNK == 0 and T % GRID_W == 0
        self.B, self.T, self.L = B, T, L
        self.BT = B * T
        self.R = B * T + B * L


def _select_row_group(geom, row0, tm, tab_ref):
    r = row0 + lax.broadcasted_iota(jnp.int32, (tm, 1), 0)
    out = tab_ref[geom.B]
    for b in range(geom.B):
        out = jnp.where((r >= b * geom.T) & (r < (b + 1) * geom.T), tab_ref[b], out)
    return out


def _seq_edge_masks(geom, row0, tm):
    r = row0 + lax.broadcasted_iota(jnp.int32, (tm, 1), 0)
    is_lat = r < geom.BT
    pos = jnp.where(is_lat, r & (geom.T - 1), (r - geom.BT) & (geom.L - 1))
    last = jnp.where(is_lat, geom.T - 1, geom.L - 1)
    return pos == 0, pos == last


def _shifted_rows(x, prev8, next8, first, last):
    tm = x.shape[0]
    rid = lax.broadcasted_iota(jnp.int32, (tm, 1), 0)
    up = jnp.where(rid == 0, prev8[7:8, :], pltpu.roll(x, 1, axis=0))
    dn = jnp.where(rid == tm - 1, next8[0:1, :], pltpu.roll(x, tm - 1, axis=0))
    return jnp.where(first, 0.0, up), jnp.where(last, 0.0, dn)


def _halo_specs(tm, R, width, col_of):
    nb8 = tm // 8
    prev = pl.BlockSpec((8, width), lambda i, *a: (jnp.maximum(i * nb8 - 1, 0), col_of(i, *a)))
    nxt = pl.BlockSpec((8, width), lambda i, *a: (jnp.minimum((i + 1) * nb8, R // 8 - 1), col_of(i, *a)))
    return prev, nxt


def _ada_kernel(c_ref, w_ref, b_ref, o_ref):
    c = c_ref[...]
    o_ref[...] = _dot(c * jax.nn.sigmoid(c), w_ref[...], HIGHEST) + b_ref[...]


def ada_modulation(cvec, w, b):
    G, D = cvec.shape
    N = w.shape[1]
    tn = _pick_tile(N, (1024, 512, 256, 128))
    return pl.pallas_call(
        _ada_kernel,
        grid=(N // tn,),
        in_specs=[pl.BlockSpec((G, D), lambda j: (0, 0)),
                  pl.BlockSpec((D, tn), lambda j: (0, j)),
                  pl.BlockSpec((1, tn), lambda j: (0, j))],
        out_specs=pl.BlockSpec((G, tn), lambda j: (0, j)),
        out_shape=jax.ShapeDtypeStruct((G, N), F32),
        compiler_params=_cparams("arbitrary"),
    )(cvec, w, b.reshape(1, N))


def _norm_modulate(x, g, sc, sh):
    n = x * lax.rsqrt(jnp.mean(x * x, axis=-1, keepdims=True) + NORM_EPS) * g
    return (n * (1.0 + sc) + sh).astype(BF16)


PROLOGUE_ROWS = 256


def _norm_modulate_rows(geom, row0, x_ref, g_ref, sc_ref, sh_ref, h_ref, h_off):
    tm = x_ref.shape[0]
    step = PROLOGUE_ROWS if tm % PROLOGUE_ROWS == 0 else tm
    one_group = geom.T % step == 0 and geom.BT % step == 0
    g = g_ref[...]
    for r in range(0, tm, step):
        if one_group:
            sc, sh = sc_ref[geom.B], sh_ref[geom.B]
            for b in range(geom.B):
                in_b = (row0 + r >= b * geom.T) & (row0 + r < (b + 1) * geom.T)
                sc, sh = jnp.where(in_b, sc_ref[b], sc), jnp.where(in_b, sh_ref[b], sh)
        else:
            sc = _select_row_group(geom, row0 + r, step, sc_ref)
            sh = _select_row_group(geom, row0 + r, step, sh_ref)
        h_ref[h_off + r:h_off + r + step] = _norm_modulate(x_ref[r:r + step], g, sc, sh)


def _nmm_kernel(x_ref, g_ref, sh_ref, sc_ref, w_ref, o_ref, h_ref, *, geom, tm, act):
    @pl.when(pl.program_id(1) == 0)
    def _():
        _norm_modulate_rows(geom, pl.program_id(0) * tm, x_ref, g_ref, sc_ref, sh_ref, h_ref, 0)

    acc = _dot(h_ref[...], w_ref[...])
    if act == "sigmoid":
        acc = jax.nn.sigmoid(acc)
    o_ref[...] = acc.astype(o_ref.dtype)


def norm_mod_matmul(geom, x, g, shift, scale, w, *, tm, tn, act=None, out_dtype=F32):
    R, K = x.shape
    N = w.shape[1]
    assert R % tm == 0 and N % tn == 0
    full = lambda a: pl.BlockSpec(a.shape, lambda i, j: (0,) * a.ndim)
    return pl.pallas_call(
        functools.partial(_nmm_kernel, geom=geom, tm=tm, act=act),
        grid=(R // tm, N // tn),
        in_specs=[pl.BlockSpec((tm, K), lambda i, j: (i, 0), pipeline_mode=pl.Buffered(1)),
                  pl.BlockSpec((1, K), lambda i, j: (0, 0)),
                  full(shift), full(scale),
                  pl.BlockSpec((K, tn), lambda i, j: (0, j))],
        out_specs=pl.BlockSpec((tm, tn), lambda i, j: (i, j)),
        out_shape=jax.ShapeDtypeStruct((R, N), out_dtype),
        scratch_shapes=[pltpu.VMEM((tm, K), BF16)],
        compiler_params=_cparams("parallel", "arbitrary"),
    )(x, g.reshape(1, K), shift, scale, w)


def _mm_resid_kernel(y_ref, w_ref, r_ref, gate_ref, o_ref, *, geom, tm):
    gate = _select_row_group(geom, pl.program_id(0) * tm, tm, gate_ref)
    o_ref[...] = r_ref[...] + gate * _dot(y_ref[...], w_ref[...])


def matmul_gated_residual(geom, y, w, resid, gate, *, tm, tn):
    R, K = y.shape
    N = w.shape[1]
    G = gate.shape[0]
    assert R % tm == 0 and N % tn == 0
    return pl.pallas_call(
        functools.partial(_mm_resid_kernel, geom=geom, tm=tm),
        grid=(R // tm, N // tn),
        in_specs=[pl.BlockSpec((tm, K), lambda i, j: (i, 0)),
                  pl.BlockSpec((K, tn), lambda i, j: (0, j)),
                  pl.BlockSpec((tm, tn), lambda i, j: (i, j)),
                  pl.BlockSpec((G, 1, tn), lambda i, j: (0, 0, j))],
        out_specs=pl.BlockSpec((tm, tn), lambda i, j: (i, j)),
        out_shape=jax.ShapeDtypeStruct((R, N), F32),
        compiler_params=_cparams("parallel", "arbitrary"),
    )(y, w, resid, gate)


FFN_HALO = 16


def _ffn_in_kernel(x_ref, xp_ref, xn_ref, g_ref, sh_ref, sc_ref, wg_ref, wu_ref, cw_ref, cb_ref, o_ref, h_ref,
                   *, geom, tm):
    row0 = pl.program_id(0) * tm
    H = FFN_HALO

    @pl.when(pl.program_id(1) == 0)
    def _():
        _norm_modulate_rows(geom, row0, xp_ref, g_ref, sc_ref, sh_ref, h_ref, 0)
        _norm_modulate_rows(geom, row0, x_ref, g_ref, sc_ref, sh_ref, h_ref, H)
        _norm_modulate_rows(geom, row0 + tm - H, xn_ref, g_ref, sc_ref, sh_ref, h_ref, H + tm)

    first, last = _seq_edge_masks(geom, row0, tm)
    gt = _dot(h_ref[...], wg_ref[...])
    u = _dot(h_ref[H:H + tm], wu_ref[...])
    up = jnp.where(first, 0.0, pltpu.roll(gt, 1, axis=0)[H:H + tm])
    dn = jnp.where(last, 0.0, pltpu.roll(gt, tm + 2 * H - 1, axis=0)[H:H + tm])
    cw = cw_ref[...]
    conv = cb_ref[...] + up * cw[0:1, :]
    conv = conv + gt[H:H + tm] * cw[1:2, :]
    conv = conv + dn * cw[2:3, :]
    o_ref[...] = (jax.nn.gelu(conv, approximate=True) * u).astype(o_ref.dtype)


def ffn_in_conv_glu(geom, x, g, shift, scale, w_in, conv_w, conv_b, *, tm, tf):
    R, K = x.shape
    F = w_in.shape[1] // 2
    nj = F // tf
    nbh = tm // FFN_HALO
    assert R % tm == 0 and F % tf == 0 and tm % FFN_HALO == 0
    full = lambda a: pl.BlockSpec(a.shape, lambda i, j: (0,) * a.ndim)
    return pl.pallas_call(
        functools.partial(_ffn_in_kernel, geom=geom, tm=tm),
        grid=(R // tm, nj),
        in_specs=[pl.BlockSpec((tm, K), lambda i, j: (i, 0), pipeline_mode=pl.Buffered(1)),
                  pl.BlockSpec((FFN_HALO, K), lambda i, j: (jnp.maximum(i * nbh - 1, 0), 0)),
                  pl.BlockSpec((FFN_HALO, K), lambda i, j: (jnp.minimum((i + 1) * nbh, R // FFN_HALO - 1), 0)),
                  pl.BlockSpec((1, K), lambda i, j: (0, 0)),
                  full(shift), full(scale),
                  pl.BlockSpec((K, tf), lambda i, j: (0, j)),
                  pl.BlockSpec((K, tf), lambda i, j: (0, nj + j)),
                  pl.BlockSpec((CONV_W, tf), lambda i, j: (0, j)),
                  pl.BlockSpec((1, tf), lambda i, j: (0, j))],
        out_specs=pl.BlockSpec((tm, tf), lambda i, j: (i, j)),
        out_shape=jax.ShapeDtypeStruct((R, F), BF16),
        scratch_shapes=[pltpu.VMEM((tm + 2 * FFN_HALO, K), BF16)],
        compiler_params=_cparams("parallel", "arbitrary"),
    )(x, x, x, g.reshape(1, K), shift, scale, w_in, w_in, conv_w, conv_b.reshape(1, F))


def _merge_kernel(oa_ref, ob_ref, oc_ref, ga_ref, gb_ref, gc_ref, w_ref, o_ref):
    y = ga_ref[...] * _dot(oa_ref[...], w_ref[0])
    y = y + gb_ref[...] * _dot(ob_ref[...], w_ref[1])
    y = y + gc_ref[...] * _dot(oc_ref[...], w_ref[2])
    o_ref[...] = y.astype(o_ref.dtype)


def merge_branches(oa, ob, oc, gates, wb, *, tm, tn):
    R, K = oa.shape
    D = wb.shape[2]
    nj = D // tn
    bspec = pl.BlockSpec((tm, K), lambda i, j: (i, 0))
    gspec = lambda k: pl.BlockSpec((tm, tn), lambda i, j: (i, k * nj + j))
    return pl.pallas_call(
        _merge_kernel,
        grid=(R // tm, nj),
        in_specs=[bspec, bspec, bspec, gspec(0), gspec(1), gspec(2),
                  pl.BlockSpec((3, K, tn), lambda i, j: (0, 0, j))],
        out_specs=pl.BlockSpec((tm, tn), lambda i, j: (i, j)),
        out_shape=jax.ShapeDtypeStruct((R, D), BF16),
        compiler_params=_cparams("parallel", "arbitrary"),
    )(oa, ob, oc, gates, gates, gates, wb)


def _rmsnorm_kernel(x_ref, g_ref, o_ref):
    x = x_ref[...]
    o_ref[...] = x * lax.rsqrt(jnp.mean(x * x, axis=-1, keepdims=True) + NORM_EPS) * g_ref[...]


def final_rmsnorm(x, g, rows, *, tm):
    D = x.shape[1]
    return pl.pallas_call(
        _rmsnorm_kernel,
        grid=(rows // tm,),
        in_specs=[pl.BlockSpec((tm, D), lambda i: (i, 0)), pl.BlockSpec((1, D), lambda i: (0, 0))],
        out_specs=pl.BlockSpec((tm, D), lambda i: (i, 0)),
        out_shape=jax.ShapeDtypeStruct((rows, D), F32),
        compiler_params=_cparams("parallel"),
    )(x, g.reshape(1, D))


def _rot_half64(z):
    n = z.shape[-1]
    lane = lax.broadcasted_iota(jnp.int32, z.shape, z.ndim - 1)
    return jnp.where((lane & 63) < 32, pltpu.roll(z, n - 32, axis=z.ndim - 1), pltpu.roll(z, 32, axis=z.ndim - 1))


def _rope_tables(geom, dim):
    nf = dim // 4
    inv = ROPE_BASE ** (-jnp.arange(nf, dtype=F32) / nf)
    rows = geom.T // GRID_W
    row = jnp.repeat(jnp.arange(rows, dtype=F32), GRID_W)
    col = jnp.tile(jnp.arange(GRID_W, dtype=F32), rows)
    ang = jnp.concatenate([row[:, None] * inv, col[:, None] * inv], axis=-1)
    cos, sin = jnp.cos(ang), jnp.sin(ang)
    cos_t = jnp.concatenate([cos, cos], axis=-1)
    sin_t = jnp.concatenate([-sin, sin], axis=-1)
    nctx = geom.B * geom.L
    cos_f = jnp.concatenate([jnp.tile(cos_t, (geom.B, 1)), jnp.ones((nctx, dim), F32)], axis=0)
    sin_f = jnp.concatenate([jnp.tile(sin_t, (geom.B, 1)), jnp.zeros((nctx, dim), F32)], axis=0)
    return cos_f, sin_f


def _wa_prep_kernel(z_ref, cos_ref, sin_ref, q_ref, k_ref, v_ref):
    cos = cos_ref[...]
    sin = sin_ref[...]
    for c in range(WA_WIDTH // 128):
        z = z_ref[:, c * 128:(c + 1) * 128]
        q_ref[:, c * 128:(c + 1) * 128] = ((z * cos + _rot_half64(z) * sin) * WA_SCALE).astype(BF16)
    for c in range(WA_KV_WIDTH // 128):
        z = z_ref[:, WA_WIDTH + c * 128:WA_WIDTH + (c + 1) * 128]
        k_ref[:, c * 128:(c + 1) * 128] = (z * cos + _rot_half64(z) * sin).astype(BF16)
    v_ref[...] = z_ref[:, WA_WIDTH + WA_KV_WIDTH:].astype(BF16)


def wa_prep(z, cos, sin, *, tm):
    R = z.shape[0]
    row = lambda w: pl.BlockSpec((tm, w), lambda i: (i, 0))
    return pl.pallas_call(
        _wa_prep_kernel,
        grid=(R // tm,),
        in_specs=[row(WA_WIDTH + 2 * WA_KV_WIDTH), row(128), row(128)],
        out_specs=[row(WA_WIDTH), row(WA_KV_WIDTH), row(WA_KV_WIDTH)],
        out_shape=[jax.ShapeDtypeStruct((R, WA_WIDTH), BF16),
                   jax.ShapeDtypeStruct((R, WA_KV_WIDTH), BF16),
                   jax.ShapeDtypeStruct((R, WA_KV_WIDTH), BF16)],
        compiler_params=_cparams("parallel"),
    )(z, cos, sin)


def _wa_attn_kernel(*refs, local, nqb, tq):
    if local:
        sink_ref, q_ref, kp_ref, kc_ref, kn_ref, vp_ref, vc_ref, vn_ref, kx_ref, vx_ref, o_ref = refs
    else:
        sink_ref, q_ref, kx_ref, vx_ref, o_ref = refs
    i = pl.program_id(1)
    nk_ctx = kx_ref.shape[0]
    rows = WA_GROUP * tq
    qpos = lax.broadcasted_iota(jnp.int32, (rows, 1), 0) & (tq - 1)
    head_in_group = lax.broadcasted_iota(jnp.int32, (rows, 1), 0) >> int(np.log2(tq))
    if local:
        off_prev = jnp.where(i > 0, 0, tq)
        off_next = jnp.where(i < nqb - 1, 0, tq)
        j = lax.broadcasted_iota(jnp.int32, (1, 3 * tq + nk_ctx), 1)
        valid = ((j >= tq) & (j < 2 * tq)) | (j >= 3 * tq)
        valid = valid | ((j < tq) & (j >= qpos + off_prev))
        valid = valid | ((j >= 2 * tq) & (j < 3 * tq) & ((j - 2 * tq) <= qpos - off_next))
    for g in range(WA_KV_HEADS):
        ks = slice(g * WA_HEAD_DIM, (g + 1) * WA_HEAD_DIM)
        qg = jnp.concatenate(
            [q_ref[:, (g * WA_GROUP + a) * WA_HEAD_DIM:(g * WA_GROUP + a + 1) * WA_HEAD_DIM] for a in range(WA_GROUP)],
            axis=0)
        sink = jnp.zeros((rows, 1), F32)
        for a in range(WA_GROUP):
            sink = jnp.where(head_in_group == a, sink_ref[g * WA_GROUP + a], sink)
        if local:
            kcat = jnp.concatenate([kp_ref[:, ks], kc_ref[:, ks], kn_ref[:, ks], kx_ref[:, ks]], axis=0)
            vcat = jnp.concatenate([vp_ref[:, ks], vc_ref[:, ks], vn_ref[:, ks], vx_ref[:, ks]], axis=0)
        else:
            kcat, vcat = kx_ref[:, ks], vx_ref[:, ks]
        s = _dot_nt(qg, kcat)
        if local:
            s = jnp.where(valid, s, NEG_INF)
        m = jnp.maximum(jnp.max(s, axis=-1, keepdims=True), sink)
        e = jnp.exp(s - m)
        denom = jnp.sum(e, axis=-1, keepdims=True) + jnp.exp(sink - m)
        o = _dot((e / denom).astype(BF16), vcat)
        for a in range(WA_GROUP):
            h = g * WA_GROUP + a
            o_ref[:, h * WA_HEAD_DIM:(h + 1) * WA_HEAD_DIM] = o[a * tq:(a + 1) * tq].astype(o_ref.dtype)


def wa_attention(geom, q, k, v, sink, *, local):
    B, T, L = geom.B, geom.T, geom.L
    sink_spec = pl.BlockSpec(memory_space=pltpu.SMEM)
    ctx_spec = pl.BlockSpec((L, WA_KV_WIDTH), lambda b, i: (geom.BT // L + b, 0))
    if local:
        tq = WINDOW
        nqb = T // tq
        kv = lambda f: pl.BlockSpec((tq, WA_KV_WIDTH), lambda b, i: (b * nqb + f(i), 0))
        prev = lambda i: jnp.maximum(i - 1, 0)
        cur = lambda i: i
        nxt = lambda i: jnp.minimum(i + 1, nqb - 1)
        in_specs = [sink_spec, pl.BlockSpec((tq, WA_WIDTH), lambda b, i: (b * nqb + i, 0)),
                    kv(prev), kv(cur), kv(nxt), kv(prev), kv(cur), kv(nxt), ctx_spec, ctx_spec]
        args = (sink, q, k, k, k, v, v, v, k, v)
        out_rows, out_spec = geom.BT, pl.BlockSpec((tq, WA_WIDTH), lambda b, i: (b * nqb + i, 0))
    else:
        tq, nqb = L, 1
        in_specs = [sink_spec, pl.BlockSpec((tq, WA_WIDTH), lambda b, i: (geom.BT // L + b, 0)), ctx_spec, ctx_spec]
        args = (sink, q, k, v)
        out_rows, out_spec = B * L, pl.BlockSpec((tq, WA_WIDTH), lambda b, i: (b, 0))
    return pl.pallas_call(
        functools.partial(_wa_attn_kernel, local=local, nqb=nqb, tq=tq),
        grid=(B, nqb),
        in_specs=in_specs,
        out_specs=out_spec,
        out_shape=jax.ShapeDtypeStruct((out_rows, WA_WIDTH), BF16),
        compiler_params=_cparams("parallel", "arbitrary"),
    )(*args)


def _mla_prep_kernel(z_ref, qg_ref, kvg_ref, wq_ref, wkv_ref, cos_ref, sin_ref, q_ref, k_ref, v_ref):
    def norm(x, g):
        return (x * lax.rsqrt(jnp.mean(x * x, axis=-1, keepdims=True) + NORM_EPS) * g).astype(BF16)

    cos = cos_ref[...]
    sin = sin_ref[...]
    q = _dot(norm(z_ref[:, :MLA_Q_LORA], qg_ref[...]), wq_ref[...])
    kv = _dot(norm(z_ref[:, MLA_Q_LORA:MLA_Q_LORA + MLA_KV_LORA], kvg_ref[...]), wkv_ref[...])
    kr = z_ref[:, MLA_Q_LORA + MLA_KV_LORA:MLA_Q_LORA + MLA_KV_LORA + 128]
    kr = (kr * cos[:, 128:] + _rot_half64(kr) * sin[:, 128:]).astype(BF16)
    for h in range(MLA_HEADS):
        qh = q[:, h * MLA_QK_PAD:(h + 1) * MLA_QK_PAD]
        q_ref[h] = ((qh * cos + _rot_half64(qh) * sin) * (MLA_SCALE * LOG2_E)).astype(BF16)
        k_ref[h, :, :MLA_NOPE] = kv[:, h * 256:h * 256 + MLA_NOPE].astype(BF16)
        k_ref[h, :, MLA_NOPE:] = kr
        v_ref[h, :, :MLA_V] = kv[:, h * 256 + MLA_NOPE:(h + 1) * 256].astype(BF16)
        v_ref[h, :, MLA_V:] = jnp.ones((kv.shape[0], MLA_V_PAD - MLA_V), BF16)


def mla_prep(z, qnorm_g, kvnorm_g, wq, wkv, cos, sin, *, tm):
    R, Z = z.shape
    full = lambda a: pl.BlockSpec(a.shape, lambda i: (0,) * a.ndim)
    qg, kvg = qnorm_g.reshape(1, -1), kvnorm_g.reshape(1, -1)
    hd = lambda w: pl.BlockSpec((MLA_HEADS, tm, w), lambda i: (0, i, 0))
    return pl.pallas_call(
        _mla_prep_kernel,
        grid=(R // tm,),
        in_specs=[pl.BlockSpec((tm, Z), lambda i: (i, 0)), full(qg), full(kvg), full(wq), full(wkv),
                  pl.BlockSpec((tm, MLA_QK_PAD), lambda i: (i, 0)), pl.BlockSpec((tm, MLA_QK_PAD), lambda i: (i, 0))],
        out_specs=[hd(MLA_QK_PAD), hd(MLA_QK_PAD), hd(MLA_V_PAD)],
        out_shape=[jax.ShapeDtypeStruct((MLA_HEADS, R, MLA_QK_PAD), BF16),
                   jax.ShapeDtypeStruct((MLA_HEADS, R, MLA_QK_PAD), BF16),
                   jax.ShapeDtypeStruct((MLA_HEADS, R, MLA_V_PAD), BF16)],
        compiler_params=_cparams("parallel"),
    )(z, qg, kvg, wq, wkv, cos, sin)


def _mla_flash_kernel(*refs, with_latent, sub):
    if with_latent:
        q_ref, kx_ref, vx_ref, k_ref, v_ref, o_ref, m_ref, acc_ref = refs
    else:
        q_ref, kx_ref, vx_ref, o_ref, m_ref, acc_ref = refs
    ki = pl.program_id(3)
    q = q_ref[0]

    def update(s, v, m_old, acc_old):
        cols = [s[:, c * 128:(c + 1) * 128] for c in range(s.shape[1] // 128)]
        mx = functools.reduce(jnp.maximum, cols)
        m_new = jnp.maximum(m_old, jnp.max(mx, axis=-1, keepdims=True))
        alpha = jnp.exp2(m_old - m_new)
        p = jnp.concatenate([jnp.exp2(c - m_new).astype(BF16) for c in cols], axis=-1)
        return m_new, jnp.concatenate([alpha, alpha], axis=-1) * acc_old + _dot(p, v)

    @pl.when(ki == 0)
    def _():
        tq = q.shape[0]
        m, acc = update(_dot_nt(q, kx_ref[0]), vx_ref[0], jnp.full((tq, 128), NEG_INF, F32),
                        jnp.zeros((tq, MLA_V_PAD), F32))
        m_ref[...], acc_ref[...] = m, acc

    if with_latent:
        nsub = k_ref.shape[1] // sub
        m, acc = m_ref[...], acc_ref[...]
        s_next = _dot_nt(q, k_ref[0, 0:sub, :])
        for j in range(nsub):
            s = s_next
            if j + 1 < nsub:
                s_next = _dot_nt(q, k_ref[0, (j + 1) * sub:(j + 2) * sub, :])
            m, acc = update(s, v_ref[0, j * sub:(j + 1) * sub, :], m, acc)
        m_ref[...], acc_ref[...] = m, acc

    @pl.when(ki == pl.num_programs(3) - 1)
    def _():
        acc = acc_ref[...]
        o_ref[...] = (acc[:, :MLA_V] / acc[:, MLA_V:]).astype(o_ref.dtype)


def mla_attention(geom, q, k, v, *, with_latent, tq, tk):
    B, T, L = geom.B, geom.T, geom.L
    cblk = geom.BT // L
    ctx_k = pl.BlockSpec((1, L, MLA_QK_PAD), lambda b, h, qi, ki: (h, cblk + b, 0))
    ctx_v = pl.BlockSpec((1, L, MLA_V_PAD), lambda b, h, qi, ki: (h, cblk + b, 0))
    if with_latent:
        nq, nk = T // tq, T // tk
        in_specs = [pl.BlockSpec((1, tq, MLA_QK_PAD), lambda b, h, qi, ki: (h, b * nq + qi, 0)), ctx_k, ctx_v,
                    pl.BlockSpec((1, tk, MLA_QK_PAD), lambda b, h, qi, ki: (h, b * nk + ki, 0)),
                    pl.BlockSpec((1, tk, MLA_V_PAD), lambda b, h, qi, ki: (h, b * nk + ki, 0))]
        args = (q, k, v, k, v)
        out_rows, out_spec = geom.BT, pl.BlockSpec((tq, MLA_V), lambda b, h, qi, ki: (b * nq + qi, h))
    else:
        tq, nq, nk = L, 1, 1
        in_specs = [pl.BlockSpec((1, tq, MLA_QK_PAD), lambda b, h, qi, ki: (h, cblk + b, 0)), ctx_k, ctx_v]
        args = (q, k, v)
        out_rows, out_spec = B * L, pl.BlockSpec((tq, MLA_V), lambda b, h, qi, ki: (b, h))
    return pl.pallas_call(
        functools.partial(_mla_flash_kernel, with_latent=with_latent, sub=min(MLA_SUB_KEYS, tk)),
        grid=(B, MLA_HEADS, nq, nk),
        in_specs=in_specs,
        out_specs=out_spec,
        out_shape=jax.ShapeDtypeStruct((out_rows, MLA_HEADS * MLA_V), BF16),
        scratch_shapes=[pltpu.VMEM((tq, 128), F32), pltpu.VMEM((tq, MLA_V_PAD), F32)],
        compiler_params=_cparams("parallel", "parallel", "parallel", "arbitrary"),
    )(*args)


RW_Z_R, RW_Z_K, RW_Z_V = 0, RW_WIDTH, 2 * RW_WIDTH
RW_Z_LORA = 3 * RW_WIDTH
RW_Z_COLS = 3 * RW_WIDTH + 5 * LORA_PAD


def _dot3(a, b):
    return _mmx(_pieces(a, 3), _pieces(b, 3), _dot)


def _head_sum(x, ones):
    ones = (ones.astype(BF16),)
    return jnp.concatenate(
        [_mmx(_pieces(x[:, c * 128:(c + 1) * 128], 3), ones, _dot) for c in range(x.shape[1] // 128)], axis=-1)


def _rw_prep_kernel(*refs, geom, tm, has_vres):
    if has_vres:
        (z_ref, zp_ref, zn_ref, mu_ref, w0_ref, w2_ref, a0_ref, a2_ref, g2_ref, kk_ref, ka_ref, rk_ref, ones_ref,
         vf_ref, v0_ref, v1_ref, v2_ref,
         r_o, v_o, kk_o, lwf_o, kf_o, bf_o, lwb_o, kb_o, bb_o, g_o, bonus_o) = refs
    else:
        (z_ref, zp_ref, zn_ref, mu_ref, w0_ref, w2_ref, a0_ref, a2_ref, g2_ref, kk_ref, ka_ref, rk_ref, ones_ref,
         r_o, v_o, kk_o, lwf_o, kf_o, bf_o, lwb_o, kb_o, bb_o, g_o, bonus_o) = refs
    first, last = _seq_edge_masks(geom, pl.program_id(0) * tm, tm)
    z = z_ref[...]
    up, dn = _shifted_rows(z, zp_ref[...], zn_ref[...], first, last)
    z = z + mu_ref[...] * (0.5 * (up + dn) - z)
    r = z[:, RW_Z_R:RW_Z_R + RW_WIDTH]
    k = z[:, RW_Z_K:RW_Z_K + RW_WIDTH]
    v = z[:, RW_Z_V:RW_Z_V + RW_WIDTH]
    lora = lambda n: z[:, RW_Z_LORA + n * LORA_PAD:RW_Z_LORA + (n + 1) * LORA_PAD]
    ones = ones_ref[...]
    if has_vres:
        mix = jax.nn.sigmoid(v0_ref[...] + _dot3(_dot3(v, v1_ref[...]), v2_ref[...]))
        v = v + (vf_ref[...] - v) * mix
    g_o[...] = _dot3(jax.nn.sigmoid(lora(4)), g2_ref[...])
    kk = k * kk_ref[...]
    kk = kk / jnp.maximum(jnp.sqrt(_head_sum(kk * kk, ones)), 1e-12)
    ksum = None
    for d, (lw_o, k_o, b_o) in enumerate(((lwf_o, kf_o, bf_o), (lwb_o, kb_o, bb_o))):
        x = -(w0_ref[d:d + 1, :] + _dot3(jnp.tanh(lora(d)), w2_ref[d]))
        softplus = jnp.maximum(x, 0.0) + jnp.log1p(jnp.exp(-jnp.abs(x)))
        lw_o[...] = -jnp.exp(-softplus - 0.5)
        a = jax.nn.sigmoid(a0_ref[d:d + 1, :] + _dot3(lora(2 + d), a2_ref[d]))
        kd = k * (1.0 + (a - 1.0) * ka_ref[...])
        k_o[...] = kd
        b_o[...] = kk * a
        ksum = kd if ksum is None else ksum + kd
    r_o[...] = r
    v_o[...] = v
    kk_o[...] = kk
    bonus_o[...] = _head_sum(r * ksum * rk_ref[...], ones) * v


def rw_prep(geom, z, p, v_first, vres, *, tm):
    R = z.shape[0]
    has_vres = vres is not None
    full = lambda a: pl.BlockSpec(a.shape, lambda i: (0,) * a.ndim)
    row = pl.BlockSpec((tm, RW_WIDTH), lambda i: (i, 0))
    prev, nxt = _halo_specs(tm, R, RW_Z_COLS, lambda i: 0)
    params = [p["mu"], p["w0"], p["w2"], p["a0"], p["a2"], p["g2"], p["kk"], p["ka"], p["rk"], p["ones"]]
    in_specs = [pl.BlockSpec((tm, RW_Z_COLS), lambda i: (i, 0)), prev, nxt] + [full(a) for a in params]
    args = [z, z, z] + params
    if has_vres:
        in_specs += [row] + [full(a) for a in vres]
        args += [v_first] + list(vres)
    return pl.pallas_call(
        functools.partial(_rw_prep_kernel, geom=geom, tm=tm, has_vres=has_vres),
        grid=(R // tm,),
        in_specs=in_specs,
        out_specs=[row] * 11,
        out_shape=[jax.ShapeDtypeStruct((R, RW_WIDTH), F32)] * 11,
        compiler_params=_cparams("parallel"),
    )(*args)


def _pieces(x, passes):
    hi = x.astype(BF16)
    if passes == 1:
        return (hi,)
    return hi, (x - hi.astype(F32)).astype(BF16)


def _mmx(a, b, dot):
    out = dot(a[0], b[0])
    if len(a) > 1:
        out = out + dot(a[1], b[0])
    if len(b) > 1:
        out = out + dot(a[0], b[1])
    return out


def _stack_pair(first_head, x):
    return jnp.concatenate([jnp.where(first_head, x, 0.0), jnp.where(first_head, 0.0, x)], axis=0)


def _fold_pair(x):
    half = x.shape[0] // 2
    return x[:half] + x[half:]


def _rw_chunk_kernel(r_ref, v_ref, kk_ref, lwf_ref, kf_ref, bf_ref, lwb_ref, kb_ref, bb_ref,
                     rrf_o, ylf_o, mf_o, nf_o, ef_o, rrb_o, ylb_o, mb_o, nb_o, eb_o):
    C = RW_CHUNK
    PW = RW_PAIR
    ri = lax.broadcasted_iota(jnp.int32, (PW, PW), 0)
    ci = lax.broadcasted_iota(jnp.int32, (PW, PW), 1)
    eye = ri == ci
    ri, ci = ri & (C - 1), ci & (C - 1)
    first_head = lax.broadcasted_iota(jnp.int32, (1, PW), 1) < RW_HEAD_DIM
    stack = functools.partial(_stack_pair, first_head)
    r, v, kk = r_ref[...], v_ref[...], kk_ref[...]
    jobs = []
    for lw_ref, k_ref, b_ref, outs, before, tot_row in (
            (lwf_ref, kf_ref, bf_ref, (rrf_o, ylf_o, mf_o, nf_o, ef_o), ci < ri, C - 1),
            (lwb_ref, kb_ref, bb_ref, (rrb_o, ylb_o, mb_o, nb_o, eb_o), ci > ri, 0)):
        lw, kd, bd = lw_ref[...], k_ref[...], b_ref[...]
        incl = (before | eye)[:C, :C].astype(BF16)
        c, rest = None, lw
        for _ in range(3):
            piece = rest.astype(BF16)
            rest = rest - piece.astype(F32)
            part = _dot(incl, piece)
            c = part if c is None else c + part
        c_tot = c[tot_row:tot_row + 1, :]
        at = -kk * jnp.exp(c - lw)
        rt = r * jnp.exp(c)
        e_neg = jnp.exp(-c)
        bt, kt = bd * e_neg, kd * e_neg
        e_rest = jnp.exp(c_tot - c)
        bc, kc = bd * e_rest, kd * e_rest
        outs[4][...] = jnp.broadcast_to(jnp.exp(c_tot), (8, RW_WIDTH))
        for p in range(RW_WIDTH // PW):
            ps = slice(p * PW, (p + 1) * PW)
            jobs.append(dict(ps=ps, outs=outs, before=before, incl=before | eye, rt_pair=rt[:, ps],
                             at=stack(at[:, ps]), rt=stack(rt[:, ps]), bt=stack(bt[:, ps]), kt=stack(kt[:, ps]),
                             bc=stack(bc[:, ps]), kc=stack(kc[:, ps]), v=stack(v[:, ps])))
    for j in jobs:
        p = _mmx(_pieces(jnp.concatenate([j["at"], j["rt"]], axis=0), RW_PASSES_LOCAL),
                 _pieces(jnp.concatenate([j["bt"], j["kt"]], axis=0), RW_PASSES_LOCAL), _dot_nt)
        j["a_ab"] = jnp.where(j["before"], p[:PW, :PW], 0.0)
        j["a_ak"] = jnp.where(j["before"], p[:PW, PW:], 0.0)
        j["a_rb"] = jnp.where(j["incl"], p[PW:, :PW], 0.0)
        j["a_rk"] = jnp.where(j["incl"], p[PW:, PW:], 0.0)
        j["vp"] = _pieces(j["v"], RW_PASSES_LOCAL)
    for j in jobs:
        j["w1"] = _mmx(_pieces(j["a_ak"], RW_PASSES_LOCAL), j["vp"], _dot)
        j["tinv"] = jnp.where(eye, 1.0, j["a_ab"])
        j["pw"] = j["a_ab"]
    for _ in range(int(np.log2(C)) - 1):
        for j in jobs:
            pw = _pieces(j["pw"], RW_PASSES_LOCAL)
            j["pw"] = _mmx(pw, pw, _dot)
        for j in jobs:
            j["tinv"] = j["tinv"] + _mmx(_pieces(j["tinv"], RW_PASSES_LOCAL), _pieces(j["pw"], RW_PASSES_LOCAL), _dot)
    for j in jobs:
        tw = _mmx(_pieces(j["tinv"], RW_PASSES_LOCAL),
                  _pieces(jnp.concatenate([j["at"], j["w1"]], axis=1), RW_PASSES_LOCAL), _dot)
        j["tw"] = tw
        j["twp"] = _pieces(tw, RW_PASSES_LOCAL)
    for j in jobs:
        rr_o, yl_o, m_o, n_o, _ = j["outs"]
        ps = j["ps"]
        ry = _mmx(_pieces(j["a_rb"], RW_PASSES_LOCAL), j["twp"], _dot)
        yk = _mmx(_pieces(j["a_rk"], RW_PASSES_LOCAL), j["vp"], _dot)
        rr_o[:, ps] = j["rt_pair"] + _fold_pair(ry[:, :PW])
        yl_o[:, ps] = _fold_pair(ry[:, PW:] + yk)
        bcp = _pieces(j["bc"], RW_PASSES_STATE)
        m_o[:, ps] = _fold_pair(_mmx(bcp, _pieces(j["tw"][:, :PW], RW_PASSES_STATE), _dot_tn))
        uv = jnp.concatenate([j["tw"][:, PW:], j["v"]], axis=0)
        bk = jnp.concatenate([j["bc"], j["kc"]], axis=0)
        n_o[:, ps] = _fold_pair(_mmx(_pieces(uv, RW_PASSES_STATE), _pieces(bk, RW_PASSES_STATE), _dot_tn))


def rw_chunk(arrs):
    R = arrs[0].shape[0]
    blk = pl.BlockSpec((RW_CHUNK, RW_WIDTH), lambda i: (i, 0))
    eblk = pl.BlockSpec((8, RW_WIDTH), lambda i: (i, 0))
    tok = jax.ShapeDtypeStruct((R, RW_WIDTH), F32)
    dec = jax.ShapeDtypeStruct((R // RW_CHUNK * 8, RW_WIDTH), F32)
    return pl.pallas_call(
        _rw_chunk_kernel,
        grid=(R // RW_CHUNK,),
        in_specs=[blk] * 9,
        out_specs=[blk] * 4 + [eblk] + [blk] * 4 + [eblk],
        out_shape=[tok] * 4 + [dec] + [tok] * 4 + [dec],
        compiler_params=_cparams("parallel"),
    )(*arrs)


def _rw_scan_kernel(rrf_ref, ylf_ref, mf_ref, nf_ref, ef_ref, rrb_ref, ylb_ref, mb_ref, nb_ref, eb_ref,
                    yf_o, yb_o, s_ref, *, nchunk):
    C = RW_CHUNK
    PW = RW_PAIR
    first_head = lax.broadcasted_iota(jnp.int32, (1, PW), 1) < RW_HEAD_DIM
    stack = functools.partial(_stack_pair, first_head)

    @pl.when(pl.program_id(1) == 0)
    def _():
        s_ref[...] = jnp.zeros_like(s_ref)

    dirs = ((rrf_ref, ylf_ref, mf_ref, nf_ref, ef_ref, yf_o), (rrb_ref, ylb_ref, mb_ref, nb_ref, eb_ref, yb_o))
    pairs = [slice(p * PW, (p + 1) * PW) for p in range(RW_WIDTH // PW)]
    state = [[s_ref[d, :, ps] for ps in pairs] for d in range(2)]
    for step in range(nchunk):
        for d, (rr_ref, yl_ref, m_ref, n_ref, e_ref, y_o) in enumerate(dirs):
            c = step if d == 0 else nchunk - 1 - step
            rows = slice(c * C, (c + 1) * C)
            for p, ps in enumerate(pairs):
                s = state[d][p]
                sp = _pieces(stack(s), RW_PASSES_STATE)
                y = _mmx(_pieces(stack(rr_ref[rows, ps]), RW_PASSES_STATE), sp, _dot_nt)
                y_o[rows, ps] = _fold_pair(y) + yl_ref[rows, ps]
                sm = _mmx(sp, _pieces(stack(m_ref[rows, ps]), RW_PASSES_STATE), _dot_nt)
                state[d][p] = s * e_ref[c * 8:c * 8 + 1, ps] + _fold_pair(sm) + n_ref[rows, ps]
    for d in range(2):
        for p, ps in enumerate(pairs):
            s_ref[d, :, ps] = state[d][p]


def rw_scan(geom, chunk_out):
    B, T, L = geom.B, geom.T, geom.L
    blk = L
    nchunk = blk // RW_CHUNK
    nlat = T // blk
    cblk = geom.BT // blk
    fwd_i = lambda b, s: (jnp.where(s == 0, cblk + b, b * nlat + s - 1), 0)
    bwd_i = lambda b, s: (jnp.where(s == 0, cblk + b, b * nlat + nlat - s), 0)
    tok = lambda f: pl.BlockSpec((blk, RW_WIDTH), f)
    dec = lambda f: pl.BlockSpec((8 * nchunk, RW_WIDTH), f)
    return pl.pallas_call(
        functools.partial(_rw_scan_kernel, nchunk=nchunk),
        grid=(B, nlat + 1),
        in_specs=[tok(fwd_i)] * 4 + [dec(fwd_i)] + [tok(bwd_i)] * 4 + [dec(bwd_i)],
        out_specs=[tok(fwd_i), tok(bwd_i)],
        out_shape=[jax.ShapeDtypeStruct((geom.R, RW_WIDTH), F32)] * 2,
        scratch_shapes=[pltpu.VMEM((2, RW_HEAD_DIM, RW_WIDTH), F32)],
        compiler_params=_cparams("parallel", "arbitrary"),
    )(*chunk_out)


def _rw_post_kernel(yf_ref, yb_ref, bonus_ref, g_ref, lng_ref, lnb_ref, ones_ref, o_ref):
    ones = ones_ref[...]
    y = yf_ref[...] + yb_ref[...]
    mean = _head_sum(y, ones) * (1.0 / RW_HEAD_DIM)
    yc = y - mean
    var = _head_sum(yc * yc, ones) * (1.0 / RW_HEAD_DIM)
    y = yc * lax.rsqrt(var + RW_LNX_EPS) * lng_ref[...] + lnb_ref[...]
    o_ref[...] = ((y + bonus_ref[...]) * g_ref[...]).astype(o_ref.dtype)


def rw_post(yf, yb, bonus, g, lnx_g, lnx_b, ones, *, tm):
    R = yf.shape[0]
    row = pl.BlockSpec((tm, RW_WIDTH), lambda i: (i, 0))
    full = lambda a: pl.BlockSpec(a.shape, lambda i: (0,) * a.ndim)
    return pl.pallas_call(
        _rw_post_kernel,
        grid=(R // tm,),
        in_specs=[row] * 4 + [full(lnx_g), full(lnx_b), full(ones)],
        out_specs=row,
        out_shape=jax.ShapeDtypeStruct((R, RW_WIDTH), BF16),
        compiler_params=_cparams("parallel"),
    )(yf, yb, bonus, g, lnx_g, lnx_b, ones)


def _rw_in_cols(w):
    parts = [w[..., :3 * RW_WIDTH]]
    off = 3 * RW_WIDTH
    for n in (RW_DECAY_LORA, RW_DECAY_LORA, RW_AAA_LORA, RW_AAA_LORA, RW_GATE_LORA):
        parts.append(_pad_cols(w[..., off:off + n], LORA_PAD))
        off += n
    return jnp.concatenate(parts, axis=-1)


def _pad_rows(w, n):
    return jnp.pad(w, [(0, 0)] * (w.ndim - 2) + [(0, n - w.shape[-2]), (0, 0)])


def _mla_wq_cols(w):
    w = w.reshape(w.shape[0], MLA_HEADS, MLA_NOPE + MLA_ROPE)
    return _pad_cols(w, MLA_QK_PAD).reshape(w.shape[0], MLA_HEADS * MLA_QK_PAD)


def kernel(x, c, ctx, c_ctx, ada_w, ada_b, norm1_g, w_in, rw_mu, rw_w0, rw_w2, rw_a0, rw_a2, rw_g2, rw_kk, rw_ka,
           rw_rk, rw_lnx_g, rw_lnx_b, rw_v0, rw_v1, rw_v2, wa_sink, mla_qnorm_g, mla_kvnorm_g, mla_w_uq, mla_w_ukv,
           w_branch, w_out, norm2_g, ffn_w_in, ffn_conv_w, ffn_conv_b, ffn_w_out, final_norm_g):
    B, T, D = x.shape
    L = ctx.shape[1]
    depth = w_in.shape[0]
    F = ffn_w_out.shape[1]
    geom = Geom(B, T, L)
    tm = _pick_tile(T, (512, 256, 128))
    assert (B * L) % tm == 0
    tmm = _pick_tile(geom.R, (1280, 1024, 640, 512, 256, 128))

    rw_cols = 3 * RW_WIDTH + 2 * RW_DECAY_LORA + 2 * RW_AAA_LORA + RW_GATE_LORA
    wa_cols = WA_WIDTH + 2 * WA_KV_WIDTH
    mla_cols = MLA_Q_LORA + MLA_KV_LORA + MLA_ROPE
    mla_cols_pad = MLA_Q_LORA + MLA_KV_LORA + 128

    cos_wa, sin_wa = _rope_tables(geom, WA_HEAD_DIM)
    cos_wa, sin_wa = jnp.tile(cos_wa, (1, 2)), jnp.tile(sin_wa, (1, 2))
    cos_m, sin_m = _rope_tables(geom, MLA_ROPE)
    one, zero = jnp.ones((geom.R, MLA_NOPE), F32), jnp.zeros((geom.R, MLA_NOPE), F32)
    cos_mla = jnp.concatenate([one, cos_m, one[:, :64]], axis=-1)
    sin_mla = jnp.concatenate([zero, sin_m, zero[:, :64]], axis=-1)
    lane = np.arange(128)
    ones_blk = jnp.asarray((lane[:, None] // RW_HEAD_DIM) == (lane[None, :] // RW_HEAD_DIM), F32)

    xs = jnp.concatenate([x.reshape(B * T, D), ctx.reshape(B * L, D)], axis=0)
    cvec = jnp.concatenate([c, c_ctx[None, :], jnp.zeros((8 - (B + 1) % 8, D), F32)], axis=0)
    v_first = None
    for l in range(depth):
        need_ctx = l < depth - 1
        mod = ada_modulation(cvec, ada_w[l], ada_b[l])
        mod = [mod[:, k * D:(k + 1) * D].reshape(-1, 1, D) for k in range(6)]

        w = w_in[l]
        w_rw = _pad_cols(_rw_in_cols(w[:, :rw_cols]), _round_up(RW_Z_COLS, 768)).astype(BF16)
        w_wa = w[:, rw_cols:rw_cols + wa_cols].astype(BF16)
        w_mla = _pad_cols(w[:, rw_cols + wa_cols:rw_cols + wa_cols + mla_cols], mla_cols_pad).astype(BF16)
        w_gate = w[:, rw_cols + wa_cols + mla_cols:].astype(BF16)
        nmm = functools.partial(norm_mod_matmul, geom, xs, norm1_g[l], mod[0], mod[1], tm=tmm)
        z_rw = nmm(w_rw, tn=768)
        z_wa = nmm(w_wa, tn=_pick_tile(wa_cols, (768, 512)))
        z_mla = nmm(w_mla, tn=mla_cols_pad)
        gates = nmm(w_gate, tn=_pick_tile(3 * D, (1024, 768, 512, 256, 128)), act="sigmoid", out_dtype=BF16)

        rw_p = dict(
            mu=_rw_in_cols(rw_mu[l][None, :]), w0=rw_w0[l], w2=_pad_rows(rw_w2[l], LORA_PAD), a0=rw_a0[l],
            a2=_pad_rows(rw_a2[l], LORA_PAD), g2=_pad_rows(rw_g2[l], LORA_PAD), kk=rw_kk[l][None, :],
            ka=rw_ka[l][None, :], rk=rw_rk[l].reshape(1, RW_WIDTH), ones=ones_blk)
        vres = None if l == 0 else (rw_v0[l - 1][None, :], rw_v1[l - 1], rw_v2[l - 1])
        r, v, kk, lwf, kf, bf, lwb, kb, bb, g, bonus = rw_prep(geom, z_rw, rw_p, v_first, vres, tm=min(tm, 256))
        if l == 0:
            v_first = v
        yf, yb = rw_scan(geom, rw_chunk((r, v, kk, lwf, kf, bf, lwb, kb, bb)))
        o_a = rw_post(yf, yb, bonus, g, rw_lnx_g[l][None, :], rw_lnx_b[l][None, :], ones_blk, tm=tm)

        q_wa, k_wa, v_wa = wa_prep(z_wa, cos_wa, sin_wa, tm=tm)
        ob_l = wa_attention(geom, q_wa, k_wa, v_wa, wa_sink[l], local=True)
        parts = [ob_l]
        if need_ctx:
            parts.append(wa_attention(geom, q_wa, k_wa, v_wa, wa_sink[l], local=False))
        else:
            parts.append(jnp.zeros((B * L, WA_WIDTH), BF16))
        o_b = jnp.concatenate(parts, axis=0)

        wq = _mla_wq_cols(mla_w_uq[l]).astype(BF16)
        q_m, k_m, v_m = mla_prep(z_mla, mla_qnorm_g[l], mla_kvnorm_g[l], wq, mla_w_ukv[l].astype(BF16),
                                 cos_mla, sin_mla, tm=tm)
        tq = _pick_tile(T, (1024, 512, 256, 128))
        tk = _pick_tile(T, (2048, 1024, 512, 256, 128))
        parts = [mla_attention(geom, q_m, k_m, v_m, with_latent=True, tq=tq, tk=tk)]
        if need_ctx:
            parts.append(mla_attention(geom, q_m, k_m, v_m, with_latent=False, tq=L, tk=L))
        else:
            parts.append(jnp.zeros((B * L, MLA_HEADS * MLA_V), BF16))
        o_c = jnp.concatenate(parts, axis=0)

        tn_d = _pick_tile(D, (1024, 512, 256, 128))
        y = merge_branches(o_a, o_b, o_c, gates, w_branch[l].astype(BF16), tm=tmm, tn=tn_d)
        xs = matmul_gated_residual(geom, y, w_out[l].astype(BF16), xs, mod[2], tm=tmm, tn=tn_d)

        tf = _pick_tile(F, (512, 256, 128))
        hmid = ffn_in_conv_glu(geom, xs, norm2_g[l], mod[3], mod[4], ffn_w_in[l].astype(BF16), ffn_conv_w[l],
                               ffn_conv_b[l], tm=tmm, tf=tf)
        xs = matmul_gated_residual(geom, hmid, ffn_w_out[l].astype(BF16), xs, mod[5], tm=tmm,
                                   tn=_pick_tile(D, (512, 256, 128)))

    out = final_rmsnorm(xs, final_norm_g, B * T, tm=tm)
    return out.reshape(B, T, D)
```

```python
import functools

import jax
import jax.numpy as jnp
import numpy as np
from jax import lax
from jax.experimental import pallas as pl
from jax.experimental.pallas import tpu as pltpu

F32 = jnp.float32
BF16 = jnp.bfloat16
HIGHEST = lax.Precision.HIGHEST

NORM_EPS = 1e-6
NEG_INF = -1e30
GRID_W = 64
ROPE_BASE = 10000.0

RW_HEADS = 16
RW_HEAD_DIM = 64
RW_WIDTH = RW_HEADS * RW_HEAD_DIM
RW_DECAY_LORA = 96
RW_AAA_LORA = 96
RW_GATE_LORA = 64
RW_LNX_EPS = 64e-5
RW_CHUNK = 64
RW_PAIR = 2 * RW_HEAD_DIM
RW_PASSES_LOCAL = 1
RW_PASSES_STATE = 3
RW_PASSES_TRANSITION = 1
LORA_PAD = 128

WA_HEADS = 16
WA_KV_HEADS = 4
WA_GROUP = WA_HEADS // WA_KV_HEADS
WA_HEAD_DIM = 64
WA_WIDTH = WA_HEADS * WA_HEAD_DIM
WA_KV_WIDTH = WA_KV_HEADS * WA_HEAD_DIM
WA_V_PAD = WA_KV_HEADS * 128
WINDOW = 128
WA_SCALE = WA_HEAD_DIM ** -0.5

MLA_HEADS = 8
MLA_NOPE = 128
MLA_ROPE = 64
MLA_V = 128
MLA_Q_LORA = 512
MLA_KV_LORA = 512
MLA_QK_PAD = 256
MLA_V_PAD = 256
MLA_SCALE = (MLA_NOPE + MLA_ROPE) ** -0.5
MLA_SUB_KEYS = 2048
LOG2_E = 1.4426950408889634

CONV_W = 3
VMEM_LIMIT_BYTES = 56 * 1024 * 1024


def _cparams(*sem):
    return pltpu.CompilerParams(dimension_semantics=sem, vmem_limit_bytes=VMEM_LIMIT_BYTES)


def _dot(a, b, precision=None):
    return jnp.dot(a, b, preferred_element_type=F32, precision=precision)


def _dot_nt(a, b, precision=None):
    return lax.dot_general(a, b, (((1,), (1,)), ((), ())), preferred_element_type=F32, precision=precision)


def _dot_tn(a, b, precision=None):
    return lax.dot_general(a, b, (((0,), (0,)), ((), ())), preferred_element_type=F32, precision=precision)


def _pick_tile(n, candidates):
    for c in candidates:
        if n % c == 0:
            return c
    raise ValueError(f"no tile in {candidates} divides {n}")


def _pad_cols(w, n):
    return jnp.pad(w, [(0, 0)] * (w.ndim - 1) + [(0, n - w.shape[-1])])


def _round_up(n, m):
    return (n + m - 1) // m * m


class Geom:
    def __init__(self, B, T, L):
        assert T & (T - 1) == 0 and L & (L - 1) == 0, "sequence lengths must be powers of two"
        assert T % L == 0 and L % RW_CHUNK == 0 and T % GRID_W == 0
        self.B, self.T, self.L = B, T, L
        self.BT = B * T
        self.R = B * T + B * L


def _select_row_group(geom, row0, tm, tab_ref):
    r = row0 + lax.broadcasted_iota(jnp.int32, (tm, 1), 0)
    out = tab_ref[geom.B]
    for b in range(geom.B):
        out = jnp.where((r >= b * geom.T) & (r < (b + 1) * geom.T), tab_ref[b], out)
    return out


def _seq_edge_masks(geom, row0, tm):
    r = row0 + lax.broadcasted_iota(jnp.int32, (tm, 1), 0)
    is_lat = r < geom.BT
    pos = jnp.where(is_lat, r & (geom.T - 1), (r - geom.BT) & (geom.L - 1))
    last = jnp.where(is_lat, geom.T - 1, geom.L - 1)
    return pos == 0, pos == last


def _shifted_rows(x, prev8, next8, first, last):
    tm = x.shape[0]
    rid = lax.broadcasted_iota(jnp.int32, (tm, 1), 0)
    up = jnp.where(rid == 0, prev8[7:8, :], pltpu.roll(x, 1, axis=0))
    dn = jnp.where(rid == tm - 1, next8[0:1, :], pltpu.roll(x, tm - 1, axis=0))
    return jnp.where(first, 0.0, up), jnp.where(last, 0.0, dn)


def _halo_specs(tm, R, width, col_of):
    nb8 = tm // 8
    prev = pl.BlockSpec((8, width), lambda i, *a: (jnp.maximum(i * nb8 - 1, 0), col_of(i, *a)))
    nxt = pl.BlockSpec((8, width), lambda i, *a: (jnp.minimum((i + 1) * nb8, R // 8 - 1), col_of(i, *a)))
    return prev, nxt


def _ada_kernel(c_ref, w_ref, b_ref, o_ref):
    c = c_ref[...]
    o_ref[...] = _dot(c * jax.nn.sigmoid(c), w_ref[...], HIGHEST) + b_ref[...]


def ada_modulation(cvec, w, b):
    G, D = cvec.shape
    N = w.shape[1]
    tn = _pick_tile(N, (1024, 512, 256, 128))
    return pl.pallas_call(
        _ada_kernel,
        grid=(N // tn,),
        in_specs=[pl.BlockSpec((G, D), lambda j: (0, 0)),
                  pl.BlockSpec((D, tn), lambda j: (0, j)),
                  pl.BlockSpec((1, tn), lambda j: (0, j))],
        out_specs=pl.BlockSpec((G, tn), lambda j: (0, j)),
        out_shape=jax.ShapeDtypeStruct((G, N), F32),
        compiler_params=_cparams("arbitrary"),
    )(cvec, w, b.reshape(1, N))


def _norm_modulate(x, g, sc, sh):
    n = x * lax.rsqrt(jnp.mean(x * x, axis=-1, keepdims=True) + NORM_EPS) * g
    return (n * (1.0 + sc) + sh).astype(BF16)


PROLOGUE_ROWS = 256


def _norm_modulate_rows(geom, row0, x_ref, g_ref, sc_ref, sh_ref, h_ref, h_off):
    tm = x_ref.shape[0]
    step = PROLOGUE_ROWS if tm % PROLOGUE_ROWS == 0 else tm
    one_group = geom.T % step == 0 and geom.BT % step == 0
    g = g_ref[...]
    for r in range(0, tm, step):
        if one_group:
            sc, sh = sc_ref[geom.B], sh_ref[geom.B]
            for b in range(geom.B):
                in_b = (row0 + r >= b * geom.T) & (row0 + r < (b + 1) * geom.T)
                sc, sh = jnp.where(in_b, sc_ref[b], sc), jnp.where(in_b, sh_ref[b], sh)
        else:
            sc = _select_row_group(geom, row0 + r, step, sc_ref)
            sh = _select_row_group(geom, row0 + r, step, sh_ref)
        h_ref[h_off + r:h_off + r + step] = _norm_modulate(x_ref[r:r + step], g, sc, sh)


def _nmm_kernel(x_ref, g_ref, sh_ref, sc_ref, w_ref, o_ref, h_ref, *, geom, tm, act):
    @pl.when(pl.program_id(1) == 0)
    def _():
        _norm_modulate_rows(geom, pl.program_id(0) * tm, x_ref, g_ref, sc_ref, sh_ref, h_ref, 0)

    acc = _dot(h_ref[...], w_ref[...])
    if act == "sigmoid":
        acc = jax.nn.sigmoid(acc)
    o_ref[...] = acc.astype(o_ref.dtype)


def norm_mod_matmul(geom, x, g, shift, scale, w, *, tm, tn, act=None, out_dtype=F32):
    R, K = x.shape
    N = w.shape[1]
    assert R % tm == 0 and N % tn == 0
    full = lambda a: pl.BlockSpec(a.shape, lambda i, j: (0,) * a.ndim)
    return pl.pallas_call(
        functools.partial(_nmm_kernel, geom=geom, tm=tm, act=act),
        grid=(R // tm, N // tn),
        in_specs=[pl.BlockSpec((tm, K), lambda i, j: (i, 0), pipeline_mode=pl.Buffered(1)),
                  pl.BlockSpec((1, K), lambda i, j: (0, 0)),
                  full(shift), full(scale),
                  pl.BlockSpec((K, tn), lambda i, j: (0, j))],
        out_specs=pl.BlockSpec((tm, tn), lambda i, j: (i, j)),
        out_shape=jax.ShapeDtypeStruct((R, N), out_dtype),
        scratch_shapes=[pltpu.VMEM((tm, K), BF16)],
        compiler_params=_cparams("parallel", "arbitrary"),
    )(x, g.reshape(1, K), shift, scale, w)


def _mm_resid_kernel(y_ref, w_ref, r_ref, gate_ref, o_ref, *, geom, tm):
    gate = _select_row_group(geom, pl.program_id(0) * tm, tm, gate_ref)
    o_ref[...] = r_ref[...] + gate * _dot(y_ref[...], w_ref[...])


def matmul_gated_residual(geom, y, w, resid, gate, *, tm, tn):
    R, K = y.shape
    N = w.shape[1]
    G = gate.shape[0]
    assert R % tm == 0 and N % tn == 0
    return pl.pallas_call(
        functools.partial(_mm_resid_kernel, geom=geom, tm=tm),
        grid=(R // tm, N // tn),
        in_specs=[pl.BlockSpec((tm, K), lambda i, j: (i, 0)),
                  pl.BlockSpec((K, tn), lambda i, j: (0, j)),
                  pl.BlockSpec((tm, tn), lambda i, j: (i, j)),
                  pl.BlockSpec((G, 1, tn), lambda i, j: (0, 0, j))],
        out_specs=pl.BlockSpec((tm, tn), lambda i, j: (i, j)),
        out_shape=jax.ShapeDtypeStruct((R, N), F32),
        compiler_params=_cparams("parallel", "arbitrary"),
    )(y, w, resid, gate)


FFN_HALO = 16


def _ffn_in_kernel(x_ref, xp_ref, xn_ref, g_ref, sh_ref, sc_ref, wg_ref, wu_ref, cw_ref, cb_ref, o_ref, h_ref,
                   *, geom, tm):
    row0 = pl.program_id(0) * tm
    H = FFN_HALO

    @pl.when(pl.program_id(1) == 0)
    def _():
        _norm_modulate_rows(geom, row0, xp_ref, g_ref, sc_ref, sh_ref, h_ref, 0)
        _norm_modulate_rows(geom, row0, x_ref, g_ref, sc_ref, sh_ref, h_ref, H)
        _norm_modulate_rows(geom, row0 + tm - H, xn_ref, g_ref, sc_ref, sh_ref, h_ref, H + tm)

    first, last = _seq_edge_masks(geom, row0, tm)
    gt = _dot(h_ref[...], wg_ref[...])
    u = _dot(h_ref[H:H + tm], wu_ref[...])
    up = jnp.where(first, 0.0, pltpu.roll(gt, 1, axis=0)[H:H + tm])
    dn = jnp.where(last, 0.0, pltpu.roll(gt, tm + 2 * H - 1, axis=0)[H:H + tm])
    cw = cw_ref[...]
    conv = cb_ref[...] + up * cw[0:1, :]
    conv = conv + gt[H:H + tm] * cw[1:2, :]
    conv = conv + dn * cw[2:3, :]
    o_ref[...] = (jax.nn.gelu(conv, approximate=True) * u).astype(o_ref.dtype)


def ffn_in_conv_glu(geom, x, g, shift, scale, w_in, conv_w, conv_b, *, tm, tf):
    R, K = x.shape
    F = w_in.shape[1] // 2
    nj = F // tf
    nbh = tm // FFN_HALO
    assert R % tm == 0 and F % tf == 0 and tm % FFN_HALO == 0
    full = lambda a: pl.BlockSpec(a.shape, lambda i, j: (0,) * a.ndim)
    return pl.pallas_call(
        functools.partial(_ffn_in_kernel, geom=geom, tm=tm),
        grid=(R // tm, nj),
        in_specs=[pl.BlockSpec((tm, K), lambda i, j: (i, 0), pipeline_mode=pl.Buffered(1)),
                  pl.BlockSpec((FFN_HALO, K), lambda i, j: (jnp.maximum(i * nbh - 1, 0), 0)),
                  pl.BlockSpec((FFN_HALO, K), lambda i, j: (jnp.minimum((i + 1) * nbh, R // FFN_HALO - 1), 0)),
                  pl.BlockSpec((1, K), lambda i, j: (0, 0)),
                  full(shift), full(scale),
                  pl.BlockSpec((K, tf), lambda i, j: (0, j)),
                  pl.BlockSpec((K, tf), lambda i, j: (0, nj + j)),
                  pl.BlockSpec((CONV_W, tf), lambda i, j: (0, j)),
                  pl.BlockSpec((1, tf), lambda i, j: (0, j))],
        out_specs=pl.BlockSpec((tm, tf), lambda i, j: (i, j)),
        out_shape=jax.ShapeDtypeStruct((R, F), BF16),
        scratch_shapes=[pltpu.VMEM((tm + 2 * FFN_HALO, K), BF16)],
        compiler_params=_cparams("parallel", "arbitrary"),
    )(x, x, x, g.reshape(1, K), shift, scale, w_in, w_in, conv_w, conv_b.reshape(1, F))


def _merge_kernel(oa_ref, ob_ref, oc_ref, ga_ref, gb_ref, gc_ref, w_ref, o_ref):
    y = ga_ref[...] * _dot(oa_ref[...], w_ref[0])
    y = y + gb_ref[...] * _dot(ob_ref[...], w_ref[1])
    y = y + gc_ref[...] * _dot(oc_ref[...], w_ref[2])
    o_ref[...] = y.astype(o_ref.dtype)


def merge_branches(oa, ob, oc, gates, wb, *, tm, tn):
    R, K = oa.shape
    D = wb.shape[2]
    nj = D // tn
    bspec = pl.BlockSpec((tm, K), lambda i, j: (i, 0))
    gspec = lambda k: pl.BlockSpec((tm, tn), lambda i, j: (i, k * nj + j))
    return pl.pallas_call(
        _merge_kernel,
        grid=(R // tm, nj),
        in_specs=[bspec, bspec, bspec, gspec(0), gspec(1), gspec(2),
                  pl.BlockSpec((3, K, tn), lambda i, j: (0, 0, j))],
        out_specs=pl.BlockSpec((tm, tn), lambda i, j: (i, j)),
        out_shape=jax.ShapeDtypeStruct((R, D), BF16),
        compiler_params=_cparams("parallel", "arbitrary"),
    )(oa, ob, oc, gates, gates, gates, wb)


def _rmsnorm_kernel(x_ref, g_ref, o_ref):
    x = x_ref[...]
    o_ref[...] = x * lax.rsqrt(jnp.mean(x * x, axis=-1, keepdims=True) + NORM_EPS) * g_ref[...]


def final_rmsnorm(x, g, rows, *, tm):
    D = x.shape[1]
    return pl.pallas_call(
        _rmsnorm_kernel,
        grid=(rows // tm,),
        in_specs=[pl.BlockSpec((tm, D), lambda i: (i, 0)), pl.BlockSpec((1, D), lambda i: (0, 0))],
        out_specs=pl.BlockSpec((tm, D), lambda i: (i, 0)),
        out_shape=jax.ShapeDtypeStruct((rows, D), F32),
        compiler_params=_cparams("parallel"),
    )(x, g.reshape(1, D))


def _rot_half64(z):
    n = z.shape[-1]
    lane = lax.broadcasted_iota(jnp.int32, z.shape, z.ndim - 1)
    return jnp.where((lane & 63) < 32, pltpu.roll(z, n - 32, axis=z.ndim - 1), pltpu.roll(z, 32, axis=z.ndim - 1))


def _rope_tables(geom, dim):
    nf = dim // 4
    inv = ROPE_BASE ** (-jnp.arange(nf, dtype=F32) / nf)
    rows = geom.T // GRID_W
    row = jnp.repeat(jnp.arange(rows, dtype=F32), GRID_W)
    col = jnp.tile(jnp.arange(GRID_W, dtype=F32), rows)
    ang = jnp.concatenate([row[:, None] * inv, col[:, None] * inv], axis=-1)
    cos, sin = jnp.cos(ang), jnp.sin(ang)
    cos_t = jnp.concatenate([cos, cos], axis=-1)
    sin_t = jnp.concatenate([-sin, sin], axis=-1)
    nctx = geom.B * geom.L
    cos_f = jnp.concatenate([jnp.tile(cos_t, (geom.B, 1)), jnp.ones((nctx, dim), F32)], axis=0)
    sin_f = jnp.concatenate([jnp.tile(sin_t, (geom.B, 1)), jnp.zeros((nctx, dim), F32)], axis=0)
    return cos_f, sin_f


def _wa_prep_kernel(z_ref, cos_ref, sin_ref, q_ref, k_ref, v_ref):
    cos = cos_ref[...]
    sin = sin_ref[...]
    for c in range(WA_WIDTH // 128):
        z = z_ref[:, c * 128:(c + 1) * 128]
        q_ref[:, c * 128:(c + 1) * 128] = ((z * cos + _rot_half64(z) * sin) * (WA_SCALE * LOG2_E)).astype(BF16)
    for c in range(WA_KV_WIDTH // 128):
        z = z_ref[:, WA_WIDTH + c * 128:WA_WIDTH + (c + 1) * 128]
        k_ref[:, c * 128:(c + 1) * 128] = (z * cos + _rot_half64(z) * sin).astype(BF16)
    v_ref[...] = jnp.ones(v_ref.shape, BF16)
    for g in range(WA_KV_HEADS):
        src = WA_WIDTH + WA_KV_WIDTH + g * WA_HEAD_DIM
        v_ref[:, g * 128:g * 128 + WA_HEAD_DIM] = z_ref[:, src:src + WA_HEAD_DIM].astype(BF16)


def wa_prep(z, cos, sin, *, tm):
    R = z.shape[0]
    row = lambda w: pl.BlockSpec((tm, w), lambda i: (i, 0))
    return pl.pallas_call(
        _wa_prep_kernel,
        grid=(R // tm,),
        in_specs=[row(WA_WIDTH + 2 * WA_KV_WIDTH), row(128), row(128)],
        out_specs=[row(WA_WIDTH), row(WA_KV_WIDTH), row(WA_V_PAD)],
        out_shape=[jax.ShapeDtypeStruct((R, WA_WIDTH), BF16),
                   jax.ShapeDtypeStruct((R, WA_KV_WIDTH), BF16),
                   jax.ShapeDtypeStruct((R, WA_V_PAD), BF16)],
        compiler_params=_cparams("parallel"),
    )(z, cos, sin)


def _wa_attn_kernel(*refs, local, nqb, tq):
    if local:
        sink_ref, q_ref, kp_ref, kc_ref, kn_ref, vp_ref, vc_ref, vn_ref, kx_ref, vx_ref, o_ref = refs
    else:
        sink_ref, q_ref, kx_ref, vx_ref, o_ref = refs
    i = pl.program_id(1)
    nk_ctx = kx_ref.shape[0]
    rows = WA_GROUP * tq
    qpos = lax.broadcasted_iota(jnp.int32, (rows, 1), 0) & (tq - 1)
    head_in_group = lax.broadcasted_iota(jnp.int32, (rows, 1), 0) >> int(np.log2(tq))
    if local:
        off_prev = jnp.where(i > 0, 0, tq)
        off_next = jnp.where(i < nqb - 1, 0, tq)
        j = lax.broadcasted_iota(jnp.int32, (1, 3 * tq + nk_ctx), 1)
        valid = ((j >= tq) & (j < 2 * tq)) | (j >= 3 * tq)
        valid = valid | ((j < tq) & (j >= qpos + off_prev))
        valid = valid | ((j >= 2 * tq) & (j < 3 * tq) & ((j - 2 * tq) <= qpos - off_next))
    for g in range(WA_KV_HEADS):
        ks = slice(g * WA_HEAD_DIM, (g + 1) * WA_HEAD_DIM)
        qg = jnp.concatenate(
            [q_ref[:, (g * WA_GROUP + a) * WA_HEAD_DIM:(g * WA_GROUP + a + 1) * WA_HEAD_DIM] for a in range(WA_GROUP)],
            axis=0)
        vs = slice(g * 128, (g + 1) * 128)
        sink = jnp.zeros((rows, 1), F32)
        for a in range(WA_GROUP):
            sink = jnp.where(head_in_group == a, sink_ref[g * WA_GROUP + a] * LOG2_E, sink)
        if local:
            kcat = jnp.concatenate([kp_ref[:, ks], kc_ref[:, ks], kn_ref[:, ks], kx_ref[:, ks]], axis=0)
            vcat = jnp.concatenate([vp_ref[:, vs], vc_ref[:, vs], vn_ref[:, vs], vx_ref[:, vs]], axis=0)
        else:
            kcat, vcat = kx_ref[:, ks], vx_ref[:, vs]
        s = _dot_nt(qg, kcat)
        if local:
            s = jnp.where(valid, s, NEG_INF)
        m = jnp.maximum(jnp.max(s, axis=-1, keepdims=True), sink)
        o = _dot(jnp.exp2(s - m).astype(BF16), vcat)
        o = o[:, :WA_HEAD_DIM] / (o[:, WA_HEAD_DIM:] + jnp.exp2(sink - m))
        for a in range(WA_GROUP):
            h = g * WA_GROUP + a
            o_ref[:, h * WA_HEAD_DIM:(h + 1) * WA_HEAD_DIM] = o[a * tq:(a + 1) * tq].astype(o_ref.dtype)


def wa_attention(geom, q, k, v, sink, *, local):
    B, T, L = geom.B, geom.T, geom.L
    sink_spec = pl.BlockSpec(memory_space=pltpu.SMEM)
    ctx_spec = lambda w: pl.BlockSpec((L, w), lambda b, i: (geom.BT // L + b, 0))
    if local:
        tq = WINDOW
        nqb = T // tq
        kv = lambda f, w: pl.BlockSpec((tq, w), lambda b, i: (b * nqb + f(i), 0))
        prev = lambda i: jnp.maximum(i - 1, 0)
        cur = lambda i: i
        nxt = lambda i: jnp.minimum(i + 1, nqb - 1)
        in_specs = [sink_spec, pl.BlockSpec((tq, WA_WIDTH), lambda b, i: (b * nqb + i, 0)),
                    kv(prev, WA_KV_WIDTH), kv(cur, WA_KV_WIDTH), kv(nxt, WA_KV_WIDTH),
                    kv(prev, WA_V_PAD), kv(cur, WA_V_PAD), kv(nxt, WA_V_PAD),
                    ctx_spec(WA_KV_WIDTH), ctx_spec(WA_V_PAD)]
        args = (sink, q, k, k, k, v, v, v, k, v)
        out_rows, out_spec = geom.BT, pl.BlockSpec((tq, WA_WIDTH), lambda b, i: (b * nqb + i, 0))
    else:
        tq, nqb = L, 1
        in_specs = [sink_spec, pl.BlockSpec((tq, WA_WIDTH), lambda b, i: (geom.BT // L + b, 0)),
                    ctx_spec(WA_KV_WIDTH), ctx_spec(WA_V_PAD)]
        args = (sink, q, k, v)
        out_rows, out_spec = B * L, pl.BlockSpec((tq, WA_WIDTH), lambda b, i: (b, 0))
    return pl.pallas_call(
        functools.partial(_wa_attn_kernel, local=local, nqb=nqb, tq=tq),
        grid=(B, nqb),
        in_specs=in_specs,
        out_specs=out_spec,
        out_shape=jax.ShapeDtypeStruct((out_rows, WA_WIDTH), BF16),
        compiler_params=_cparams("parallel", "arbitrary"),
    )(*args)


def _mla_prep_kernel(z_ref, qg_ref, kvg_ref, wq_ref, wkv_ref, cos_ref, sin_ref, q_ref, k_ref, v_ref):
    def norm(x, g):
        return (x * lax.rsqrt(jnp.mean(x * x, axis=-1, keepdims=True) + NORM_EPS) * g).astype(BF16)

    cos = cos_ref[...]
    sin = sin_ref[...]
    q = _dot(norm(z_ref[:, :MLA_Q_LORA], qg_ref[...]), wq_ref[...])
    kv = _dot(norm(z_ref[:, MLA_Q_LORA:MLA_Q_LORA + MLA_KV_LORA], kvg_ref[...]), wkv_ref[...])
    kr = z_ref[:, MLA_Q_LORA + MLA_KV_LORA:MLA_Q_LORA + MLA_KV_LORA + 128]
    kr = (kr * cos[:, 128:] + _rot_half64(kr) * sin[:, 128:]).astype(BF16)
    for h in range(MLA_HEADS):
        qh = q[:, h * MLA_QK_PAD:(h + 1) * MLA_QK_PAD]
        q_ref[h] = ((qh * cos + _rot_half64(qh) * sin) * (MLA_SCALE * LOG2_E)).astype(BF16)
        k_ref[h, :, :MLA_NOPE] = kv[:, h * 256:h * 256 + MLA_NOPE].astype(BF16)
        k_ref[h, :, MLA_NOPE:] = kr
        v_ref[h, :, :MLA_V] = kv[:, h * 256 + MLA_NOPE:(h + 1) * 256].astype(BF16)
        v_ref[h, :, MLA_V:] = jnp.ones((kv.shape[0], MLA_V_PAD - MLA_V), BF16)


def mla_prep(z, qnorm_g, kvnorm_g, wq, wkv, cos, sin, *, tm):
    R, Z = z.shape
    full = lambda a: pl.BlockSpec(a.shape, lambda i: (0,) * a.ndim)
    qg, kvg = qnorm_g.reshape(1, -1), kvnorm_g.reshape(1, -1)
    hd = lambda w: pl.BlockSpec((MLA_HEADS, tm, w), lambda i: (0, i, 0))
    return pl.pallas_call(
        _mla_prep_kernel,
        grid=(R // tm,),
        in_specs=[pl.BlockSpec((tm, Z), lambda i: (i, 0)), full(qg), full(kvg), full(wq), full(wkv),
                  pl.BlockSpec((tm, MLA_QK_PAD), lambda i: (i, 0)), pl.BlockSpec((tm, MLA_QK_PAD), lambda i: (i, 0))],
        out_specs=[hd(MLA_QK_PAD), hd(MLA_QK_PAD), hd(MLA_V_PAD)],
        out_shape=[jax.ShapeDtypeStruct((MLA_HEADS, R, MLA_QK_PAD), BF16),
                   jax.ShapeDtypeStruct((MLA_HEADS, R, MLA_QK_PAD), BF16),
                   jax.ShapeDtypeStruct((MLA_HEADS, R, MLA_V_PAD), BF16)],
        compiler_params=_cparams("parallel"),
    )(z, qg, kvg, wq, wkv, cos, sin)


def _mla_flash_kernel(*refs, with_latent, sub):
    if with_latent:
        q_ref, kx_ref, vx_ref, k_ref, v_ref, o_ref, m_ref, acc_ref = refs
    else:
        q_ref, kx_ref, vx_ref, o_ref, m_ref, acc_ref = refs
    ki = pl.program_id(3)
    q = q_ref[0]

    def update(s, v, m_old, acc_old):
        cols = [s[:, c * 128:(c + 1) * 128] for c in range(s.shape[1] // 128)]
        mx = functools.reduce(jnp.maximum, cols)
        m_new = jnp.maximum(m_old, jnp.max(mx, axis=-1, keepdims=True))
        alpha = jnp.exp2(m_old - m_new)
        p = jnp.concatenate([jnp.exp2(c - m_new).astype(BF16) for c in cols], axis=-1)
        return m_new, jnp.concatenate([alpha, alpha], axis=-1) * acc_old + _dot(p, v)

    @pl.when(ki == 0)
    def _():
        tq = q.shape[0]
        m, acc = update(_dot_nt(q, kx_ref[0]), vx_ref[0], jnp.full((tq, 128), NEG_INF, F32),
                        jnp.zeros((tq, MLA_V_PAD), F32))
        m_ref[...], acc_ref[...] = m, acc

    if with_latent:
        nsub = k_ref.shape[1] // sub
        m, acc = m_ref[...], acc_ref[...]
        s_next = _dot_nt(q, k_ref[0, 0:sub, :])
        for j in range(nsub):
            s = s_next
            if j + 1 < nsub:
                s_next = _dot_nt(q, k_ref[0, (j + 1) * sub:(j + 2) * sub, :])
            m, acc = update(s, v_ref[0, j * sub:(j + 1) * sub, :], m, acc)
        m_ref[...], acc_ref[...] = m, acc

    @pl.when(ki == pl.num_programs(3) - 1)
    def _():
        acc = acc_ref[...]
        o_ref[...] = (acc[:, :MLA_V] / acc[:, MLA_V:]).astype(o_ref.dtype)


def mla_attention(geom, q, k, v, *, with_latent, tq, tk):
    B, T, L = geom.B, geom.T, geom.L
    cblk = geom.BT // L
    ctx_k = pl.BlockSpec((1, L, MLA_QK_PAD), lambda b, h, qi, ki: (h, cblk + b, 0))
    ctx_v = pl.BlockSpec((1, L, MLA_V_PAD), lambda b, h, qi, ki: (h, cblk + b, 0))
    if with_latent:
        nq, nk = T // tq, T // tk
        in_specs = [pl.BlockSpec((1, tq, MLA_QK_PAD), lambda b, h, qi, ki: (h, b * nq + qi, 0)), ctx_k, ctx_v,
                    pl.BlockSpec((1, tk, MLA_QK_PAD), lambda b, h, qi, ki: (h, b * nk + ki, 0)),
                    pl.BlockSpec((1, tk, MLA_V_PAD), lambda b, h, qi, ki: (h, b * nk + ki, 0))]
        args = (q, k, v, k, v)
        out_rows, out_spec = geom.BT, pl.BlockSpec((tq, MLA_V), lambda b, h, qi, ki: (b * nq + qi, h))
    else:
        tq, nq, nk = L, 1, 1
        in_specs = [pl.BlockSpec((1, tq, MLA_QK_PAD), lambda b, h, qi, ki: (h, cblk + b, 0)), ctx_k, ctx_v]
        args = (q, k, v)
        out_rows, out_spec = B * L, pl.BlockSpec((tq, MLA_V), lambda b, h, qi, ki: (b, h))
    return pl.pallas_call(
        functools.partial(_mla_flash_kernel, with_latent=with_latent, sub=min(MLA_SUB_KEYS, tk)),
        grid=(B, MLA_HEADS, nq, nk),
        in_specs=in_specs,
        out_specs=out_spec,
        out_shape=jax.ShapeDtypeStruct((out_rows, MLA_HEADS * MLA_V), BF16),
        scratch_shapes=[pltpu.VMEM((tq, 128), F32), pltpu.VMEM((tq, MLA_V_PAD), F32)],
        compiler_params=_cparams("parallel", "parallel", "parallel", "arbitrary"),
    )(*args)


RW_Z_R, RW_Z_K, RW_Z_V = 0, RW_WIDTH, 2 * RW_WIDTH
RW_Z_LORA = 3 * RW_WIDTH
RW_Z_COLS = 3 * RW_WIDTH + 5 * LORA_PAD


def _dot3(a, b):
    return _mmx(_pieces(a, 3), _pieces(b, 3), _dot)


def _head_sum(x, ones):
    ones = (ones.astype(BF16),)
    return jnp.concatenate(
        [_mmx(_pieces(x[:, c * 128:(c + 1) * 128], 3), ones, _dot) for c in range(x.shape[1] // 128)], axis=-1)


def _rw_prep_kernel(*refs, geom, tm, has_vres):
    if has_vres:
        (z_ref, zp_ref, zn_ref, mu_ref, w0_ref, w2_ref, a0_ref, a2_ref, g2_ref, kk_ref, ka_ref, rk_ref, ones_ref,
         vf_ref, v0_ref, v1_ref, v2_ref,
         r_o, v_o, kk_o, lwf_o, kf_o, bf_o, lwb_o, kb_o, bb_o, g_o, bonus_o) = refs
    else:
        (z_ref, zp_ref, zn_ref, mu_ref, w0_ref, w2_ref, a0_ref, a2_ref, g2_ref, kk_ref, ka_ref, rk_ref, ones_ref,
         r_o, v_o, kk_o, lwf_o, kf_o, bf_o, lwb_o, kb_o, bb_o, g_o, bonus_o) = refs
    first, last = _seq_edge_masks(geom, pl.program_id(0) * tm, tm)
    z = z_ref[...]
    up, dn = _shifted_rows(z, zp_ref[...], zn_ref[...], first, last)
    z = z + mu_ref[...] * (0.5 * (up + dn) - z)
    r = z[:, RW_Z_R:RW_Z_R + RW_WIDTH]
    k = z[:, RW_Z_K:RW_Z_K + RW_WIDTH]
    v = z[:, RW_Z_V:RW_Z_V + RW_WIDTH]
    lora = lambda n: z[:, RW_Z_LORA + n * LORA_PAD:RW_Z_LORA + (n + 1) * LORA_PAD]
    ones = ones_ref[...]
    if has_vres:
        mix = jax.nn.sigmoid(v0_ref[...] + _dot3(_dot3(v, v1_ref[...]), v2_ref[...]))
        v = v + (vf_ref[...] - v) * mix
    g_o[...] = _dot3(jax.nn.sigmoid(lora(4)), g2_ref[...])
    kk = k * kk_ref[...]
    kk = kk / jnp.maximum(jnp.sqrt(_head_sum(kk * kk, ones)), 1e-12)
    ksum = None
    for d, (lw_o, k_o, b_o) in enumerate(((lwf_o, kf_o, bf_o), (lwb_o, kb_o, bb_o))):
        x = -(w0_ref[d:d + 1, :] + _dot3(jnp.tanh(lora(d)), w2_ref[d]))
        softplus = jnp.maximum(x, 0.0) + jnp.log1p(jnp.exp(-jnp.abs(x)))
        lw_o[...] = -jnp.exp(-softplus - 0.5)
        a = jax.nn.sigmoid(a0_ref[d:d + 1, :] + _dot3(lora(2 + d), a2_ref[d]))
        kd = k * (1.0 + (a - 1.0) * ka_ref[...])
        k_o[...] = kd
        b_o[...] = kk * a
        ksum = kd if ksum is None else ksum + kd
    r_o[...] = r
    v_o[...] = v
    kk_o[...] = kk
    bonus_o[...] = _head_sum(r * ksum * rk_ref[...], ones) * v


def rw_prep(geom, z, p, v_first, vres, *, tm):
    R = z.shape[0]
    has_vres = vres is not None
    full = lambda a: pl.BlockSpec(a.shape, lambda i: (0,) * a.ndim)
    row = pl.BlockSpec((tm, RW_WIDTH), lambda i: (i, 0))
    prev, nxt = _halo_specs(tm, R, RW_Z_COLS, lambda i: 0)
    params = [p["mu"], p["w0"], p["w2"], p["a0"], p["a2"], p["g2"], p["kk"], p["ka"], p["rk"], p["ones"]]
    in_specs = [pl.BlockSpec((tm, RW_Z_COLS), lambda i: (i, 0)), prev, nxt] + [full(a) for a in params]
    args = [z, z, z] + params
    if has_vres:
        in_specs += [row] + [full(a) for a in vres]
        args += [v_first] + list(vres)
    return pl.pallas_call(
        functools.partial(_rw_prep_kernel, geom=geom, tm=tm, has_vres=has_vres),
        grid=(R // tm,),
        in_specs=in_specs,
        out_specs=[row] * 11,
        out_shape=[jax.ShapeDtypeStruct((R, RW_WIDTH), F32)] * 11,
        compiler_params=_cparams("parallel"),
    )(*args)


def _pieces(x, passes):
    hi = x.astype(BF16)
    if passes == 1:
        return (hi,)
    return hi, (x - hi.astype(F32)).astype(BF16)


def _mmx(a, b, dot):
    out = dot(a[0], b[0])
    if len(a) > 1:
        out = out + dot(a[1], b[0])
    if len(b) > 1:
        out = out + dot(a[0], b[1])
    return out


def _stack_pair(first_head, x):
    return jnp.concatenate([jnp.where(first_head, x, 0.0), jnp.where(first_head, 0.0, x)], axis=0)


def _fold_pair(x):
    half = x.shape[0] // 2
    return x[:half] + x[half:]


def _rw_chunk_kernel(r_ref, v_ref, kk_ref, lwf_ref, kf_ref, bf_ref, lwb_ref, kb_ref, bb_ref,
                     rrf_o, ylf_o, mf_o, nf_o, ef_o, rrb_o, ylb_o, mb_o, nb_o, eb_o):
    C = RW_CHUNK
    PW = RW_PAIR
    ri = lax.broadcasted_iota(jnp.int32, (PW, PW), 0)
    ci = lax.broadcasted_iota(jnp.int32, (PW, PW), 1)
    eye = ri == ci
    ri, ci = ri & (C - 1), ci & (C - 1)
    first_head = lax.broadcasted_iota(jnp.int32, (1, PW), 1) < RW_HEAD_DIM
    stack = functools.partial(_stack_pair, first_head)
    r, v, kk = r_ref[...], v_ref[...], kk_ref[...]
    jobs = []
    for lw_ref, k_ref, b_ref, outs, before, tot_row in (
            (lwf_ref, kf_ref, bf_ref, (rrf_o, ylf_o, mf_o, nf_o, ef_o), ci < ri, C - 1),
            (lwb_ref, kb_ref, bb_ref, (rrb_o, ylb_o, mb_o, nb_o, eb_o), ci > ri, 0)):
        lw, kd, bd = lw_ref[...], k_ref[...], b_ref[...]
        incl = (before | eye)[:C, :C].astype(BF16)
        c, rest = None, lw
        for _ in range(3):
            piece = rest.astype(BF16)
            rest = rest - piece.astype(F32)
            part = _dot(incl, piece)
            c = part if c is None else c + part
        c_tot = c[tot_row:tot_row + 1, :]
        at = -kk * jnp.exp(c - lw)
        rt = r * jnp.exp(c)
        e_neg = jnp.exp(-c)
        bt, kt = bd * e_neg, kd * e_neg
        e_rest = jnp.exp(c_tot - c)
        bc, kc = bd * e_rest, kd * e_rest
        outs[4][...] = jnp.broadcast_to(jnp.exp(c_tot), (8, RW_WIDTH))
        for p in range(RW_WIDTH // PW):
            ps = slice(p * PW, (p + 1) * PW)
            jobs.append(dict(ps=ps, outs=outs, before=before, incl=before | eye, rt_pair=rt[:, ps],
                             at=stack(at[:, ps]), rt=stack(rt[:, ps]), bt=stack(bt[:, ps]), kt=stack(kt[:, ps]),
                             bc=stack(bc[:, ps]), kc=stack(kc[:, ps]), v=stack(v[:, ps])))
    for j in jobs:
        p = _mmx(_pieces(jnp.concatenate([j["at"], j["rt"]], axis=0), RW_PASSES_LOCAL),
                 _pieces(jnp.concatenate([j["bt"], j["kt"]], axis=0), RW_PASSES_LOCAL), _dot_nt)
        j["a_ab"] = jnp.where(j["before"], p[:PW, :PW], 0.0)
        j["a_ak"] = jnp.where(j["before"], p[:PW, PW:], 0.0)
        j["a_rb"] = jnp.where(j["incl"], p[PW:, :PW], 0.0)
        j["a_rk"] = jnp.where(j["incl"], p[PW:, PW:], 0.0)
        j["vp"] = _pieces(j["v"], RW_PASSES_LOCAL)
    for j in jobs:
        j["w1"] = _mmx(_pieces(j["a_ak"], RW_PASSES_LOCAL), j["vp"], _dot)
        j["tinv"] = jnp.where(eye, 1.0, j["a_ab"])
        j["pw"] = j["a_ab"]
    for _ in range(int(np.log2(C)) - 1):
        for j in jobs:
            pw = _pieces(j["pw"], RW_PASSES_LOCAL)
            j["pw"] = _mmx(pw, pw, _dot)
        for j in jobs:
            j["tinv"] = j["tinv"] + _mmx(_pieces(j["tinv"], RW_PASSES_LOCAL), _pieces(j["pw"], RW_PASSES_LOCAL), _dot)
    for j in jobs:
        tw = _mmx(_pieces(j["tinv"], RW_PASSES_LOCAL),
                  _pieces(jnp.concatenate([j["at"], j["w1"]], axis=1), RW_PASSES_LOCAL), _dot)
        j["tw"] = tw
        j["twp"] = _pieces(tw, RW_PASSES_LOCAL)
    for j in jobs:
        rr_o, yl_o, m_o, n_o, _ = j["outs"]
        ps = j["ps"]
        ry = _mmx(_pieces(j["a_rb"], RW_PASSES_LOCAL), j["twp"], _dot)
        yk = _mmx(_pieces(j["a_rk"], RW_PASSES_LOCAL), j["vp"], _dot)
        rr_o[:, ps] = j["rt_pair"] + _fold_pair(ry[:, :PW])
        yl_o[:, ps] = _fold_pair(ry[:, PW:] + yk)
        m_o[:, ps] = _fold_pair(_mmx(_pieces(j["bc"], RW_PASSES_TRANSITION),
                                     _pieces(j["tw"][:, :PW], RW_PASSES_TRANSITION), _dot_tn))
        uv = jnp.concatenate([j["tw"][:, PW:], j["v"]], axis=0)
        bk = jnp.concatenate([j["bc"], j["kc"]], axis=0)
        n_o[:, ps] = _fold_pair(_mmx(_pieces(uv, RW_PASSES_STATE), _pieces(bk, RW_PASSES_STATE), _dot_tn))


def rw_chunk(arrs):
    R = arrs[0].shape[0]
    blk = pl.BlockSpec((RW_CHUNK, RW_WIDTH), lambda i: (i, 0))
    eblk = pl.BlockSpec((8, RW_WIDTH), lambda i: (i, 0))
    tok = jax.ShapeDtypeStruct((R, RW_WIDTH), F32)
    dec = jax.ShapeDtypeStruct((R // RW_CHUNK * 8, RW_WIDTH), F32)
    return pl.pallas_call(
        _rw_chunk_kernel,
        grid=(R // RW_CHUNK,),
        in_specs=[blk] * 9,
        out_specs=[blk] * 4 + [eblk] + [blk] * 4 + [eblk],
        out_shape=[tok] * 4 + [dec] + [tok] * 4 + [dec],
        compiler_params=_cparams("parallel"),
    )(*arrs)


def _rw_scan_kernel(rrf_ref, ylf_ref, mf_ref, nf_ref, ef_ref, rrb_ref, ylb_ref, mb_ref, nb_ref, eb_ref,
                    yf_o, yb_o, s_ref, *, nchunk):
    C = RW_CHUNK
    PW = RW_PAIR
    first_head = lax.broadcasted_iota(jnp.int32, (1, PW), 1) < RW_HEAD_DIM
    stack = functools.partial(_stack_pair, first_head)

    @pl.when(pl.program_id(1) == 0)
    def _():
        s_ref[...] = jnp.zeros_like(s_ref)

    dirs = ((rrf_ref, ylf_ref, mf_ref, nf_ref, ef_ref, yf_o), (rrb_ref, ylb_ref, mb_ref, nb_ref, eb_ref, yb_o))
    pairs = [slice(p * PW, (p + 1) * PW) for p in range(RW_WIDTH // PW)]
    state = [[s_ref[d, :, ps] for ps in pairs] for d in range(2)]
    for step in range(nchunk):
        for d, (rr_ref, yl_ref, m_ref, n_ref, e_ref, y_o) in enumerate(dirs):
            c = step if d == 0 else nchunk - 1 - step
            rows = slice(c * C, (c + 1) * C)
            for p, ps in enumerate(pairs):
                s = state[d][p]
                sp = _pieces(stack(s), RW_PASSES_STATE)
                y = _mmx(_pieces(stack(rr_ref[rows, ps]), RW_PASSES_STATE), sp, _dot_nt)
                y_o[rows, ps] = _fold_pair(y) + yl_ref[rows, ps]
                sm = _mmx(sp[:RW_PASSES_TRANSITION], _pieces(stack(m_ref[rows, ps]), RW_PASSES_TRANSITION), _dot_nt)
                state[d][p] = s * e_ref[c * 8:c * 8 + 1, ps] + _fold_pair(sm) + n_ref[rows, ps]
    for d in range(2):
        for p, ps in enumerate(pairs):
            s_ref[d, :, ps] = state[d][p]


def rw_scan(geom, chunk_out):
    B, T, L = geom.B, geom.T, geom.L
    blk = L
    nchunk = blk // RW_CHUNK
    nlat = T // blk
    cblk = geom.BT // blk
    fwd_i = lambda b, s: (jnp.where(s == 0, cblk + b, b * nlat + s - 1), 0)
    bwd_i = lambda b, s: (jnp.where(s == 0, cblk + b, b * nlat + nlat - s), 0)
    tok = lambda f: pl.BlockSpec((blk, RW_WIDTH), f)
    dec = lambda f: pl.BlockSpec((8 * nchunk, RW_WIDTH), f)
    return pl.pallas_call(
        functools.partial(_rw_scan_kernel, nchunk=nchunk),
        grid=(B, nlat + 1),
        in_specs=[tok(fwd_i)] * 4 + [dec(fwd_i)] + [tok(bwd_i)] * 4 + [dec(bwd_i)],
        out_specs=[tok(fwd_i), tok(bwd_i)],
        out_shape=[jax.ShapeDtypeStruct((geom.R, RW_WIDTH), F32)] * 2,
        scratch_shapes=[pltpu.VMEM((2, RW_HEAD_DIM, RW_WIDTH), F32)],
        compiler_params=_cparams("parallel", "arbitrary"),
    )(*chunk_out)


def _rw_post_kernel(yf_ref, yb_ref, bonus_ref, g_ref, lng_ref, lnb_ref, ones_ref, o_ref):
    ones = ones_ref[...]
    y = yf_ref[...] + yb_ref[...]
    mean = _head_sum(y, ones) * (1.0 / RW_HEAD_DIM)
    yc = y - mean
    var = _head_sum(yc * yc, ones) * (1.0 / RW_HEAD_DIM)
    y = yc * lax.rsqrt(var + RW_LNX_EPS) * lng_ref[...] + lnb_ref[...]
    o_ref[...] = ((y + bonus_ref[...]) * g_ref[...]).astype(o_ref.dtype)


def rw_post(yf, yb, bonus, g, lnx_g, lnx_b, ones, *, tm):
    R = yf.shape[0]
    row = pl.BlockSpec((tm, RW_WIDTH), lambda i: (i, 0))
    full = lambda a: pl.BlockSpec(a.shape, lambda i: (0,) * a.ndim)
    return pl.pallas_call(
        _rw_post_kernel,
        grid=(R // tm,),
        in_specs=[row] * 4 + [full(lnx_g), full(lnx_b), full(ones)],
        out_specs=row,
        out_shape=jax.ShapeDtypeStruct((R, RW_WIDTH), BF16),
        compiler_params=_cparams("parallel"),
    )(yf, yb, bonus, g, lnx_g, lnx_b, ones)


def _rw_in_cols(w):
    parts = [w[..., :3 * RW_WIDTH]]
    off = 3 * RW_WIDTH
    for n in (RW_DECAY_LORA, RW_DECAY_LORA, RW_AAA_LORA, RW_AAA_LORA, RW_GATE_LORA):
        parts.append(_pad_cols(w[..., off:off + n], LORA_PAD))
        off += n
    return jnp.concatenate(parts, axis=-1)


def _pad_rows(w, n):
    return jnp.pad(w, [(0, 0)] * (w.ndim - 2) + [(0, n - w.shape[-2]), (0, 0)])


def _mla_wq_cols(w):
    w = w.reshape(w.shape[0], MLA_HEADS, MLA_NOPE + MLA_ROPE)
    return _pad_cols(w, MLA_QK_PAD).reshape(w.shape[0], MLA_HEADS * MLA_QK_PAD)


def kernel(x, c, ctx, c_ctx, ada_w, ada_b, norm1_g, w_in, rw_mu, rw_w0, rw_w2, rw_a0, rw_a2, rw_g2, rw_kk, rw_ka,
           rw_rk, rw_lnx_g, rw_lnx_b, rw_v0, rw_v1, rw_v2, wa_sink, mla_qnorm_g, mla_kvnorm_g, mla_w_uq, mla_w_ukv,
           w_branch, w_out, norm2_g, ffn_w_in, ffn_conv_w, ffn_conv_b, ffn_w_out, final_norm_g):
    B, T, D = x.shape
    L = ctx.shape[1]
    depth = w_in.shape[0]
    F = ffn_w_out.shape[1]
    geom = Geom(B, T, L)
    tm = _pick_tile(T, (512, 256, 128))
    assert (B * L) % tm == 0
    tmm = _pick_tile(geom.R, (1280, 1024, 640, 512, 256, 128))

    rw_cols = 3 * RW_WIDTH + 2 * RW_DECAY_LORA + 2 * RW_AAA_LORA + RW_GATE_LORA
    wa_cols = WA_WIDTH + 2 * WA_KV_WIDTH
    mla_cols = MLA_Q_LORA + MLA_KV_LORA + MLA_ROPE
    mla_cols_pad = MLA_Q_LORA + MLA_KV_LORA + 128

    cos_wa, sin_wa = _rope_tables(geom, WA_HEAD_DIM)
    cos_wa, sin_wa = jnp.tile(cos_wa, (1, 2)), jnp.tile(sin_wa, (1, 2))
    cos_m, sin_m = _rope_tables(geom, MLA_ROPE)
    one, zero = jnp.ones((geom.R, MLA_NOPE), F32), jnp.zeros((geom.R, MLA_NOPE), F32)
    cos_mla = jnp.concatenate([one, cos_m, one[:, :64]], axis=-1)
    sin_mla = jnp.concatenate([zero, sin_m, zero[:, :64]], axis=-1)
    lane = np.arange(128)
    ones_blk = jnp.asarray((lane[:, None] // RW_HEAD_DIM) == (lane[None, :] // RW_HEAD_DIM), F32)

    xs = jnp.concatenate([x.reshape(B * T, D), ctx.reshape(B * L, D)], axis=0)
    cvec = jnp.concatenate([c, c_ctx[None, :], jnp.zeros((8 - (B + 1) % 8, D), F32)], axis=0)
    v_first = None
    for l in range(depth):
        need_ctx = l < depth - 1
        mod = ada_modulation(cvec, ada_w[l], ada_b[l])
        mod = [mod[:, k * D:(k + 1) * D].reshape(-1, 1, D) for k in range(6)]

        w = w_in[l]
        w_rw = _pad_cols(_rw_in_cols(w[:, :rw_cols]), _round_up(RW_Z_COLS, 768)).astype(BF16)
        w_wa = w[:, rw_cols:rw_cols + wa_cols].astype(BF16)
        w_mla = _pad_cols(w[:, rw_cols + wa_cols:rw_cols + wa_cols + mla_cols], mla_cols_pad).astype(BF16)
        w_gate = w[:, rw_cols + wa_cols + mla_cols:].astype(BF16)
        nmm = functools.partial(norm_mod_matmul, geom, xs, norm1_g[l], mod[0], mod[1], tm=tmm)
        z_rw = nmm(w_rw, tn=768)
        z_wa = nmm(w_wa, tn=_pick_tile(wa_cols, (768, 512)))
        z_mla = nmm(w_mla, tn=mla_cols_pad)
        gates = nmm(w_gate, tn=_pick_tile(3 * D, (1024, 768, 512, 256, 128)), act="sigmoid", out_dtype=BF16)

        rw_p = dict(
            mu=_rw_in_cols(rw_mu[l][None, :]), w0=rw_w0[l], w2=_pad_rows(rw_w2[l], LORA_PAD), a0=rw_a0[l],
            a2=_pad_rows(rw_a2[l], LORA_PAD), g2=_pad_rows(rw_g2[l], LORA_PAD), kk=rw_kk[l][None, :],
            ka=rw_ka[l][None, :], rk=rw_rk[l].reshape(1, RW_WIDTH), ones=ones_blk)
        vres = None if l == 0 else (rw_v0[l - 1][None, :], rw_v1[l - 1], rw_v2[l - 1])
        r, v, kk, lwf, kf, bf, lwb, kb, bb, g, bonus = rw_prep(geom, z_rw, rw_p, v_first, vres, tm=min(tm, 256))
        if l == 0:
            v_first = v
        yf, yb = rw_scan(geom, rw_chunk((r, v, kk, lwf, kf, bf, lwb, kb, bb)))
        o_a = rw_post(yf, yb, bonus, g, rw_lnx_g[l][None, :], rw_lnx_b[l][None, :], ones_blk, tm=tm)

        q_wa, k_wa, v_wa = wa_prep(z_wa, cos_wa, sin_wa, tm=tm)
        ob_l = wa_attention(geom, q_wa, k_wa, v_wa, wa_sink[l], local=True)
        parts = [ob_l]
        if need_ctx:
            parts.append(wa_attention(geom, q_wa, k_wa, v_wa, wa_sink[l], local=False))
        else:
            parts.append(jnp.zeros((B * L, WA_WIDTH), BF16))
        o_b = jnp.concatenate(parts, axis=0)

        wq = _mla_wq_cols(mla_w_uq[l]).astype(BF16)
        q_m, k_m, v_m = mla_prep(z_mla, mla_qnorm_g[l], mla_kvnorm_g[l], wq, mla_w_ukv[l].astype(BF16),
                                 cos_mla, sin_mla, tm=tm)
        tq = _pick_tile(T, (1024, 512, 256, 128))
        tk = _pick_tile(T, (4096, 2048, 1024, 512, 256, 128))
        parts = [mla_attention(geom, q_m, k_m, v_m, with_latent=True, tq=tq, tk=tk)]
        if need_ctx:
            parts.append(mla_attention(geom, q_m, k_m, v_m, with_latent=False, tq=L, tk=L))
        else:
            parts.append(jnp.zeros((B * L, MLA_HEADS * MLA_V), BF16))
        o_c = jnp.concatenate(parts, axis=0)

        tn_d = _pick_tile(D, (1024, 512, 256, 128))
        y = merge_branches(o_a, o_b, o_c, gates, w_branch[l].astype(BF16), tm=tmm, tn=tn_d)
        xs = matmul_gated_residual(geom, y, w_out[l].astype(BF16), xs, mod[2], tm=tmm, tn=tn_d)

        tf = _pick_tile(F, (512, 256, 128))
        hmid = ffn_in_conv_glu(geom, xs, norm2_g[l], mod[3], mod[4], ffn_w_in[l].astype(BF16), ffn_conv_w[l],
                               ffn_conv_b[l], tm=tmm, tf=tf)
        xs = matmul_gated_residual(geom, hmid, ffn_w_out[l].astype(BF16), xs, mod[5], tm=tmm,
                                   tn=_pick_tile(D, (512, 256, 128)))

    out = final_rmsnorm(xs, final_norm_g, B * T, tm=tm)
    return out.reshape(B, T, D)
```

```python
import functools

import jax
import jax.numpy as jnp
import numpy as np
from jax import lax
from jax.experimental import pallas as pl
from jax.experimental.pallas import tpu as pltpu

F32 = jnp.float32
BF16 = jnp.bfloat16
HIGHEST = lax.Precision.HIGHEST

NORM_EPS = 1e-6
NEG_INF = -1e30
GRID_W = 64
ROPE_BASE = 10000.0

RW_HEADS = 16
RW_HEAD_DIM = 64
RW_WIDTH = RW_HEADS * RW_HEAD_DIM
RW_DECAY_LORA = 96
RW_AAA_LORA = 96
RW_GATE_LORA = 64
RW_LNX_EPS = 64e-5
RW_CHUNK = 64
RW_PAIR = 2 * RW_HEAD_DIM
RW_PASSES_LOCAL = 1
RW_PASSES_STATE = 3
RW_PASSES_TRANSITION = 1
LORA_PAD = 128

WA_HEADS = 16
WA_KV_HEADS = 4
WA_GROUP = WA_HEADS // WA_KV_HEADS
WA_HEAD_DIM = 64
WA_WIDTH = WA_HEADS * WA_HEAD_DIM
WA_KV_WIDTH = WA_KV_HEADS * WA_HEAD_DIM
WA_V_PAD = WA_KV_HEADS * 128
WINDOW = 128
WA_SCALE = WA_HEAD_DIM ** -0.5

MLA_HEADS = 8
MLA_NOPE = 128
MLA_ROPE = 64
MLA_V = 128
MLA_Q_LORA = 512
MLA_KV_LORA = 512
MLA_QK_PAD = 256
MLA_V_PAD = 256
MLA_SCALE = (MLA_NOPE + MLA_ROPE) ** -0.5
MLA_SUB_KEYS = 2048
LOG2_E = 1.4426950408889634

CONV_W = 3
VMEM_LIMIT_BYTES = 56 * 1024 * 1024


def _cparams(*sem):
    return pltpu.CompilerParams(dimension_semantics=sem, vmem_limit_bytes=VMEM_LIMIT_BYTES)


def _dot(a, b, precision=None):
    return jnp.dot(a, b, preferred_element_type=F32, precision=precision)


def _dot_nt(a, b, precision=None):
    return lax.dot_general(a, b, (((1,), (1,)), ((), ())), preferred_element_type=F32, precision=precision)


def _dot_tn(a, b, precision=None):
    return lax.dot_general(a, b, (((0,), (0,)), ((), ())), preferred_element_type=F32, precision=precision)


def _pick_tile(n, candidates):
    for c in candidates:
        if n % c == 0:
            return c
    raise ValueError(f"no tile in {candidates} divides {n}")


def _pad_cols(w, n):
    return jnp.pad(w, [(0, 0)] * (w.ndim - 1) + [(0, n - w.shape[-1])])


def _round_up(n, m):
    return (n + m - 1) // m * m


class Geom:
    def __init__(self, B, T, L):
        assert T & (T - 1) == 0 and L & (L - 1) == 0, "sequence lengths must be powers of two"
        assert T % L == 0 and L % RW_CHUNK == 0 and T % GRID_W == 0
        self.B, self.T, self.L = B, T, L
        self.BT = B * T
        self.R = B * T + B * L


def _select_row_group(geom, row0, tm, tab_ref):
    r = row0 + lax.broadcasted_iota(jnp.int32, (tm, 1), 0)
    out = tab_ref[geom.B]
    for b in range(geom.B):
        out = jnp.where((r >= b * geom.T) & (r < (b + 1) * geom.T), tab_ref[b], out)
    return out


def _seq_edge_masks(geom, row0, tm):
    r = row0 + lax.broadcasted_iota(jnp.int32, (tm, 1), 0)
    is_lat = r < geom.BT
    pos = jnp.where(is_lat, r & (geom.T - 1), (r - geom.BT) & (geom.L - 1))
    last = jnp.where(is_lat, geom.T - 1, geom.L - 1)
    return pos == 0, pos == last


def _shifted_rows(x, prev8, next8, first, last):
    tm = x.shape[0]
    rid = lax.broadcasted_iota(jnp.int32, (tm, 1), 0)
    up = jnp.where(rid == 0, prev8[7:8, :], pltpu.roll(x, 1, axis=0))
    dn = jnp.where(rid == tm - 1, next8[0:1, :], pltpu.roll(x, tm - 1, axis=0))
    return jnp.where(first, 0.0, up), jnp.where(last, 0.0, dn)


def _halo_specs(tm, R, width, col_of):
    nb8 = tm // 8
    prev = pl.BlockSpec((8, width), lambda i, *a: (jnp.maximum(i * nb8 - 1, 0), col_of(i, *a)))
    nxt = pl.BlockSpec((8, width), lambda i, *a: (jnp.minimum((i + 1) * nb8, R // 8 - 1), col_of(i, *a)))
    return prev, nxt


def _ada_kernel(c_ref, w_ref, b_ref, o_ref):
    c = c_ref[...]
    o_ref[...] = _dot(c * jax.nn.sigmoid(c), w_ref[...], HIGHEST) + b_ref[...]


def ada_modulation(cvec, w, b):
    G, D = cvec.shape
    N = w.shape[1]
    tn = _pick_tile(N, (1024, 512, 256, 128))
    return pl.pallas_call(
        _ada_kernel,
        grid=(N // tn,),
        in_specs=[pl.BlockSpec((G, D), lambda j: (0, 0)),
                  pl.BlockSpec((D, tn), lambda j: (0, j)),
                  pl.BlockSpec((1, tn), lambda j: (0, j))],
        out_specs=pl.BlockSpec((G, tn), lambda j: (0, j)),
        out_shape=jax.ShapeDtypeStruct((G, N), F32),
        compiler_params=_cparams("arbitrary"),
    )(cvec, w, b.reshape(1, N))


def _norm_modulate(x, g, sc, sh):
    gain = g * (1.0 + sc)
    return (x * lax.rsqrt(jnp.mean(x * x, axis=-1, keepdims=True) + NORM_EPS) * gain + sh).astype(BF16)


PROLOGUE_ROWS = 256


def _norm_modulate_rows(geom, row0, x_ref, g_ref, sc_ref, sh_ref, h_ref, h_off):
    tm = x_ref.shape[0]
    step = PROLOGUE_ROWS if tm % PROLOGUE_ROWS == 0 else tm
    one_group = geom.T % step == 0 and geom.BT % step == 0
    g = g_ref[...]
    for r in range(0, tm, step):
        if one_group:
            sc, sh = sc_ref[geom.B], sh_ref[geom.B]
            for b in range(geom.B):
                in_b = (row0 + r >= b * geom.T) & (row0 + r < (b + 1) * geom.T)
                sc, sh = jnp.where(in_b, sc_ref[b], sc), jnp.where(in_b, sh_ref[b], sh)
        else:
            sc = _select_row_group(geom, row0 + r, step, sc_ref)
            sh = _select_row_group(geom, row0 + r, step, sh_ref)
        h_ref[h_off + r:h_off + r + step] = _norm_modulate(x_ref[r:r + step], g, sc, sh)


def _in_proj_kernel(x_ref, g_ref, sh_ref, sc_ref, w_ref, zrw_ref, zwa_ref, zmla_ref, gate_ref, h_ref,
                    *, geom, tm, starts):
    j = pl.program_id(1)

    @pl.when(j == 0)
    def _():
        _norm_modulate_rows(geom, pl.program_id(0) * tm, x_ref, g_ref, sc_ref, sh_ref, h_ref, 0)

    acc = _dot(h_ref[...], w_ref[...])
    wa0, mla0, gate0 = starts

    @pl.when(j < wa0)
    def _():
        zrw_ref[...] = acc

    @pl.when((j >= wa0) & (j < mla0))
    def _():
        zwa_ref[...] = acc

    @pl.when((j >= mla0) & (j < gate0))
    def _():
        zmla_ref[...] = acc

    @pl.when(j >= gate0)
    def _():
        gate_ref[...] = jax.nn.sigmoid(acc).astype(gate_ref.dtype)


def in_projection(geom, x, g, shift, scale, w_rw, w_wa, w_mla, w_gate, *, tm, tn):
    R, K = x.shape
    widths = [w.shape[1] for w in (w_rw, w_wa, w_mla, w_gate)]
    assert R % tm == 0 and all(n % tn == 0 for n in widths)
    tiles = [n // tn for n in widths]
    starts = tuple(int(v) for v in np.cumsum(tiles)[:3])
    w_all = jnp.concatenate([w_rw, w_wa, w_mla, w_gate], axis=1)
    full = lambda a: pl.BlockSpec(a.shape, lambda i, j: (0,) * a.ndim)
    out_spec = lambda first, n: pl.BlockSpec((tm, tn), lambda i, j: (i, jnp.clip(j - first, 0, n - 1)))
    firsts = (0,) + starts
    return pl.pallas_call(
        functools.partial(_in_proj_kernel, geom=geom, tm=tm, starts=starts),
        grid=(R // tm, sum(tiles)),
        in_specs=[pl.BlockSpec((tm, K), lambda i, j: (i, 0), pipeline_mode=pl.Buffered(1)),
                  pl.BlockSpec((1, K), lambda i, j: (0, 0)),
                  full(shift), full(scale),
                  pl.BlockSpec((K, tn), lambda i, j: (0, j))],
        out_specs=[out_spec(f, n) for f, n in zip(firsts, tiles)],
        out_shape=[jax.ShapeDtypeStruct((R, widths[0]), F32), jax.ShapeDtypeStruct((R, widths[1]), F32),
                   jax.ShapeDtypeStruct((R, widths[2]), F32), jax.ShapeDtypeStruct((R, widths[3]), BF16)],
        scratch_shapes=[pltpu.VMEM((tm, K), BF16)],
        compiler_params=_cparams("parallel", "arbitrary"),
    )(x, g.reshape(1, K), shift, scale, w_all)


def _mm_resid_kernel(y_ref, w_ref, r_ref, gate_ref, o_ref, *, geom, tm):
    gate = _select_row_group(geom, pl.program_id(0) * tm, tm, gate_ref)
    o_ref[...] = r_ref[...] + gate * _dot(y_ref[...], w_ref[...])


def matmul_gated_residual(geom, y, w, resid, gate, *, tm, tn):
    R, K = y.shape
    N = w.shape[1]
    G = gate.shape[0]
    assert R % tm == 0 and N % tn == 0
    return pl.pallas_call(
        functools.partial(_mm_resid_kernel, geom=geom, tm=tm),
        grid=(R // tm, N // tn),
        in_specs=[pl.BlockSpec((tm, K), lambda i, j: (i, 0)),
                  pl.BlockSpec((K, tn), lambda i, j: (0, j)),
                  pl.BlockSpec((tm, tn), lambda i, j: (i, j)),
                  pl.BlockSpec((G, 1, tn), lambda i, j: (0, 0, j))],
        out_specs=pl.BlockSpec((tm, tn), lambda i, j: (i, j)),
        out_shape=jax.ShapeDtypeStruct((R, N), F32),
        compiler_params=_cparams("parallel", "arbitrary"),
    )(y, w, resid, gate)


FFN_HALO = 16


def _ffn_in_kernel(x_ref, xp_ref, xn_ref, g_ref, sh_ref, sc_ref, wg_ref, wu_ref, cw_ref, cb_ref, o_ref, h_ref,
                   *, geom, tm):
    row0 = pl.program_id(0) * tm
    H = FFN_HALO

    @pl.when(pl.program_id(1) == 0)
    def _():
        _norm_modulate_rows(geom, row0, xp_ref, g_ref, sc_ref, sh_ref, h_ref, 0)
        _norm_modulate_rows(geom, row0, x_ref, g_ref, sc_ref, sh_ref, h_ref, H)
        _norm_modulate_rows(geom, row0 + tm - H, xn_ref, g_ref, sc_ref, sh_ref, h_ref, H + tm)

    first, last = _seq_edge_masks(geom, row0, tm)
    gt = _dot(h_ref[...], wg_ref[...])
    u = _dot(h_ref[H:H + tm], wu_ref[...])
    up = jnp.where(first, 0.0, pltpu.roll(gt, 1, axis=0)[H:H + tm])
    dn = jnp.where(last, 0.0, pltpu.roll(gt, tm + 2 * H - 1, axis=0)[H:H + tm])
    cw = cw_ref[...]
    conv = cb_ref[...] + up * cw[0:1, :]
    conv = conv + gt[H:H + tm] * cw[1:2, :]
    conv = conv + dn * cw[2:3, :]
    o_ref[...] = (jax.nn.gelu(conv, approximate=True) * u).astype(o_ref.dtype)


def ffn_in_conv_glu(geom, x, g, shift, scale, w_in, conv_w, conv_b, *, tm, tf):
    R, K = x.shape
    F = w_in.shape[1] // 2
    nj = F // tf
    nbh = tm // FFN_HALO
    assert R % tm == 0 and F % tf == 0 and tm % FFN_HALO == 0
    full = lambda a: pl.BlockSpec(a.shape, lambda i, j: (0,) * a.ndim)
    return pl.pallas_call(
        functools.partial(_ffn_in_kernel, geom=geom, tm=tm),
        grid=(R // tm, nj),
        in_specs=[pl.BlockSpec((tm, K), lambda i, j: (i, 0), pipeline_mode=pl.Buffered(1)),
                  pl.BlockSpec((FFN_HALO, K), lambda i, j: (jnp.maximum(i * nbh - 1, 0), 0)),
                  pl.BlockSpec((FFN_HALO, K), lambda i, j: (jnp.minimum((i + 1) * nbh, R // FFN_HALO - 1), 0)),
                  pl.BlockSpec((1, K), lambda i, j: (0, 0)),
                  full(shift), full(scale),
                  pl.BlockSpec((K, tf), lambda i, j: (0, j)),
                  pl.BlockSpec((K, tf), lambda i, j: (0, nj + j)),
                  pl.BlockSpec((CONV_W, tf), lambda i, j: (0, j)),
                  pl.BlockSpec((1, tf), lambda i, j: (0, j))],
        out_specs=pl.BlockSpec((tm, tf), lambda i, j: (i, j)),
        out_shape=jax.ShapeDtypeStruct((R, F), BF16),
        scratch_shapes=[pltpu.VMEM((tm + 2 * FFN_HALO, K), BF16)],
        compiler_params=_cparams("parallel", "arbitrary"),
    )(x, x, x, g.reshape(1, K), shift, scale, w_in, w_in, conv_w, conv_b.reshape(1, F))


def _merge_kernel(oa_ref, ob_ref, oc_ref, ga_ref, gb_ref, gc_ref, w_ref, o_ref):
    y = ga_ref[...] * _dot(oa_ref[...], w_ref[0])
    y = y + gb_ref[...] * _dot(ob_ref[...], w_ref[1])
    y = y + gc_ref[...] * _dot(oc_ref[...], w_ref[2])
    o_ref[...] = y.astype(o_ref.dtype)


def merge_branches(oa, ob, oc, gates, wb, *, tm, tn):
    R, K = oa.shape
    D = wb.shape[2]
    nj = D // tn
    bspec = pl.BlockSpec((tm, K), lambda i, j: (i, 0))
    gspec = lambda k: pl.BlockSpec((tm, tn), lambda i, j: (i, k * nj + j))
    return pl.pallas_call(
        _merge_kernel,
        grid=(R // tm, nj),
        in_specs=[bspec, bspec, bspec, gspec(0), gspec(1), gspec(2),
                  pl.BlockSpec((3, K, tn), lambda i, j: (0, 0, j))],
        out_specs=pl.BlockSpec((tm, tn), lambda i, j: (i, j)),
        out_shape=jax.ShapeDtypeStruct((R, D), BF16),
        compiler_params=_cparams("parallel", "arbitrary"),
    )(oa, ob, oc, gates, gates, gates, wb)


def _rmsnorm_kernel(x_ref, g_ref, o_ref):
    x = x_ref[...]
    o_ref[...] = x * lax.rsqrt(jnp.mean(x * x, axis=-1, keepdims=True) + NORM_EPS) * g_ref[...]


def final_rmsnorm(x, g, rows, *, tm):
    D = x.shape[1]
    return pl.pallas_call(
        _rmsnorm_kernel,
        grid=(rows // tm,),
        in_specs=[pl.BlockSpec((tm, D), lambda i: (i, 0)), pl.BlockSpec((1, D), lambda i: (0, 0))],
        out_specs=pl.BlockSpec((tm, D), lambda i: (i, 0)),
        out_shape=jax.ShapeDtypeStruct((rows, D), F32),
        compiler_params=_cparams("parallel"),
    )(x, g.reshape(1, D))


def _rot_half64(z):
    n = z.shape[-1]
    lane = lax.broadcasted_iota(jnp.int32, z.shape, z.ndim - 1)
    return jnp.where((lane & 63) < 32, pltpu.roll(z, n - 32, axis=z.ndim - 1), pltpu.roll(z, 32, axis=z.ndim - 1))


def _rope_tables(geom, dim):
    nf = dim // 4
    inv = ROPE_BASE ** (-jnp.arange(nf, dtype=F32) / nf)
    rows = geom.T // GRID_W
    row = jnp.repeat(jnp.arange(rows, dtype=F32), GRID_W)
    col = jnp.tile(jnp.arange(GRID_W, dtype=F32), rows)
    ang = jnp.concatenate([row[:, None] * inv, col[:, None] * inv], axis=-1)
    cos, sin = jnp.cos(ang), jnp.sin(ang)
    cos_t = jnp.concatenate([cos, cos], axis=-1)
    sin_t = jnp.concatenate([-sin, sin], axis=-1)
    nctx = geom.B * geom.L
    cos_f = jnp.concatenate([jnp.tile(cos_t, (geom.B, 1)), jnp.ones((nctx, dim), F32)], axis=0)
    sin_f = jnp.concatenate([jnp.tile(sin_t, (geom.B, 1)), jnp.zeros((nctx, dim), F32)], axis=0)
    return cos_f, sin_f


def _wa_prep_kernel(z_ref, cos_ref, sin_ref, q_ref, k_ref, v_ref):
    cos = cos_ref[...]
    sin = sin_ref[...]
    for c in range(WA_WIDTH // 128):
        z = z_ref[:, c * 128:(c + 1) * 128]
        q_ref[:, c * 128:(c + 1) * 128] = ((z * cos + _rot_half64(z) * sin) * (WA_SCALE * LOG2_E)).astype(BF16)
    for c in range(WA_KV_WIDTH // 128):
        z = z_ref[:, WA_WIDTH + c * 128:WA_WIDTH + (c + 1) * 128]
        k_ref[:, c * 128:(c + 1) * 128] = (z * cos + _rot_half64(z) * sin).astype(BF16)
    v_ref[...] = jnp.ones(v_ref.shape, BF16)
    for g in range(WA_KV_HEADS):
        src = WA_WIDTH + WA_KV_WIDTH + g * WA_HEAD_DIM
        v_ref[:, g * 128:g * 128 + WA_HEAD_DIM] = z_ref[:, src:src + WA_HEAD_DIM].astype(BF16)


def wa_prep(z, cos, sin, *, tm):
    R = z.shape[0]
    row = lambda w: pl.BlockSpec((tm, w), lambda i: (i, 0))
    return pl.pallas_call(
        _wa_prep_kernel,
        grid=(R // tm,),
        in_specs=[row(WA_WIDTH + 2 * WA_KV_WIDTH), row(128), row(128)],
        out_specs=[row(WA_WIDTH), row(WA_KV_WIDTH), row(WA_V_PAD)],
        out_shape=[jax.ShapeDtypeStruct((R, WA_WIDTH), BF16),
                   jax.ShapeDtypeStruct((R, WA_KV_WIDTH), BF16),
                   jax.ShapeDtypeStruct((R, WA_V_PAD), BF16)],
        compiler_params=_cparams("parallel"),
    )(z, cos, sin)


def _wa_attn_kernel(*refs, local, nqb, tq):
    if local:
        sink_ref, q_ref, kp_ref, kc_ref, kn_ref, vp_ref, vc_ref, vn_ref, kx_ref, vx_ref, o_ref = refs
    else:
        sink_ref, q_ref, kx_ref, vx_ref, o_ref = refs
    i = pl.program_id(1)
    nk_ctx = kx_ref.shape[0]
    rows = WA_GROUP * tq
    qpos = lax.broadcasted_iota(jnp.int32, (rows, 1), 0) & (tq - 1)
    head_in_group = lax.broadcasted_iota(jnp.int32, (rows, 1), 0) >> int(np.log2(tq))
    if local:
        off_prev = jnp.where(i > 0, 0, tq)
        off_next = jnp.where(i < nqb - 1, 0, tq)
        j = lax.broadcasted_iota(jnp.int32, (1, 3 * tq + nk_ctx), 1)
        valid = ((j >= tq) & (j < 2 * tq)) | (j >= 3 * tq)
        valid = valid | ((j < tq) & (j >= qpos + off_prev))
        valid = valid | ((j >= 2 * tq) & (j < 3 * tq) & ((j - 2 * tq) <= qpos - off_next))
    for g in range(WA_KV_HEADS):
        ks = slice(g * WA_HEAD_DIM, (g + 1) * WA_HEAD_DIM)
        qg = jnp.concatenate(
            [q_ref[:, (g * WA_GROUP + a) * WA_HEAD_DIM:(g * WA_GROUP + a + 1) * WA_HEAD_DIM] for a in range(WA_GROUP)],
            axis=0)
        vs = slice(g * 128, (g + 1) * 128)
        sink = jnp.zeros((rows, 1), F32)
        for a in range(WA_GROUP):
            sink = jnp.where(head_in_group == a, sink_ref[g * WA_GROUP + a] * LOG2_E, sink)
        if local:
            kcat = jnp.concatenate([kp_ref[:, ks], kc_ref[:, ks], kn_ref[:, ks], kx_ref[:, ks]], axis=0)
            vcat = jnp.concatenate([vp_ref[:, vs], vc_ref[:, vs], vn_ref[:, vs], vx_ref[:, vs]], axis=0)
        else:
            kcat, vcat = kx_ref[:, ks], vx_ref[:, vs]
        s = _dot_nt(qg, kcat)
        if local:
            s = jnp.where(valid, s, NEG_INF)
        m = jnp.maximum(jnp.max(s, axis=-1, keepdims=True), sink)
        o = _dot(jnp.exp2(s - m).astype(BF16), vcat)
        o = o[:, :WA_HEAD_DIM] / (o[:, WA_HEAD_DIM:] + jnp.exp2(sink - m))
        for a in range(WA_GROUP):
            h = g * WA_GROUP + a
            o_ref[:, h * WA_HEAD_DIM:(h + 1) * WA_HEAD_DIM] = o[a * tq:(a + 1) * tq].astype(o_ref.dtype)


def wa_attention(geom, q, k, v, sink, *, local):
    B, T, L = geom.B, geom.T, geom.L
    sink_spec = pl.BlockSpec(memory_space=pltpu.SMEM)
    ctx_spec = lambda w: pl.BlockSpec((L, w), lambda b, i: (geom.BT // L + b, 0))
    if local:
        tq = WINDOW
        nqb = T // tq
        kv = lambda f, w: pl.BlockSpec((tq, w), lambda b, i: (b * nqb + f(i), 0))
        prev = lambda i: jnp.maximum(i - 1, 0)
        cur = lambda i: i
        nxt = lambda i: jnp.minimum(i + 1, nqb - 1)
        in_specs = [sink_spec, pl.BlockSpec((tq, WA_WIDTH), lambda b, i: (b * nqb + i, 0)),
                    kv(prev, WA_KV_WIDTH), kv(cur, WA_KV_WIDTH), kv(nxt, WA_KV_WIDTH),
                    kv(prev, WA_V_PAD), kv(cur, WA_V_PAD), kv(nxt, WA_V_PAD),
                    ctx_spec(WA_KV_WIDTH), ctx_spec(WA_V_PAD)]
        args = (sink, q, k, k, k, v, v, v, k, v)
        out_rows, out_spec = geom.BT, pl.BlockSpec((tq, WA_WIDTH), lambda b, i: (b * nqb + i, 0))
    else:
        tq, nqb = L, 1
        in_specs = [sink_spec, pl.BlockSpec((tq, WA_WIDTH), lambda b, i: (geom.BT // L + b, 0)),
                    ctx_spec(WA_KV_WIDTH), ctx_spec(WA_V_PAD)]
        args = (sink, q, k, v)
        out_rows, out_spec = B * L, pl.BlockSpec((tq, WA_WIDTH), lambda b, i: (b, 0))
    return pl.pallas_call(
        functools.partial(_wa_attn_kernel, local=local, nqb=nqb, tq=tq),
        grid=(B, nqb),
        in_specs=in_specs,
        out_specs=out_spec,
        out_shape=jax.ShapeDtypeStruct((out_rows, WA_WIDTH), BF16),
        compiler_params=_cparams("parallel", "arbitrary"),
    )(*args)


def _mla_prep_kernel(z_ref, qg_ref, kvg_ref, wq_ref, wkv_ref, cos_ref, sin_ref, q_ref, k_ref, v_ref):
    def norm(x, g):
        return (x * lax.rsqrt(jnp.mean(x * x, axis=-1, keepdims=True) + NORM_EPS) * g).astype(BF16)

    cos = cos_ref[...]
    sin = sin_ref[...]
    q = _dot(norm(z_ref[:, :MLA_Q_LORA], qg_ref[...]), wq_ref[...])
    kv = _dot(norm(z_ref[:, MLA_Q_LORA:MLA_Q_LORA + MLA_KV_LORA], kvg_ref[...]), wkv_ref[...])
    kr = z_ref[:, MLA_Q_LORA + MLA_KV_LORA:MLA_Q_LORA + MLA_KV_LORA + 128]
    kr = (kr * cos[:, 128:] + _rot_half64(kr) * sin[:, 128:]).astype(BF16)
    for h in range(MLA_HEADS):
        qh = q[:, h * MLA_QK_PAD:(h + 1) * MLA_QK_PAD]
        q_ref[h] = ((qh * cos + _rot_half64(qh) * sin) * (MLA_SCALE * LOG2_E)).astype(BF16)
        k_ref[h, :, :MLA_NOPE] = kv[:, h * 256:h * 256 + MLA_NOPE].astype(BF16)
        k_ref[h, :, MLA_NOPE:] = kr
        v_ref[h, :, :MLA_V] = kv[:, h * 256 + MLA_NOPE:(h + 1) * 256].astype(BF16)
        v_ref[h, :, MLA_V:] = jnp.ones((kv.shape[0], MLA_V_PAD - MLA_V), BF16)


def mla_prep(z, qnorm_g, kvnorm_g, wq, wkv, cos, sin, *, tm):
    R, Z = z.shape
    full = lambda a: pl.BlockSpec(a.shape, lambda i: (0,) * a.ndim)
    qg, kvg = qnorm_g.reshape(1, -1), kvnorm_g.reshape(1, -1)
    hd = lambda w: pl.BlockSpec((MLA_HEADS, tm, w), lambda i: (0, i, 0))
    return pl.pallas_call(
        _mla_prep_kernel,
        grid=(R // tm,),
        in_specs=[pl.BlockSpec((tm, Z), lambda i: (i, 0)), full(qg), full(kvg), full(wq), full(wkv),
                  pl.BlockSpec((tm, MLA_QK_PAD), lambda i: (i, 0)), pl.BlockSpec((tm, MLA_QK_PAD), lambda i: (i, 0))],
        out_specs=[hd(MLA_QK_PAD), hd(MLA_QK_PAD), hd(MLA_V_PAD)],
        out_shape=[jax.ShapeDtypeStruct((MLA_HEADS, R, MLA_QK_PAD), BF16),
                   jax.ShapeDtypeStruct((MLA_HEADS, R, MLA_QK_PAD), BF16),
                   jax.ShapeDtypeStruct((MLA_HEADS, R, MLA_V_PAD), BF16)],
        compiler_params=_cparams("parallel"),
    )(z, qg, kvg, wq, wkv, cos, sin)


def _mla_flash_kernel(*refs, with_latent, sub):
    if with_latent:
        q_ref, kx_ref, vx_ref, k_ref, v_ref, o_ref, m_ref, acc_ref = refs
    else:
        q_ref, kx_ref, vx_ref, o_ref, m_ref, acc_ref = refs
    ki = pl.program_id(3)
    q = q_ref[0]

    def update(s, v, m_old, acc_old):
        cols = [s[:, c * 128:(c + 1) * 128] for c in range(s.shape[1] // 128)]
        mx = functools.reduce(jnp.maximum, cols)
        m_new = jnp.maximum(m_old, jnp.max(mx, axis=-1, keepdims=True))
        alpha = jnp.exp2(m_old - m_new)
        p = jnp.concatenate([jnp.exp2(c - m_new).astype(BF16) for c in cols], axis=-1)
        return m_new, jnp.concatenate([alpha, alpha], axis=-1) * acc_old + _dot(p, v)

    @pl.when(ki == 0)
    def _():
        tq = q.shape[0]
        m, acc = update(_dot_nt(q, kx_ref[0]), vx_ref[0], jnp.full((tq, 128), NEG_INF, F32),
                        jnp.zeros((tq, MLA_V_PAD), F32))
        m_ref[...], acc_ref[...] = m, acc

    if with_latent:
        nsub = k_ref.shape[1] // sub
        m, acc = m_ref[...], acc_ref[...]
        s_next = _dot_nt(q, k_ref[0, 0:sub, :])
        for j in range(nsub):
            s = s_next
            if j + 1 < nsub:
                s_next = _dot_nt(q, k_ref[0, (j + 1) * sub:(j + 2) * sub, :])
            m, acc = update(s, v_ref[0, j * sub:(j + 1) * sub, :], m, acc)
        m_ref[...], acc_ref[...] = m, acc

    @pl.when(ki == pl.num_programs(3) - 1)
    def _():
        acc = acc_ref[...]
        o_ref[...] = (acc[:, :MLA_V] / acc[:, MLA_V:]).astype(o_ref.dtype)


def mla_attention(geom, q, k, v, *, with_latent, tq, tk):
    B, T, L = geom.B, geom.T, geom.L
    cblk = geom.BT // L
    ctx_k = pl.BlockSpec((1, L, MLA_QK_PAD), lambda b, h, qi, ki: (h, cblk + b, 0))
    ctx_v = pl.BlockSpec((1, L, MLA_V_PAD), lambda b, h, qi, ki: (h, cblk + b, 0))
    if with_latent:
        nq, nk = T // tq, T // tk
        in_specs = [pl.BlockSpec((1, tq, MLA_QK_PAD), lambda b, h, qi, ki: (h, b * nq + qi, 0)), ctx_k, ctx_v,
                    pl.BlockSpec((1, tk, MLA_QK_PAD), lambda b, h, qi, ki: (h, b * nk + ki, 0)),
                    pl.BlockSpec((1, tk, MLA_V_PAD), lambda b, h, qi, ki: (h, b * nk + ki, 0))]
        args = (q, k, v, k, v)
        out_rows, out_spec = geom.BT, pl.BlockSpec((tq, MLA_V), lambda b, h, qi, ki: (b * nq + qi, h))
    else:
        tq, nq, nk = L, 1, 1
        in_specs = [pl.BlockSpec((1, tq, MLA_QK_PAD), lambda b, h, qi, ki: (h, cblk + b, 0)), ctx_k, ctx_v]
        args = (q, k, v)
        out_rows, out_spec = B * L, pl.BlockSpec((tq, MLA_V), lambda b, h, qi, ki: (b, h))
    return pl.pallas_call(
        functools.partial(_mla_flash_kernel, with_latent=with_latent, sub=min(MLA_SUB_KEYS, tk)),
        grid=(B, MLA_HEADS, nq, nk),
        in_specs=in_specs,
        out_specs=out_spec,
        out_shape=jax.ShapeDtypeStruct((out_rows, MLA_HEADS * MLA_V), BF16),
        scratch_shapes=[pltpu.VMEM((tq, 128), F32), pltpu.VMEM((tq, MLA_V_PAD), F32)],
        compiler_params=_cparams("parallel", "parallel", "parallel", "arbitrary"),
    )(*args)


RW_Z_R, RW_Z_K, RW_Z_V = 0, RW_WIDTH, 2 * RW_WIDTH
RW_Z_LORA = 3 * RW_WIDTH
RW_Z_COLS = 3 * RW_WIDTH + 5 * LORA_PAD


def _dot3(a, b):
    return _mmx(_pieces(a, 3), _pieces(b, 3), _dot)


def _head_sum(x, ones):
    ones = (ones.astype(BF16),)
    return jnp.concatenate(
        [_mmx(_pieces(x[:, c * 128:(c + 1) * 128], 3), ones, _dot) for c in range(x.shape[1] // 128)], axis=-1)


def _rw_prep_kernel(*refs, geom, tm, has_vres):
    if has_vres:
        (z_ref, zp_ref, zn_ref, mu_ref, w0_ref, w2_ref, a0_ref, a2_ref, g2_ref, kk_ref, ka_ref, rk_ref, ones_ref,
         vf_ref, v0_ref, v1_ref, v2_ref,
         r_o, v_o, kk_o, lwf_o, kf_o, bf_o, lwb_o, kb_o, bb_o, g_o, bonus_o) = refs
    else:
        (z_ref, zp_ref, zn_ref, mu_ref, w0_ref, w2_ref, a0_ref, a2_ref, g2_ref, kk_ref, ka_ref, rk_ref, ones_ref,
         r_o, v_o, kk_o, lwf_o, kf_o, bf_o, lwb_o, kb_o, bb_o, g_o, bonus_o) = refs
    first, last = _seq_edge_masks(geom, pl.program_id(0) * tm, tm)
    z = z_ref[...]
    up, dn = _shifted_rows(z, zp_ref[...], zn_ref[...], first, last)
    z = z + mu_ref[...] * (0.5 * (up + dn) - z)
    r = z[:, RW_Z_R:RW_Z_R + RW_WIDTH]
    k = z[:, RW_Z_K:RW_Z_K + RW_WIDTH]
    v = z[:, RW_Z_V:RW_Z_V + RW_WIDTH]
    lora = lambda n: z[:, RW_Z_LORA + n * LORA_PAD:RW_Z_LORA + (n + 1) * LORA_PAD]
    ones = ones_ref[...]
    if has_vres:
        mix = jax.nn.sigmoid(v0_ref[...] + _dot3(_dot3(v, v1_ref[...]), v2_ref[...]))
        v = v + (vf_ref[...] - v) * mix
    g_o[...] = _dot3(jax.nn.sigmoid(lora(4)), g2_ref[...])
    kk = k * kk_ref[...]
    kk = kk / jnp.maximum(jnp.sqrt(_head_sum(kk * kk, ones)), 1e-12)
    ksum = None
    for d, (lw_o, k_o, b_o) in enumerate(((lwf_o, kf_o, bf_o), (lwb_o, kb_o, bb_o))):
        x = -(w0_ref[d:d + 1, :] + _dot3(jnp.tanh(lora(d)), w2_ref[d]))
        softplus = jnp.maximum(x, 0.0) + jnp.log1p(jnp.exp(-jnp.abs(x)))
        lw_o[...] = -jnp.exp(-softplus - 0.5)
        a = jax.nn.sigmoid(a0_ref[d:d + 1, :] + _dot3(lora(2 + d), a2_ref[d]))
        kd = k * (1.0 + (a - 1.0) * ka_ref[...])
        k_o[...] = kd
        b_o[...] = kk * a
        ksum = kd if ksum is None else ksum + kd
    r_o[...] = r
    v_o[...] = v
    kk_o[...] = kk
    bonus_o[...] = _head_sum(r * ksum * rk_ref[...], ones) * v


def rw_prep(geom, z, p, v_first, vres, *, tm):
    R = z.shape[0]
    has_vres = vres is not None
    full = lambda a: pl.BlockSpec(a.shape, lambda i: (0,) * a.ndim)
    row = pl.BlockSpec((tm, RW_WIDTH), lambda i: (i, 0))
    prev, nxt = _halo_specs(tm, R, RW_Z_COLS, lambda i: 0)
    params = [p["mu"], p["w0"], p["w2"], p["a0"], p["a2"], p["g2"], p["kk"], p["ka"], p["rk"], p["ones"]]
    in_specs = [pl.BlockSpec((tm, RW_Z_COLS), lambda i: (i, 0)), prev, nxt] + [full(a) for a in params]
    args = [z, z, z] + params
    if has_vres:
        in_specs += [row] + [full(a) for a in vres]
        args += [v_first] + list(vres)
    return pl.pallas_call(
        functools.partial(_rw_prep_kernel, geom=geom, tm=tm, has_vres=has_vres),
        grid=(R // tm,),
        in_specs=in_specs,
        out_specs=[row] * 11,
        out_shape=[jax.ShapeDtypeStruct((R, RW_WIDTH), F32)] * 11,
        compiler_params=_cparams("parallel"),
    )(*args)


def _pieces(x, passes):
    hi = x.astype(BF16)
    if passes == 1:
        return (hi,)
    return hi, (x - hi.astype(F32)).astype(BF16)


def _mmx(a, b, dot):
    out = dot(a[0], b[0])
    if len(a) > 1:
        out = out + dot(a[1], b[0])
    if len(b) > 1:
        out = out + dot(a[0], b[1])
    return out


def _stack_pair(first_head, x):
    return jnp.concatenate([jnp.where(first_head, x, 0.0), jnp.where(first_head, 0.0, x)], axis=0)


def _fold_pair(x):
    half = x.shape[0] // 2
    return x[:half] + x[half:]


def _rw_chunk_kernel(r_ref, v_ref, kk_ref, lwf_ref, kf_ref, bf_ref, lwb_ref, kb_ref, bb_ref,
                     rrf_o, ylf_o, mf_o, nf_o, ef_o, rrb_o, ylb_o, mb_o, nb_o, eb_o):
    C = RW_CHUNK
    PW = RW_PAIR
    ri = lax.broadcasted_iota(jnp.int32, (PW, PW), 0)
    ci = lax.broadcasted_iota(jnp.int32, (PW, PW), 1)
    eye = ri == ci
    ri, ci = ri & (C - 1), ci & (C - 1)
    first_head = lax.broadcasted_iota(jnp.int32, (1, PW), 1) < RW_HEAD_DIM
    stack = functools.partial(_stack_pair, first_head)
    r, v, kk = r_ref[...], v_ref[...], kk_ref[...]
    jobs = []
    for lw_ref, k_ref, b_ref, outs, before, tot_row in (
            (lwf_ref, kf_ref, bf_ref, (rrf_o, ylf_o, mf_o, nf_o, ef_o), ci < ri, C - 1),
            (lwb_ref, kb_ref, bb_ref, (rrb_o, ylb_o, mb_o, nb_o, eb_o), ci > ri, 0)):
        lw, kd, bd = lw_ref[...], k_ref[...], b_ref[...]
        incl = (before | eye)[:C, :C].astype(BF16)
        c, rest = None, lw
        for _ in range(3):
            piece = rest.astype(BF16)
            rest = rest - piece.astype(F32)
            part = _dot(incl, piece)
            c = part if c is None else c + part
        c_tot = c[tot_row:tot_row + 1, :]
        at = -kk * jnp.exp(c - lw)
        rt = r * jnp.exp(c)
        e_neg = jnp.exp(-c)
        bt, kt = bd * e_neg, kd * e_neg
        e_rest = jnp.exp(c_tot - c)
        bc, kc = bd * e_rest, kd * e_rest
        outs[4][...] = jnp.broadcast_to(jnp.exp(c_tot), (8, RW_WIDTH))
        for p in range(RW_WIDTH // PW):
            ps = slice(p * PW, (p + 1) * PW)
            jobs.append(dict(ps=ps, outs=outs, before=before, incl=before | eye, rt_pair=rt[:, ps],
                             at=stack(at[:, ps]), rt=stack(rt[:, ps]), bt=stack(bt[:, ps]), kt=stack(kt[:, ps]),
                             bc=stack(bc[:, ps]), kc=stack(kc[:, ps]), v=stack(v[:, ps])))
    for j in jobs:
        p = _mmx(_pieces(jnp.concatenate([j["at"], j["rt"]], axis=0), RW_PASSES_LOCAL),
                 _pieces(jnp.concatenate([j["bt"], j["kt"]], axis=0), RW_PASSES_LOCAL), _dot_nt)
        j["a_ab"] = jnp.where(j["before"], p[:PW, :PW], 0.0)
        j["a_ak"] = jnp.where(j["before"], p[:PW, PW:], 0.0)
        j["a_rb"] = jnp.where(j["incl"], p[PW:, :PW], 0.0)
        j["a_rk"] = jnp.where(j["incl"], p[PW:, PW:], 0.0)
        j["vp"] = _pieces(j["v"], RW_PASSES_LOCAL)
    for j in jobs:
        j["w1"] = _mmx(_pieces(j["a_ak"], RW_PASSES_LOCAL), j["vp"], _dot)
        j["tinv"] = jnp.where(eye, 1.0, j["a_ab"])
        j["pw"] = j["a_ab"]
    for _ in range(int(np.log2(C)) - 1):
        for j in jobs:
            pw = _pieces(j["pw"], RW_PASSES_LOCAL)
            j["pw"] = _mmx(pw, pw, _dot)
        for j in jobs:
            j["tinv"] = j["tinv"] + _mmx(_pieces(j["tinv"], RW_PASSES_LOCAL), _pieces(j["pw"], RW_PASSES_LOCAL), _dot)
    for j in jobs:
        tw = _mmx(_pieces(j["tinv"], RW_PASSES_LOCAL),
                  _pieces(jnp.concatenate([j["at"], j["w1"]], axis=1), RW_PASSES_LOCAL), _dot)
        j["tw"] = tw
        j["twp"] = _pieces(tw, RW_PASSES_LOCAL)
    for j in jobs:
        rr_o, yl_o, m_o, n_o, _ = j["outs"]
        ps = j["ps"]
        ry = _mmx(_pieces(j["a_rb"], RW_PASSES_LOCAL), j["twp"], _dot)
        yk = _mmx(_pieces(j["a_rk"], RW_PASSES_LOCAL), j["vp"], _dot)
        rr_o[:, ps] = j["rt_pair"] + _fold_pair(ry[:, :PW])
        yl_o[:, ps] = _fold_pair(ry[:, PW:] + yk)
        m_o[:, ps] = _fold_pair(_mmx(_pieces(j["bc"], RW_PASSES_TRANSITION),
                                     _pieces(j["tw"][:, :PW], RW_PASSES_TRANSITION), _dot_tn))
        uv = jnp.concatenate([j["tw"][:, PW:], j["v"]], axis=0)
        bk = jnp.concatenate([j["bc"], j["kc"]], axis=0)
        n_o[:, ps] = _fold_pair(_mmx(_pieces(uv, RW_PASSES_STATE), _pieces(bk, RW_PASSES_STATE), _dot_tn))


def rw_chunk(arrs):
    R = arrs[0].shape[0]
    blk = pl.BlockSpec((RW_CHUNK, RW_WIDTH), lambda i: (i, 0))
    eblk = pl.BlockSpec((8, RW_WIDTH), lambda i: (i, 0))
    tok = jax.ShapeDtypeStruct((R, RW_WIDTH), F32)
    dec = jax.ShapeDtypeStruct((R // RW_CHUNK * 8, RW_WIDTH), F32)
    return pl.pallas_call(
        _rw_chunk_kernel,
        grid=(R // RW_CHUNK,),
        in_specs=[blk] * 9,
        out_specs=[blk] * 4 + [eblk] + [blk] * 4 + [eblk],
        out_shape=[tok] * 4 + [dec] + [tok] * 4 + [dec],
        compiler_params=_cparams("parallel"),
    )(*arrs)


def _rw_scan_kernel(rrf_ref, ylf_ref, mf_ref, nf_ref, ef_ref, rrb_ref, ylb_ref, mb_ref, nb_ref, eb_ref,
                    yf_o, yb_o, s_ref, *, nchunk):
    C = RW_CHUNK
    PW = RW_PAIR
    first_head = lax.broadcasted_iota(jnp.int32, (1, PW), 1) < RW_HEAD_DIM
    stack = functools.partial(_stack_pair, first_head)

    @pl.when(pl.program_id(1) == 0)
    def _():
        s_ref[...] = jnp.zeros_like(s_ref)

    dirs = ((rrf_ref, ylf_ref, mf_ref, nf_ref, ef_ref, yf_o), (rrb_ref, ylb_ref, mb_ref, nb_ref, eb_ref, yb_o))
    pairs = [slice(p * PW, (p + 1) * PW) for p in range(RW_WIDTH // PW)]
    state = [[s_ref[d, :, ps] for ps in pairs] for d in range(2)]
    for step in range(nchunk):
        for d, (rr_ref, yl_ref, m_ref, n_ref, e_ref, y_o) in enumerate(dirs):
            c = step if d == 0 else nchunk - 1 - step
            rows = slice(c * C, (c + 1) * C)
            for p, ps in enumerate(pairs):
                s = state[d][p]
                sp = _pieces(stack(s), RW_PASSES_STATE)
                y = _mmx(_pieces(stack(rr_ref[rows, ps]), RW_PASSES_STATE), sp, _dot_nt)
                y_o[rows, ps] = _fold_pair(y) + yl_ref[rows, ps]
                sm = _mmx(sp[:RW_PASSES_TRANSITION], _pieces(stack(m_ref[rows, ps]), RW_PASSES_TRANSITION), _dot_nt)
                state[d][p] = s * e_ref[c * 8:c * 8 + 1, ps] + _fold_pair(sm) + n_ref[rows, ps]
    for d in range(2):
        for p, ps in enumerate(pairs):
            s_ref[d, :, ps] = state[d][p]


def rw_scan(geom, chunk_out):
    B, T, L = geom.B, geom.T, geom.L
    blk = L
    nchunk = blk // RW_CHUNK
    nlat = T // blk
    cblk = geom.BT // blk
    fwd_i = lambda b, s: (jnp.where(s == 0, cblk + b, b * nlat + s - 1), 0)
    bwd_i = lambda b, s: (jnp.where(s == 0, cblk + b, b * nlat + nlat - s), 0)
    tok = lambda f: pl.BlockSpec((blk, RW_WIDTH), f)
    dec = lambda f: pl.BlockSpec((8 * nchunk, RW_WIDTH), f)
    return pl.pallas_call(
        functools.partial(_rw_scan_kernel, nchunk=nchunk),
        grid=(B, nlat + 1),
        in_specs=[tok(fwd_i)] * 4 + [dec(fwd_i)] + [tok(bwd_i)] * 4 + [dec(bwd_i)],
        out_specs=[tok(fwd_i), tok(bwd_i)],
        out_shape=[jax.ShapeDtypeStruct((geom.R, RW_WIDTH), F32)] * 2,
        scratch_shapes=[pltpu.VMEM((2, RW_HEAD_DIM, RW_WIDTH), F32)],
        compiler_params=_cparams("parallel", "arbitrary"),
    )(*chunk_out)


def _rw_post_kernel(yf_ref, yb_ref, bonus_ref, g_ref, lng_ref, lnb_ref, ones_ref, o_ref):
    ones = ones_ref[...]
    y = yf_ref[...] + yb_ref[...]
    mean = _head_sum(y, ones) * (1.0 / RW_HEAD_DIM)
    yc = y - mean
    var = _head_sum(yc * yc, ones) * (1.0 / RW_HEAD_DIM)
    y = yc * lax.rsqrt(var + RW_LNX_EPS) * lng_ref[...] + lnb_ref[...]
    o_ref[...] = ((y + bonus_ref[...]) * g_ref[...]).astype(o_ref.dtype)


def rw_post(yf, yb, bonus, g, lnx_g, lnx_b, ones, *, tm):
    R = yf.shape[0]
    row = pl.BlockSpec((tm, RW_WIDTH), lambda i: (i, 0))
    full = lambda a: pl.BlockSpec(a.shape, lambda i: (0,) * a.ndim)
    return pl.pallas_call(
        _rw_post_kernel,
        grid=(R // tm,),
        in_specs=[row] * 4 + [full(lnx_g), full(lnx_b), full(ones)],
        out_specs=row,
        out_shape=jax.ShapeDtypeStruct((R, RW_WIDTH), BF16),
        compiler_params=_cparams("parallel"),
    )(yf, yb, bonus, g, lnx_g, lnx_b, ones)


def _rw_in_cols(w):
    parts = [w[..., :3 * RW_WIDTH]]
    off = 3 * RW_WIDTH
    for n in (RW_DECAY_LORA, RW_DECAY_LORA, RW_AAA_LORA, RW_AAA_LORA, RW_GATE_LORA):
        parts.append(_pad_cols(w[..., off:off + n], LORA_PAD))
        off += n
    return jnp.concatenate(parts, axis=-1)


def _pad_rows(w, n):
    return jnp.pad(w, [(0, 0)] * (w.ndim - 2) + [(0, n - w.shape[-2]), (0, 0)])


def _mla_wq_cols(w):
    w = w.reshape(w.shape[0], MLA_HEADS, MLA_NOPE + MLA_ROPE)
    return _pad_cols(w, MLA_QK_PAD).reshape(w.shape[0], MLA_HEADS * MLA_QK_PAD)


def kernel(x, c, ctx, c_ctx, ada_w, ada_b, norm1_g, w_in, rw_mu, rw_w0, rw_w2, rw_a0, rw_a2, rw_g2, rw_kk, rw_ka,
           rw_rk, rw_lnx_g, rw_lnx_b, rw_v0, rw_v1, rw_v2, wa_sink, mla_qnorm_g, mla_kvnorm_g, mla_w_uq, mla_w_ukv,
           w_branch, w_out, norm2_g, ffn_w_in, ffn_conv_w, ffn_conv_b, ffn_w_out, final_norm_g):
    B, T, D = x.shape
    L = ctx.shape[1]
    depth = w_in.shape[0]
    F = ffn_w_out.shape[1]
    geom = Geom(B, T, L)
    tm = _pick_tile(T, (512, 256, 128))
    assert (B * L) % tm == 0
    tmm = _pick_tile(geom.R, (1280, 1024, 640, 512, 256, 128))

    rw_cols = 3 * RW_WIDTH + 2 * RW_DECAY_LORA + 2 * RW_AAA_LORA + RW_GATE_LORA
    wa_cols = WA_WIDTH + 2 * WA_KV_WIDTH
    mla_cols = MLA_Q_LORA + MLA_KV_LORA + MLA_ROPE
    mla_cols_pad = MLA_Q_LORA + MLA_KV_LORA + 128

    cos_wa, sin_wa = _rope_tables(geom, WA_HEAD_DIM)
    cos_wa, sin_wa = jnp.tile(cos_wa, (1, 2)), jnp.tile(sin_wa, (1, 2))
    cos_m, sin_m = _rope_tables(geom, MLA_ROPE)
    one, zero = jnp.ones((geom.R, MLA_NOPE), F32), jnp.zeros((geom.R, MLA_NOPE), F32)
    cos_mla = jnp.concatenate([one, cos_m, one[:, :64]], axis=-1)
    sin_mla = jnp.concatenate([zero, sin_m, zero[:, :64]], axis=-1)
    lane = np.arange(128)
    ones_blk = jnp.asarray((lane[:, None] // RW_HEAD_DIM) == (lane[None, :] // RW_HEAD_DIM), F32)

    xs = jnp.concatenate([x.reshape(B * T, D), ctx.reshape(B * L, D)], axis=0)
    cvec = jnp.concatenate([c, c_ctx[None, :], jnp.zeros((8 - (B + 1) % 8, D), F32)], axis=0)
    v_first = None
    for l in range(depth):
        need_ctx = l < depth - 1
        mod = ada_modulation(cvec, ada_w[l], ada_b[l])
        mod = [mod[:, k * D:(k + 1) * D].reshape(-1, 1, D) for k in range(6)]

        w = w_in[l]
        tn_in = _pick_tile(3 * D, (768, 512, 256, 128))
        w_rw = _pad_cols(_rw_in_cols(w[:, :rw_cols]), _round_up(RW_Z_COLS, tn_in)).astype(BF16)
        w_wa = _pad_cols(w[:, rw_cols:rw_cols + wa_cols], _round_up(wa_cols, tn_in)).astype(BF16)
        w_mla = _pad_cols(w[:, rw_cols + wa_cols:rw_cols + wa_cols + mla_cols],
                          _round_up(mla_cols_pad, tn_in)).astype(BF16)
        w_gate = w[:, rw_cols + wa_cols + mla_cols:].astype(BF16)
        z_rw, z_wa, z_mla, gates = in_projection(geom, xs, norm1_g[l], mod[0], mod[1], w_rw, w_wa, w_mla, w_gate,
                                                 tm=tmm, tn=tn_in)

        rw_p = dict(
            mu=_rw_in_cols(rw_mu[l][None, :]), w0=rw_w0[l], w2=_pad_rows(rw_w2[l], LORA_PAD), a0=rw_a0[l],
            a2=_pad_rows(rw_a2[l], LORA_PAD), g2=_pad_rows(rw_g2[l], LORA_PAD), kk=rw_kk[l][None, :],
            ka=rw_ka[l][None, :], rk=rw_rk[l].reshape(1, RW_WIDTH), ones=ones_blk)
        vres = None if l == 0 else (rw_v0[l - 1][None, :], rw_v1[l - 1], rw_v2[l - 1])
        r, v, kk, lwf, kf, bf, lwb, kb, bb, g, bonus = rw_prep(geom, z_rw, rw_p, v_first, vres, tm=min(tm, 256))
        if l == 0:
            v_first = v
        yf, yb = rw_scan(geom, rw_chunk((r, v, kk, lwf, kf, bf, lwb, kb, bb)))
        o_a = rw_post(yf, yb, bonus, g, rw_lnx_g[l][None, :], rw_lnx_b[l][None, :], ones_blk, tm=tm)

        q_wa, k_wa, v_wa = wa_prep(z_wa, cos_wa, sin_wa, tm=tm)
        ob_l = wa_attention(geom, q_wa, k_wa, v_wa, wa_sink[l], local=True)
        parts = [ob_l]
        if need_ctx:
            parts.append(wa_attention(geom, q_wa, k_wa, v_wa, wa_sink[l], local=False))
        else:
            parts.append(jnp.zeros((B * L, WA_WIDTH), BF16))
        o_b = jnp.concatenate(parts, axis=0)

        wq = _mla_wq_cols(mla_w_uq[l]).astype(BF16)
        q_m, k_m, v_m = mla_prep(z_mla, mla_qnorm_g[l], mla_kvnorm_g[l], wq, mla_w_ukv[l].astype(BF16),
                                 cos_mla, sin_mla, tm=tm)
        tq = _pick_tile(T, (1024, 512, 256, 128))
        tk = _pick_tile(T, (4096, 2048, 1024, 512, 256, 128))
        parts = [mla_attention(geom, q_m, k_m, v_m, with_latent=True, tq=tq, tk=tk)]
        if need_ctx:
            parts.append(mla_attention(geom, q_m, k_m, v_m, with_latent=False, tq=L, tk=L))
        else:
            parts.append(jnp.zeros((B * L, MLA_HEADS * MLA_V), BF16))
        o_c = jnp.concatenate(parts, axis=0)

        tn_d = _pick_tile(D, (1024, 512, 256, 128))
        y = merge_branches(o_a, o_b, o_c, gates, w_branch[l].astype(BF16), tm=tmm, tn=tn_d)
        xs = matmul_gated_residual(geom, y, w_out[l].astype(BF16), xs, mod[2], tm=tmm, tn=tn_d)

        tf = _pick_tile(F, (512, 256, 128))
        hmid = ffn_in_conv_glu(geom, xs, norm2_g[l], mod[3], mod[4], ffn_w_in[l].astype(BF16), ffn_conv_w[l],
                               ffn_conv_b[l], tm=tmm, tf=tf)
        xs = matmul_gated_residual(geom, hmid, ffn_w_out[l].astype(BF16), xs, mod[5], tm=tmm,
                                   tn=_pick_tile(D, (512, 256, 128)))

    out = final_rmsnorm(xs, final_norm_g, B * T, tm=tm)
    return out.reshape(B, T, D)
```

```python
import functools

import jax
import jax.numpy as jnp
import numpy as np
from jax import lax
from jax.experimental import pallas as pl
from jax.experimental.pallas import tpu as pltpu

F32 = jnp.float32
BF16 = jnp.bfloat16
HIGHEST = lax.Precision.HIGHEST

NORM_EPS = 1e-6
NEG_INF = -1e30
GRID_W = 64
ROPE_BASE = 10000.0

RW_HEADS = 16
RW_HEAD_DIM = 64
RW_WIDTH = RW_HEADS * RW_HEAD_DIM
RW_DECAY_LORA = 96
RW_AAA_LORA = 96
RW_GATE_LORA = 64
RW_LNX_EPS = 64e-5
RW_CHUNK = 64
RW_PAIR = 2 * RW_HEAD_DIM
RW_PASSES_LOCAL = 1
RW_PASSES_STATE = 3
RW_PASSES_TRANSITION = 1
LORA_PAD = 128

WA_HEADS = 16
WA_KV_HEADS = 4
WA_GROUP = WA_HEADS // WA_KV_HEADS
WA_HEAD_DIM = 64
WA_WIDTH = WA_HEADS * WA_HEAD_DIM
WA_KV_WIDTH = WA_KV_HEADS * WA_HEAD_DIM
WA_V_PAD = WA_KV_HEADS * 128
WINDOW = 128
WA_SCALE = WA_HEAD_DIM ** -0.5

MLA_HEADS = 8
MLA_NOPE = 128
MLA_ROPE = 64
MLA_V = 128
MLA_Q_LORA = 512
MLA_KV_LORA = 512
MLA_QK_PAD = 256
MLA_V_PAD = 256
MLA_SCALE = (MLA_NOPE + MLA_ROPE) ** -0.5
MLA_SUB_KEYS = 2048
LOG2_E = 1.4426950408889634

CONV_W = 3
VMEM_LIMIT_BYTES = 56 * 1024 * 1024


def _cparams(*sem):
    return pltpu.CompilerParams(dimension_semantics=sem, vmem_limit_bytes=VMEM_LIMIT_BYTES)


def _dot(a, b, precision=None):
    return jnp.dot(a, b, preferred_element_type=F32, precision=precision)


def _dot_nt(a, b, precision=None):
    return lax.dot_general(a, b, (((1,), (1,)), ((), ())), preferred_element_type=F32, precision=precision)


def _dot_tn(a, b, precision=None):
    return lax.dot_general(a, b, (((0,), (0,)), ((), ())), preferred_element_type=F32, precision=precision)


def _pick_tile(n, candidates):
    for c in candidates:
        if n % c == 0:
            return c
    raise ValueError(f"no tile in {candidates} divides {n}")


def _pad_cols(w, n):
    return jnp.pad(w, [(0, 0)] * (w.ndim - 1) + [(0, n - w.shape[-1])])


def _round_up(n, m):
    return (n + m - 1) // m * m


class Geom:
    def __init__(self, B, T, L):
        assert T & (T - 1) == 0 and L & (L - 1) == 0, "sequence lengths must be powers of two"
        assert T % L == 0 and L % RW_CHUNK == 0 and T % GRID_W == 0
        self.B, self.T, self.L = B, T, L
        self.BT = B * T
        self.R = B * T + B * L


def _select_row_group(geom, row0, tm, tab_ref):
    r = row0 + lax.broadcasted_iota(jnp.int32, (tm, 1), 0)
    out = tab_ref[geom.B]
    for b in range(geom.B):
        out = jnp.where((r >= b * geom.T) & (r < (b + 1) * geom.T), tab_ref[b], out)
    return out


def _seq_edge_masks(geom, row0, tm):
    r = row0 + lax.broadcasted_iota(jnp.int32, (tm, 1), 0)
    is_lat = r < geom.BT
    pos = jnp.where(is_lat, r & (geom.T - 1), (r - geom.BT) & (geom.L - 1))
    last = jnp.where(is_lat, geom.T - 1, geom.L - 1)
    return pos == 0, pos == last


def _shifted_rows(x, prev8, next8, first, last):
    tm = x.shape[0]
    rid = lax.broadcasted_iota(jnp.int32, (tm, 1), 0)
    up = jnp.where(rid == 0, prev8[7:8, :], pltpu.roll(x, 1, axis=0))
    dn = jnp.where(rid == tm - 1, next8[0:1, :], pltpu.roll(x, tm - 1, axis=0))
    return jnp.where(first, 0.0, up), jnp.where(last, 0.0, dn)


def _halo_specs(tm, R, width, col_of):
    nb8 = tm // 8
    prev = pl.BlockSpec((8, width), lambda i, *a: (jnp.maximum(i * nb8 - 1, 0), col_of(i, *a)))
    nxt = pl.BlockSpec((8, width), lambda i, *a: (jnp.minimum((i + 1) * nb8, R // 8 - 1), col_of(i, *a)))
    return prev, nxt


def _ada_kernel(c_ref, w_ref, b_ref, o_ref):
    c = c_ref[...]
    o_ref[...] = _dot(c * jax.nn.sigmoid(c), w_ref[...], HIGHEST) + b_ref[...]


def ada_modulation(cvec, w, b):
    G, D = cvec.shape
    N = w.shape[1]
    tn = _pick_tile(N, (1024, 512, 256, 128))
    return pl.pallas_call(
        _ada_kernel,
        grid=(N // tn,),
        in_specs=[pl.BlockSpec((G, D), lambda j: (0, 0)),
                  pl.BlockSpec((D, tn), lambda j: (0, j)),
                  pl.BlockSpec((1, tn), lambda j: (0, j))],
        out_specs=pl.BlockSpec((G, tn), lambda j: (0, j)),
        out_shape=jax.ShapeDtypeStruct((G, N), F32),
        compiler_params=_cparams("arbitrary"),
    )(cvec, w, b.reshape(1, N))


def _norm_modulate(x, g, sc, sh):
    gain = g * (1.0 + sc)
    return (x * lax.rsqrt(jnp.mean(x * x, axis=-1, keepdims=True) + NORM_EPS) * gain + sh).astype(BF16)


PROLOGUE_ROWS = 256


def _norm_modulate_rows(geom, row0, x_ref, g_ref, sc_ref, sh_ref, h_ref, h_off):
    tm = x_ref.shape[0]
    step = PROLOGUE_ROWS if tm % PROLOGUE_ROWS == 0 else tm
    one_group = geom.T % step == 0 and geom.BT % step == 0
    g = g_ref[...]
    for r in range(0, tm, step):
        if one_group:
            sc, sh = sc_ref[geom.B], sh_ref[geom.B]
            for b in range(geom.B):
                in_b = (row0 + r >= b * geom.T) & (row0 + r < (b + 1) * geom.T)
                sc, sh = jnp.where(in_b, sc_ref[b], sc), jnp.where(in_b, sh_ref[b], sh)
        else:
            sc = _select_row_group(geom, row0 + r, step, sc_ref)
            sh = _select_row_group(geom, row0 + r, step, sh_ref)
        h_ref[h_off + r:h_off + r + step] = _norm_modulate(x_ref[r:r + step], g, sc, sh)


def _in_proj_kernel(x_ref, g_ref, sh_ref, sc_ref, w_ref, zrw_ref, zwa_ref, zmla_ref, gate_ref, h_ref,
                    *, geom, tm, starts):
    j = pl.program_id(1)

    @pl.when(j == 0)
    def _():
        _norm_modulate_rows(geom, pl.program_id(0) * tm, x_ref, g_ref, sc_ref, sh_ref, h_ref, 0)

    acc = _dot(h_ref[...], w_ref[...])
    wa0, mla0, gate0 = starts

    @pl.when(j < wa0)
    def _():
        zrw_ref[...] = acc

    @pl.when((j >= wa0) & (j < mla0))
    def _():
        zwa_ref[...] = acc

    @pl.when((j >= mla0) & (j < gate0))
    def _():
        zmla_ref[...] = acc

    @pl.when(j >= gate0)
    def _():
        gate_ref[...] = jax.nn.sigmoid(acc).astype(gate_ref.dtype)


def in_projection(geom, x, g, shift, scale, w_rw, w_wa, w_mla, w_gate, *, tm, tn):
    R, K = x.shape
    widths = [w.shape[1] for w in (w_rw, w_wa, w_mla, w_gate)]
    assert R % tm == 0 and all(n % tn == 0 for n in widths)
    tiles = [n // tn for n in widths]
    starts = tuple(int(v) for v in np.cumsum(tiles)[:3])
    w_all = jnp.concatenate([w_rw, w_wa, w_mla, w_gate], axis=1)
    full = lambda a: pl.BlockSpec(a.shape, lambda i, j: (0,) * a.ndim)
    out_spec = lambda first, n: pl.BlockSpec((tm, tn), lambda i, j: (i, jnp.clip(j - first, 0, n - 1)))
    firsts = (0,) + starts
    return pl.pallas_call(
        functools.partial(_in_proj_kernel, geom=geom, tm=tm, starts=starts),
        grid=(R // tm, sum(tiles)),
        in_specs=[pl.BlockSpec((tm, K), lambda i, j: (i, 0), pipeline_mode=pl.Buffered(1)),
                  pl.BlockSpec((1, K), lambda i, j: (0, 0)),
                  full(shift), full(scale),
                  pl.BlockSpec((K, tn), lambda i, j: (0, j))],
        out_specs=[out_spec(f, n) for f, n in zip(firsts, tiles)],
        out_shape=[jax.ShapeDtypeStruct((R, widths[0]), F32), jax.ShapeDtypeStruct((R, widths[1]), F32),
                   jax.ShapeDtypeStruct((R, widths[2]), F32), jax.ShapeDtypeStruct((R, widths[3]), BF16)],
        scratch_shapes=[pltpu.VMEM((tm, K), BF16)],
        compiler_params=_cparams("parallel", "arbitrary"),
    )(x, g.reshape(1, K), shift, scale, w_all)


def _mm_resid_kernel(y_ref, w_ref, r_ref, gate_ref, o_ref, *, geom, tm):
    gate = _select_row_group(geom, pl.program_id(0) * tm, tm, gate_ref)
    o_ref[...] = r_ref[...] + gate * _dot(y_ref[...], w_ref[...])


def matmul_gated_residual(geom, y, w, resid, gate, *, tm, tn):
    R, K = y.shape
    N = w.shape[1]
    G = gate.shape[0]
    assert R % tm == 0 and N % tn == 0
    return pl.pallas_call(
        functools.partial(_mm_resid_kernel, geom=geom, tm=tm),
        grid=(R // tm, N // tn),
        in_specs=[pl.BlockSpec((tm, K), lambda i, j: (i, 0)),
                  pl.BlockSpec((K, tn), lambda i, j: (0, j)),
                  pl.BlockSpec((tm, tn), lambda i, j: (i, j)),
                  pl.BlockSpec((G, 1, tn), lambda i, j: (0, 0, j))],
        out_specs=pl.BlockSpec((tm, tn), lambda i, j: (i, j)),
        out_shape=jax.ShapeDtypeStruct((R, N), F32),
        compiler_params=_cparams("parallel", "arbitrary"),
    )(y, w, resid, gate)


FFN_HALO = 16


def _ffn_in_kernel(x_ref, xp_ref, xn_ref, g_ref, sh_ref, sc_ref, wg_ref, wu_ref, cw_ref, cb_ref, o_ref, h_ref,
                   *, geom, tm):
    row0 = pl.program_id(0) * tm
    H = FFN_HALO

    @pl.when(pl.program_id(1) == 0)
    def _():
        _norm_modulate_rows(geom, row0, xp_ref, g_ref, sc_ref, sh_ref, h_ref, 0)
        _norm_modulate_rows(geom, row0, x_ref, g_ref, sc_ref, sh_ref, h_ref, H)
        _norm_modulate_rows(geom, row0 + tm - H, xn_ref, g_ref, sc_ref, sh_ref, h_ref, H + tm)

    first, last = _seq_edge_masks(geom, row0, tm)
    gt = _dot(h_ref[...], wg_ref[...])
    u = _dot(h_ref[H:H + tm], wu_ref[...])
    up = jnp.where(first, 0.0, pltpu.roll(gt, 1, axis=0)[H:H + tm])
    dn = jnp.where(last, 0.0, pltpu.roll(gt, tm + 2 * H - 1, axis=0)[H:H + tm])
    cw = cw_ref[...]
    conv = cb_ref[...] + up * cw[0:1, :]
    conv = conv + gt[H:H + tm] * cw[1:2, :]
    conv = conv + dn * cw[2:3, :]
    o_ref[...] = (jax.nn.gelu(conv, approximate=True) * u).astype(o_ref.dtype)


def ffn_in_conv_glu(geom, x, g, shift, scale, w_in, conv_w, conv_b, *, tm, tf):
    R, K = x.shape
    F = w_in.shape[1] // 2
    nj = F // tf
    nbh = tm // FFN_HALO
    assert R % tm == 0 and F % tf == 0 and tm % FFN_HALO == 0
    full = lambda a: pl.BlockSpec(a.shape, lambda i, j: (0,) * a.ndim)
    return pl.pallas_call(
        functools.partial(_ffn_in_kernel, geom=geom, tm=tm),
        grid=(R // tm, nj),
        in_specs=[pl.BlockSpec((tm, K), lambda i, j: (i, 0), pipeline_mode=pl.Buffered(1)),
                  pl.BlockSpec((FFN_HALO, K), lambda i, j: (jnp.maximum(i * nbh - 1, 0), 0)),
                  pl.BlockSpec((FFN_HALO, K), lambda i, j: (jnp.minimum((i + 1) * nbh, R // FFN_HALO - 1), 0)),
                  pl.BlockSpec((1, K), lambda i, j: (0, 0)),
                  full(shift), full(scale),
                  pl.BlockSpec((K, tf), lambda i, j: (0, j)),
                  pl.BlockSpec((K, tf), lambda i, j: (0, nj + j)),
                  pl.BlockSpec((CONV_W, tf), lambda i, j: (0, j)),
                  pl.BlockSpec((1, tf), lambda i, j: (0, j))],
        out_specs=pl.BlockSpec((tm, tf), lambda i, j: (i, j)),
        out_shape=jax.ShapeDtypeStruct((R, F), BF16),
        scratch_shapes=[pltpu.VMEM((tm + 2 * FFN_HALO, K), BF16)],
        compiler_params=_cparams("parallel", "arbitrary"),
    )(x, x, x, g.reshape(1, K), shift, scale, w_in, w_in, conv_w, conv_b.reshape(1, F))


def _merge_kernel(oa_ref, ob_ref, oc_ref, ga_ref, gb_ref, gc_ref, w_ref, o_ref):
    y = ga_ref[...] * _dot(oa_ref[...], w_ref[0])
    y = y + gb_ref[...] * _dot(ob_ref[...], w_ref[1])
    y = y + gc_ref[...] * _dot(oc_ref[...], w_ref[2])
    o_ref[...] = y.astype(o_ref.dtype)


def merge_branches(oa, ob, oc, gates, wb, *, tm, tn):
    R, K = oa.shape
    D = wb.shape[2]
    nj = D // tn
    bspec = pl.BlockSpec((tm, K), lambda i, j: (i, 0))
    gspec = lambda k: pl.BlockSpec((tm, tn), lambda i, j: (i, k * nj + j))
    return pl.pallas_call(
        _merge_kernel,
        grid=(R // tm, nj),
        in_specs=[bspec, bspec, bspec, gspec(0), gspec(1), gspec(2),
                  pl.BlockSpec((3, K, tn), lambda i, j: (0, 0, j))],
        out_specs=pl.BlockSpec((tm, tn), lambda i, j: (i, j)),
        out_shape=jax.ShapeDtypeStruct((R, D), BF16),
        compiler_params=_cparams("parallel", "arbitrary"),
    )(oa, ob, oc, gates, gates, gates, wb)


def _rmsnorm_kernel(x_ref, g_ref, o_ref):
    x = x_ref[...]
    o_ref[...] = x * lax.rsqrt(jnp.mean(x * x, axis=-1, keepdims=True) + NORM_EPS) * g_ref[...]


def final_rmsnorm(x, g, rows, *, tm):
    D = x.shape[1]
    return pl.pallas_call(
        _rmsnorm_kernel,
        grid=(rows // tm,),
        in_specs=[pl.BlockSpec((tm, D), lambda i: (i, 0)), pl.BlockSpec((1, D), lambda i: (0, 0))],
        out_specs=pl.BlockSpec((tm, D), lambda i: (i, 0)),
        out_shape=jax.ShapeDtypeStruct((rows, D), F32),
        compiler_params=_cparams("parallel"),
    )(x, g.reshape(1, D))


def _rot_half64(z):
    n = z.shape[-1]
    lane = lax.broadcasted_iota(jnp.int32, z.shape, z.ndim - 1)
    return jnp.where((lane & 63) < 32, pltpu.roll(z, n - 32, axis=z.ndim - 1), pltpu.roll(z, 32, axis=z.ndim - 1))


def _rope_tables(geom, dim):
    nf = dim // 4
    inv = ROPE_BASE ** (-jnp.arange(nf, dtype=F32) / nf)
    rows = geom.T // GRID_W
    row = jnp.repeat(jnp.arange(rows, dtype=F32), GRID_W)
    col = jnp.tile(jnp.arange(GRID_W, dtype=F32), rows)
    ang = jnp.concatenate([row[:, None] * inv, col[:, None] * inv], axis=-1)
    cos, sin = jnp.cos(ang), jnp.sin(ang)
    cos_t = jnp.concatenate([cos, cos], axis=-1)
    sin_t = jnp.concatenate([-sin, sin], axis=-1)
    nctx = geom.B * geom.L
    cos_f = jnp.concatenate([jnp.tile(cos_t, (geom.B, 1)), jnp.ones((nctx, dim), F32)], axis=0)
    sin_f = jnp.concatenate([jnp.tile(sin_t, (geom.B, 1)), jnp.zeros((nctx, dim), F32)], axis=0)
    return cos_f, sin_f


def _wa_prep_kernel(z_ref, cos_ref, sin_ref, q_ref, k_ref, v_ref):
    cos = cos_ref[...]
    sin = sin_ref[...]
    for c in range(WA_WIDTH // 128):
        z = z_ref[:, c * 128:(c + 1) * 128]
        q_ref[:, c * 128:(c + 1) * 128] = ((z * cos + _rot_half64(z) * sin) * (WA_SCALE * LOG2_E)).astype(BF16)
    for c in range(WA_KV_WIDTH // 128):
        z = z_ref[:, WA_WIDTH + c * 128:WA_WIDTH + (c + 1) * 128]
        k_ref[:, c * 128:(c + 1) * 128] = (z * cos + _rot_half64(z) * sin).astype(BF16)
    v_ref[...] = jnp.ones(v_ref.shape, BF16)
    for g in range(WA_KV_HEADS):
        src = WA_WIDTH + WA_KV_WIDTH + g * WA_HEAD_DIM
        v_ref[:, g * 128:g * 128 + WA_HEAD_DIM] = z_ref[:, src:src + WA_HEAD_DIM].astype(BF16)


def wa_prep(z, cos, sin, *, tm):
    R = z.shape[0]
    row = lambda w: pl.BlockSpec((tm, w), lambda i: (i, 0))
    return pl.pallas_call(
        _wa_prep_kernel,
        grid=(R // tm,),
        in_specs=[row(WA_WIDTH + 2 * WA_KV_WIDTH), row(128), row(128)],
        out_specs=[row(WA_WIDTH), row(WA_KV_WIDTH), row(WA_V_PAD)],
        out_shape=[jax.ShapeDtypeStruct((R, WA_WIDTH), BF16),
                   jax.ShapeDtypeStruct((R, WA_KV_WIDTH), BF16),
                   jax.ShapeDtypeStruct((R, WA_V_PAD), BF16)],
        compiler_params=_cparams("parallel"),
    )(z, cos, sin)


def _wa_attn_kernel(*refs, local, nqb, tq):
    if local:
        sink_ref, q_ref, kp_ref, kc_ref, kn_ref, vp_ref, vc_ref, vn_ref, kx_ref, vx_ref, o_ref = refs
    else:
        sink_ref, q_ref, kx_ref, vx_ref, o_ref = refs
    i = pl.program_id(1)
    nk_ctx = kx_ref.shape[0]
    rows = WA_GROUP * tq
    qpos = lax.broadcasted_iota(jnp.int32, (rows, 1), 0) & (tq - 1)
    head_in_group = lax.broadcasted_iota(jnp.int32, (rows, 1), 0) >> int(np.log2(tq))
    if local:
        off_prev = jnp.where(i > 0, 0, tq)
        off_next = jnp.where(i < nqb - 1, 0, tq)
        j = lax.broadcasted_iota(jnp.int32, (1, 3 * tq + nk_ctx), 1)
        valid = ((j >= tq) & (j < 2 * tq)) | (j >= 3 * tq)
        valid = valid | ((j < tq) & (j >= qpos + off_prev))
        valid = valid | ((j >= 2 * tq) & (j < 3 * tq) & ((j - 2 * tq) <= qpos - off_next))
    for g in range(WA_KV_HEADS):
        ks = slice(g * WA_HEAD_DIM, (g + 1) * WA_HEAD_DIM)
        qg = jnp.concatenate(
            [q_ref[:, (g * WA_GROUP + a) * WA_HEAD_DIM:(g * WA_GROUP + a + 1) * WA_HEAD_DIM] for a in range(WA_GROUP)],
            axis=0)
        vs = slice(g * 128, (g + 1) * 128)
        sink = jnp.zeros((rows, 1), F32)
        for a in range(WA_GROUP):
            sink = jnp.where(head_in_group == a, sink_ref[g * WA_GROUP + a] * LOG2_E, sink)
        if local:
            kcat = jnp.concatenate([kp_ref[:, ks], kc_ref[:, ks], kn_ref[:, ks], kx_ref[:, ks]], axis=0)
            vcat = jnp.concatenate([vp_ref[:, vs], vc_ref[:, vs], vn_ref[:, vs], vx_ref[:, vs]], axis=0)
        else:
            kcat, vcat = kx_ref[:, ks], vx_ref[:, vs]
        s = _dot_nt(qg, kcat)
        if local:
            s = jnp.where(valid, s, NEG_INF)
        m = jnp.maximum(jnp.max(s, axis=-1, keepdims=True), sink)
        o = _dot(jnp.exp2(s - m).astype(BF16), vcat)
        o = o[:, :WA_HEAD_DIM] / (o[:, WA_HEAD_DIM:] + jnp.exp2(sink - m))
        for a in range(WA_GROUP):
            h = g * WA_GROUP + a
            o_ref[:, h * WA_HEAD_DIM:(h + 1) * WA_HEAD_DIM] = o[a * tq:(a + 1) * tq].astype(o_ref.dtype)


def wa_attention(geom, q, k, v, sink, *, local):
    B, T, L = geom.B, geom.T, geom.L
    sink_spec = pl.BlockSpec(memory_space=pltpu.SMEM)
    ctx_spec = lambda w: pl.BlockSpec((L, w), lambda b, i: (geom.BT // L + b, 0))
    if local:
        tq = WINDOW
        nqb = T // tq
        kv = lambda f, w: pl.BlockSpec((tq, w), lambda b, i: (b * nqb + f(i), 0))
        prev = lambda i: jnp.maximum(i - 1, 0)
        cur = lambda i: i
        nxt = lambda i: jnp.minimum(i + 1, nqb - 1)
        in_specs = [sink_spec, pl.BlockSpec((tq, WA_WIDTH), lambda b, i: (b * nqb + i, 0)),
                    kv(prev, WA_KV_WIDTH), kv(cur, WA_KV_WIDTH), kv(nxt, WA_KV_WIDTH),
                    kv(prev, WA_V_PAD), kv(cur, WA_V_PAD), kv(nxt, WA_V_PAD),
                    ctx_spec(WA_KV_WIDTH), ctx_spec(WA_V_PAD)]
        args = (sink, q, k, k, k, v, v, v, k, v)
        out_rows, out_spec = geom.BT, pl.BlockSpec((tq, WA_WIDTH), lambda b, i: (b * nqb + i, 0))
    else:
        tq, nqb = L, 1
        in_specs = [sink_spec, pl.BlockSpec((tq, WA_WIDTH), lambda b, i: (geom.BT // L + b, 0)),
                    ctx_spec(WA_KV_WIDTH), ctx_spec(WA_V_PAD)]
        args = (sink, q, k, v)
        out_rows, out_spec = B * L, pl.BlockSpec((tq, WA_WIDTH), lambda b, i: (b, 0))
    return pl.pallas_call(
        functools.partial(_wa_attn_kernel, local=local, nqb=nqb, tq=tq),
        grid=(B, nqb),
        in_specs=in_specs,
        out_specs=out_spec,
        out_shape=jax.ShapeDtypeStruct((out_rows, WA_WIDTH), BF16),
        compiler_params=_cparams("parallel", "arbitrary"),
    )(*args)


def _mla_prep_kernel(z_ref, qg_ref, kvg_ref, wq_ref, wkv_ref, cos_ref, sin_ref, q_ref, k_ref, v_ref):
    def norm(x, g):
        return (x * lax.rsqrt(jnp.mean(x * x, axis=-1, keepdims=True) + NORM_EPS) * g).astype(BF16)

    cos = cos_ref[...]
    sin = sin_ref[...]
    q = _dot(norm(z_ref[:, :MLA_Q_LORA], qg_ref[...]), wq_ref[...])
    kv = _dot(norm(z_ref[:, MLA_Q_LORA:MLA_Q_LORA + MLA_KV_LORA], kvg_ref[...]), wkv_ref[...])
    kr = z_ref[:, MLA_Q_LORA + MLA_KV_LORA:MLA_Q_LORA + MLA_KV_LORA + 128]
    kr = (kr * cos[:, 128:] + _rot_half64(kr) * sin[:, 128:]).astype(BF16)
    for h in range(MLA_HEADS):
        qh = q[:, h * MLA_QK_PAD:(h + 1) * MLA_QK_PAD]
        q_ref[h] = ((qh * cos + _rot_half64(qh) * sin) * (MLA_SCALE * LOG2_E)).astype(BF16)
        k_ref[h, :, :MLA_NOPE] = kv[:, h * 256:h * 256 + MLA_NOPE].astype(BF16)
        k_ref[h, :, MLA_NOPE:] = kr
        v_ref[h, :, :MLA_V] = kv[:, h * 256 + MLA_NOPE:(h + 1) * 256].astype(BF16)
        v_ref[h, :, MLA_V:] = jnp.ones((kv.shape[0], MLA_V_PAD - MLA_V), BF16)


def mla_prep(z, qnorm_g, kvnorm_g, wq, wkv, cos, sin, *, tm):
    R, Z = z.shape
    full = lambda a: pl.BlockSpec(a.shape, lambda i: (0,) * a.ndim)
    qg, kvg = qnorm_g.reshape(1, -1), kvnorm_g.reshape(1, -1)
    hd = lambda w: pl.BlockSpec((MLA_HEADS, tm, w), lambda i: (0, i, 0))
    return pl.pallas_call(
        _mla_prep_kernel,
        grid=(R // tm,),
        in_specs=[pl.BlockSpec((tm, Z), lambda i: (i, 0)), full(qg), full(kvg), full(wq), full(wkv),
                  pl.BlockSpec((tm, MLA_QK_PAD), lambda i: (i, 0)), pl.BlockSpec((tm, MLA_QK_PAD), lambda i: (i, 0))],
        out_specs=[hd(MLA_QK_PAD), hd(MLA_QK_PAD), hd(MLA_V_PAD)],
        out_shape=[jax.ShapeDtypeStruct((MLA_HEADS, R, MLA_QK_PAD), BF16),
                   jax.ShapeDtypeStruct((MLA_HEADS, R, MLA_QK_PAD), BF16),
                   jax.ShapeDtypeStruct((MLA_HEADS, R, MLA_V_PAD), BF16)],
        compiler_params=_cparams("parallel"),
    )(z, qg, kvg, wq, wkv, cos, sin)


def _mla_flash_kernel(*refs, with_latent, sub):
    if with_latent:
        q_ref, kx_ref, vx_ref, k_ref, v_ref, o_ref, m_ref, acc_ref = refs
    else:
        q_ref, kx_ref, vx_ref, o_ref, m_ref, acc_ref = refs
    ki = pl.program_id(3)
    q = q_ref[0]

    def update(s, v, m_old, acc_old):
        cols = [s[:, c * 128:(c + 1) * 128] for c in range(s.shape[1] // 128)]
        mx = functools.reduce(jnp.maximum, cols)
        m_new = jnp.maximum(m_old, jnp.max(mx, axis=-1, keepdims=True))
        alpha = jnp.exp2(m_old - m_new)
        p = jnp.concatenate([jnp.exp2(c - m_new).astype(BF16) for c in cols], axis=-1)
        return m_new, jnp.concatenate([alpha, alpha], axis=-1) * acc_old + _dot(p, v)

    @pl.when(ki == 0)
    def _():
        tq = q.shape[0]
        m, acc = update(_dot_nt(q, kx_ref[0]), vx_ref[0], jnp.full((tq, 128), NEG_INF, F32),
                        jnp.zeros((tq, MLA_V_PAD), F32))
        m_ref[...], acc_ref[...] = m, acc

    if with_latent:
        nsub = k_ref.shape[1] // sub
        m, acc = m_ref[...], acc_ref[...]
        s_next = _dot_nt(q, k_ref[0, 0:sub, :])
        for j in range(nsub):
            s = s_next
            if j + 1 < nsub:
                s_next = _dot_nt(q, k_ref[0, (j + 1) * sub:(j + 2) * sub, :])
            m, acc = update(s, v_ref[0, j * sub:(j + 1) * sub, :], m, acc)
        m_ref[...], acc_ref[...] = m, acc

    @pl.when(ki == pl.num_programs(3) - 1)
    def _():
        acc = acc_ref[...]
        o_ref[...] = (acc[:, :MLA_V] / acc[:, MLA_V:]).astype(o_ref.dtype)


def mla_attention(geom, q, k, v, *, with_latent, tq, tk):
    B, T, L = geom.B, geom.T, geom.L
    cblk = geom.BT // L
    ctx_k = pl.BlockSpec((1, L, MLA_QK_PAD), lambda b, h, qi, ki: (h, cblk + b, 0))
    ctx_v = pl.BlockSpec((1, L, MLA_V_PAD), lambda b, h, qi, ki: (h, cblk + b, 0))
    if with_latent:
        nq, nk = T // tq, T // tk
        in_specs = [pl.BlockSpec((1, tq, MLA_QK_PAD), lambda b, h, qi, ki: (h, b * nq + qi, 0)), ctx_k, ctx_v,
                    pl.BlockSpec((1, tk, MLA_QK_PAD), lambda b, h, qi, ki: (h, b * nk + ki, 0)),
                    pl.BlockSpec((1, tk, MLA_V_PAD), lambda b, h, qi, ki: (h, b * nk + ki, 0))]
        args = (q, k, v, k, v)
        out_rows, out_spec = geom.BT, pl.BlockSpec((tq, MLA_V), lambda b, h, qi, ki: (b * nq + qi, h))
    else:
        tq, nq, nk = L, 1, 1
        in_specs = [pl.BlockSpec((1, tq, MLA_QK_PAD), lambda b, h, qi, ki: (h, cblk + b, 0)), ctx_k, ctx_v]
        args = (q, k, v)
        out_rows, out_spec = B * L, pl.BlockSpec((tq, MLA_V), lambda b, h, qi, ki: (b, h))
    return pl.pallas_call(
        functools.partial(_mla_flash_kernel, with_latent=with_latent, sub=min(MLA_SUB_KEYS, tk)),
        grid=(B, MLA_HEADS, nq, nk),
        in_specs=in_specs,
        out_specs=out_spec,
        out_shape=jax.ShapeDtypeStruct((out_rows, MLA_HEADS * MLA_V), BF16),
        scratch_shapes=[pltpu.VMEM((tq, 128), F32), pltpu.VMEM((tq, MLA_V_PAD), F32)],
        compiler_params=_cparams("parallel", "parallel", "parallel", "arbitrary"),
    )(*args)


RW_Z_R, RW_Z_K, RW_Z_V = 0, RW_WIDTH, 2 * RW_WIDTH
RW_Z_LORA = 3 * RW_WIDTH
RW_Z_COLS = 3 * RW_WIDTH + 5 * LORA_PAD


def _dot3(a, b):
    return _mmx(_pieces(a, 3), _pieces(b, 3), _dot)


def _head_sum(x, ones):
    ones = (ones.astype(BF16),)
    return jnp.concatenate(
        [_mmx(_pieces(x[:, c * 128:(c + 1) * 128], 3), ones, _dot) for c in range(x.shape[1] // 128)], axis=-1)


def _rw_prep_kernel(*refs, geom, tm, has_vres):
    if has_vres:
        (z_ref, zp_ref, zn_ref, mu_ref, w0_ref, w2_ref, a0_ref, a2_ref, g2_ref, kk_ref, ka_ref, rk_ref, ones_ref,
         vf_ref, v0_ref, v1_ref, v2_ref,
         r_o, v_o, kk_o, lwf_o, kf_o, bf_o, lwb_o, kb_o, bb_o, g_o, bonus_o) = refs
    else:
        (z_ref, zp_ref, zn_ref, mu_ref, w0_ref, w2_ref, a0_ref, a2_ref, g2_ref, kk_ref, ka_ref, rk_ref, ones_ref,
         r_o, v_o, kk_o, lwf_o, kf_o, bf_o, lwb_o, kb_o, bb_o, g_o, bonus_o) = refs
    first, last = _seq_edge_masks(geom, pl.program_id(0) * tm, tm)
    z = z_ref[...]
    up, dn = _shifted_rows(z, zp_ref[...], zn_ref[...], first, last)
    z = z + mu_ref[...] * (0.5 * (up + dn) - z)
    r = z[:, RW_Z_R:RW_Z_R + RW_WIDTH]
    k = z[:, RW_Z_K:RW_Z_K + RW_WIDTH]
    v = z[:, RW_Z_V:RW_Z_V + RW_WIDTH]
    lora = lambda n: z[:, RW_Z_LORA + n * LORA_PAD:RW_Z_LORA + (n + 1) * LORA_PAD]
    ones = ones_ref[...]
    if has_vres:
        mix = jax.nn.sigmoid(v0_ref[...] + _dot3(_dot3(v, v1_ref[...]), v2_ref[...]))
        v = v + (vf_ref[...] - v) * mix
    g_o[...] = _dot3(jax.nn.sigmoid(lora(4)), g2_ref[...])
    kk = k * kk_ref[...]
    kk = kk / jnp.maximum(jnp.sqrt(_head_sum(kk * kk, ones)), 1e-12)
    ksum = None
    for d, (lw_o, k_o, b_o) in enumerate(((lwf_o, kf_o, bf_o), (lwb_o, kb_o, bb_o))):
        x = -(w0_ref[d:d + 1, :] + _dot3(jnp.tanh(lora(d)), w2_ref[d]))
        softplus = jnp.maximum(x, 0.0) + jnp.log1p(jnp.exp(-jnp.abs(x)))
        lw_o[...] = -jnp.exp(-softplus - 0.5)
        a = jax.nn.sigmoid(a0_ref[d:d + 1, :] + _dot3(lora(2 + d), a2_ref[d]))
        kd = k * (1.0 + (a - 1.0) * ka_ref[...])
        k_o[...] = kd
        b_o[...] = kk * a
        ksum = kd if ksum is None else ksum + kd
    r_o[...] = r
    v_o[...] = v
    kk_o[...] = kk
    bonus_o[...] = _head_sum(r * ksum * rk_ref[...], ones) * v


def rw_prep(geom, z, p, v_first, vres, *, tm):
    R = z.shape[0]
    has_vres = vres is not None
    full = lambda a: pl.BlockSpec(a.shape, lambda i: (0,) * a.ndim)
    row = pl.BlockSpec((tm, RW_WIDTH), lambda i: (i, 0))
    prev, nxt = _halo_specs(tm, R, RW_Z_COLS, lambda i: 0)
    params = [p["mu"], p["w0"], p["w2"], p["a0"], p["a2"], p["g2"], p["kk"], p["ka"], p["rk"], p["ones"]]
    in_specs = [pl.BlockSpec((tm, RW_Z_COLS), lambda i: (i, 0)), prev, nxt] + [full(a) for a in params]
    args = [z, z, z] + params
    if has_vres:
        in_specs += [row] + [full(a) for a in vres]
        args += [v_first] + list(vres)
    return pl.pallas_call(
        functools.partial(_rw_prep_kernel, geom=geom, tm=tm, has_vres=has_vres),
        grid=(R // tm,),
        in_specs=in_specs,
        out_specs=[row] * 11,
        out_shape=[jax.ShapeDtypeStruct((R, RW_WIDTH), F32)] * 11,
        compiler_params=_cparams("parallel"),
    )(*args)


def _pieces(x, passes):
    hi = x.astype(BF16)
    if passes == 1:
        return (hi,)
    return hi, (x - hi.astype(F32)).astype(BF16)


def _mmx(a, b, dot):
    out = dot(a[0], b[0])
    if len(a) > 1:
        out = out + dot(a[1], b[0])
    if len(b) > 1:
        out = out + dot(a[0], b[1])
    return out


def _stack_pair(first_head, x):
    return jnp.concatenate([jnp.where(first_head, x, 0.0), jnp.where(first_head, 0.0, x)], axis=0)


def _fold_pair(x):
    half = x.shape[0] // 2
    return x[:half] + x[half:]


def _rw_fused_scan_kernel(rf_ref, vf_ref, kkf_ref, lwf_ref, kf_ref, bf_ref,
                          rb_ref, vb_ref, kkb_ref, lwb_ref, kb_ref, bb_ref, yf_o, yb_o, s_ref, *, nchunk):
    C = RW_CHUNK
    PW = RW_PAIR
    ri = lax.broadcasted_iota(jnp.int32, (PW, PW), 0)
    ci = lax.broadcasted_iota(jnp.int32, (PW, PW), 1)
    eye = ri == ci
    ri, ci = ri & (C - 1), ci & (C - 1)
    first_head = lax.broadcasted_iota(jnp.int32, (1, PW), 1) < RW_HEAD_DIM
    stack = functools.partial(_stack_pair, first_head)

    @pl.when(pl.program_id(1) == 0)
    def _():
        s_ref[...] = jnp.zeros_like(s_ref)

    dirs = ((rf_ref, vf_ref, kkf_ref, lwf_ref, kf_ref, bf_ref, yf_o, ci < ri, C - 1),
            (rb_ref, vb_ref, kkb_ref, lwb_ref, kb_ref, bb_ref, yb_o, ci > ri, 0))

    def chunk_step(step, carry):
        jobs = []
        for d, (r_ref, v_ref, kk_ref, lw_ref, k_ref, b_ref, y_o, before, tot_row) in enumerate(dirs):
            chunk = step if d == 0 else nchunk - 1 - step
            rows = pl.ds(pl.multiple_of(chunk * C, C), C)
            r, v, kk = r_ref[rows, :], v_ref[rows, :], kk_ref[rows, :]
            lw, kd, bd = lw_ref[rows, :], k_ref[rows, :], b_ref[rows, :]
            incl = (before | eye)[:C, :C].astype(BF16)
            cum, rest = None, lw
            for _ in range(3):
                piece = rest.astype(BF16)
                rest = rest - piece.astype(F32)
                part = _dot(incl, piece)
                cum = part if cum is None else cum + part
            c_tot = cum[tot_row:tot_row + 1, :]
            at = -kk * jnp.exp(cum - lw)
            rt = r * jnp.exp(cum)
            e_neg = jnp.exp(-cum)
            bt, kt = bd * e_neg, kd * e_neg
            e_rest = jnp.exp(c_tot - cum)
            bc, kc = bd * e_rest, kd * e_rest
            e_tot = jnp.exp(c_tot)
            for p in range(RW_WIDTH // PW):
                ps = slice(p * PW, (p + 1) * PW)
                jobs.append(dict(d=d, ps=ps, rows=rows, y_o=y_o, before=before, incl=before | eye, e=e_tot[:, ps],
                                 at=stack(at[:, ps]), rt=stack(rt[:, ps]), bt=stack(bt[:, ps]), kt=stack(kt[:, ps]),
                                 bc=stack(bc[:, ps]), kc=stack(kc[:, ps]), v=stack(v[:, ps])))
        for j in jobs:
            p = _mmx(_pieces(jnp.concatenate([j["at"], j["rt"]], axis=0), RW_PASSES_LOCAL),
                     _pieces(jnp.concatenate([j["bt"], j["kt"]], axis=0), RW_PASSES_LOCAL), _dot_nt)
            j["a_ab"] = jnp.where(j["before"], p[:PW, :PW], 0.0)
            j["a_ak"] = jnp.where(j["before"], p[:PW, PW:], 0.0)
            j["a_rb"] = jnp.where(j["incl"], p[PW:, :PW], 0.0)
            j["a_rk"] = jnp.where(j["incl"], p[PW:, PW:], 0.0)
            j["vp"] = _pieces(j["v"], RW_PASSES_LOCAL)
        for j in jobs:
            j["w1"] = _mmx(_pieces(j["a_ak"], RW_PASSES_LOCAL), j["vp"], _dot)
            j["tinv"] = jnp.where(eye, 1.0, j["a_ab"])
            j["pw"] = j["a_ab"]
        for _ in range(int(np.log2(C)) - 1):
            for j in jobs:
                pw = _pieces(j["pw"], RW_PASSES_LOCAL)
                j["pw"] = _mmx(pw, pw, _dot)
            for j in jobs:
                j["tinv"] = j["tinv"] + _mmx(_pieces(j["tinv"], RW_PASSES_LOCAL), _pieces(j["pw"], RW_PASSES_LOCAL),
                                             _dot)
        for j in jobs:
            tw = _mmx(_pieces(j["tinv"], RW_PASSES_LOCAL),
                      _pieces(jnp.concatenate([j["at"], j["w1"]], axis=1), RW_PASSES_LOCAL), _dot)
            j["tw"] = tw
            j["twp"] = _pieces(tw, RW_PASSES_LOCAL)
        for j in jobs:
            ry = _mmx(_pieces(j["a_rb"], RW_PASSES_LOCAL), j["twp"], _dot)
            yk = _mmx(_pieces(j["a_rk"], RW_PASSES_LOCAL), j["vp"], _dot)
            j["rr"] = j["rt"] + ry[:, :PW]
            j["yl"] = _fold_pair(ry[:, PW:] + yk)
            j["m"] = _mmx(_pieces(j["bc"], RW_PASSES_TRANSITION), _pieces(j["tw"][:, :PW], RW_PASSES_TRANSITION),
                          _dot_tn)
            uv = jnp.concatenate([j["tw"][:, PW:], j["v"]], axis=0)
            bk = jnp.concatenate([j["bc"], j["kc"]], axis=0)
            j["n"] = _fold_pair(_mmx(_pieces(uv, RW_PASSES_STATE), _pieces(bk, RW_PASSES_STATE), _dot_tn))
        for j in jobs:
            s = s_ref[j["d"], :, j["ps"]]
            sp = _pieces(stack(s), RW_PASSES_STATE)
            y = _mmx(_pieces(j["rr"], RW_PASSES_STATE), sp, _dot_nt)
            j["y_o"][j["rows"], j["ps"]] = _fold_pair(y) + j["yl"]
            sm = _mmx(sp[:RW_PASSES_TRANSITION], _pieces(j["m"], RW_PASSES_TRANSITION), _dot_nt)
            s_ref[j["d"], :, j["ps"]] = s * j["e"] + _fold_pair(sm) + j["n"]
        return carry

    lax.fori_loop(0, nchunk, chunk_step, 0)


def rw_fused_scan(geom, r, v, kk, lwf, kf, bf, lwb, kb, bb):
    B, T, L = geom.B, geom.T, geom.L
    blk = L
    nlat = T // blk
    cblk = geom.BT // blk
    fwd = pl.BlockSpec((blk, RW_WIDTH), lambda b, s: (jnp.where(s == 0, cblk + b, b * nlat + s - 1), 0))
    bwd = pl.BlockSpec((blk, RW_WIDTH), lambda b, s: (jnp.where(s == 0, cblk + b, b * nlat + nlat - s), 0))
    return pl.pallas_call(
        functools.partial(_rw_fused_scan_kernel, nchunk=blk // RW_CHUNK),
        grid=(B, nlat + 1),
        in_specs=[fwd] * 6 + [bwd] * 6,
        out_specs=[fwd, bwd],
        out_shape=[jax.ShapeDtypeStruct((geom.R, RW_WIDTH), F32)] * 2,
        scratch_shapes=[pltpu.VMEM((2, RW_HEAD_DIM, RW_WIDTH), F32)],
        compiler_params=_cparams("parallel", "arbitrary"),
    )(r, v, kk, lwf, kf, bf, r, v, kk, lwb, kb, bb)


def _rw_post_kernel(yf_ref, yb_ref, bonus_ref, g_ref, lng_ref, lnb_ref, ones_ref, o_ref):
    ones = ones_ref[...]
    y = yf_ref[...] + yb_ref[...]
    mean = _head_sum(y, ones) * (1.0 / RW_HEAD_DIM)
    yc = y - mean
    var = _head_sum(yc * yc, ones) * (1.0 / RW_HEAD_DIM)
    y = yc * lax.rsqrt(var + RW_LNX_EPS) * lng_ref[...] + lnb_ref[...]
    o_ref[...] = ((y + bonus_ref[...]) * g_ref[...]).astype(o_ref.dtype)


def rw_post(yf, yb, bonus, g, lnx_g, lnx_b, ones, *, tm):
    R = yf.shape[0]
    row = pl.BlockSpec((tm, RW_WIDTH), lambda i: (i, 0))
    full = lambda a: pl.BlockSpec(a.shape, lambda i: (0,) * a.ndim)
    return pl.pallas_call(
        _rw_post_kernel,
        grid=(R // tm,),
        in_specs=[row] * 4 + [full(lnx_g), full(lnx_b), full(ones)],
        out_specs=row,
        out_shape=jax.ShapeDtypeStruct((R, RW_WIDTH), BF16),
        compiler_params=_cparams("parallel"),
    )(yf, yb, bonus, g, lnx_g, lnx_b, ones)


def _rw_in_cols(w):
    parts = [w[..., :3 * RW_WIDTH]]
    off = 3 * RW_WIDTH
    for n in (RW_DECAY_LORA, RW_DECAY_LORA, RW_AAA_LORA, RW_AAA_LORA, RW_GATE_LORA):
        parts.append(_pad_cols(w[..., off:off + n], LORA_PAD))
        off += n
    return jnp.concatenate(parts, axis=-1)


def _pad_rows(w, n):
    return jnp.pad(w, [(0, 0)] * (w.ndim - 2) + [(0, n - w.shape[-2]), (0, 0)])


def _mla_wq_cols(w):
    w = w.reshape(w.shape[0], MLA_HEADS, MLA_NOPE + MLA_ROPE)
    return _pad_cols(w, MLA_QK_PAD).reshape(w.shape[0], MLA_HEADS * MLA_QK_PAD)


def kernel(x, c, ctx, c_ctx, ada_w, ada_b, norm1_g, w_in, rw_mu, rw_w0, rw_w2, rw_a0, rw_a2, rw_g2, rw_kk, rw_ka,
           rw_rk, rw_lnx_g, rw_lnx_b, rw_v0, rw_v1, rw_v2, wa_sink, mla_qnorm_g, mla_kvnorm_g, mla_w_uq, mla_w_ukv,
           w_branch, w_out, norm2_g, ffn_w_in, ffn_conv_w, ffn_conv_b, ffn_w_out, final_norm_g):
    B, T, D = x.shape
    L = ctx.shape[1]
    depth = w_in.shape[0]
    F = ffn_w_out.shape[1]
    geom = Geom(B, T, L)
    tm = _pick_tile(T, (512, 256, 128))
    assert (B * L) % tm == 0
    tmm = _pick_tile(geom.R, (1280, 1024, 640, 512, 256, 128))

    rw_cols = 3 * RW_WIDTH + 2 * RW_DECAY_LORA + 2 * RW_AAA_LORA + RW_GATE_LORA
    wa_cols = WA_WIDTH + 2 * WA_KV_WIDTH
    mla_cols = MLA_Q_LORA + MLA_KV_LORA + MLA_ROPE
    mla_cols_pad = MLA_Q_LORA + MLA_KV_LORA + 128

    cos_wa, sin_wa = _rope_tables(geom, WA_HEAD_DIM)
    cos_wa, sin_wa = jnp.tile(cos_wa, (1, 2)), jnp.tile(sin_wa, (1, 2))
    cos_m, sin_m = _rope_tables(geom, MLA_ROPE)
    one, zero = jnp.ones((geom.R, MLA_NOPE), F32), jnp.zeros((geom.R, MLA_NOPE), F32)
    cos_mla = jnp.concatenate([one, cos_m, one[:, :64]], axis=-1)
    sin_mla = jnp.concatenate([zero, sin_m, zero[:, :64]], axis=-1)
    lane = np.arange(128)
    ones_blk = jnp.asarray((lane[:, None] // RW_HEAD_DIM) == (lane[None, :] // RW_HEAD_DIM), F32)

    xs = jnp.concatenate([x.reshape(B * T, D), ctx.reshape(B * L, D)], axis=0)
    cvec = jnp.concatenate([c, c_ctx[None, :], jnp.zeros((8 - (B + 1) % 8, D), F32)], axis=0)
    v_first = None
    for l in range(depth):
        need_ctx = l < depth - 1
        mod = ada_modulation(cvec, ada_w[l], ada_b[l])
        mod = [mod[:, k * D:(k + 1) * D].reshape(-1, 1, D) for k in range(6)]

        w = w_in[l]
        tn_in = _pick_tile(3 * D, (768, 512, 256, 128))
        w_rw = _pad_cols(_rw_in_cols(w[:, :rw_cols]), _round_up(RW_Z_COLS, tn_in)).astype(BF16)
        w_wa = _pad_cols(w[:, rw_cols:rw_cols + wa_cols], _round_up(wa_cols, tn_in)).astype(BF16)
        w_mla = _pad_cols(w[:, rw_cols + wa_cols:rw_cols + wa_cols + mla_cols],
                          _round_up(mla_cols_pad, tn_in)).astype(BF16)
        w_gate = w[:, rw_cols + wa_cols + mla_cols:].astype(BF16)
        z_rw, z_wa, z_mla, gates = in_projection(geom, xs, norm1_g[l], mod[0], mod[1], w_rw, w_wa, w_mla, w_gate,
                                                 tm=tmm, tn=tn_in)

        rw_p = dict(
            mu=_rw_in_cols(rw_mu[l][None, :]), w0=rw_w0[l], w2=_pad_rows(rw_w2[l], LORA_PAD), a0=rw_a0[l],
            a2=_pad_rows(rw_a2[l], LORA_PAD), g2=_pad_rows(rw_g2[l], LORA_PAD), kk=rw_kk[l][None, :],
            ka=rw_ka[l][None, :], rk=rw_rk[l].reshape(1, RW_WIDTH), ones=ones_blk)
        vres = None if l == 0 else (rw_v0[l - 1][None, :], rw_v1[l - 1], rw_v2[l - 1])
        r, v, kk, lwf, kf, bf, lwb, kb, bb, g, bonus = rw_prep(geom, z_rw, rw_p, v_first, vres, tm=min(tm, 256))
        if l == 0:
            v_first = v
        yf, yb = rw_fused_scan(geom, r, v, kk, lwf, kf, bf, lwb, kb, bb)
        o_a = rw_post(yf, yb, bonus, g, rw_lnx_g[l][None, :], rw_lnx_b[l][None, :], ones_blk, tm=tm)

        q_wa, k_wa, v_wa = wa_prep(z_wa, cos_wa, sin_wa, tm=tm)
        ob_l = wa_attention(geom, q_wa, k_wa, v_wa, wa_sink[l], local=True)
        parts = [ob_l]
        if need_ctx:
            parts.append(wa_attention(geom, q_wa, k_wa, v_wa, wa_sink[l], local=False))
        else:
            parts.append(jnp.zeros((B * L, WA_WIDTH), BF16))
        o_b = jnp.concatenate(parts, axis=0)

        wq = _mla_wq_cols(mla_w_uq[l]).astype(BF16)
        q_m, k_m, v_m = mla_prep(z_mla, mla_qnorm_g[l], mla_kvnorm_g[l], wq, mla_w_ukv[l].astype(BF16),
                                 cos_mla, sin_mla, tm=tm)
        tq = _pick_tile(T, (1024, 512, 256, 128))
        tk = _pick_tile(T, (4096, 2048, 1024, 512, 256, 128))
        parts = [mla_attention(geom, q_m, k_m, v_m, with_latent=True, tq=tq, tk=tk)]
        if need_ctx:
            parts.append(mla_attention(geom, q_m, k_m, v_m, with_latent=False, tq=L, tk=L))
        else:
            parts.append(jnp.zeros((B * L, MLA_HEADS * MLA_V), BF16))
        o_c = jnp.concatenate(parts, axis=0)

        tn_d = _pick_tile(D, (1024, 512, 256, 128))
        y = merge_branches(o_a, o_b, o_c, gates, w_branch[l].astype(BF16), tm=tmm, tn=tn_d)
        xs = matmul_gated_residual(geom, y, w_out[l].astype(BF16), xs, mod[2], tm=tmm, tn=tn_d)

        tf = _pick_tile(F, (512, 256, 128))
        hmid = ffn_in_conv_glu(geom, xs, norm2_g[l], mod[3], mod[4], ffn_w_in[l].astype(BF16), ffn_conv_w[l],
                               ffn_conv_b[l], tm=tmm, tf=tf)
        xs = matmul_gated_residual(geom, hmid, ffn_w_out[l].astype(BF16), xs, mod[5], tm=tmm,
                                   tn=_pick_tile(D, (512, 256, 128)))

    out = final_rmsnorm(xs, final_norm_g, B * T, tm=tm)
    return out.reshape(B, T, D)
```

```python
import functools

import jax
import jax.numpy as jnp
import numpy as np
from jax import lax
from jax.experimental import pallas as pl
from jax.experimental.pallas import tpu as pltpu

F32 = jnp.float32
BF16 = jnp.bfloat16
HIGHEST = lax.Precision.HIGHEST

NORM_EPS = 1e-6
NEG_INF = -1e30
GRID_W = 64
ROPE_BASE = 10000.0

RW_HEADS = 16
RW_HEAD_DIM = 64
RW_WIDTH = RW_HEADS * RW_HEAD_DIM
RW_DECAY_LORA = 96
RW_AAA_LORA = 96
RW_GATE_LORA = 64
RW_LNX_EPS = 64e-5
RW_CHUNK = 64
RW_PAIR = 2 * RW_HEAD_DIM
RW_PASSES_LOCAL = 1
RW_PASSES_STATE = 3
RW_PASSES_TRANSITION = 1
LORA_PAD = 128

WA_HEADS = 16
WA_KV_HEADS = 4
WA_GROUP = WA_HEADS // WA_KV_HEADS
WA_HEAD_DIM = 64
WA_WIDTH = WA_HEADS * WA_HEAD_DIM
WA_KV_WIDTH = WA_KV_HEADS * WA_HEAD_DIM
WA_V_PAD = WA_KV_HEADS * 128
WINDOW = 128
WA_SCALE = WA_HEAD_DIM ** -0.5

MLA_HEADS = 8
MLA_NOPE = 128
MLA_ROPE = 64
MLA_V = 128
MLA_Q_LORA = 512
MLA_KV_LORA = 512
MLA_QK_PAD = 256
MLA_V_PAD = 256
MLA_SCALE = (MLA_NOPE + MLA_ROPE) ** -0.5
MLA_SUB_KEYS = 2048
LOG2_E = 1.4426950408889634

CONV_W = 3
VMEM_LIMIT_BYTES = 56 * 1024 * 1024


def _cparams(*sem):
    return pltpu.CompilerParams(dimension_semantics=sem, vmem_limit_bytes=VMEM_LIMIT_BYTES)


def _dot(a, b, precision=None):
    return jnp.dot(a, b, preferred_element_type=F32, precision=precision)


def _dot_nt(a, b, precision=None):
    return lax.dot_general(a, b, (((1,), (1,)), ((), ())), preferred_element_type=F32, precision=precision)


def _dot_tn(a, b, precision=None):
    return lax.dot_general(a, b, (((0,), (0,)), ((), ())), preferred_element_type=F32, precision=precision)


def _pick_tile(n, candidates):
    for c in candidates:
        if n % c == 0:
            return c
    raise ValueError(f"no tile in {candidates} divides {n}")


def _pad_cols(w, n):
    return jnp.pad(w, [(0, 0)] * (w.ndim - 1) + [(0, n - w.shape[-1])])


def _round_up(n, m):
    return (n + m - 1) // m * m


class Geom:
    def __init__(self, B, T, L):
        assert T & (T - 1) == 0 and L & (L - 1) == 0, "sequence lengths must be powers of two"
        assert T % L == 0 and L % RW_CHUNK == 0 and T % GRID_W == 0
        self.B, self.T, self.L = B, T, L
        self.BT = B * T
        self.R = B * T + B * L


def _select_row_group(geom, row0, tm, tab_ref):
    r = row0 + lax.broadcasted_iota(jnp.int32, (tm, 1), 0)
    out = tab_ref[geom.B]
    for b in range(geom.B):
        out = jnp.where((r >= b * geom.T) & (r < (b + 1) * geom.T), tab_ref[b], out)
    return out


def _seq_edge_masks(geom, row0, tm):
    r = row0 + lax.broadcasted_iota(jnp.int32, (tm, 1), 0)
    is_lat = r < geom.BT
    pos = jnp.where(is_lat, r & (geom.T - 1), (r - geom.BT) & (geom.L - 1))
    last = jnp.where(is_lat, geom.T - 1, geom.L - 1)
    return pos == 0, pos == last


def _shifted_rows(x, prev8, next8, first, last):
    tm = x.shape[0]
    rid = lax.broadcasted_iota(jnp.int32, (tm, 1), 0)
    up = jnp.where(rid == 0, prev8[7:8, :], pltpu.roll(x, 1, axis=0))
    dn = jnp.where(rid == tm - 1, next8[0:1, :], pltpu.roll(x, tm - 1, axis=0))
    return jnp.where(first, 0.0, up), jnp.where(last, 0.0, dn)


def _halo_specs(tm, R, width, col_of):
    nb8 = tm // 8
    prev = pl.BlockSpec((8, width), lambda i, *a: (jnp.maximum(i * nb8 - 1, 0), col_of(i, *a)))
    nxt = pl.BlockSpec((8, width), lambda i, *a: (jnp.minimum((i + 1) * nb8, R // 8 - 1), col_of(i, *a)))
    return prev, nxt


def _ada_kernel(c_ref, w_ref, b_ref, o_ref):
    c = c_ref[...]
    o_ref[...] = _dot(c * jax.nn.sigmoid(c), w_ref[...], HIGHEST) + b_ref[...]


def ada_modulation(cvec, w, b):
    G, D = cvec.shape
    N = w.shape[1]
    tn = _pick_tile(N, (1024, 512, 256, 128))
    return pl.pallas_call(
        _ada_kernel,
        grid=(N // tn,),
        in_specs=[pl.BlockSpec((G, D), lambda j: (0, 0)),
                  pl.BlockSpec((D, tn), lambda j: (0, j)),
                  pl.BlockSpec((1, tn), lambda j: (0, j))],
        out_specs=pl.BlockSpec((G, tn), lambda j: (0, j)),
        out_shape=jax.ShapeDtypeStruct((G, N), F32),
        compiler_params=_cparams("arbitrary"),
    )(cvec, w, b.reshape(1, N))


def _norm_modulate(x, g, sc, sh):
    gain = g * (1.0 + sc)
    return (x * lax.rsqrt(jnp.mean(x * x, axis=-1, keepdims=True) + NORM_EPS) * gain + sh).astype(BF16)


PROLOGUE_ROWS = 128


def _norm_modulate_rows(geom, row0, x_ref, g_ref, sc_ref, sh_ref, h_ref, h_off):
    tm = x_ref.shape[0]
    step = PROLOGUE_ROWS if tm % PROLOGUE_ROWS == 0 else tm
    one_group = geom.T % step == 0 and geom.BT % step == 0
    g = g_ref[...]
    for r in range(0, tm, step):
        if one_group:
            sc, sh = sc_ref[geom.B], sh_ref[geom.B]
            for b in range(geom.B):
                in_b = (row0 + r >= b * geom.T) & (row0 + r < (b + 1) * geom.T)
                sc, sh = jnp.where(in_b, sc_ref[b], sc), jnp.where(in_b, sh_ref[b], sh)
        else:
            sc = _select_row_group(geom, row0 + r, step, sc_ref)
            sh = _select_row_group(geom, row0 + r, step, sh_ref)
        h_ref[h_off + r:h_off + r + step] = _norm_modulate(x_ref[r:r + step], g, sc, sh)


def _in_proj_kernel(x_ref, g_ref, sh_ref, sc_ref, w_ref, zrw_ref, zwa_ref, zmla_ref, gate_ref, h_ref,
                    *, geom, tm, starts):
    j = pl.program_id(1)

    @pl.when(j == 0)
    def _():
        _norm_modulate_rows(geom, pl.program_id(0) * tm, x_ref, g_ref, sc_ref, sh_ref, h_ref, 0)

    acc = _dot(h_ref[...], w_ref[...])
    wa0, mla0, gate0 = starts

    @pl.when(j < wa0)
    def _():
        zrw_ref[...] = acc

    @pl.when((j >= wa0) & (j < mla0))
    def _():
        zwa_ref[...] = acc

    @pl.when((j >= mla0) & (j < gate0))
    def _():
        zmla_ref[...] = acc

    @pl.when(j >= gate0)
    def _():
        gate_ref[...] = jax.nn.sigmoid(acc).astype(gate_ref.dtype)


def in_projection(geom, x, g, shift, scale, w_rw, w_wa, w_mla, w_gate, *, tm, tn):
    R, K = x.shape
    widths = [w.shape[1] for w in (w_rw, w_wa, w_mla, w_gate)]
    assert R % tm == 0 and all(n % tn == 0 for n in widths)
    tiles = [n // tn for n in widths]
    starts = tuple(int(v) for v in np.cumsum(tiles)[:3])
    w_all = jnp.concatenate([w_rw, w_wa, w_mla, w_gate], axis=1)
    full = lambda a: pl.BlockSpec(a.shape, lambda i, j: (0,) * a.ndim)
    out_spec = lambda first, n: pl.BlockSpec((tm, tn), lambda i, j: (i, jnp.clip(j - first, 0, n - 1)))
    firsts = (0,) + starts
    return pl.pallas_call(
        functools.partial(_in_proj_kernel, geom=geom, tm=tm, starts=starts),
        grid=(R // tm, sum(tiles)),
        in_specs=[pl.BlockSpec((tm, K), lambda i, j: (i, 0), pipeline_mode=pl.Buffered(1)),
                  pl.BlockSpec((1, K), lambda i, j: (0, 0)),
                  full(shift), full(scale),
                  pl.BlockSpec((K, tn), lambda i, j: (0, j))],
        out_specs=[out_spec(f, n) for f, n in zip(firsts, tiles)],
        out_shape=[jax.ShapeDtypeStruct((R, widths[0]), F32), jax.ShapeDtypeStruct((R, widths[1]), F32),
                   jax.ShapeDtypeStruct((R, widths[2]), F32), jax.ShapeDtypeStruct((R, widths[3]), BF16)],
        scratch_shapes=[pltpu.VMEM((tm, K), BF16)],
        compiler_params=_cparams("parallel", "arbitrary"),
    )(x, g.reshape(1, K), shift, scale, w_all)


def _mm_resid_kernel(y_ref, w_ref, r_ref, gate_ref, o_ref, *, geom, tm):
    gate = _select_row_group(geom, pl.program_id(0) * tm, tm, gate_ref)
    o_ref[...] = r_ref[...] + gate * _dot(y_ref[...], w_ref[...])


def matmul_gated_residual(geom, y, w, resid, gate, *, tm, tn):
    R, K = y.shape
    N = w.shape[1]
    G = gate.shape[0]
    assert R % tm == 0 and N % tn == 0
    return pl.pallas_call(
        functools.partial(_mm_resid_kernel, geom=geom, tm=tm),
        grid=(R // tm, N // tn),
        in_specs=[pl.BlockSpec((tm, K), lambda i, j: (i, 0)),
                  pl.BlockSpec((K, tn), lambda i, j: (0, j)),
                  pl.BlockSpec((tm, tn), lambda i, j: (i, j)),
                  pl.BlockSpec((G, 1, tn), lambda i, j: (0, 0, j))],
        out_specs=pl.BlockSpec((tm, tn), lambda i, j: (i, j)),
        out_shape=jax.ShapeDtypeStruct((R, N), F32),
        compiler_params=_cparams("parallel", "arbitrary"),
    )(y, w, resid, gate)


FFN_HALO = 16


def _ffn_in_kernel(x_ref, xp_ref, xn_ref, g_ref, sh_ref, sc_ref, wg_ref, wu_ref, cw_ref, cb_ref, o_ref, h_ref,
                   *, geom, tm):
    row0 = pl.program_id(0) * tm
    H = FFN_HALO

    @pl.when(pl.program_id(1) == 0)
    def _():
        _norm_modulate_rows(geom, row0, xp_ref, g_ref, sc_ref, sh_ref, h_ref, 0)
        _norm_modulate_rows(geom, row0, x_ref, g_ref, sc_ref, sh_ref, h_ref, H)
        _norm_modulate_rows(geom, row0 + tm - H, xn_ref, g_ref, sc_ref, sh_ref, h_ref, H + tm)

    first, last = _seq_edge_masks(geom, row0, tm)
    gt = _dot(h_ref[...], wg_ref[...])
    u = _dot(h_ref[H:H + tm], wu_ref[...])
    up = jnp.where(first, 0.0, pltpu.roll(gt, 1, axis=0)[H:H + tm])
    dn = jnp.where(last, 0.0, pltpu.roll(gt, tm + 2 * H - 1, axis=0)[H:H + tm])
    cw = cw_ref[...]
    conv = cb_ref[...] + up * cw[0:1, :]
    conv = conv + gt[H:H + tm] * cw[1:2, :]
    conv = conv + dn * cw[2:3, :]
    o_ref[...] = (jax.nn.gelu(conv, approximate=True) * u).astype(o_ref.dtype)


def ffn_in_conv_glu(geom, x, g, shift, scale, w_in, conv_w, conv_b, *, tm, tf):
    R, K = x.shape
    F = w_in.shape[1] // 2
    nj = F // tf
    nbh = tm // FFN_HALO
    assert R % tm == 0 and F % tf == 0 and tm % FFN_HALO == 0
    full = lambda a: pl.BlockSpec(a.shape, lambda i, j: (0,) * a.ndim)
    return pl.pallas_call(
        functools.partial(_ffn_in_kernel, geom=geom, tm=tm),
        grid=(R // tm, nj),
        in_specs=[pl.BlockSpec((tm, K), lambda i, j: (i, 0), pipeline_mode=pl.Buffered(1)),
                  pl.BlockSpec((FFN_HALO, K), lambda i, j: (jnp.maximum(i * nbh - 1, 0), 0)),
                  pl.BlockSpec((FFN_HALO, K), lambda i, j: (jnp.minimum((i + 1) * nbh, R // FFN_HALO - 1), 0)),
                  pl.BlockSpec((1, K), lambda i, j: (0, 0)),
                  full(shift), full(scale),
                  pl.BlockSpec((K, tf), lambda i, j: (0, j)),
                  pl.BlockSpec((K, tf), lambda i, j: (0, nj + j)),
                  pl.BlockSpec((CONV_W, tf), lambda i, j: (0, j)),
                  pl.BlockSpec((1, tf), lambda i, j: (0, j))],
        out_specs=pl.BlockSpec((tm, tf), lambda i, j: (i, j)),
        out_shape=jax.ShapeDtypeStruct((R, F), BF16),
        scratch_shapes=[pltpu.VMEM((tm + 2 * FFN_HALO, K), BF16)],
        compiler_params=_cparams("parallel", "arbitrary"),
    )(x, x, x, g.reshape(1, K), shift, scale, w_in, w_in, conv_w, conv_b.reshape(1, F))


def _merge_kernel(oa_ref, ob_ref, oc_ref, ga_ref, gb_ref, gc_ref, w_ref, o_ref):
    y = ga_ref[...] * _dot(oa_ref[...], w_ref[0])
    y = y + gb_ref[...] * _dot(ob_ref[...], w_ref[1])
    y = y + gc_ref[...] * _dot(oc_ref[...], w_ref[2])
    o_ref[...] = y.astype(o_ref.dtype)


def merge_branches(oa, ob, oc, gates, wb, *, tm, tn):
    R, K = oa.shape
    D = wb.shape[2]
    nj = D // tn
    bspec = pl.BlockSpec((tm, K), lambda i, j: (i, 0))
    gspec = lambda k: pl.BlockSpec((tm, tn), lambda i, j: (i, k * nj + j))
    return pl.pallas_call(
        _merge_kernel,
        grid=(R // tm, nj),
        in_specs=[bspec, bspec, bspec, gspec(0), gspec(1), gspec(2),
                  pl.BlockSpec((3, K, tn), lambda i, j: (0, 0, j))],
        out_specs=pl.BlockSpec((tm, tn), lambda i, j: (i, j)),
        out_shape=jax.ShapeDtypeStruct((R, D), BF16),
        compiler_params=_cparams("parallel", "arbitrary"),
    )(oa, ob, oc, gates, gates, gates, wb)


def _rmsnorm_kernel(x_ref, g_ref, o_ref):
    x = x_ref[...]
    o_ref[...] = x * lax.rsqrt(jnp.mean(x * x, axis=-1, keepdims=True) + NORM_EPS) * g_ref[...]


def final_rmsnorm(x, g, rows, *, tm):
    D = x.shape[1]
    return pl.pallas_call(
        _rmsnorm_kernel,
        grid=(rows // tm,),
        in_specs=[pl.BlockSpec((tm, D), lambda i: (i, 0)), pl.BlockSpec((1, D), lambda i: (0, 0))],
        out_specs=pl.BlockSpec((tm, D), lambda i: (i, 0)),
        out_shape=jax.ShapeDtypeStruct((rows, D), F32),
        compiler_params=_cparams("parallel"),
    )(x, g.reshape(1, D))


def _rot_half64(z):
    n = z.shape[-1]
    lane = lax.broadcasted_iota(jnp.int32, z.shape, z.ndim - 1)
    return jnp.where((lane & 63) < 32, pltpu.roll(z, n - 32, axis=z.ndim - 1), pltpu.roll(z, 32, axis=z.ndim - 1))


def _rope_tables(geom, dim):
    nf = dim // 4
    inv = ROPE_BASE ** (-jnp.arange(nf, dtype=F32) / nf)
    rows = geom.T // GRID_W
    row = jnp.repeat(jnp.arange(rows, dtype=F32), GRID_W)
    col = jnp.tile(jnp.arange(GRID_W, dtype=F32), rows)
    ang = jnp.concatenate([row[:, None] * inv, col[:, None] * inv], axis=-1)
    cos, sin = jnp.cos(ang), jnp.sin(ang)
    cos_t = jnp.concatenate([cos, cos], axis=-1)
    sin_t = jnp.concatenate([-sin, sin], axis=-1)
    nctx = geom.B * geom.L
    cos_f = jnp.concatenate([jnp.tile(cos_t, (geom.B, 1)), jnp.ones((nctx, dim), F32)], axis=0)
    sin_f = jnp.concatenate([jnp.tile(sin_t, (geom.B, 1)), jnp.zeros((nctx, dim), F32)], axis=0)
    return cos_f, sin_f


def _wa_prep_kernel(z_ref, cos_ref, sin_ref, q_ref, k_ref, v_ref):
    cos = cos_ref[...]
    sin = sin_ref[...]
    for c in range(WA_WIDTH // 128):
        z = z_ref[:, c * 128:(c + 1) * 128]
        q_ref[:, c * 128:(c + 1) * 128] = ((z * cos + _rot_half64(z) * sin) * (WA_SCALE * LOG2_E)).astype(BF16)
    for c in range(WA_KV_WIDTH // 128):
        z = z_ref[:, WA_WIDTH + c * 128:WA_WIDTH + (c + 1) * 128]
        k_ref[:, c * 128:(c + 1) * 128] = (z * cos + _rot_half64(z) * sin).astype(BF16)
    v_ref[...] = jnp.ones(v_ref.shape, BF16)
    for g in range(WA_KV_HEADS):
        src = WA_WIDTH + WA_KV_WIDTH + g * WA_HEAD_DIM
        v_ref[:, g * 128:g * 128 + WA_HEAD_DIM] = z_ref[:, src:src + WA_HEAD_DIM].astype(BF16)


def wa_prep(z, cos, sin, *, tm):
    R = z.shape[0]
    row = lambda w: pl.BlockSpec((tm, w), lambda i: (i, 0))
    return pl.pallas_call(
        _wa_prep_kernel,
        grid=(R // tm,),
        in_specs=[row(WA_WIDTH + 2 * WA_KV_WIDTH), row(128), row(128)],
        out_specs=[row(WA_WIDTH), row(WA_KV_WIDTH), row(WA_V_PAD)],
        out_shape=[jax.ShapeDtypeStruct((R, WA_WIDTH), BF16),
                   jax.ShapeDtypeStruct((R, WA_KV_WIDTH), BF16),
                   jax.ShapeDtypeStruct((R, WA_V_PAD), BF16)],
        compiler_params=_cparams("parallel"),
    )(z, cos, sin)


def _wa_attn_kernel(*refs, local, nqb, tq):
    if local:
        sink_ref, q_ref, kp_ref, kc_ref, kn_ref, vp_ref, vc_ref, vn_ref, kx_ref, vx_ref, o_ref = refs
    else:
        sink_ref, q_ref, kx_ref, vx_ref, o_ref = refs
    i = pl.program_id(1)
    nk_ctx = kx_ref.shape[0]
    rows = WA_GROUP * tq
    qpos = lax.broadcasted_iota(jnp.int32, (rows, 1), 0) & (tq - 1)
    head_in_group = lax.broadcasted_iota(jnp.int32, (rows, 1), 0) >> int(np.log2(tq))
    if local:
        off_prev = jnp.where(i > 0, 0, tq)
        off_next = jnp.where(i < nqb - 1, 0, tq)
        j = lax.broadcasted_iota(jnp.int32, (1, 3 * tq + nk_ctx), 1)
        valid = ((j >= tq) & (j < 2 * tq)) | (j >= 3 * tq)
        valid = valid | ((j < tq) & (j >= qpos + off_prev))
        valid = valid | ((j >= 2 * tq) & (j < 3 * tq) & ((j - 2 * tq) <= qpos - off_next))
    for g in range(WA_KV_HEADS):
        ks = slice(g * WA_HEAD_DIM, (g + 1) * WA_HEAD_DIM)
        qg = jnp.concatenate(
            [q_ref[:, (g * WA_GROUP + a) * WA_HEAD_DIM:(g * WA_GROUP + a + 1) * WA_HEAD_DIM] for a in range(WA_GROUP)],
            axis=0)
        vs = slice(g * 128, (g + 1) * 128)
        sink = jnp.zeros((rows, 1), F32)
        for a in range(WA_GROUP):
            sink = jnp.where(head_in_group == a, sink_ref[g * WA_GROUP + a] * LOG2_E, sink)
        if local:
            kcat = jnp.concatenate([kp_ref[:, ks], kc_ref[:, ks], kn_ref[:, ks], kx_ref[:, ks]], axis=0)
            vcat = jnp.concatenate([vp_ref[:, vs], vc_ref[:, vs], vn_ref[:, vs], vx_ref[:, vs]], axis=0)
        else:
            kcat, vcat = kx_ref[:, ks], vx_ref[:, vs]
        s = _dot_nt(qg, kcat)
        if local:
            s = jnp.where(valid, s, NEG_INF)
        m = jnp.maximum(jnp.max(s, axis=-1, keepdims=True), sink)
        o = _dot(jnp.exp2(s - m).astype(BF16), vcat)
        o = o[:, :WA_HEAD_DIM] / (o[:, WA_HEAD_DIM:] + jnp.exp2(sink - m))
        for a in range(WA_GROUP):
            h = g * WA_GROUP + a
            o_ref[:, h * WA_HEAD_DIM:(h + 1) * WA_HEAD_DIM] = o[a * tq:(a + 1) * tq].astype(o_ref.dtype)


def wa_attention(geom, q, k, v, sink, *, local):
    B, T, L = geom.B, geom.T, geom.L
    sink_spec = pl.BlockSpec(memory_space=pltpu.SMEM)
    ctx_spec = lambda w: pl.BlockSpec((L, w), lambda b, i: (geom.BT // L + b, 0))
    if local:
        tq = WINDOW
        nqb = T // tq
        kv = lambda f, w: pl.BlockSpec((tq, w), lambda b, i: (b * nqb + f(i), 0))
        prev = lambda i: jnp.maximum(i - 1, 0)
        cur = lambda i: i
        nxt = lambda i: jnp.minimum(i + 1, nqb - 1)
        in_specs = [sink_spec, pl.BlockSpec((tq, WA_WIDTH), lambda b, i: (b * nqb + i, 0)),
                    kv(prev, WA_KV_WIDTH), kv(cur, WA_KV_WIDTH), kv(nxt, WA_KV_WIDTH),
                    kv(prev, WA_V_PAD), kv(cur, WA_V_PAD), kv(nxt, WA_V_PAD),
                    ctx_spec(WA_KV_WIDTH), ctx_spec(WA_V_PAD)]
        args = (sink, q, k, k, k, v, v, v, k, v)
        out_rows, out_spec = geom.BT, pl.BlockSpec((tq, WA_WIDTH), lambda b, i: (b * nqb + i, 0))
    else:
        tq, nqb = L, 1
        in_specs = [sink_spec, pl.BlockSpec((tq, WA_WIDTH), lambda b, i: (geom.BT // L + b, 0)),
                    ctx_spec(WA_KV_WIDTH), ctx_spec(WA_V_PAD)]
        args = (sink, q, k, v)
        out_rows, out_spec = B * L, pl.BlockSpec((tq, WA_WIDTH), lambda b, i: (b, 0))
    return pl.pallas_call(
        functools.partial(_wa_attn_kernel, local=local, nqb=nqb, tq=tq),
        grid=(B, nqb),
        in_specs=in_specs,
        out_specs=out_spec,
        out_shape=jax.ShapeDtypeStruct((out_rows, WA_WIDTH), BF16),
        compiler_params=_cparams("parallel", "arbitrary"),
    )(*args)


def _mla_prep_kernel(z_ref, qg_ref, kvg_ref, wq_ref, wkv_ref, cos_ref, sin_ref, q_ref, k_ref, v_ref):
    def norm(x, g):
        return (x * lax.rsqrt(jnp.mean(x * x, axis=-1, keepdims=True) + NORM_EPS) * g).astype(BF16)

    cos = cos_ref[...]
    sin = sin_ref[...]
    q = _dot(norm(z_ref[:, :MLA_Q_LORA], qg_ref[...]), wq_ref[...])
    kv = _dot(norm(z_ref[:, MLA_Q_LORA:MLA_Q_LORA + MLA_KV_LORA], kvg_ref[...]), wkv_ref[...])
    kr = z_ref[:, MLA_Q_LORA + MLA_KV_LORA:MLA_Q_LORA + MLA_KV_LORA + 128]
    kr = (kr * cos[:, 128:] + _rot_half64(kr) * sin[:, 128:]).astype(BF16)
    for h in range(MLA_HEADS):
        qh = q[:, h * MLA_QK_PAD:(h + 1) * MLA_QK_PAD]
        q_ref[h] = ((qh * cos + _rot_half64(qh) * sin) * (MLA_SCALE * LOG2_E)).astype(BF16)
        k_ref[h, :, :MLA_NOPE] = kv[:, h * 256:h * 256 + MLA_NOPE].astype(BF16)
        k_ref[h, :, MLA_NOPE:] = kr
        v_ref[h, :, :MLA_V] = kv[:, h * 256 + MLA_NOPE:(h + 1) * 256].astype(BF16)
        v_ref[h, :, MLA_V:] = jnp.ones((kv.shape[0], MLA_V_PAD - MLA_V), BF16)


def mla_prep(z, qnorm_g, kvnorm_g, wq, wkv, cos, sin, *, tm):
    R, Z = z.shape
    full = lambda a: pl.BlockSpec(a.shape, lambda i: (0,) * a.ndim)
    qg, kvg = qnorm_g.reshape(1, -1), kvnorm_g.reshape(1, -1)
    hd = lambda w: pl.BlockSpec((MLA_HEADS, tm, w), lambda i: (0, i, 0))
    return pl.pallas_call(
        _mla_prep_kernel,
        grid=(R // tm,),
        in_specs=[pl.BlockSpec((tm, Z), lambda i: (i, 0)), full(qg), full(kvg), full(wq), full(wkv),
                  pl.BlockSpec((tm, MLA_QK_PAD), lambda i: (i, 0)), pl.BlockSpec((tm, MLA_QK_PAD), lambda i: (i, 0))],
        out_specs=[hd(MLA_QK_PAD), hd(MLA_QK_PAD), hd(MLA_V_PAD)],
        out_shape=[jax.ShapeDtypeStruct((MLA_HEADS, R, MLA_QK_PAD), BF16),
                   jax.ShapeDtypeStruct((MLA_HEADS, R, MLA_QK_PAD), BF16),
                   jax.ShapeDtypeStruct((MLA_HEADS, R, MLA_V_PAD), BF16)],
        compiler_params=_cparams("parallel"),
    )(z, qg, kvg, wq, wkv, cos, sin)


def _mla_flash_kernel(*refs, with_latent, sub):
    if with_latent:
        q_ref, kx_ref, vx_ref, k_ref, v_ref, o_ref, m_ref, acc_ref = refs
    else:
        q_ref, kx_ref, vx_ref, o_ref, m_ref, acc_ref = refs
    ki = pl.program_id(3)
    q = q_ref[0]

    def update(s, v, m_old, acc_old):
        cols = [s[:, c * 128:(c + 1) * 128] for c in range(s.shape[1] // 128)]
        mx = functools.reduce(jnp.maximum, cols)
        m_new = jnp.maximum(m_old, jnp.max(mx, axis=-1, keepdims=True))
        alpha = jnp.exp2(m_old - m_new)
        p = jnp.concatenate([jnp.exp2(c - m_new).astype(BF16) for c in cols], axis=-1)
        return m_new, jnp.concatenate([alpha, alpha], axis=-1) * acc_old + _dot(p, v)

    @pl.when(ki == 0)
    def _():
        tq = q.shape[0]
        m, acc = update(_dot_nt(q, kx_ref[0]), vx_ref[0], jnp.full((tq, 128), NEG_INF, F32),
                        jnp.zeros((tq, MLA_V_PAD), F32))
        m_ref[...], acc_ref[...] = m, acc

    if with_latent:
        nsub = k_ref.shape[1] // sub
        m, acc = m_ref[...], acc_ref[...]
        s_next = _dot_nt(q, k_ref[0, 0:sub, :])
        for j in range(nsub):
            s = s_next
            if j + 1 < nsub:
                s_next = _dot_nt(q, k_ref[0, (j + 1) * sub:(j + 2) * sub, :])
            m, acc = update(s, v_ref[0, j * sub:(j + 1) * sub, :], m, acc)
        m_ref[...], acc_ref[...] = m, acc

    @pl.when(ki == pl.num_programs(3) - 1)
    def _():
        acc = acc_ref[...]
        o_ref[...] = (acc[:, :MLA_V] / acc[:, MLA_V:]).astype(o_ref.dtype)


def mla_attention(geom, q, k, v, *, with_latent, tq, tk):
    B, T, L = geom.B, geom.T, geom.L
    cblk = geom.BT // L
    ctx_k = pl.BlockSpec((1, L, MLA_QK_PAD), lambda b, h, qi, ki: (h, cblk + b, 0))
    ctx_v = pl.BlockSpec((1, L, MLA_V_PAD), lambda b, h, qi, ki: (h, cblk + b, 0))
    if with_latent:
        nq, nk = T // tq, T // tk
        in_specs = [pl.BlockSpec((1, tq, MLA_QK_PAD), lambda b, h, qi, ki: (h, b * nq + qi, 0)), ctx_k, ctx_v,
                    pl.BlockSpec((1, tk, MLA_QK_PAD), lambda b, h, qi, ki: (h, b * nk + ki, 0)),
                    pl.BlockSpec((1, tk, MLA_V_PAD), lambda b, h, qi, ki: (h, b * nk + ki, 0))]
        args = (q, k, v, k, v)
        out_rows, out_spec = geom.BT, pl.BlockSpec((tq, MLA_V), lambda b, h, qi, ki: (b * nq + qi, h))
    else:
        tq, nq, nk = L, 1, 1
        in_specs = [pl.BlockSpec((1, tq, MLA_QK_PAD), lambda b, h, qi, ki: (h, cblk + b, 0)), ctx_k, ctx_v]
        args = (q, k, v)
        out_rows, out_spec = B * L, pl.BlockSpec((tq, MLA_V), lambda b, h, qi, ki: (b, h))
    return pl.pallas_call(
        functools.partial(_mla_flash_kernel, with_latent=with_latent, sub=min(MLA_SUB_KEYS, tk)),
        grid=(B, MLA_HEADS, nq, nk),
        in_specs=in_specs,
        out_specs=out_spec,
        out_shape=jax.ShapeDtypeStruct((out_rows, MLA_HEADS * MLA_V), BF16),
        scratch_shapes=[pltpu.VMEM((tq, 128), F32), pltpu.VMEM((tq, MLA_V_PAD), F32)],
        compiler_params=_cparams("parallel", "parallel", "parallel", "arbitrary"),
    )(*args)


RW_Z_R, RW_Z_K, RW_Z_V = 0, RW_WIDTH, 2 * RW_WIDTH
RW_Z_LORA = 3 * RW_WIDTH
RW_Z_COLS = 3 * RW_WIDTH + 5 * LORA_PAD


def _dot3(a, b):
    return _mmx(_pieces(a, 3), _pieces(b, 3), _dot)


def _head_sum(x, ones):
    ones = (ones.astype(BF16),)
    return jnp.concatenate(
        [_mmx(_pieces(x[:, c * 128:(c + 1) * 128], 3), ones, _dot) for c in range(x.shape[1] // 128)], axis=-1)


def _rw_prep_kernel(*refs, geom, tm, has_vres):
    if has_vres:
        (z_ref, zp_ref, zn_ref, mu_ref, w0_ref, w2_ref, a0_ref, a2_ref, g2_ref, kk_ref, ka_ref, rk_ref, ones_ref,
         vf_ref, v0_ref, v1_ref, v2_ref,
         r_o, v_o, kk_o, lwf_o, kf_o, bf_o, lwb_o, kb_o, bb_o, g_o, bonus_o) = refs
    else:
        (z_ref, zp_ref, zn_ref, mu_ref, w0_ref, w2_ref, a0_ref, a2_ref, g2_ref, kk_ref, ka_ref, rk_ref, ones_ref,
         r_o, v_o, kk_o, lwf_o, kf_o, bf_o, lwb_o, kb_o, bb_o, g_o, bonus_o) = refs
    first, last = _seq_edge_masks(geom, pl.program_id(0) * tm, tm)
    z = z_ref[...]
    up, dn = _shifted_rows(z, zp_ref[...], zn_ref[...], first, last)
    z = z + mu_ref[...] * (0.5 * (up + dn) - z)
    r = z[:, RW_Z_R:RW_Z_R + RW_WIDTH]
    k = z[:, RW_Z_K:RW_Z_K + RW_WIDTH]
    v = z[:, RW_Z_V:RW_Z_V + RW_WIDTH]
    lora = lambda n: z[:, RW_Z_LORA + n * LORA_PAD:RW_Z_LORA + (n + 1) * LORA_PAD]
    ones = ones_ref[...]
    if has_vres:
        mix = jax.nn.sigmoid(v0_ref[...] + _dot3(_dot3(v, v1_ref[...]), v2_ref[...]))
        v = v + (vf_ref[...] - v) * mix
    g_o[...] = _dot3(jax.nn.sigmoid(lora(4)), g2_ref[...])
    kk = k * kk_ref[...]
    kk = kk / jnp.maximum(jnp.sqrt(_head_sum(kk * kk, ones)), 1e-12)
    ksum = None
    for d, (lw_o, k_o, b_o) in enumerate(((lwf_o, kf_o, bf_o), (lwb_o, kb_o, bb_o))):
        x = -(w0_ref[d:d + 1, :] + _dot3(jnp.tanh(lora(d)), w2_ref[d]))
        softplus = jnp.maximum(x, 0.0) + jnp.log1p(jnp.exp(-jnp.abs(x)))
        lw_o[...] = -jnp.exp(-softplus - 0.5)
        a = jax.nn.sigmoid(a0_ref[d:d + 1, :] + _dot3(lora(2 + d), a2_ref[d]))
        kd = k * (1.0 + (a - 1.0) * ka_ref[...])
        k_o[...] = kd
        b_o[...] = kk * a
        ksum = kd if ksum is None else ksum + kd
    r_o[...] = r
    v_o[...] = v
    kk_o[...] = kk
    bonus_o[...] = _head_sum(r * ksum * rk_ref[...], ones) * v


def rw_prep(geom, z, p, v_first, vres, *, tm):
    R = z.shape[0]
    has_vres = vres is not None
    full = lambda a: pl.BlockSpec(a.shape, lambda i: (0,) * a.ndim)
    row = pl.BlockSpec((tm, RW_WIDTH), lambda i: (i, 0))
    prev, nxt = _halo_specs(tm, R, RW_Z_COLS, lambda i: 0)
    params = [p["mu"], p["w0"], p["w2"], p["a0"], p["a2"], p["g2"], p["kk"], p["ka"], p["rk"], p["ones"]]
    in_specs = [pl.BlockSpec((tm, RW_Z_COLS), lambda i: (i, 0)), prev, nxt] + [full(a) for a in params]
    args = [z, z, z] + params
    if has_vres:
        in_specs += [row] + [full(a) for a in vres]
        args += [v_first] + list(vres)
    return pl.pallas_call(
        functools.partial(_rw_prep_kernel, geom=geom, tm=tm, has_vres=has_vres),
        grid=(R // tm,),
        in_specs=in_specs,
        out_specs=[row] * 11,
        out_shape=[jax.ShapeDtypeStruct((R, RW_WIDTH), F32)] * 11,
        compiler_params=_cparams("parallel"),
    )(*args)


def _pieces(x, passes):
    hi = x.astype(BF16)
    if passes == 1:
        return (hi,)
    return hi, (x - hi.astype(F32)).astype(BF16)


def _mmx(a, b, dot):
    out = dot(a[0], b[0])
    if len(a) > 1:
        out = out + dot(a[1], b[0])
    if len(b) > 1:
        out = out + dot(a[0], b[1])
    return out


def _stack_pair(first_head, x):
    return jnp.concatenate([jnp.where(first_head, x, 0.0), jnp.where(first_head, 0.0, x)], axis=0)


def _fold_pair(x):
    half = x.shape[0] // 2
    return x[:half] + x[half:]


def _rw_fused_scan_kernel(rf_ref, vf_ref, kkf_ref, lwf_ref, kf_ref, bf_ref,
                          rb_ref, vb_ref, kkb_ref, lwb_ref, kb_ref, bb_ref, yf_o, yb_o, s_ref, *, nchunk):
    C = RW_CHUNK
    PW = RW_PAIR
    ri = lax.broadcasted_iota(jnp.int32, (PW, PW), 0)
    ci = lax.broadcasted_iota(jnp.int32, (PW, PW), 1)
    eye = ri == ci
    ri, ci = ri & (C - 1), ci & (C - 1)
    first_head = lax.broadcasted_iota(jnp.int32, (1, PW), 1) < RW_HEAD_DIM
    stack = functools.partial(_stack_pair, first_head)

    @pl.when(pl.program_id(1) == 0)
    def _():
        s_ref[...] = jnp.zeros_like(s_ref)

    dirs = ((rf_ref, vf_ref, kkf_ref, lwf_ref, kf_ref, bf_ref, yf_o, ci < ri, C - 1),
            (rb_ref, vb_ref, kkb_ref, lwb_ref, kb_ref, bb_ref, yb_o, ci > ri, 0))

    def chunk_step(step, carry):
        jobs = []
        for d, (r_ref, v_ref, kk_ref, lw_ref, k_ref, b_ref, y_o, before, tot_row) in enumerate(dirs):
            chunk = step if d == 0 else nchunk - 1 - step
            rows = pl.ds(pl.multiple_of(chunk * C, C), C)
            r, v, kk = r_ref[rows, :], v_ref[rows, :], kk_ref[rows, :]
            lw, kd, bd = lw_ref[rows, :], k_ref[rows, :], b_ref[rows, :]
            incl = (before | eye)[:C, :C].astype(BF16)
            cum, rest = None, lw
            for _ in range(3):
                piece = rest.astype(BF16)
                rest = rest - piece.astype(F32)
                part = _dot(incl, piece)
                cum = part if cum is None else cum + part
            c_tot = cum[tot_row:tot_row + 1, :]
            at = -kk * jnp.exp(cum - lw)
            rt = r * jnp.exp(cum)
            e_neg = jnp.exp(-cum)
            bt, kt = bd * e_neg, kd * e_neg
            e_rest = jnp.exp(c_tot - cum)
            bc, kc = bd * e_rest, kd * e_rest
            e_tot = jnp.exp(c_tot)
            for p in range(RW_WIDTH // PW):
                ps = slice(p * PW, (p + 1) * PW)
                jobs.append(dict(d=d, ps=ps, rows=rows, y_o=y_o, before=before, incl=before | eye, e=e_tot[:, ps],
                                 at=stack(at[:, ps]), rt=stack(rt[:, ps]), bt=stack(bt[:, ps]), kt=stack(kt[:, ps]),
                                 bc=stack(bc[:, ps]), kc=stack(kc[:, ps]), v=stack(v[:, ps])))
        for j in jobs:
            p = _mmx(_pieces(jnp.concatenate([j["at"], j["rt"]], axis=0), RW_PASSES_LOCAL),
                     _pieces(jnp.concatenate([j["bt"], j["kt"]], axis=0), RW_PASSES_LOCAL), _dot_nt)
            j["a_ab"] = jnp.where(j["before"], p[:PW, :PW], 0.0)
            j["a_ak"] = jnp.where(j["before"], p[:PW, PW:], 0.0)
            j["a_rb"] = jnp.where(j["incl"], p[PW:, :PW], 0.0)
            j["a_rk"] = jnp.where(j["incl"], p[PW:, PW:], 0.0)
            j["vp"] = _pieces(j["v"], RW_PASSES_LOCAL)
        for j in jobs:
            j["w1"] = _mmx(_pieces(j["a_ak"], RW_PASSES_LOCAL), j["vp"], _dot)
            j["tinv"] = jnp.where(eye, 1.0, j["a_ab"])
            j["pw"] = j["a_ab"]
        for _ in range(int(np.log2(C)) - 1):
            for j in jobs:
                pw = _pieces(j["pw"], RW_PASSES_LOCAL)
                j["pw"] = _mmx(pw, pw, _dot)
            for j in jobs:
                j["tinv"] = j["tinv"] + _mmx(_pieces(j["tinv"], RW_PASSES_LOCAL), _pieces(j["pw"], RW_PASSES_LOCAL),
                                             _dot)
        for j in jobs:
            tw = _mmx(_pieces(j["tinv"], RW_PASSES_LOCAL),
                      _pieces(jnp.concatenate([j["at"], j["w1"]], axis=1), RW_PASSES_LOCAL), _dot)
            j["tw"] = tw
            j["twp"] = _pieces(tw, RW_PASSES_LOCAL)
        for j in jobs:
            ry = _mmx(_pieces(j["a_rb"], RW_PASSES_LOCAL), j["twp"], _dot)
            yk = _mmx(_pieces(j["a_rk"], RW_PASSES_LOCAL), j["vp"], _dot)
            j["rr"] = j["rt"] + ry[:, :PW]
            j["yl"] = _fold_pair(ry[:, PW:] + yk)
            j["m"] = _mmx(_pieces(j["bc"], RW_PASSES_TRANSITION), _pieces(j["tw"][:, :PW], RW_PASSES_TRANSITION),
                          _dot_tn)
            uv = jnp.concatenate([j["tw"][:, PW:], j["v"]], axis=0)
            bk = jnp.concatenate([j["bc"], j["kc"]], axis=0)
            j["n"] = _fold_pair(_mmx(_pieces(uv, RW_PASSES_STATE), _pieces(bk, RW_PASSES_STATE), _dot_tn))
        for j in jobs:
            s = s_ref[j["d"], :, j["ps"]]
            sp = _pieces(stack(s), RW_PASSES_STATE)
            y = _mmx(_pieces(j["rr"], RW_PASSES_STATE), sp, _dot_nt)
            j["y_o"][j["rows"], j["ps"]] = _fold_pair(y) + j["yl"]
            sm = _mmx(sp[:RW_PASSES_TRANSITION], _pieces(j["m"], RW_PASSES_TRANSITION), _dot_nt)
            s_ref[j["d"], :, j["ps"]] = s * j["e"] + _fold_pair(sm) + j["n"]
        return carry

    lax.fori_loop(0, nchunk, chunk_step, 0)


def rw_fused_scan(geom, r, v, kk, lwf, kf, bf, lwb, kb, bb):
    B, T, L = geom.B, geom.T, geom.L
    blk = L
    nlat = T // blk
    cblk = geom.BT // blk
    fwd = pl.BlockSpec((blk, RW_WIDTH), lambda b, s: (jnp.where(s == 0, cblk + b, b * nlat + s - 1), 0))
    bwd = pl.BlockSpec((blk, RW_WIDTH), lambda b, s: (jnp.where(s == 0, cblk + b, b * nlat + nlat - s), 0))
    return pl.pallas_call(
        functools.partial(_rw_fused_scan_kernel, nchunk=blk // RW_CHUNK),
        grid=(B, nlat + 1),
        in_specs=[fwd] * 6 + [bwd] * 6,
        out_specs=[fwd, bwd],
        out_shape=[jax.ShapeDtypeStruct((geom.R, RW_WIDTH), F32)] * 2,
        scratch_shapes=[pltpu.VMEM((2, RW_HEAD_DIM, RW_WIDTH), F32)],
        compiler_params=_cparams("parallel", "arbitrary"),
    )(r, v, kk, lwf, kf, bf, r, v, kk, lwb, kb, bb)


def _rw_post_kernel(yf_ref, yb_ref, bonus_ref, g_ref, lng_ref, lnb_ref, ones_ref, o_ref):
    ones = ones_ref[...]
    y = yf_ref[...] + yb_ref[...]
    mean = _head_sum(y, ones) * (1.0 / RW_HEAD_DIM)
    yc = y - mean
    var = _head_sum(yc * yc, ones) * (1.0 / RW_HEAD_DIM)
    y = yc * lax.rsqrt(var + RW_LNX_EPS) * lng_ref[...] + lnb_ref[...]
    o_ref[...] = ((y + bonus_ref[...]) * g_ref[...]).astype(o_ref.dtype)


def rw_post(yf, yb, bonus, g, lnx_g, lnx_b, ones, *, tm):
    R = yf.shape[0]
    row = pl.BlockSpec((tm, RW_WIDTH), lambda i: (i, 0))
    full = lambda a: pl.BlockSpec(a.shape, lambda i: (0,) * a.ndim)
    return pl.pallas_call(
        _rw_post_kernel,
        grid=(R // tm,),
        in_specs=[row] * 4 + [full(lnx_g), full(lnx_b), full(ones)],
        out_specs=row,
        out_shape=jax.ShapeDtypeStruct((R, RW_WIDTH), BF16),
        compiler_params=_cparams("parallel"),
    )(yf, yb, bonus, g, lnx_g, lnx_b, ones)


def _rw_in_cols(w):
    parts = [w[..., :3 * RW_WIDTH]]
    off = 3 * RW_WIDTH
    for n in (RW_DECAY_LORA, RW_DECAY_LORA, RW_AAA_LORA, RW_AAA_LORA, RW_GATE_LORA):
        parts.append(_pad_cols(w[..., off:off + n], LORA_PAD))
        off += n
    return jnp.concatenate(parts, axis=-1)


def _pad_rows(w, n):
    return jnp.pad(w, [(0, 0)] * (w.ndim - 2) + [(0, n - w.shape[-2]), (0, 0)])


def _mla_wq_cols(w):
    w = w.reshape(w.shape[0], MLA_HEADS, MLA_NOPE + MLA_ROPE)
    return _pad_cols(w, MLA_QK_PAD).reshape(w.shape[0], MLA_HEADS * MLA_QK_PAD)


def kernel(x, c, ctx, c_ctx, ada_w, ada_b, norm1_g, w_in, rw_mu, rw_w0, rw_w2, rw_a0, rw_a2, rw_g2, rw_kk, rw_ka,
           rw_rk, rw_lnx_g, rw_lnx_b, rw_v0, rw_v1, rw_v2, wa_sink, mla_qnorm_g, mla_kvnorm_g, mla_w_uq, mla_w_ukv,
           w_branch, w_out, norm2_g, ffn_w_in, ffn_conv_w, ffn_conv_b, ffn_w_out, final_norm_g):
    B, T, D = x.shape
    L = ctx.shape[1]
    depth = w_in.shape[0]
    F = ffn_w_out.shape[1]
    geom = Geom(B, T, L)
    tm = _pick_tile(T, (512, 256, 128))
    assert (B * L) % tm == 0
    tmm = _pick_tile(geom.R, (1280, 1024, 640, 512, 256, 128))
    tmm_wide = _pick_tile(geom.R, (1664, 1280, 1024, 640, 512, 256, 128))

    rw_cols = 3 * RW_WIDTH + 2 * RW_DECAY_LORA + 2 * RW_AAA_LORA + RW_GATE_LORA
    wa_cols = WA_WIDTH + 2 * WA_KV_WIDTH
    mla_cols = MLA_Q_LORA + MLA_KV_LORA + MLA_ROPE
    mla_cols_pad = MLA_Q_LORA + MLA_KV_LORA + 128

    cos_wa, sin_wa = _rope_tables(geom, WA_HEAD_DIM)
    cos_wa, sin_wa = jnp.tile(cos_wa, (1, 2)), jnp.tile(sin_wa, (1, 2))
    cos_m, sin_m = _rope_tables(geom, MLA_ROPE)
    one, zero = jnp.ones((geom.R, MLA_NOPE), F32), jnp.zeros((geom.R, MLA_NOPE), F32)
    cos_mla = jnp.concatenate([one, cos_m, one[:, :64]], axis=-1)
    sin_mla = jnp.concatenate([zero, sin_m, zero[:, :64]], axis=-1)
    lane = np.arange(128)
    ones_blk = jnp.asarray((lane[:, None] // RW_HEAD_DIM) == (lane[None, :] // RW_HEAD_DIM), F32)

    xs = jnp.concatenate([x.reshape(B * T, D), ctx.reshape(B * L, D)], axis=0)
    cvec = jnp.concatenate([c, c_ctx[None, :], jnp.zeros((8 - (B + 1) % 8, D), F32)], axis=0)
    v_first = None
    for l in range(depth):
        need_ctx = l < depth - 1
        mod = ada_modulation(cvec, ada_w[l], ada_b[l])
        mod = [mod[:, k * D:(k + 1) * D].reshape(-1, 1, D) for k in range(6)]

        w = w_in[l]
        tn_in = _pick_tile(3 * D, (768, 512, 256, 128))
        w_rw = _pad_cols(_rw_in_cols(w[:, :rw_cols]), _round_up(RW_Z_COLS, tn_in)).astype(BF16)
        w_wa = _pad_cols(w[:, rw_cols:rw_cols + wa_cols], _round_up(wa_cols, tn_in)).astype(BF16)
        w_mla = _pad_cols(w[:, rw_cols + wa_cols:rw_cols + wa_cols + mla_cols],
                          _round_up(mla_cols_pad, tn_in)).astype(BF16)
        w_gate = w[:, rw_cols + wa_cols + mla_cols:].astype(BF16)
        z_rw, z_wa, z_mla, gates = in_projection(geom, xs, norm1_g[l], mod[0], mod[1], w_rw, w_wa, w_mla, w_gate,
                                                 tm=tmm, tn=tn_in)

        rw_p = dict(
            mu=_rw_in_cols(rw_mu[l][None, :]), w0=rw_w0[l], w2=_pad_rows(rw_w2[l], LORA_PAD), a0=rw_a0[l],
            a2=_pad_rows(rw_a2[l], LORA_PAD), g2=_pad_rows(rw_g2[l], LORA_PAD), kk=rw_kk[l][None, :],
            ka=rw_ka[l][None, :], rk=rw_rk[l].reshape(1, RW_WIDTH), ones=ones_blk)
        vres = None if l == 0 else (rw_v0[l - 1][None, :], rw_v1[l - 1], rw_v2[l - 1])
        r, v, kk, lwf, kf, bf, lwb, kb, bb, g, bonus = rw_prep(geom, z_rw, rw_p, v_first, vres, tm=min(tm, 256))
        if l == 0:
            v_first = v
        yf, yb = rw_fused_scan(geom, r, v, kk, lwf, kf, bf, lwb, kb, bb)
        o_a = rw_post(yf, yb, bonus, g, rw_lnx_g[l][None, :], rw_lnx_b[l][None, :], ones_blk, tm=tm)

        q_wa, k_wa, v_wa = wa_prep(z_wa, cos_wa, sin_wa, tm=tm)
        ob_l = wa_attention(geom, q_wa, k_wa, v_wa, wa_sink[l], local=True)
        parts = [ob_l]
        if need_ctx:
            parts.append(wa_attention(geom, q_wa, k_wa, v_wa, wa_sink[l], local=False))
        else:
            parts.append(jnp.zeros((B * L, WA_WIDTH), BF16))
        o_b = jnp.concatenate(parts, axis=0)

        wq = _mla_wq_cols(mla_w_uq[l]).astype(BF16)
        q_m, k_m, v_m = mla_prep(z_mla, mla_qnorm_g[l], mla_kvnorm_g[l], wq, mla_w_ukv[l].astype(BF16),
                                 cos_mla, sin_mla, tm=tm)
        tq = _pick_tile(T, (1024, 512, 256, 128))
        tk = _pick_tile(T, (4096, 2048, 1024, 512, 256, 128))
        parts = [mla_attention(geom, q_m, k_m, v_m, with_latent=True, tq=tq, tk=tk)]
        if need_ctx:
            parts.append(mla_attention(geom, q_m, k_m, v_m, with_latent=False, tq=L, tk=L))
        else:
            parts.append(jnp.zeros((B * L, MLA_HEADS * MLA_V), BF16))
        o_c = jnp.concatenate(parts, axis=0)

        tn_d = _pick_tile(D, (1024, 512, 256, 128))
        y = merge_branches(o_a, o_b, o_c, gates, w_branch[l].astype(BF16), tm=tmm, tn=tn_d)
        xs = matmul_gated_residual(geom, y, w_out[l].astype(BF16), xs, mod[2], tm=tmm, tn=tn_d)

        tf = _pick_tile(F, (512, 256, 128))
        hmid = ffn_in_conv_glu(geom, xs, norm2_g[l], mod[3], mod[4], ffn_w_in[l].astype(BF16), ffn_conv_w[l],
                               ffn_conv_b[l], tm=tmm_wide, tf=tf)
        xs = matmul_gated_residual(geom, hmid, ffn_w_out[l].astype(BF16), xs, mod[5], tm=tmm,
                                   tn=_pick_tile(D, (512, 256, 128)))

    out = final_rmsnorm(xs, final_norm_g, B * T, tm=tm)
    return out.reshape(B, T, D)
```

```python
import functools

import jax
import jax.numpy as jnp
import numpy as np
from jax import lax
from jax.experimental import pallas as pl
from jax.experimental.pallas import tpu as pltpu

F32 = jnp.float32
BF16 = jnp.bfloat16
HIGHEST = lax.Precision.HIGHEST

NORM_EPS = 1e-6
NEG_INF = -1e30
GRID_W = 64
ROPE_BASE = 10000.0

RW_HEADS = 16
RW_HEAD_DIM = 64
RW_WIDTH = RW_HEADS * RW_HEAD_DIM
RW_DECAY_LORA = 96
RW_AAA_LORA = 96
RW_GATE_LORA = 64
RW_LNX_EPS = 64e-5
RW_CHUNK = 64
RW_PAIR = 2 * RW_HEAD_DIM
RW_PASSES_LOCAL = 1
RW_PASSES_STATE = 3
RW_PASSES_TRANSITION = 1
LORA_PAD = 128

WA_HEADS = 16
WA_KV_HEADS = 4
WA_GROUP = WA_HEADS // WA_KV_HEADS
WA_HEAD_DIM = 64
WA_WIDTH = WA_HEADS * WA_HEAD_DIM
WA_KV_WIDTH = WA_KV_HEADS * WA_HEAD_DIM
WA_V_PAD = WA_KV_HEADS * 128
WINDOW = 128
WA_SCALE = WA_HEAD_DIM ** -0.5

MLA_HEADS = 8
MLA_NOPE = 128
MLA_ROPE = 64
MLA_V = 128
MLA_Q_LORA = 512
MLA_KV_LORA = 512
MLA_QK_PAD = 256
MLA_V_PAD = 256
MLA_SCALE = (MLA_NOPE + MLA_ROPE) ** -0.5
MLA_SUB_KEYS = 2048
LOG2_E = 1.4426950408889634

CONV_W = 3
VMEM_LIMIT_BYTES = 56 * 1024 * 1024


def _cparams(*sem):
    return pltpu.CompilerParams(dimension_semantics=sem, vmem_limit_bytes=VMEM_LIMIT_BYTES)


def _dot(a, b, precision=None):
    return jnp.dot(a, b, preferred_element_type=F32, precision=precision)


def _dot_nt(a, b, precision=None):
    return lax.dot_general(a, b, (((1,), (1,)), ((), ())), preferred_element_type=F32, precision=precision)


def _dot_tn(a, b, precision=None):
    return lax.dot_general(a, b, (((0,), (0,)), ((), ())), preferred_element_type=F32, precision=precision)


def _pick_tile(n, candidates):
    for c in candidates:
        if n % c == 0:
            return c
    raise ValueError(f"no tile in {candidates} divides {n}")


def _pad_cols(w, n):
    return jnp.pad(w, [(0, 0)] * (w.ndim - 1) + [(0, n - w.shape[-1])])


def _round_up(n, m):
    return (n + m - 1) // m * m


def _column_tiles(w, tn):
    K, N = w.shape
    return w.reshape(K, N // tn, tn).transpose(1, 0, 2)


class Geom:
    def __init__(self, B, T, L):
        assert T & (T - 1) == 0 and L & (L - 1) == 0, "sequence lengths must be powers of two"
        assert T % L == 0 and L % RW_CHUNK == 0 and T % GRID_W == 0
        self.B, self.T, self.L = B, T, L
        self.BT = B * T
        self.R = B * T + B * L


def _select_row_group(geom, row0, tm, tab_ref):
    r = row0 + lax.broadcasted_iota(jnp.int32, (tm, 1), 0)
    out = tab_ref[geom.B]
    for b in range(geom.B):
        out = jnp.where((r >= b * geom.T) & (r < (b + 1) * geom.T), tab_ref[b], out)
    return out


def _seq_edge_masks(geom, row0, tm):
    r = row0 + lax.broadcasted_iota(jnp.int32, (tm, 1), 0)
    is_lat = r < geom.BT
    pos = jnp.where(is_lat, r & (geom.T - 1), (r - geom.BT) & (geom.L - 1))
    last = jnp.where(is_lat, geom.T - 1, geom.L - 1)
    return pos == 0, pos == last


def _shifted_rows(x, prev8, next8, first, last):
    tm = x.shape[0]
    rid = lax.broadcasted_iota(jnp.int32, (tm, 1), 0)
    up = jnp.where(rid == 0, prev8[7:8, :], pltpu.roll(x, 1, axis=0))
    dn = jnp.where(rid == tm - 1, next8[0:1, :], pltpu.roll(x, tm - 1, axis=0))
    return jnp.where(first, 0.0, up), jnp.where(last, 0.0, dn)


def _halo_specs(tm, R, width, col_of):
    nb8 = tm // 8
    prev = pl.BlockSpec((8, width), lambda i, *a: (jnp.maximum(i * nb8 - 1, 0), col_of(i, *a)))
    nxt = pl.BlockSpec((8, width), lambda i, *a: (jnp.minimum((i + 1) * nb8, R // 8 - 1), col_of(i, *a)))
    return prev, nxt


def _ada_kernel(c_ref, w_ref, b_ref, o_ref):
    c = c_ref[...]
    o_ref[...] = _dot(c * jax.nn.sigmoid(c), w_ref[...], HIGHEST) + b_ref[...]


def ada_modulation(cvec, w, b):
    G, D = cvec.shape
    N = w.shape[1]
    tn = _pick_tile(N, (1024, 512, 256, 128))
    return pl.pallas_call(
        _ada_kernel,
        grid=(N // tn,),
        in_specs=[pl.BlockSpec((G, D), lambda j: (0, 0)),
                  pl.BlockSpec((D, tn), lambda j: (0, j)),
                  pl.BlockSpec((1, tn), lambda j: (0, j))],
        out_specs=pl.BlockSpec((G, tn), lambda j: (0, j)),
        out_shape=jax.ShapeDtypeStruct((G, N), F32),
        compiler_params=_cparams("arbitrary"),
    )(cvec, w, b.reshape(1, N))


def _norm_modulate(x, g, sc, sh):
    gain = g * (1.0 + sc)
    return (x * lax.rsqrt(jnp.mean(x * x, axis=-1, keepdims=True) + NORM_EPS) * gain + sh).astype(BF16)


PROLOGUE_ROWS = 256


def _norm_modulate_rows(geom, row0, x_ref, g_ref, sc_ref, sh_ref, h_ref, h_off):
    tm = x_ref.shape[0]
    step = PROLOGUE_ROWS if tm % PROLOGUE_ROWS == 0 else tm
    one_group = geom.T % step == 0 and geom.BT % step == 0
    g = g_ref[...]
    for r in range(0, tm, step):
        if one_group:
            sc, sh = sc_ref[geom.B], sh_ref[geom.B]
            for b in range(geom.B):
                in_b = (row0 + r >= b * geom.T) & (row0 + r < (b + 1) * geom.T)
                sc, sh = jnp.where(in_b, sc_ref[b], sc), jnp.where(in_b, sh_ref[b], sh)
        else:
            sc = _select_row_group(geom, row0 + r, step, sc_ref)
            sh = _select_row_group(geom, row0 + r, step, sh_ref)
        h_ref[h_off + r:h_off + r + step] = _norm_modulate(x_ref[r:r + step], g, sc, sh)


def _in_proj_kernel(x_ref, g_ref, sh_ref, sc_ref, w_ref, zrw_ref, zwa_ref, zmla_ref, gate_ref, h_ref,
                    *, geom, tm, starts):
    j = pl.program_id(1)

    @pl.when(j == 0)
    def _():
        _norm_modulate_rows(geom, pl.program_id(0) * tm, x_ref, g_ref, sc_ref, sh_ref, h_ref, 0)

    acc = _dot(h_ref[...], w_ref[0])
    wa0, mla0, gate0 = starts

    @pl.when(j < wa0)
    def _():
        zrw_ref[...] = acc

    @pl.when((j >= wa0) & (j < mla0))
    def _():
        zwa_ref[...] = acc

    @pl.when((j >= mla0) & (j < gate0))
    def _():
        zmla_ref[...] = acc

    @pl.when(j >= gate0)
    def _():
        gate_ref[...] = jax.nn.sigmoid(acc).astype(gate_ref.dtype)


def in_projection(geom, x, g, shift, scale, w_rw, w_wa, w_mla, w_gate, *, tm, tn):
    R, K = x.shape
    widths = [w.shape[1] for w in (w_rw, w_wa, w_mla, w_gate)]
    assert R % tm == 0 and all(n % tn == 0 for n in widths)
    tiles = [n // tn for n in widths]
    starts = tuple(int(v) for v in np.cumsum(tiles)[:3])
    w_all = _column_tiles(jnp.concatenate([w_rw, w_wa, w_mla, w_gate], axis=1), tn)
    full = lambda a: pl.BlockSpec(a.shape, lambda i, j: (0,) * a.ndim)
    out_spec = lambda first, n: pl.BlockSpec((tm, tn), lambda i, j: (i, jnp.clip(j - first, 0, n - 1)))
    firsts = (0,) + starts
    return pl.pallas_call(
        functools.partial(_in_proj_kernel, geom=geom, tm=tm, starts=starts),
        grid=(R // tm, sum(tiles)),
        in_specs=[pl.BlockSpec((tm, K), lambda i, j: (i, 0), pipeline_mode=pl.Buffered(1)),
                  pl.BlockSpec((1, K), lambda i, j: (0, 0)),
                  full(shift), full(scale),
                  pl.BlockSpec((1, K, tn), lambda i, j: (j, 0, 0))],
        out_specs=[out_spec(f, n) for f, n in zip(firsts, tiles)],
        out_shape=[jax.ShapeDtypeStruct((R, widths[0]), F32), jax.ShapeDtypeStruct((R, widths[1]), F32),
                   jax.ShapeDtypeStruct((R, widths[2]), F32), jax.ShapeDtypeStruct((R, widths[3]), BF16)],
        scratch_shapes=[pltpu.VMEM((tm, K), BF16)],
        compiler_params=_cparams("parallel", "arbitrary"),
    )(x, g.reshape(1, K), shift, scale, w_all)


def _mm_resid_kernel(y_ref, w_ref, r_ref, gate_ref, o_ref, *, geom, tm):
    gate = _select_row_group(geom, pl.program_id(0) * tm, tm, gate_ref)
    o_ref[...] = r_ref[...] + gate * _dot(y_ref[...], w_ref[0])


def matmul_gated_residual(geom, y, w, resid, gate, *, tm, tn):
    R, K = y.shape
    N = w.shape[1]
    G = gate.shape[0]
    assert R % tm == 0 and N % tn == 0
    w = _column_tiles(w, tn)
    return pl.pallas_call(
        functools.partial(_mm_resid_kernel, geom=geom, tm=tm),
        grid=(R // tm, N // tn),
        in_specs=[pl.BlockSpec((tm, K), lambda i, j: (i, 0)),
                  pl.BlockSpec((1, K, tn), lambda i, j: (j, 0, 0)),
                  pl.BlockSpec((tm, tn), lambda i, j: (i, j)),
                  pl.BlockSpec((G, 1, tn), lambda i, j: (0, 0, j))],
        out_specs=pl.BlockSpec((tm, tn), lambda i, j: (i, j)),
        out_shape=jax.ShapeDtypeStruct((R, N), F32),
        compiler_params=_cparams("parallel", "arbitrary"),
    )(y, w, resid, gate)


FFN_HALO = 16


def _ffn_in_kernel(x_ref, xp_ref, xn_ref, g_ref, sh_ref, sc_ref, wg_ref, wu_ref, cw_ref, cb_ref, o_ref, h_ref,
                   *, geom, tm):
    row0 = pl.program_id(0) * tm
    H = FFN_HALO

    @pl.when(pl.program_id(1) == 0)
    def _():
        _norm_modulate_rows(geom, row0, xp_ref, g_ref, sc_ref, sh_ref, h_ref, 0)
        _norm_modulate_rows(geom, row0, x_ref, g_ref, sc_ref, sh_ref, h_ref, H)
        _norm_modulate_rows(geom, row0 + tm - H, xn_ref, g_ref, sc_ref, sh_ref, h_ref, H + tm)

    first, last = _seq_edge_masks(geom, row0, tm)
    gt = _dot(h_ref[...], wg_ref[0])
    u = _dot(h_ref[H:H + tm], wu_ref[0])
    up = jnp.where(first, 0.0, pltpu.roll(gt, 1, axis=0)[H:H + tm])
    dn = jnp.where(last, 0.0, pltpu.roll(gt, tm + 2 * H - 1, axis=0)[H:H + tm])
    cw = cw_ref[...]
    conv = cb_ref[...] + up * cw[0:1, :]
    conv = conv + gt[H:H + tm] * cw[1:2, :]
    conv = conv + dn * cw[2:3, :]
    o_ref[...] = (jax.nn.gelu(conv, approximate=True) * u).astype(o_ref.dtype)


def ffn_in_conv_glu(geom, x, g, shift, scale, w_in, conv_w, conv_b, *, tm, tf):
    R, K = x.shape
    F = w_in.shape[1] // 2
    nj = F // tf
    nbh = tm // FFN_HALO
    assert R % tm == 0 and F % tf == 0 and tm % FFN_HALO == 0
    w_tiles = _column_tiles(w_in, tf)
    full = lambda a: pl.BlockSpec(a.shape, lambda i, j: (0,) * a.ndim)
    return pl.pallas_call(
        functools.partial(_ffn_in_kernel, geom=geom, tm=tm),
        grid=(R // tm, nj),
        in_specs=[pl.BlockSpec((tm, K), lambda i, j: (i, 0), pipeline_mode=pl.Buffered(1)),
                  pl.BlockSpec((FFN_HALO, K), lambda i, j: (jnp.maximum(i * nbh - 1, 0), 0)),
                  pl.BlockSpec((FFN_HALO, K), lambda i, j: (jnp.minimum((i + 1) * nbh, R // FFN_HALO - 1), 0)),
                  pl.BlockSpec((1, K), lambda i, j: (0, 0)),
                  full(shift), full(scale),
                  pl.BlockSpec((1, K, tf), lambda i, j: (j, 0, 0)),
                  pl.BlockSpec((1, K, tf), lambda i, j: (nj + j, 0, 0)),
                  pl.BlockSpec((CONV_W, tf), lambda i, j: (0, j)),
                  pl.BlockSpec((1, tf), lambda i, j: (0, j))],
        out_specs=pl.BlockSpec((tm, tf), lambda i, j: (i, j)),
        out_shape=jax.ShapeDtypeStruct((R, F), BF16),
        scratch_shapes=[pltpu.VMEM((tm + 2 * FFN_HALO, K), BF16)],
        compiler_params=_cparams("parallel", "arbitrary"),
    )(x, x, x, g.reshape(1, K), shift, scale, w_tiles, w_tiles, conv_w, conv_b.reshape(1, F))


def _merge_kernel(oa_ref, ob_ref, oc_ref, ga_ref, gb_ref, gc_ref, w_ref, o_ref):
    y = ga_ref[...] * _dot(oa_ref[...], w_ref[0])
    y = y + gb_ref[...] * _dot(ob_ref[...], w_ref[1])
    y = y + gc_ref[...] * _dot(oc_ref[...], w_ref[2])
    o_ref[...] = y.astype(o_ref.dtype)


def merge_branches(oa, ob, oc, gates, wb, *, tm, tn):
    R, K = oa.shape
    D = wb.shape[2]
    nj = D // tn
    bspec = pl.BlockSpec((tm, K), lambda i, j: (i, 0))
    gspec = lambda k: pl.BlockSpec((tm, tn), lambda i, j: (i, k * nj + j))
    return pl.pallas_call(
        _merge_kernel,
        grid=(R // tm, nj),
        in_specs=[bspec, bspec, bspec, gspec(0), gspec(1), gspec(2),
                  pl.BlockSpec((3, K, tn), lambda i, j: (0, 0, j))],
        out_specs=pl.BlockSpec((tm, tn), lambda i, j: (i, j)),
        out_shape=jax.ShapeDtypeStruct((R, D), BF16),
        compiler_params=_cparams("parallel", "arbitrary"),
    )(oa, ob, oc, gates, gates, gates, wb)


def _rmsnorm_kernel(x_ref, g_ref, o_ref):
    x = x_ref[...]
    o_ref[...] = x * lax.rsqrt(jnp.mean(x * x, axis=-1, keepdims=True) + NORM_EPS) * g_ref[...]


def final_rmsnorm(x, g, rows, *, tm):
    D = x.shape[1]
    return pl.pallas_call(
        _rmsnorm_kernel,
        grid=(rows // tm,),
        in_specs=[pl.BlockSpec((tm, D), lambda i: (i, 0)), pl.BlockSpec((1, D), lambda i: (0, 0))],
        out_specs=pl.BlockSpec((tm, D), lambda i: (i, 0)),
        out_shape=jax.ShapeDtypeStruct((rows, D), F32),
        compiler_params=_cparams("parallel"),
    )(x, g.reshape(1, D))


def _rot_half64(z):
    n = z.shape[-1]
    lane = lax.broadcasted_iota(jnp.int32, z.shape, z.ndim - 1)
    return jnp.where((lane & 63) < 32, pltpu.roll(z, n - 32, axis=z.ndim - 1), pltpu.roll(z, 32, axis=z.ndim - 1))


def _rope_tables(geom, dim):
    nf = dim // 4
    inv = ROPE_BASE ** (-jnp.arange(nf, dtype=F32) / nf)
    rows = geom.T // GRID_W
    row = jnp.repeat(jnp.arange(rows, dtype=F32), GRID_W)
    col = jnp.tile(jnp.arange(GRID_W, dtype=F32), rows)
    ang = jnp.concatenate([row[:, None] * inv, col[:, None] * inv], axis=-1)
    cos, sin = jnp.cos(ang), jnp.sin(ang)
    cos_t = jnp.concatenate([cos, cos], axis=-1)
    sin_t = jnp.concatenate([-sin, sin], axis=-1)
    nctx = geom.B * geom.L
    cos_f = jnp.concatenate([jnp.tile(cos_t, (geom.B, 1)), jnp.ones((nctx, dim), F32)], axis=0)
    sin_f = jnp.concatenate([jnp.tile(sin_t, (geom.B, 1)), jnp.zeros((nctx, dim), F32)], axis=0)
    return cos_f, sin_f


def _wa_prep_kernel(z_ref, cos_ref, sin_ref, q_ref, k_ref, v_ref):
    cos = cos_ref[...]
    sin = sin_ref[...]
    for c in range(WA_WIDTH // 128):
        z = z_ref[:, c * 128:(c + 1) * 128]
        q_ref[:, c * 128:(c + 1) * 128] = ((z * cos + _rot_half64(z) * sin) * (WA_SCALE * LOG2_E)).astype(BF16)
    for c in range(WA_KV_WIDTH // 128):
        z = z_ref[:, WA_WIDTH + c * 128:WA_WIDTH + (c + 1) * 128]
        k_ref[:, c * 128:(c + 1) * 128] = (z * cos + _rot_half64(z) * sin).astype(BF16)
    v_ref[...] = jnp.ones(v_ref.shape, BF16)
    for g in range(WA_KV_HEADS):
        src = WA_WIDTH + WA_KV_WIDTH + g * WA_HEAD_DIM
        v_ref[:, g * 128:g * 128 + WA_HEAD_DIM] = z_ref[:, src:src + WA_HEAD_DIM].astype(BF16)


def wa_prep(z, cos, sin, *, tm):
    R = z.shape[0]
    row = lambda w: pl.BlockSpec((tm, w), lambda i: (i, 0))
    return pl.pallas_call(
        _wa_prep_kernel,
        grid=(R // tm,),
        in_specs=[row(WA_WIDTH + 2 * WA_KV_WIDTH), row(128), row(128)],
        out_specs=[row(WA_WIDTH), row(WA_KV_WIDTH), row(WA_V_PAD)],
        out_shape=[jax.ShapeDtypeStruct((R, WA_WIDTH), BF16),
                   jax.ShapeDtypeStruct((R, WA_KV_WIDTH), BF16),
                   jax.ShapeDtypeStruct((R, WA_V_PAD), BF16)],
        compiler_params=_cparams("parallel"),
    )(z, cos, sin)


def _wa_attn_kernel(*refs, local, nqb, tq):
    if local:
        sink_ref, q_ref, kp_ref, kc_ref, kn_ref, vp_ref, vc_ref, vn_ref, kx_ref, vx_ref, o_ref = refs
    else:
        sink_ref, q_ref, kx_ref, vx_ref, o_ref = refs
    i = pl.program_id(1)
    nk_ctx = kx_ref.shape[0]
    rows = WA_GROUP * tq
    qpos = lax.broadcasted_iota(jnp.int32, (rows, 1), 0) & (tq - 1)
    head_in_group = lax.broadcasted_iota(jnp.int32, (rows, 1), 0) >> int(np.log2(tq))
    if local:
        off_prev = jnp.where(i > 0, 0, tq)
        off_next = jnp.where(i < nqb - 1, 0, tq)
        j = lax.broadcasted_iota(jnp.int32, (1, 3 * tq + nk_ctx), 1)
        valid = ((j >= tq) & (j < 2 * tq)) | (j >= 3 * tq)
        valid = valid | ((j < tq) & (j >= qpos + off_prev))
        valid = valid | ((j >= 2 * tq) & (j < 3 * tq) & ((j - 2 * tq) <= qpos - off_next))
    for g in range(WA_KV_HEADS):
        ks = slice(g * WA_HEAD_DIM, (g + 1) * WA_HEAD_DIM)
        qg = jnp.concatenate(
            [q_ref[:, (g * WA_GROUP + a) * WA_HEAD_DIM:(g * WA_GROUP + a + 1) * WA_HEAD_DIM] for a in range(WA_GROUP)],
            axis=0)
        vs = slice(g * 128, (g + 1) * 128)
        sink = jnp.zeros((rows, 1), F32)
        for a in range(WA_GROUP):
            sink = jnp.where(head_in_group == a, sink_ref[g * WA_GROUP + a] * LOG2_E, sink)
        if local:
            kcat = jnp.concatenate([kp_ref[:, ks], kc_ref[:, ks], kn_ref[:, ks], kx_ref[:, ks]], axis=0)
            vcat = jnp.concatenate([vp_ref[:, vs], vc_ref[:, vs], vn_ref[:, vs], vx_ref[:, vs]], axis=0)
        else:
            kcat, vcat = kx_ref[:, ks], vx_ref[:, vs]
        s = _dot_nt(qg, kcat)
        if local:
            s = jnp.where(valid, s, NEG_INF)
        m = jnp.maximum(jnp.max(s, axis=-1, keepdims=True), sink)
        o = _dot(jnp.exp2(s - m).astype(BF16), vcat)
        o = o[:, :WA_HEAD_DIM] / (o[:, WA_HEAD_DIM:] + jnp.exp2(sink - m))
        for a in range(WA_GROUP):
            h = g * WA_GROUP + a
            o_ref[:, h * WA_HEAD_DIM:(h + 1) * WA_HEAD_DIM] = o[a * tq:(a + 1) * tq].astype(o_ref.dtype)


def wa_attention(geom, q, k, v, sink, *, local):
    B, T, L = geom.B, geom.T, geom.L
    sink_spec = pl.BlockSpec(memory_space=pltpu.SMEM)
    ctx_spec = lambda w: pl.BlockSpec((L, w), lambda b, i: (geom.BT // L + b, 0))
    if local:
        tq = WINDOW
        nqb = T // tq
        kv = lambda f, w: pl.BlockSpec((tq, w), lambda b, i: (b * nqb + f(i), 0))
        prev = lambda i: jnp.maximum(i - 1, 0)
        cur = lambda i: i
        nxt = lambda i: jnp.minimum(i + 1, nqb - 1)
        in_specs = [sink_spec, pl.BlockSpec((tq, WA_WIDTH), lambda b, i: (b * nqb + i, 0)),
                    kv(prev, WA_KV_WIDTH), kv(cur, WA_KV_WIDTH), kv(nxt, WA_KV_WIDTH),
                    kv(prev, WA_V_PAD), kv(cur, WA_V_PAD), kv(nxt, WA_V_PAD),
                    ctx_spec(WA_KV_WIDTH), ctx_spec(WA_V_PAD)]
        args = (sink, q, k, k, k, v, v, v, k, v)
        out_rows, out_spec = geom.BT, pl.BlockSpec((tq, WA_WIDTH), lambda b, i: (b * nqb + i, 0))
    else:
        tq, nqb = L, 1
        in_specs = [sink_spec, pl.BlockSpec((tq, WA_WIDTH), lambda b, i: (geom.BT // L + b, 0)),
                    ctx_spec(WA_KV_WIDTH), ctx_spec(WA_V_PAD)]
        args = (sink, q, k, v)
        out_rows, out_spec = B * L, pl.BlockSpec((tq, WA_WIDTH), lambda b, i: (b, 0))
    return pl.pallas_call(
        functools.partial(_wa_attn_kernel, local=local, nqb=nqb, tq=tq),
        grid=(B, nqb),
        in_specs=in_specs,
        out_specs=out_spec,
        out_shape=jax.ShapeDtypeStruct((out_rows, WA_WIDTH), BF16),
        compiler_params=_cparams("parallel", "arbitrary"),
    )(*args)


def _mla_prep_kernel(z_ref, qg_ref, kvg_ref, wq_ref, wkv_ref, cos_ref, sin_ref, q_ref, k_ref, v_ref):
    def norm(x, g):
        return (x * lax.rsqrt(jnp.mean(x * x, axis=-1, keepdims=True) + NORM_EPS) * g).astype(BF16)

    cos = cos_ref[...]
    sin = sin_ref[...]
    q = _dot(norm(z_ref[:, :MLA_Q_LORA], qg_ref[...]), wq_ref[...])
    kv = _dot(norm(z_ref[:, MLA_Q_LORA:MLA_Q_LORA + MLA_KV_LORA], kvg_ref[...]), wkv_ref[...])
    kr = z_ref[:, MLA_Q_LORA + MLA_KV_LORA:MLA_Q_LORA + MLA_KV_LORA + 128]
    kr = (kr * cos[:, 128:] + _rot_half64(kr) * sin[:, 128:]).astype(BF16)
    for h in range(MLA_HEADS):
        qh = q[:, h * MLA_QK_PAD:(h + 1) * MLA_QK_PAD]
        q_ref[h] = ((qh * cos + _rot_half64(qh) * sin) * (MLA_SCALE * LOG2_E)).astype(BF16)
        k_ref[h, :, :MLA_NOPE] = kv[:, h * 256:h * 256 + MLA_NOPE].astype(BF16)
        k_ref[h, :, MLA_NOPE:] = kr
        v_ref[h, :, :MLA_V] = kv[:, h * 256 + MLA_NOPE:(h + 1) * 256].astype(BF16)
        v_ref[h, :, MLA_V:] = jnp.ones((kv.shape[0], MLA_V_PAD - MLA_V), BF16)


def mla_prep(z, qnorm_g, kvnorm_g, wq, wkv, cos, sin, *, tm):
    R, Z = z.shape
    full = lambda a: pl.BlockSpec(a.shape, lambda i: (0,) * a.ndim)
    qg, kvg = qnorm_g.reshape(1, -1), kvnorm_g.reshape(1, -1)
    hd = lambda w: pl.BlockSpec((MLA_HEADS, tm, w), lambda i: (0, i, 0))
    return pl.pallas_call(
        _mla_prep_kernel,
        grid=(R // tm,),
        in_specs=[pl.BlockSpec((tm, Z), lambda i: (i, 0)), full(qg), full(kvg), full(wq), full(wkv),
                  pl.BlockSpec((tm, MLA_QK_PAD), lambda i: (i, 0)), pl.BlockSpec((tm, MLA_QK_PAD), lambda i: (i, 0))],
        out_specs=[hd(MLA_QK_PAD), hd(MLA_QK_PAD), hd(MLA_V_PAD)],
        out_shape=[jax.ShapeDtypeStruct((MLA_HEADS, R, MLA_QK_PAD), BF16),
                   jax.ShapeDtypeStruct((MLA_HEADS, R, MLA_QK_PAD), BF16),
                   jax.ShapeDtypeStruct((MLA_HEADS, R, MLA_V_PAD), BF16)],
        compiler_params=_cparams("parallel"),
    )(z, qg, kvg, wq, wkv, cos, sin)


def _mla_flash_kernel(*refs, with_latent, sub):
    if with_latent:
        q_ref, kx_ref, vx_ref, k_ref, v_ref, o_ref, m_ref, acc_ref = refs
    else:
        q_ref, kx_ref, vx_ref, o_ref, m_ref, acc_ref = refs
    ki = pl.program_id(3)
    q = q_ref[0]

    def update(s, v, m_old, acc_old):
        cols = [s[:, c * 128:(c + 1) * 128] for c in range(s.shape[1] // 128)]
        mx = functools.reduce(jnp.maximum, cols)
        m_new = jnp.maximum(m_old, jnp.max(mx, axis=-1, keepdims=True))
        alpha = jnp.exp2(m_old - m_new)
        p = jnp.concatenate([jnp.exp2(c - m_new).astype(BF16) for c in cols], axis=-1)
        return m_new, jnp.concatenate([alpha, alpha], axis=-1) * acc_old + _dot(p, v)

    @pl.when(ki == 0)
    def _():
        tq = q.shape[0]
        m, acc = update(_dot_nt(q, kx_ref[0]), vx_ref[0], jnp.full((tq, 128), NEG_INF, F32),
                        jnp.zeros((tq, MLA_V_PAD), F32))
        m_ref[...], acc_ref[...] = m, acc

    if with_latent:
        nsub = k_ref.shape[1] // sub
        m, acc = m_ref[...], acc_ref[...]
        s_next = _dot_nt(q, k_ref[0, 0:sub, :])
        for j in range(nsub):
            s = s_next
            if j + 1 < nsub:
                s_next = _dot_nt(q, k_ref[0, (j + 1) * sub:(j + 2) * sub, :])
            m, acc = update(s, v_ref[0, j * sub:(j + 1) * sub, :], m, acc)
        m_ref[...], acc_ref[...] = m, acc

    @pl.when(ki == pl.num_programs(3) - 1)
    def _():
        acc = acc_ref[...]
        o_ref[...] = (acc[:, :MLA_V] / acc[:, MLA_V:]).astype(o_ref.dtype)


def mla_attention(geom, q, k, v, *, with_latent, tq, tk):
    B, T, L = geom.B, geom.T, geom.L
    cblk = geom.BT // L
    ctx_k = pl.BlockSpec((1, L, MLA_QK_PAD), lambda b, h, qi, ki: (h, cblk + b, 0))
    ctx_v = pl.BlockSpec((1, L, MLA_V_PAD), lambda b, h, qi, ki: (h, cblk + b, 0))
    if with_latent:
        nq, nk = T // tq, T // tk
        in_specs = [pl.BlockSpec((1, tq, MLA_QK_PAD), lambda b, h, qi, ki: (h, b * nq + qi, 0)), ctx_k, ctx_v,
                    pl.BlockSpec((1, tk, MLA_QK_PAD), lambda b, h, qi, ki: (h, b * nk + ki, 0)),
                    pl.BlockSpec((1, tk, MLA_V_PAD), lambda b, h, qi, ki: (h, b * nk + ki, 0))]
        args = (q, k, v, k, v)
        out_rows, out_spec = geom.BT, pl.BlockSpec((tq, MLA_V), lambda b, h, qi, ki: (b * nq + qi, h))
    else:
        tq, nq, nk = L, 1, 1
        in_specs = [pl.BlockSpec((1, tq, MLA_QK_PAD), lambda b, h, qi, ki: (h, cblk + b, 0)), ctx_k, ctx_v]
        args = (q, k, v)
        out_rows, out_spec = B * L, pl.BlockSpec((tq, MLA_V), lambda b, h, qi, ki: (b, h))
    return pl.pallas_call(
        functools.partial(_mla_flash_kernel, with_latent=with_latent, sub=min(MLA_SUB_KEYS, tk)),
        grid=(B, MLA_HEADS, nq, nk),
        in_specs=in_specs,
        out_specs=out_spec,
        out_shape=jax.ShapeDtypeStruct((out_rows, MLA_HEADS * MLA_V), BF16),
        scratch_shapes=[pltpu.VMEM((tq, 128), F32), pltpu.VMEM((tq, MLA_V_PAD), F32)],
        compiler_params=_cparams("parallel", "parallel", "parallel", "arbitrary"),
    )(*args)


RW_Z_R, RW_Z_K, RW_Z_V = 0, RW_WIDTH, 2 * RW_WIDTH
RW_Z_LORA = 3 * RW_WIDTH
RW_Z_COLS = 3 * RW_WIDTH + 5 * LORA_PAD


def _dot3(a, b):
    return _mmx(_pieces(a, 3), _pieces(b, 3), _dot)


def _head_sum(x, ones):
    ones = (ones.astype(BF16),)
    return jnp.concatenate(
        [_mmx(_pieces(x[:, c * 128:(c + 1) * 128], 3), ones, _dot) for c in range(x.shape[1] // 128)], axis=-1)


def _rw_prep_kernel(*refs, geom, tm, has_vres):
    if has_vres:
        (z_ref, zp_ref, zn_ref, mu_ref, w0_ref, w2_ref, a0_ref, a2_ref, g2_ref, kk_ref, ka_ref, rk_ref, ones_ref,
         vf_ref, v0_ref, v1_ref, v2_ref,
         r_o, v_o, kk_o, lwf_o, kf_o, bf_o, lwb_o, kb_o, bb_o, g_o, bonus_o) = refs
    else:
        (z_ref, zp_ref, zn_ref, mu_ref, w0_ref, w2_ref, a0_ref, a2_ref, g2_ref, kk_ref, ka_ref, rk_ref, ones_ref,
         r_o, v_o, kk_o, lwf_o, kf_o, bf_o, lwb_o, kb_o, bb_o, g_o, bonus_o) = refs
    first, last = _seq_edge_masks(geom, pl.program_id(0) * tm, tm)
    z = z_ref[...]
    up, dn = _shifted_rows(z, zp_ref[...], zn_ref[...], first, last)
    z = z + mu_ref[...] * (0.5 * (up + dn) - z)
    r = z[:, RW_Z_R:RW_Z_R + RW_WIDTH]
    k = z[:, RW_Z_K:RW_Z_K + RW_WIDTH]
    v = z[:, RW_Z_V:RW_Z_V + RW_WIDTH]
    lora = lambda n: z[:, RW_Z_LORA + n * LORA_PAD:RW_Z_LORA + (n + 1) * LORA_PAD]
    ones = ones_ref[...]
    if has_vres:
        mix = jax.nn.sigmoid(v0_ref[...] + _dot3(_dot3(v, v1_ref[...]), v2_ref[...]))
        v = v + (vf_ref[...] - v) * mix
    g_o[...] = _dot3(jax.nn.sigmoid(lora(4)), g2_ref[...])
    kk = k * kk_ref[...]
    kk = kk / jnp.maximum(jnp.sqrt(_head_sum(kk * kk, ones)), 1e-12)
    ksum = None
    for d, (lw_o, k_o, b_o) in enumerate(((lwf_o, kf_o, bf_o), (lwb_o, kb_o, bb_o))):
        x = -(w0_ref[d:d + 1, :] + _dot3(jnp.tanh(lora(d)), w2_ref[d]))
        softplus = jnp.maximum(x, 0.0) + jnp.log1p(jnp.exp(-jnp.abs(x)))
        lw_o[...] = -jnp.exp(-softplus - 0.5)
        a = jax.nn.sigmoid(a0_ref[d:d + 1, :] + _dot3(lora(2 + d), a2_ref[d]))
        kd = k * (1.0 + (a - 1.0) * ka_ref[...])
        k_o[...] = kd
        b_o[...] = kk * a
        ksum = kd if ksum is None else ksum + kd
    r_o[...] = r
    v_o[...] = v
    kk_o[...] = kk
    bonus_o[...] = _head_sum(r * ksum * rk_ref[...], ones) * v


def rw_prep(geom, z, p, v_first, vres, *, tm):
    R = z.shape[0]
    has_vres = vres is not None
    full = lambda a: pl.BlockSpec(a.shape, lambda i: (0,) * a.ndim)
    row = pl.BlockSpec((tm, RW_WIDTH), lambda i: (i, 0))
    prev, nxt = _halo_specs(tm, R, RW_Z_COLS, lambda i: 0)
    params = [p["mu"], p["w0"], p["w2"], p["a0"], p["a2"], p["g2"], p["kk"], p["ka"], p["rk"], p["ones"]]
    in_specs = [pl.BlockSpec((tm, RW_Z_COLS), lambda i: (i, 0)), prev, nxt] + [full(a) for a in params]
    args = [z, z, z] + params
    if has_vres:
        in_specs += [row] + [full(a) for a in vres]
        args += [v_first] + list(vres)
    return pl.pallas_call(
        functools.partial(_rw_prep_kernel, geom=geom, tm=tm, has_vres=has_vres),
        grid=(R // tm,),
        in_specs=in_specs,
        out_specs=[row] * 11,
        out_shape=[jax.ShapeDtypeStruct((R, RW_WIDTH), F32)] * 11,
        compiler_params=_cparams("parallel"),
    )(*args)


def _pieces(x, passes):
    hi = x.astype(BF16)
    if passes == 1:
        return (hi,)
    return hi, (x - hi.astype(F32)).astype(BF16)


def _mmx(a, b, dot):
    out = dot(a[0], b[0])
    if len(a) > 1:
        out = out + dot(a[1], b[0])
    if len(b) > 1:
        out = out + dot(a[0], b[1])
    return out


def _stack_pair(first_head, x):
    return jnp.concatenate([jnp.where(first_head, x, 0.0), jnp.where(first_head, 0.0, x)], axis=0)


def _fold_pair(x):
    half = x.shape[0] // 2
    return x[:half] + x[half:]


def _rw_fused_scan_kernel(rf_ref, vf_ref, kkf_ref, lwf_ref, kf_ref, bf_ref,
                          rb_ref, vb_ref, kkb_ref, lwb_ref, kb_ref, bb_ref, yf_o, yb_o, s_ref, *, nchunk):
    C = RW_CHUNK
    PW = RW_PAIR
    ri = lax.broadcasted_iota(jnp.int32, (PW, PW), 0)
    ci = lax.broadcasted_iota(jnp.int32, (PW, PW), 1)
    eye = ri == ci
    ri, ci = ri & (C - 1), ci & (C - 1)
    first_head = lax.broadcasted_iota(jnp.int32, (1, PW), 1) < RW_HEAD_DIM
    stack = functools.partial(_stack_pair, first_head)

    @pl.when(pl.program_id(1) == 0)
    def _():
        s_ref[...] = jnp.zeros_like(s_ref)

    dirs = ((rf_ref, vf_ref, kkf_ref, lwf_ref, kf_ref, bf_ref, yf_o, ci < ri, C - 1),
            (rb_ref, vb_ref, kkb_ref, lwb_ref, kb_ref, bb_ref, yb_o, ci > ri, 0))

    def chunk_step(step, carry):
        jobs = []
        for d, (r_ref, v_ref, kk_ref, lw_ref, k_ref, b_ref, y_o, before, tot_row) in enumerate(dirs):
            chunk = step if d == 0 else nchunk - 1 - step
            rows = pl.ds(pl.multiple_of(chunk * C, C), C)
            r, v, kk = r_ref[rows, :], v_ref[rows, :], kk_ref[rows, :]
            lw, kd, bd = lw_ref[rows, :], k_ref[rows, :], b_ref[rows, :]
            incl = (before | eye)[:C, :C].astype(BF16)
            cum, rest = None, lw
            for _ in range(3):
                piece = rest.astype(BF16)
                rest = rest - piece.astype(F32)
                part = _dot(incl, piece)
                cum = part if cum is None else cum + part
            c_tot = cum[tot_row:tot_row + 1, :]
            at = -kk * jnp.exp(cum - lw)
            rt = r * jnp.exp(cum)
            e_neg = jnp.exp(-cum)
            bt, kt = bd * e_neg, kd * e_neg
            e_rest = jnp.exp(c_tot - cum)
            bc, kc = bd * e_rest, kd * e_rest
            e_tot = jnp.exp(c_tot)
            for p in range(RW_WIDTH // PW):
                ps = slice(p * PW, (p + 1) * PW)
                jobs.append(dict(d=d, ps=ps, rows=rows, y_o=y_o, before=before, incl=before | eye, e=e_tot[:, ps],
                                 at=stack(at[:, ps]), rt=stack(rt[:, ps]), bt=stack(bt[:, ps]), kt=stack(kt[:, ps]),
                                 bc=stack(bc[:, ps]), kc=stack(kc[:, ps]), v=stack(v[:, ps])))
        for j in jobs:
            p = _mmx(_pieces(jnp.concatenate([j["at"], j["rt"]], axis=0), RW_PASSES_LOCAL),
                     _pieces(jnp.concatenate([j["bt"], j["kt"]], axis=0), RW_PASSES_LOCAL), _dot_nt)
            j["a_ab"] = jnp.where(j["before"], p[:PW, :PW], 0.0)
            j["a_ak"] = jnp.where(j["before"], p[:PW, PW:], 0.0)
            j["a_rb"] = jnp.where(j["incl"], p[PW:, :PW], 0.0)
            j["a_rk"] = jnp.where(j["incl"], p[PW:, PW:], 0.0)
            j["vp"] = _pieces(j["v"], RW_PASSES_LOCAL)
        for j in jobs:
            j["w1"] = _mmx(_pieces(j["a_ak"], RW_PASSES_LOCAL), j["vp"], _dot)
            j["tinv"] = jnp.where(eye, 1.0, j["a_ab"])
            j["pw"] = j["a_ab"]
        for _ in range(int(np.log2(C)) - 1):
            for j in jobs:
                pw = _pieces(j["pw"], RW_PASSES_LOCAL)
                j["pw"] = _mmx(pw, pw, _dot)
            for j in jobs:
                j["tinv"] = j["tinv"] + _mmx(_pieces(j["tinv"], RW_PASSES_LOCAL), _pieces(j["pw"], RW_PASSES_LOCAL),
                                             _dot)
        for j in jobs:
            tw = _mmx(_pieces(j["tinv"], RW_PASSES_LOCAL),
                      _pieces(jnp.concatenate([j["at"], j["w1"]], axis=1), RW_PASSES_LOCAL), _dot)
            j["tw"] = tw
            j["twp"] = _pieces(tw, RW_PASSES_LOCAL)
        for j in jobs:
            ry = _mmx(_pieces(j["a_rb"], RW_PASSES_LOCAL), j["twp"], _dot)
            yk = _mmx(_pieces(j["a_rk"], RW_PASSES_LOCAL), j["vp"], _dot)
            j["rr"] = j["rt"] + ry[:, :PW]
            j["yl"] = _fold_pair(ry[:, PW:] + yk)
            j["m"] = _mmx(_pieces(j["bc"], RW_PASSES_TRANSITION), _pieces(j["tw"][:, :PW], RW_PASSES_TRANSITION),
                          _dot_tn)
            uv = jnp.concatenate([j["tw"][:, PW:], j["v"]], axis=0)
            bk = jnp.concatenate([j["bc"], j["kc"]], axis=0)
            j["n"] = _fold_pair(_mmx(_pieces(uv, RW_PASSES_STATE), _pieces(bk, RW_PASSES_STATE), _dot_tn))
        for j in jobs:
            s = s_ref[j["d"], :, j["ps"]]
            sp = _pieces(stack(s), RW_PASSES_STATE)
            y = _mmx(_pieces(j["rr"], RW_PASSES_STATE), sp, _dot_nt)
            j["y_o"][j["rows"], j["ps"]] = _fold_pair(y) + j["yl"]
            sm = _mmx(sp[:RW_PASSES_TRANSITION], _pieces(j["m"], RW_PASSES_TRANSITION), _dot_nt)
            s_ref[j["d"], :, j["ps"]] = s * j["e"] + _fold_pair(sm) + j["n"]
        return carry

    lax.fori_loop(0, nchunk, chunk_step, 0)


def rw_fused_scan(geom, r, v, kk, lwf, kf, bf, lwb, kb, bb):
    B, T, L = geom.B, geom.T, geom.L
    blk = L
    nlat = T // blk
    cblk = geom.BT // blk
    fwd = pl.BlockSpec((blk, RW_WIDTH), lambda b, s: (jnp.where(s == 0, cblk + b, b * nlat + s - 1), 0))
    bwd = pl.BlockSpec((blk, RW_WIDTH), lambda b, s: (jnp.where(s == 0, cblk + b, b * nlat + nlat - s), 0))
    return pl.pallas_call(
        functools.partial(_rw_fused_scan_kernel, nchunk=blk // RW_CHUNK),
        grid=(B, nlat + 1),
        in_specs=[fwd] * 6 + [bwd] * 6,
        out_specs=[fwd, bwd],
        out_shape=[jax.ShapeDtypeStruct((geom.R, RW_WIDTH), F32)] * 2,
        scratch_shapes=[pltpu.VMEM((2, RW_HEAD_DIM, RW_WIDTH), F32)],
        compiler_params=_cparams("parallel", "arbitrary"),
    )(r, v, kk, lwf, kf, bf, r, v, kk, lwb, kb, bb)


def _rw_post_kernel(yf_ref, yb_ref, bonus_ref, g_ref, lng_ref, lnb_ref, ones_ref, o_ref):
    ones = ones_ref[...]
    y = yf_ref[...] + yb_ref[...]
    mean = _head_sum(y, ones) * (1.0 / RW_HEAD_DIM)
    yc = y - mean
    var = _head_sum(yc * yc, ones) * (1.0 / RW_HEAD_DIM)
    y = yc * lax.rsqrt(var + RW_LNX_EPS) * lng_ref[...] + lnb_ref[...]
    o_ref[...] = ((y + bonus_ref[...]) * g_ref[...]).astype(o_ref.dtype)


def rw_post(yf, yb, bonus, g, lnx_g, lnx_b, ones, *, tm):
    R = yf.shape[0]
    row = pl.BlockSpec((tm, RW_WIDTH), lambda i: (i, 0))
    full = lambda a: pl.BlockSpec(a.shape, lambda i: (0,) * a.ndim)
    return pl.pallas_call(
        _rw_post_kernel,
        grid=(R // tm,),
        in_specs=[row] * 4 + [full(lnx_g), full(lnx_b), full(ones)],
        out_specs=row,
        out_shape=jax.ShapeDtypeStruct((R, RW_WIDTH), BF16),
        compiler_params=_cparams("parallel"),
    )(yf, yb, bonus, g, lnx_g, lnx_b, ones)


def _rw_in_cols(w):
    parts = [w[..., :3 * RW_WIDTH]]
    off = 3 * RW_WIDTH
    for n in (RW_DECAY_LORA, RW_DECAY_LORA, RW_AAA_LORA, RW_AAA_LORA, RW_GATE_LORA):
        parts.append(_pad_cols(w[..., off:off + n], LORA_PAD))
        off += n
    return jnp.concatenate(parts, axis=-1)


def _pad_rows(w, n):
    return jnp.pad(w, [(0, 0)] * (w.ndim - 2) + [(0, n - w.shape[-2]), (0, 0)])


def _mla_wq_cols(w):
    w = w.reshape(w.shape[0], MLA_HEADS, MLA_NOPE + MLA_ROPE)
    return _pad_cols(w, MLA_QK_PAD).reshape(w.shape[0], MLA_HEADS * MLA_QK_PAD)


def kernel(x, c, ctx, c_ctx, ada_w, ada_b, norm1_g, w_in, rw_mu, rw_w0, rw_w2, rw_a0, rw_a2, rw_g2, rw_kk, rw_ka,
           rw_rk, rw_lnx_g, rw_lnx_b, rw_v0, rw_v1, rw_v2, wa_sink, mla_qnorm_g, mla_kvnorm_g, mla_w_uq, mla_w_ukv,
           w_branch, w_out, norm2_g, ffn_w_in, ffn_conv_w, ffn_conv_b, ffn_w_out, final_norm_g):
    B, T, D = x.shape
    L = ctx.shape[1]
    depth = w_in.shape[0]
    F = ffn_w_out.shape[1]
    geom = Geom(B, T, L)
    tm = _pick_tile(T, (512, 256, 128))
    assert (B * L) % tm == 0
    tmm = _pick_tile(geom.R, (1280, 1024, 640, 512, 256, 128))
    rw_cols = 3 * RW_WIDTH + 2 * RW_DECAY_LORA + 2 * RW_AAA_LORA + RW_GATE_LORA
    wa_cols = WA_WIDTH + 2 * WA_KV_WIDTH
    mla_cols = MLA_Q_LORA + MLA_KV_LORA + MLA_ROPE
    mla_cols_pad = MLA_Q_LORA + MLA_KV_LORA + 128

    cos_wa, sin_wa = _rope_tables(geom, WA_HEAD_DIM)
    cos_wa, sin_wa = jnp.tile(cos_wa, (1, 2)), jnp.tile(sin_wa, (1, 2))
    cos_m, sin_m = _rope_tables(geom, MLA_ROPE)
    one, zero = jnp.ones((geom.R, MLA_NOPE), F32), jnp.zeros((geom.R, MLA_NOPE), F32)
    cos_mla = jnp.concatenate([one, cos_m, one[:, :64]], axis=-1)
    sin_mla = jnp.concatenate([zero, sin_m, zero[:, :64]], axis=-1)
    lane = np.arange(128)
    ones_blk = jnp.asarray((lane[:, None] // RW_HEAD_DIM) == (lane[None, :] // RW_HEAD_DIM), F32)

    xs = jnp.concatenate([x.reshape(B * T, D), ctx.reshape(B * L, D)], axis=0)
    cvec = jnp.concatenate([c, c_ctx[None, :], jnp.zeros((8 - (B + 1) % 8, D), F32)], axis=0)
    v_first = None
    for l in range(depth):
        need_ctx = l < depth - 1
        mod = ada_modulation(cvec, ada_w[l], ada_b[l])
        mod = [mod[:, k * D:(k + 1) * D].reshape(-1, 1, D) for k in range(6)]

        w = w_in[l]
        tn_in = _pick_tile(3 * D, (768, 512, 256, 128))
        w_rw = _pad_cols(_rw_in_cols(w[:, :rw_cols]), _round_up(RW_Z_COLS, tn_in)).astype(BF16)
        w_wa = _pad_cols(w[:, rw_cols:rw_cols + wa_cols], _round_up(wa_cols, tn_in)).astype(BF16)
        w_mla = _pad_cols(w[:, rw_cols + wa_cols:rw_cols + wa_cols + mla_cols],
                          _round_up(mla_cols_pad, tn_in)).astype(BF16)
        w_gate = w[:, rw_cols + wa_cols + mla_cols:].astype(BF16)
        z_rw, z_wa, z_mla, gates = in_projection(geom, xs, norm1_g[l], mod[0], mod[1], w_rw, w_wa, w_mla, w_gate,
                                                 tm=tmm, tn=tn_in)

        rw_p = dict(
            mu=_rw_in_cols(rw_mu[l][None, :]), w0=rw_w0[l], w2=_pad_rows(rw_w2[l], LORA_PAD), a0=rw_a0[l],
            a2=_pad_rows(rw_a2[l], LORA_PAD), g2=_pad_rows(rw_g2[l], LORA_PAD), kk=rw_kk[l][None, :],
            ka=rw_ka[l][None, :], rk=rw_rk[l].reshape(1, RW_WIDTH), ones=ones_blk)
        vres = None if l == 0 else (rw_v0[l - 1][None, :], rw_v1[l - 1], rw_v2[l - 1])
        r, v, kk, lwf, kf, bf, lwb, kb, bb, g, bonus = rw_prep(geom, z_rw, rw_p, v_first, vres, tm=min(tm, 256))
        if l == 0:
            v_first = v
        yf, yb = rw_fused_scan(geom, r, v, kk, lwf, kf, bf, lwb, kb, bb)
        o_a = rw_post(yf, yb, bonus, g, rw_lnx_g[l][None, :], rw_lnx_b[l][None, :], ones_blk, tm=tm)

        q_wa, k_wa, v_wa = wa_prep(z_wa, cos_wa, sin_wa, tm=tm)
        ob_l = wa_attention(geom, q_wa, k_wa, v_wa, wa_sink[l], local=True)
        parts = [ob_l]
        if need_ctx:
            parts.append(wa_attention(geom, q_wa, k_wa, v_wa, wa_sink[l], local=False))
        else:
            parts.append(jnp.zeros((B * L, WA_WIDTH), BF16))
        o_b = jnp.concatenate(parts, axis=0)

        wq = _mla_wq_cols(mla_w_uq[l]).astype(BF16)
        q_m, k_m, v_m = mla_prep(z_mla, mla_qnorm_g[l], mla_kvnorm_g[l], wq, mla_w_ukv[l].astype(BF16),
                                 cos_mla, sin_mla, tm=tm)
        tq = _pick_tile(T, (1024, 512, 256, 128))
        tk = _pick_tile(T, (4096, 2048, 1024, 512, 256, 128))
        parts = [mla_attention(geom, q_m, k_m, v_m, with_latent=True, tq=tq, tk=tk)]
        if need_ctx:
            parts.append(mla_attention(geom, q_m, k_m, v_m, with_latent=False, tq=L, tk=L))
        else:
            parts.append(jnp.zeros((B * L, MLA_HEADS * MLA_V), BF16))
        o_c = jnp.concatenate(parts, axis=0)

        tn_d = _pick_tile(D, (1024, 512, 256, 128))
        y = merge_branches(o_a, o_b, o_c, gates, w_branch[l].astype(BF16), tm=tmm, tn=tn_d)
        xs = matmul_gated_residual(geom, y, w_out[l].astype(BF16), xs, mod[2], tm=tmm, tn=tn_d)

        tf = _pick_tile(F, (512, 256, 128))
        hmid = ffn_in_conv_glu(geom, xs, norm2_g[l], mod[3], mod[4], ffn_w_in[l].astype(BF16), ffn_conv_w[l],
                               ffn_conv_b[l], tm=tmm, tf=tf)
        xs = matmul_gated_residual(geom, hmid, ffn_w_out[l].astype(BF16), xs, mod[5], tm=tmm,
                                   tn=_pick_tile(D, (512, 256, 128)))

    out = final_rmsnorm(xs, final_norm_g, B * T, tm=tm)
    return out.reshape(B, T, D)
```

```python
import functools

import jax
import jax.numpy as jnp
import numpy as np
from jax import lax
from jax.experimental import pallas as pl
from jax.experimental.pallas import tpu as pltpu

F32 = jnp.float32
BF16 = jnp.bfloat16
HIGHEST = lax.Precision.HIGHEST

NORM_EPS = 1e-6
NEG_INF = -1e30
GRID_W = 64
ROPE_BASE = 10000.0

RW_HEADS = 16
RW_HEAD_DIM = 64
RW_WIDTH = RW_HEADS * RW_HEAD_DIM
RW_DECAY_LORA = 96
RW_AAA_LORA = 96
RW_GATE_LORA = 64
RW_LNX_EPS = 64e-5
RW_CHUNK = 64
RW_PAIR = 2 * RW_HEAD_DIM
RW_PASSES_LOCAL = 1
RW_PASSES_STATE = 3
RW_PASSES_TRANSITION = 1
LORA_PAD = 128

WA_HEADS = 16
WA_KV_HEADS = 4
WA_GROUP = WA_HEADS // WA_KV_HEADS
WA_HEAD_DIM = 64
WA_WIDTH = WA_HEADS * WA_HEAD_DIM
WA_KV_WIDTH = WA_KV_HEADS * WA_HEAD_DIM
WA_V_PAD = WA_KV_HEADS * 128
WINDOW = 128
WA_SCALE = WA_HEAD_DIM ** -0.5

MLA_HEADS = 8
MLA_NOPE = 128
MLA_ROPE = 64
MLA_V = 128
MLA_Q_LORA = 512
MLA_KV_LORA = 512
MLA_QK_PAD = 256
MLA_V_PAD = 256
MLA_SCALE = (MLA_NOPE + MLA_ROPE) ** -0.5
MLA_SUB_KEYS = 2048
LOG2_E = 1.4426950408889634

CONV_W = 3
VMEM_LIMIT_BYTES = 56 * 1024 * 1024


def _cparams(*sem):
    return pltpu.CompilerParams(dimension_semantics=sem, vmem_limit_bytes=VMEM_LIMIT_BYTES)


def _dot(a, b, precision=None):
    return jnp.dot(a, b, preferred_element_type=F32, precision=precision)


def _dot_nt(a, b, precision=None):
    return lax.dot_general(a, b, (((1,), (1,)), ((), ())), preferred_element_type=F32, precision=precision)


def _dot_tn(a, b, precision=None):
    return lax.dot_general(a, b, (((0,), (0,)), ((), ())), preferred_element_type=F32, precision=precision)


def _pick_tile(n, candidates):
    for c in candidates:
        if n % c == 0:
            return c
    raise ValueError(f"no tile in {candidates} divides {n}")


def _pad_cols(w, n):
    return jnp.pad(w, [(0, 0)] * (w.ndim - 1) + [(0, n - w.shape[-1])])


def _round_up(n, m):
    return (n + m - 1) // m * m


class Geom:
    def __init__(self, B, T, L):
        assert T & (T - 1) == 0 and L & (L - 1) == 0, "sequence lengths must be powers of two"
        assert T % L == 0 and L % RW_CHUNK == 0 and T % GRID_W == 0
        self.B, self.T, self.L = B, T, L
        self.BT = B * T
        self.R = B * T + B * L


def _select_row_group(geom, row0, tm, tab_ref):
    r = row0 + lax.broadcasted_iota(jnp.int32, (tm, 1), 0)
    out = tab_ref[geom.B]
    for b in range(geom.B):
        out = jnp.where((r >= b * geom.T) & (r < (b + 1) * geom.T), tab_ref[b], out)
    return out


def _seq_edge_masks(geom, row0, tm):
    r = row0 + lax.broadcasted_iota(jnp.int32, (tm, 1), 0)
    is_lat = r < geom.BT
    pos = jnp.where(is_lat, r & (geom.T - 1), (r - geom.BT) & (geom.L - 1))
    last = jnp.where(is_lat, geom.T - 1, geom.L - 1)
    return pos == 0, pos == last


def _shifted_rows(x, prev8, next8, first, last):
    tm = x.shape[0]
    rid = lax.broadcasted_iota(jnp.int32, (tm, 1), 0)
    up = jnp.where(rid == 0, prev8[7:8, :], pltpu.roll(x, 1, axis=0))
    dn = jnp.where(rid == tm - 1, next8[0:1, :], pltpu.roll(x, tm - 1, axis=0))
    return jnp.where(first, 0.0, up), jnp.where(last, 0.0, dn)


def _halo_specs(tm, R, width, col_of):
    nb8 = tm // 8
    prev = pl.BlockSpec((8, width), lambda i, *a: (jnp.maximum(i * nb8 - 1, 0), col_of(i, *a)))
    nxt = pl.BlockSpec((8, width), lambda i, *a: (jnp.minimum((i + 1) * nb8, R // 8 - 1), col_of(i, *a)))
    return prev, nxt


def _ada_kernel(c_ref, w_ref, b_ref, o_ref):
    c = c_ref[...]
    o_ref[...] = _dot(c * jax.nn.sigmoid(c), w_ref[...], HIGHEST) + b_ref[...]


def ada_modulation(cvec, w, b):
    G, D = cvec.shape
    N = w.shape[1]
    tn = _pick_tile(N, (1024, 512, 256, 128))
    return pl.pallas_call(
        _ada_kernel,
        grid=(N // tn,),
        in_specs=[pl.BlockSpec((G, D), lambda j: (0, 0)),
                  pl.BlockSpec((D, tn), lambda j: (0, j)),
                  pl.BlockSpec((1, tn), lambda j: (0, j))],
        out_specs=pl.BlockSpec((G, tn), lambda j: (0, j)),
        out_shape=jax.ShapeDtypeStruct((G, N), F32),
        compiler_params=_cparams("arbitrary"),
    )(cvec, w, b.reshape(1, N))


def _norm_modulate(x, g, sc, sh):
    gain = g * (1.0 + sc)
    return (x * lax.rsqrt(jnp.mean(x * x, axis=-1, keepdims=True) + NORM_EPS) * gain + sh).astype(BF16)


PROLOGUE_ROWS = 256


def _norm_modulate_rows(geom, row0, x_ref, g_ref, sc_ref, sh_ref, h_ref, h_off):
    tm = x_ref.shape[0]
    step = PROLOGUE_ROWS if tm % PROLOGUE_ROWS == 0 else tm
    one_group = geom.T % step == 0 and geom.BT % step == 0
    g = g_ref[...]
    for r in range(0, tm, step):
        if one_group:
            sc, sh = sc_ref[geom.B], sh_ref[geom.B]
            for b in range(geom.B):
                in_b = (row0 + r >= b * geom.T) & (row0 + r < (b + 1) * geom.T)
                sc, sh = jnp.where(in_b, sc_ref[b], sc), jnp.where(in_b, sh_ref[b], sh)
        else:
            sc = _select_row_group(geom, row0 + r, step, sc_ref)
            sh = _select_row_group(geom, row0 + r, step, sh_ref)
        h_ref[h_off + r:h_off + r + step] = _norm_modulate(x_ref[r:r + step], g, sc, sh)


def _in_proj_kernel(x_ref, g_ref, sh_ref, sc_ref, w_ref, zrw_ref, zwa_ref, zmla_ref, gate_ref, h_ref,
                    *, geom, tm, starts):
    j = pl.program_id(1)

    @pl.when(j == 0)
    def _():
        _norm_modulate_rows(geom, pl.program_id(0) * tm, x_ref, g_ref, sc_ref, sh_ref, h_ref, 0)

    wa0, mla0, gate0 = starts

    @pl.when(j < wa0)
    def _():
        zrw_ref[...] = _dot(h_ref[...], w_ref[...])

    @pl.when((j >= wa0) & (j < mla0))
    def _():
        zwa_ref[...] = _dot(h_ref[...], w_ref[...])

    @pl.when((j >= mla0) & (j < gate0))
    def _():
        zmla_ref[...] = _dot(h_ref[...], w_ref[...])

    @pl.when(j >= gate0)
    def _():
        gate_ref[...] = jax.nn.sigmoid(_dot(h_ref[...], w_ref[...])).astype(gate_ref.dtype)


def in_projection(geom, x, g, shift, scale, w_rw, w_wa, w_mla, w_gate, *, tm, tn):
    R, K = x.shape
    widths = [w.shape[1] for w in (w_rw, w_wa, w_mla, w_gate)]
    assert R % tm == 0 and all(n % tn == 0 for n in widths)
    tiles = [n // tn for n in widths]
    starts = tuple(int(v) for v in np.cumsum(tiles)[:3])
    w_all = jnp.concatenate([w_rw, w_wa, w_mla, w_gate], axis=1)
    full = lambda a: pl.BlockSpec(a.shape, lambda i, j: (0,) * a.ndim)
    out_spec = lambda first, n: pl.BlockSpec((tm, tn), lambda i, j: (i, jnp.clip(j - first, 0, n - 1)))
    firsts = (0,) + starts
    return pl.pallas_call(
        functools.partial(_in_proj_kernel, geom=geom, tm=tm, starts=starts),
        grid=(R // tm, sum(tiles)),
        in_specs=[pl.BlockSpec((tm, K), lambda i, j: (i, 0), pipeline_mode=pl.Buffered(1)),
                  pl.BlockSpec((1, K), lambda i, j: (0, 0)),
                  full(shift), full(scale),
                  pl.BlockSpec((K, tn), lambda i, j: (0, j))],
        out_specs=[out_spec(f, n) for f, n in zip(firsts, tiles)],
        out_shape=[jax.ShapeDtypeStruct((R, widths[0]), F32), jax.ShapeDtypeStruct((R, widths[1]), F32),
                   jax.ShapeDtypeStruct((R, widths[2]), F32), jax.ShapeDtypeStruct((R, widths[3]), BF16)],
        scratch_shapes=[pltpu.VMEM((tm, K), BF16)],
        compiler_params=_cparams("parallel", "arbitrary"),
    )(x, g.reshape(1, K), shift, scale, w_all)


def _mm_resid_kernel(y_ref, w_ref, r_ref, gate_ref, o_ref, *, geom, tm):
    gate = _select_row_group(geom, pl.program_id(0) * tm, tm, gate_ref)
    o_ref[...] = r_ref[...] + gate * _dot(y_ref[...], w_ref[...])


def matmul_gated_residual(geom, y, w, resid, gate, *, tm, tn):
    R, K = y.shape
    N = w.shape[1]
    G = gate.shape[0]
    assert R % tm == 0 and N % tn == 0
    return pl.pallas_call(
        functools.partial(_mm_resid_kernel, geom=geom, tm=tm),
        grid=(R // tm, N // tn),
        in_specs=[pl.BlockSpec((tm, K), lambda i, j: (i, 0)),
                  pl.BlockSpec((K, tn), lambda i, j: (0, j)),
                  pl.BlockSpec((tm, tn), lambda i, j: (i, j)),
                  pl.BlockSpec((G, 1, tn), lambda i, j: (0, 0, j))],
        out_specs=pl.BlockSpec((tm, tn), lambda i, j: (i, j)),
        out_shape=jax.ShapeDtypeStruct((R, N), F32),
        compiler_params=_cparams("parallel", "arbitrary"),
    )(y, w, resid, gate)


FFN_HALO = 16


def _ffn_in_kernel(x_ref, xp_ref, xn_ref, g_ref, sh_ref, sc_ref, wg_ref, wu_ref, cw_ref, cb_ref, o_ref, h_ref,
                   *, geom, tm):
    row0 = pl.program_id(0) * tm
    H = FFN_HALO

    @pl.when(pl.program_id(1) == 0)
    def _():
        _norm_modulate_rows(geom, row0, xp_ref, g_ref, sc_ref, sh_ref, h_ref, 0)
        _norm_modulate_rows(geom, row0, x_ref, g_ref, sc_ref, sh_ref, h_ref, H)
        _norm_modulate_rows(geom, row0 + tm - H, xn_ref, g_ref, sc_ref, sh_ref, h_ref, H + tm)

    first, last = _seq_edge_masks(geom, row0, tm)
    gt = _dot(h_ref[...], wg_ref[...])
    u = _dot(h_ref[H:H + tm], wu_ref[...])
    up = jnp.where(first, 0.0, pltpu.roll(gt, 1, axis=0)[H:H + tm])
    dn = jnp.where(last, 0.0, pltpu.roll(gt, tm + 2 * H - 1, axis=0)[H:H + tm])
    cw = cw_ref[...]
    conv = cb_ref[...] + up * cw[0:1, :]
    conv = conv + gt[H:H + tm] * cw[1:2, :]
    conv = conv + dn * cw[2:3, :]
    o_ref[...] = (jax.nn.gelu(conv, approximate=True) * u).astype(o_ref.dtype)


def ffn_in_conv_glu(geom, x, g, shift, scale, w_in, conv_w, conv_b, *, tm, tf):
    R, K = x.shape
    F = w_in.shape[1] // 2
    nj = F // tf
    nbh = tm // FFN_HALO
    assert R % tm == 0 and F % tf == 0 and tm % FFN_HALO == 0
    full = lambda a: pl.BlockSpec(a.shape, lambda i, j: (0,) * a.ndim)
    return pl.pallas_call(
        functools.partial(_ffn_in_kernel, geom=geom, tm=tm),
        grid=(R // tm, nj),
        in_specs=[pl.BlockSpec((tm, K), lambda i, j: (i, 0), pipeline_mode=pl.Buffered(1)),
                  pl.BlockSpec((FFN_HALO, K), lambda i, j: (jnp.maximum(i * nbh - 1, 0), 0)),
                  pl.BlockSpec((FFN_HALO, K), lambda i, j: (jnp.minimum((i + 1) * nbh, R // FFN_HALO - 1), 0)),
                  pl.BlockSpec((1, K), lambda i, j: (0, 0)),
                  full(shift), full(scale),
                  pl.BlockSpec((K, tf), lambda i, j: (0, j)),
                  pl.BlockSpec((K, tf), lambda i, j: (0, nj + j)),
                  pl.BlockSpec((CONV_W, tf), lambda i, j: (0, j)),
                  pl.BlockSpec((1, tf), lambda i, j: (0, j))],
        out_specs=pl.BlockSpec((tm, tf), lambda i, j: (i, j)),
        out_shape=jax.ShapeDtypeStruct((R, F), BF16),
        scratch_shapes=[pltpu.VMEM((tm + 2 * FFN_HALO, K), BF16)],
        compiler_params=_cparams("parallel", "arbitrary"),
    )(x, x, x, g.reshape(1, K), shift, scale, w_in, w_in, conv_w, conv_b.reshape(1, F))


def _merge_kernel(oa_ref, ob_ref, oc_ref, ga_ref, gb_ref, gc_ref, w_ref, o_ref):
    y = ga_ref[...] * _dot(oa_ref[...], w_ref[0])
    y = y + gb_ref[...] * _dot(ob_ref[...], w_ref[1])
    y = y + gc_ref[...] * _dot(oc_ref[...], w_ref[2])
    o_ref[...] = y.astype(o_ref.dtype)


def merge_branches(oa, ob, oc, gates, wb, *, tm, tn):
    R, K = oa.shape
    D = wb.shape[2]
    nj = D // tn
    bspec = pl.BlockSpec((tm, K), lambda i, j: (i, 0))
    gspec = lambda k: pl.BlockSpec((tm, tn), lambda i, j: (i, k * nj + j))
    return pl.pallas_call(
        _merge_kernel,
        grid=(R // tm, nj),
        in_specs=[bspec, bspec, bspec, gspec(0), gspec(1), gspec(2),
                  pl.BlockSpec((3, K, tn), lambda i, j: (0, 0, j))],
        out_specs=pl.BlockSpec((tm, tn), lambda i, j: (i, j)),
        out_shape=jax.ShapeDtypeStruct((R, D), BF16),
        compiler_params=_cparams("parallel", "arbitrary"),
    )(oa, ob, oc, gates, gates, gates, wb)


def _rmsnorm_kernel(x_ref, g_ref, o_ref):
    x = x_ref[...]
    o_ref[...] = x * lax.rsqrt(jnp.mean(x * x, axis=-1, keepdims=True) + NORM_EPS) * g_ref[...]


def final_rmsnorm(x, g, rows, *, tm):
    D = x.shape[1]
    return pl.pallas_call(
        _rmsnorm_kernel,
        grid=(rows // tm,),
        in_specs=[pl.BlockSpec((tm, D), lambda i: (i, 0)), pl.BlockSpec((1, D), lambda i: (0, 0))],
        out_specs=pl.BlockSpec((tm, D), lambda i: (i, 0)),
        out_shape=jax.ShapeDtypeStruct((rows, D), F32),
        compiler_params=_cparams("parallel"),
    )(x, g.reshape(1, D))


def _rot_half64(z):
    n = z.shape[-1]
    lane = lax.broadcasted_iota(jnp.int32, z.shape, z.ndim - 1)
    return jnp.where((lane & 63) < 32, pltpu.roll(z, n - 32, axis=z.ndim - 1), pltpu.roll(z, 32, axis=z.ndim - 1))


def _rope_tables(geom, dim):
    nf = dim // 4
    inv = ROPE_BASE ** (-jnp.arange(nf, dtype=F32) / nf)
    rows = geom.T // GRID_W
    row = jnp.repeat(jnp.arange(rows, dtype=F32), GRID_W)
    col = jnp.tile(jnp.arange(GRID_W, dtype=F32), rows)
    ang = jnp.concatenate([row[:, None] * inv, col[:, None] * inv], axis=-1)
    cos, sin = jnp.cos(ang), jnp.sin(ang)
    cos_t = jnp.concatenate([cos, cos], axis=-1)
    sin_t = jnp.concatenate([-sin, sin], axis=-1)
    nctx = geom.B * geom.L
    cos_f = jnp.concatenate([jnp.tile(cos_t, (geom.B, 1)), jnp.ones((nctx, dim), F32)], axis=0)
    sin_f = jnp.concatenate([jnp.tile(sin_t, (geom.B, 1)), jnp.zeros((nctx, dim), F32)], axis=0)
    return cos_f, sin_f


def _wa_prep_kernel(z_ref, cos_ref, sin_ref, q_ref, k_ref, v_ref):
    cos = cos_ref[...]
    sin = sin_ref[...]
    for c in range(WA_WIDTH // 128):
        z = z_ref[:, c * 128:(c + 1) * 128]
        q_ref[:, c * 128:(c + 1) * 128] = ((z * cos + _rot_half64(z) * sin) * (WA_SCALE * LOG2_E)).astype(BF16)
    for c in range(WA_KV_WIDTH // 128):
        z = z_ref[:, WA_WIDTH + c * 128:WA_WIDTH + (c + 1) * 128]
        k_ref[:, c * 128:(c + 1) * 128] = (z * cos + _rot_half64(z) * sin).astype(BF16)
    v_ref[...] = jnp.ones(v_ref.shape, BF16)
    for g in range(WA_KV_HEADS):
        src = WA_WIDTH + WA_KV_WIDTH + g * WA_HEAD_DIM
        v_ref[:, g * 128:g * 128 + WA_HEAD_DIM] = z_ref[:, src:src + WA_HEAD_DIM].astype(BF16)


def wa_prep(z, cos, sin, *, tm):
    R = z.shape[0]
    row = lambda w: pl.BlockSpec((tm, w), lambda i: (i, 0))
    return pl.pallas_call(
        _wa_prep_kernel,
        grid=(R // tm,),
        in_specs=[row(WA_WIDTH + 2 * WA_KV_WIDTH), row(128), row(128)],
        out_specs=[row(WA_WIDTH), row(WA_KV_WIDTH), row(WA_V_PAD)],
        out_shape=[jax.ShapeDtypeStruct((R, WA_WIDTH), BF16),
                   jax.ShapeDtypeStruct((R, WA_KV_WIDTH), BF16),
                   jax.ShapeDtypeStruct((R, WA_V_PAD), BF16)],
        compiler_params=_cparams("parallel"),
    )(z, cos, sin)


def _wa_attn_kernel(*refs, local, nqb, tq):
    if local:
        sink_ref, q_ref, kp_ref, kc_ref, kn_ref, vp_ref, vc_ref, vn_ref, kx_ref, vx_ref, o_ref = refs
    else:
        sink_ref, q_ref, kx_ref, vx_ref, o_ref = refs
    i = pl.program_id(1)
    nk_ctx = kx_ref.shape[0]
    rows = WA_GROUP * tq
    qpos = lax.broadcasted_iota(jnp.int32, (rows, 1), 0) & (tq - 1)
    head_in_group = lax.broadcasted_iota(jnp.int32, (rows, 1), 0) >> int(np.log2(tq))
    if local:
        off_prev = jnp.where(i > 0, 0, tq)
        off_next = jnp.where(i < nqb - 1, 0, tq)
        j = lax.broadcasted_iota(jnp.int32, (1, 3 * tq + nk_ctx), 1)
        valid = ((j >= tq) & (j < 2 * tq)) | (j >= 3 * tq)
        valid = valid | ((j < tq) & (j >= qpos + off_prev))
        valid = valid | ((j >= 2 * tq) & (j < 3 * tq) & ((j - 2 * tq) <= qpos - off_next))
    for g in range(WA_KV_HEADS):
        ks = slice(g * WA_HEAD_DIM, (g + 1) * WA_HEAD_DIM)
        qg = jnp.concatenate(
            [q_ref[:, (g * WA_GROUP + a) * WA_HEAD_DIM:(g * WA_GROUP + a + 1) * WA_HEAD_DIM] for a in range(WA_GROUP)],
            axis=0)
        vs = slice(g * 128, (g + 1) * 128)
        sink = jnp.zeros((rows, 1), F32)
        for a in range(WA_GROUP):
            sink = jnp.where(head_in_group == a, sink_ref[g * WA_GROUP + a] * LOG2_E, sink)
        if local:
            kcat = jnp.concatenate([kp_ref[:, ks], kc_ref[:, ks], kn_ref[:, ks], kx_ref[:, ks]], axis=0)
            vcat = jnp.concatenate([vp_ref[:, vs], vc_ref[:, vs], vn_ref[:, vs], vx_ref[:, vs]], axis=0)
        else:
            kcat, vcat = kx_ref[:, ks], vx_ref[:, vs]
        s = _dot_nt(qg, kcat)
        if local:
            s = jnp.where(valid, s, NEG_INF)
        m = jnp.maximum(jnp.max(s, axis=-1, keepdims=True), sink)
        o = _dot(jnp.exp2(s - m).astype(BF16), vcat)
        o = o[:, :WA_HEAD_DIM] / (o[:, WA_HEAD_DIM:] + jnp.exp2(sink - m))
        for a in range(WA_GROUP):
            h = g * WA_GROUP + a
            o_ref[:, h * WA_HEAD_DIM:(h + 1) * WA_HEAD_DIM] = o[a * tq:(a + 1) * tq].astype(o_ref.dtype)


def wa_attention(geom, q, k, v, sink, *, local):
    B, T, L = geom.B, geom.T, geom.L
    sink_spec = pl.BlockSpec(memory_space=pltpu.SMEM)
    ctx_spec = lambda w: pl.BlockSpec((L, w), lambda b, i: (geom.BT // L + b, 0))
    if local:
        tq = WINDOW
        nqb = T // tq
        kv = lambda f, w: pl.BlockSpec((tq, w), lambda b, i: (b * nqb + f(i), 0))
        prev = lambda i: jnp.maximum(i - 1, 0)
        cur = lambda i: i
        nxt = lambda i: jnp.minimum(i + 1, nqb - 1)
        in_specs = [sink_spec, pl.BlockSpec((tq, WA_WIDTH), lambda b, i: (b * nqb + i, 0)),
                    kv(prev, WA_KV_WIDTH), kv(cur, WA_KV_WIDTH), kv(nxt, WA_KV_WIDTH),
                    kv(prev, WA_V_PAD), kv(cur, WA_V_PAD), kv(nxt, WA_V_PAD),
                    ctx_spec(WA_KV_WIDTH), ctx_spec(WA_V_PAD)]
        args = (sink, q, k, k, k, v, v, v, k, v)
        out_rows, out_spec = geom.BT, pl.BlockSpec((tq, WA_WIDTH), lambda b, i: (b * nqb + i, 0))
    else:
        tq, nqb = L, 1
        in_specs = [sink_spec, pl.BlockSpec((tq, WA_WIDTH), lambda b, i: (geom.BT // L + b, 0)),
                    ctx_spec(WA_KV_WIDTH), ctx_spec(WA_V_PAD)]
        args = (sink, q, k, v)
        out_rows, out_spec = B * L, pl.BlockSpec((tq, WA_WIDTH), lambda b, i: (b, 0))
    return pl.pallas_call(
        functools.partial(_wa_attn_kernel, local=local, nqb=nqb, tq=tq),
        grid=(B, nqb),
        in_specs=in_specs,
        out_specs=out_spec,
        out_shape=jax.ShapeDtypeStruct((out_rows, WA_WIDTH), BF16),
        compiler_params=_cparams("parallel", "arbitrary"),
    )(*args)


def _mla_prep_kernel(z_ref, qg_ref, kvg_ref, wq_ref, wkv_ref, cos_ref, sin_ref, q_ref, k_ref, v_ref):
    def norm(x, g):
        return (x * lax.rsqrt(jnp.mean(x * x, axis=-1, keepdims=True) + NORM_EPS) * g).astype(BF16)

    cos = cos_ref[...]
    sin = sin_ref[...]
    q = _dot(norm(z_ref[:, :MLA_Q_LORA], qg_ref[...]), wq_ref[...])
    kv = _dot(norm(z_ref[:, MLA_Q_LORA:MLA_Q_LORA + MLA_KV_LORA], kvg_ref[...]), wkv_ref[...])
    kr = z_ref[:, MLA_Q_LORA + MLA_KV_LORA:MLA_Q_LORA + MLA_KV_LORA + 128]
    kr = (kr * cos[:, 128:] + _rot_half64(kr) * sin[:, 128:]).astype(BF16)
    for h in range(MLA_HEADS):
        qh = q[:, h * MLA_QK_PAD:(h + 1) * MLA_QK_PAD]
        q_ref[h] = ((qh * cos + _rot_half64(qh) * sin) * (MLA_SCALE * LOG2_E)).astype(BF16)
        k_ref[h, :, :MLA_NOPE] = kv[:, h * 256:h * 256 + MLA_NOPE].astype(BF16)
        k_ref[h, :, MLA_NOPE:] = kr
        v_ref[h, :, :MLA_V] = kv[:, h * 256 + MLA_NOPE:(h + 1) * 256].astype(BF16)
        v_ref[h, :, MLA_V:] = jnp.ones((kv.shape[0], MLA_V_PAD - MLA_V), BF16)


def mla_prep(z, qnorm_g, kvnorm_g, wq, wkv, cos, sin, *, tm):
    R, Z = z.shape
    full = lambda a: pl.BlockSpec(a.shape, lambda i: (0,) * a.ndim)
    qg, kvg = qnorm_g.reshape(1, -1), kvnorm_g.reshape(1, -1)
    hd = lambda w: pl.BlockSpec((MLA_HEADS, tm, w), lambda i: (0, i, 0))
    return pl.pallas_call(
        _mla_prep_kernel,
        grid=(R // tm,),
        in_specs=[pl.BlockSpec((tm, Z), lambda i: (i, 0)), full(qg), full(kvg), full(wq), full(wkv),
                  pl.BlockSpec((tm, MLA_QK_PAD), lambda i: (i, 0)), pl.BlockSpec((tm, MLA_QK_PAD), lambda i: (i, 0))],
        out_specs=[hd(MLA_QK_PAD), hd(MLA_QK_PAD), hd(MLA_V_PAD)],
        out_shape=[jax.ShapeDtypeStruct((MLA_HEADS, R, MLA_QK_PAD), BF16),
                   jax.ShapeDtypeStruct((MLA_HEADS, R, MLA_QK_PAD), BF16),
                   jax.ShapeDtypeStruct((MLA_HEADS, R, MLA_V_PAD), BF16)],
        compiler_params=_cparams("parallel"),
    )(z, qg, kvg, wq, wkv, cos, sin)


def _mla_flash_kernel(*refs, with_latent, sub):
    if with_latent:
        q_ref, kx_ref, vx_ref, k_ref, v_ref, o_ref, m_ref, acc_ref = refs
    else:
        q_ref, kx_ref, vx_ref, o_ref, m_ref, acc_ref = refs
    ki = pl.program_id(3)
    q = q_ref[0]

    def update(s, v, m_old, acc_old):
        cols = [s[:, c * 128:(c + 1) * 128] for c in range(s.shape[1] // 128)]
        mx = functools.reduce(jnp.maximum, cols)
        m_new = jnp.maximum(m_old, jnp.max(mx, axis=-1, keepdims=True))
        alpha = jnp.exp2(m_old - m_new)
        p = jnp.concatenate([jnp.exp2(c - m_new).astype(BF16) for c in cols], axis=-1)
        return m_new, jnp.concatenate([alpha, alpha], axis=-1) * acc_old + _dot(p, v)

    @pl.when(ki == 0)
    def _():
        tq = q.shape[0]
        m, acc = update(_dot_nt(q, kx_ref[0]), vx_ref[0], jnp.full((tq, 128), NEG_INF, F32),
                        jnp.zeros((tq, MLA_V_PAD), F32))
        m_ref[...], acc_ref[...] = m, acc

    if with_latent:
        nsub = k_ref.shape[1] // sub
        m, acc = m_ref[...], acc_ref[...]
        s_next = _dot_nt(q, k_ref[0, 0:sub, :])
        for j in range(nsub):
            s = s_next
            if j + 1 < nsub:
                s_next = _dot_nt(q, k_ref[0, (j + 1) * sub:(j + 2) * sub, :])
            m, acc = update(s, v_ref[0, j * sub:(j + 1) * sub, :], m, acc)
        m_ref[...], acc_ref[...] = m, acc

    @pl.when(ki == pl.num_programs(3) - 1)
    def _():
        acc = acc_ref[...]
        o_ref[...] = (acc[:, :MLA_V] / acc[:, MLA_V:]).astype(o_ref.dtype)


def mla_attention(geom, q, k, v, *, with_latent, tq, tk):
    B, T, L = geom.B, geom.T, geom.L
    cblk = geom.BT // L
    ctx_k = pl.BlockSpec((1, L, MLA_QK_PAD), lambda b, h, qi, ki: (h, cblk + b, 0))
    ctx_v = pl.BlockSpec((1, L, MLA_V_PAD), lambda b, h, qi, ki: (h, cblk + b, 0))
    if with_latent:
        nq, nk = T // tq, T // tk
        in_specs = [pl.BlockSpec((1, tq, MLA_QK_PAD), lambda b, h, qi, ki: (h, b * nq + qi, 0)), ctx_k, ctx_v,
                    pl.BlockSpec((1, tk, MLA_QK_PAD), lambda b, h, qi, ki: (h, b * nk + ki, 0)),
                    pl.BlockSpec((1, tk, MLA_V_PAD), lambda b, h, qi, ki: (h, b * nk + ki, 0))]
        args = (q, k, v, k, v)
        out_rows, out_spec = geom.BT, pl.BlockSpec((tq, MLA_V), lambda b, h, qi, ki: (b * nq + qi, h))
    else:
        tq, nq, nk = L, 1, 1
        in_specs = [pl.BlockSpec((1, tq, MLA_QK_PAD), lambda b, h, qi, ki: (h, cblk + b, 0)), ctx_k, ctx_v]
        args = (q, k, v)
        out_rows, out_spec = B * L, pl.BlockSpec((tq, MLA_V), lambda b, h, qi, ki: (b, h))
    return pl.pallas_call(
        functools.partial(_mla_flash_kernel, with_latent=with_latent, sub=min(MLA_SUB_KEYS, tk)),
        grid=(B, MLA_HEADS, nq, nk),
        in_specs=in_specs,
        out_specs=out_spec,
        out_shape=jax.ShapeDtypeStruct((out_rows, MLA_HEADS * MLA_V), BF16),
        scratch_shapes=[pltpu.VMEM((tq, 128), F32), pltpu.VMEM((tq, MLA_V_PAD), F32)],
        compiler_params=_cparams("parallel", "parallel", "parallel", "arbitrary"),
    )(*args)


RW_Z_R, RW_Z_K, RW_Z_V = 0, RW_WIDTH, 2 * RW_WIDTH
RW_Z_LORA = 3 * RW_WIDTH
RW_Z_COLS = 3 * RW_WIDTH + 5 * LORA_PAD


def _dot3(a, b):
    return _mmx(_pieces(a, 3), _pieces(b, 3), _dot)


def _head_sum(x, ones):
    ones = (ones.astype(BF16),)
    return jnp.concatenate(
        [_mmx(_pieces(x[:, c * 128:(c + 1) * 128], 3), ones, _dot) for c in range(x.shape[1] // 128)], axis=-1)


def _rw_prep_kernel(*refs, geom, tm, has_vres):
    if has_vres:
        (z_ref, zp_ref, zn_ref, mu_ref, w0_ref, w2_ref, a0_ref, a2_ref, g2_ref, kk_ref, ka_ref, rk_ref, ones_ref,
         vf_ref, v0_ref, v1_ref, v2_ref,
         r_o, v_o, kk_o, lwf_o, kf_o, bf_o, lwb_o, kb_o, bb_o, g_o, bonus_o) = refs
    else:
        (z_ref, zp_ref, zn_ref, mu_ref, w0_ref, w2_ref, a0_ref, a2_ref, g2_ref, kk_ref, ka_ref, rk_ref, ones_ref,
         r_o, v_o, kk_o, lwf_o, kf_o, bf_o, lwb_o, kb_o, bb_o, g_o, bonus_o) = refs
    first, last = _seq_edge_masks(geom, pl.program_id(0) * tm, tm)
    z = z_ref[...]
    up, dn = _shifted_rows(z, zp_ref[...], zn_ref[...], first, last)
    z = z + mu_ref[...] * (0.5 * (up + dn) - z)
    r = z[:, RW_Z_R:RW_Z_R + RW_WIDTH]
    k = z[:, RW_Z_K:RW_Z_K + RW_WIDTH]
    v = z[:, RW_Z_V:RW_Z_V + RW_WIDTH]
    lora = lambda n: z[:, RW_Z_LORA + n * LORA_PAD:RW_Z_LORA + (n + 1) * LORA_PAD]
    ones = ones_ref[...]
    if has_vres:
        mix = jax.nn.sigmoid(v0_ref[...] + _dot3(_dot3(v, v1_ref[...]), v2_ref[...]))
        v = v + (vf_ref[...] - v) * mix
    g_o[...] = _dot3(jax.nn.sigmoid(lora(4)), g2_ref[...])
    kk = k * kk_ref[...]
    kk = kk / jnp.maximum(jnp.sqrt(_head_sum(kk * kk, ones)), 1e-12)
    ksum = None
    for d, (lw_o, k_o, b_o) in enumerate(((lwf_o, kf_o, bf_o), (lwb_o, kb_o, bb_o))):
        x = -(w0_ref[d:d + 1, :] + _dot3(jnp.tanh(lora(d)), w2_ref[d]))
        softplus = jnp.maximum(x, 0.0) + jnp.log1p(jnp.exp(-jnp.abs(x)))
        lw_o[...] = -jnp.exp(-softplus - 0.5)
        a = jax.nn.sigmoid(a0_ref[d:d + 1, :] + _dot3(lora(2 + d), a2_ref[d]))
        kd = k * (1.0 + (a - 1.0) * ka_ref[...])
        k_o[...] = kd
        b_o[...] = kk * a
        ksum = kd if ksum is None else ksum + kd
    r_o[...] = r
    v_o[...] = v
    kk_o[...] = kk
    bonus_o[...] = _head_sum(r * ksum * rk_ref[...], ones) * v


def rw_prep(geom, z, p, v_first, vres, *, tm):
    R = z.shape[0]
    has_vres = vres is not None
    full = lambda a: pl.BlockSpec(a.shape, lambda i: (0,) * a.ndim)
    row = pl.BlockSpec((tm, RW_WIDTH), lambda i: (i, 0))
    prev, nxt = _halo_specs(tm, R, RW_Z_COLS, lambda i: 0)
    params = [p["mu"], p["w0"], p["w2"], p["a0"], p["a2"], p["g2"], p["kk"], p["ka"], p["rk"], p["ones"]]
    in_specs = [pl.BlockSpec((tm, RW_Z_COLS), lambda i: (i, 0)), prev, nxt] + [full(a) for a in params]
    args = [z, z, z] + params
    if has_vres:
        in_specs += [row] + [full(a) for a in vres]
        args += [v_first] + list(vres)
    return pl.pallas_call(
        functools.partial(_rw_prep_kernel, geom=geom, tm=tm, has_vres=has_vres),
        grid=(R // tm,),
        in_specs=in_specs,
        out_specs=[row] * 11,
        out_shape=[jax.ShapeDtypeStruct((R, RW_WIDTH), F32)] * 11,
        compiler_params=_cparams("parallel"),
    )(*args)


def _pieces(x, passes):
    hi = x.astype(BF16)
    if passes == 1:
        return (hi,)
    return hi, (x - hi.astype(F32)).astype(BF16)


def _mmx(a, b, dot):
    out = dot(a[0], b[0])
    if len(a) > 1:
        out = out + dot(a[1], b[0])
    if len(b) > 1:
        out = out + dot(a[0], b[1])
    return out


def _stack_pair(first_head, x):
    return jnp.concatenate([jnp.where(first_head, x, 0.0), jnp.where(first_head, 0.0, x)], axis=0)


def _fold_pair(x):
    half = x.shape[0] // 2
    return x[:half] + x[half:]


def _rw_fused_scan_kernel(rf_ref, vf_ref, kkf_ref, lwf_ref, kf_ref, bf_ref,
                          rb_ref, vb_ref, kkb_ref, lwb_ref, kb_ref, bb_ref, yf_o, yb_o, s_ref, *, nchunk):
    C = RW_CHUNK
    PW = RW_PAIR
    ri = lax.broadcasted_iota(jnp.int32, (PW, PW), 0)
    ci = lax.broadcasted_iota(jnp.int32, (PW, PW), 1)
    eye = ri == ci
    ri, ci = ri & (C - 1), ci & (C - 1)
    first_head = lax.broadcasted_iota(jnp.int32, (1, PW), 1) < RW_HEAD_DIM
    stack = functools.partial(_stack_pair, first_head)

    @pl.when(pl.program_id(1) == 0)
    def _():
        s_ref[...] = jnp.zeros_like(s_ref)

    dirs = ((rf_ref, vf_ref, kkf_ref, lwf_ref, kf_ref, bf_ref, yf_o, ci < ri, C - 1),
            (rb_ref, vb_ref, kkb_ref, lwb_ref, kb_ref, bb_ref, yb_o, ci > ri, 0))

    def chunk_step(step, carry):
        jobs = []
        for d, (r_ref, v_ref, kk_ref, lw_ref, k_ref, b_ref, y_o, before, tot_row) in enumerate(dirs):
            chunk = step if d == 0 else nchunk - 1 - step
            rows = pl.ds(pl.multiple_of(chunk * C, C), C)
            r, v, kk = r_ref[rows, :], v_ref[rows, :], kk_ref[rows, :]
            lw, kd, bd = lw_ref[rows, :], k_ref[rows, :], b_ref[rows, :]
            incl = (before | eye)[:C, :C].astype(BF16)
            cum, rest = None, lw
            for _ in range(3):
                piece = rest.astype(BF16)
                rest = rest - piece.astype(F32)
                part = _dot(incl, piece)
                cum = part if cum is None else cum + part
            c_tot = cum[tot_row:tot_row + 1, :]
            at = -kk * jnp.exp(cum - lw)
            rt = r * jnp.exp(cum)
            e_neg = jnp.exp(-cum)
            bt, kt = bd * e_neg, kd * e_neg
            e_rest = jnp.exp(c_tot - cum)
            bc, kc = bd * e_rest, kd * e_rest
            e_tot = jnp.exp(c_tot)
            for p in range(RW_WIDTH // PW):
                ps = slice(p * PW, (p + 1) * PW)
                jobs.append(dict(d=d, ps=ps, rows=rows, y_o=y_o, before=before, incl=before | eye, e=e_tot[:, ps],
                                 at=stack(at[:, ps]), rt=stack(rt[:, ps]), bt=stack(bt[:, ps]), kt=stack(kt[:, ps]),
                                 bc=stack(bc[:, ps]), kc=stack(kc[:, ps]), v=stack(v[:, ps])))
        for j in jobs:
            p = _mmx(_pieces(jnp.concatenate([j["at"], j["rt"]], axis=0), RW_PASSES_LOCAL),
                     _pieces(jnp.concatenate([j["bt"], j["kt"]], axis=0), RW_PASSES_LOCAL), _dot_nt)
            j["a_ab"] = jnp.where(j["before"], p[:PW, :PW], 0.0)
            j["a_ak"] = jnp.where(j["before"], p[:PW, PW:], 0.0)
            j["a_rb"] = jnp.where(j["incl"], p[PW:, :PW], 0.0)
            j["a_rk"] = jnp.where(j["incl"], p[PW:, PW:], 0.0)
            j["vp"] = _pieces(j["v"], RW_PASSES_LOCAL)
        for j in jobs:
            j["w1"] = _mmx(_pieces(j["a_ak"], RW_PASSES_LOCAL), j["vp"], _dot)
            j["tinv"] = jnp.where(eye, 1.0, j["a_ab"])
            j["pw"] = j["a_ab"]
        for _ in range(int(np.log2(C)) - 1):
            for j in jobs:
                pw = _pieces(j["pw"], RW_PASSES_LOCAL)
                j["pw"] = _mmx(pw, pw, _dot)
            for j in jobs:
                j["tinv"] = j["tinv"] + _mmx(_pieces(j["tinv"], RW_PASSES_LOCAL), _pieces(j["pw"], RW_PASSES_LOCAL),
                                             _dot)
        for j in jobs:
            tw = _mmx(_pieces(j["tinv"], RW_PASSES_LOCAL),
                      _pieces(jnp.concatenate([j["at"], j["w1"]], axis=1), RW_PASSES_LOCAL), _dot)
            j["tw"] = tw
            j["twp"] = _pieces(tw, RW_PASSES_LOCAL)
        for j in jobs:
            ry = _mmx(_pieces(j["a_rb"], RW_PASSES_LOCAL), j["twp"], _dot)
            yk = _mmx(_pieces(j["a_rk"], RW_PASSES_LOCAL), j["vp"], _dot)
            j["rr"] = j["rt"] + ry[:, :PW]
            j["yl"] = _fold_pair(ry[:, PW:] + yk)
            j["m"] = _mmx(_pieces(j["bc"], RW_PASSES_TRANSITION), _pieces(j["tw"][:, :PW], RW_PASSES_TRANSITION),
                          _dot_tn)
            uv = jnp.concatenate([j["tw"][:, PW:], j["v"]], axis=0)
            bk = jnp.concatenate([j["bc"], j["kc"]], axis=0)
            j["n"] = _fold_pair(_mmx(_pieces(uv, RW_PASSES_STATE), _pieces(bk, RW_PASSES_STATE), _dot_tn))
        for j in jobs:
            s = s_ref[j["d"], :, j["ps"]]
            sp = _pieces(stack(s), RW_PASSES_STATE)
            y = _mmx(_pieces(j["rr"], RW_PASSES_STATE), sp, _dot_nt)
            j["y_o"][j["rows"], j["ps"]] = _fold_pair(y) + j["yl"]
            sm = _mmx(sp[:RW_PASSES_TRANSITION], _pieces(j["m"], RW_PASSES_TRANSITION), _dot_nt)
            s_ref[j["d"], :, j["ps"]] = s * j["e"] + _fold_pair(sm) + j["n"]
        return carry

    lax.fori_loop(0, nchunk, chunk_step, 0)


def rw_fused_scan(geom, r, v, kk, lwf, kf, bf, lwb, kb, bb):
    B, T, L = geom.B, geom.T, geom.L
    blk = L
    nlat = T // blk
    cblk = geom.BT // blk
    fwd = pl.BlockSpec((blk, RW_WIDTH), lambda b, s: (jnp.where(s == 0, cblk + b, b * nlat + s - 1), 0))
    bwd = pl.BlockSpec((blk, RW_WIDTH), lambda b, s: (jnp.where(s == 0, cblk + b, b * nlat + nlat - s), 0))
    return pl.pallas_call(
        functools.partial(_rw_fused_scan_kernel, nchunk=blk // RW_CHUNK),
        grid=(B, nlat + 1),
        in_specs=[fwd] * 6 + [bwd] * 6,
        out_specs=[fwd, bwd],
        out_shape=[jax.ShapeDtypeStruct((geom.R, RW_WIDTH), F32)] * 2,
        scratch_shapes=[pltpu.VMEM((2, RW_HEAD_DIM, RW_WIDTH), F32)],
        compiler_params=_cparams("parallel", "arbitrary"),
    )(r, v, kk, lwf, kf, bf, r, v, kk, lwb, kb, bb)


def _rw_post_kernel(yf_ref, yb_ref, bonus_ref, g_ref, lng_ref, lnb_ref, ones_ref, o_ref):
    ones = ones_ref[...]
    y = yf_ref[...] + yb_ref[...]
    mean = _head_sum(y, ones) * (1.0 / RW_HEAD_DIM)
    yc = y - mean
    var = _head_sum(yc * yc, ones) * (1.0 / RW_HEAD_DIM)
    y = yc * lax.rsqrt(var + RW_LNX_EPS) * lng_ref[...] + lnb_ref[...]
    o_ref[...] = ((y + bonus_ref[...]) * g_ref[...]).astype(o_ref.dtype)


def rw_post(yf, yb, bonus, g, lnx_g, lnx_b, ones, *, tm):
    R = yf.shape[0]
    row = pl.BlockSpec((tm, RW_WIDTH), lambda i: (i, 0))
    full = lambda a: pl.BlockSpec(a.shape, lambda i: (0,) * a.ndim)
    return pl.pallas_call(
        _rw_post_kernel,
        grid=(R // tm,),
        in_specs=[row] * 4 + [full(lnx_g), full(lnx_b), full(ones)],
        out_specs=row,
        out_shape=jax.ShapeDtypeStruct((R, RW_WIDTH), BF16),
        compiler_params=_cparams("parallel"),
    )(yf, yb, bonus, g, lnx_g, lnx_b, ones)


def _rw_in_cols(w):
    parts = [w[..., :3 * RW_WIDTH]]
    off = 3 * RW_WIDTH
    for n in (RW_DECAY_LORA, RW_DECAY_LORA, RW_AAA_LORA, RW_AAA_LORA, RW_GATE_LORA):
        parts.append(_pad_cols(w[..., off:off + n], LORA_PAD))
        off += n
    return jnp.concatenate(parts, axis=-1)


def _pad_rows(w, n):
    return jnp.pad(w, [(0, 0)] * (w.ndim - 2) + [(0, n - w.shape[-2]), (0, 0)])


def _mla_wq_cols(w):
    w = w.reshape(w.shape[0], MLA_HEADS, MLA_NOPE + MLA_ROPE)
    return _pad_cols(w, MLA_QK_PAD).reshape(w.shape[0], MLA_HEADS * MLA_QK_PAD)


def kernel(x, c, ctx, c_ctx, ada_w, ada_b, norm1_g, w_in, rw_mu, rw_w0, rw_w2, rw_a0, rw_a2, rw_g2, rw_kk, rw_ka,
           rw_rk, rw_lnx_g, rw_lnx_b, rw_v0, rw_v1, rw_v2, wa_sink, mla_qnorm_g, mla_kvnorm_g, mla_w_uq, mla_w_ukv,
           w_branch, w_out, norm2_g, ffn_w_in, ffn_conv_w, ffn_conv_b, ffn_w_out, final_norm_g):
    B, T, D = x.shape
    L = ctx.shape[1]
    depth = w_in.shape[0]
    F = ffn_w_out.shape[1]
    geom = Geom(B, T, L)
    tm = _pick_tile(T, (512, 256, 128))
    assert (B * L) % tm == 0
    tmm = _pick_tile(geom.R, (1280, 1024, 640, 512, 256, 128))

    rw_cols = 3 * RW_WIDTH + 2 * RW_DECAY_LORA + 2 * RW_AAA_LORA + RW_GATE_LORA
    wa_cols = WA_WIDTH + 2 * WA_KV_WIDTH
    mla_cols = MLA_Q_LORA + MLA_KV_LORA + MLA_ROPE
    mla_cols_pad = MLA_Q_LORA + MLA_KV_LORA + 128

    cos_wa, sin_wa = _rope_tables(geom, WA_HEAD_DIM)
    cos_wa, sin_wa = jnp.tile(cos_wa, (1, 2)), jnp.tile(sin_wa, (1, 2))
    cos_m, sin_m = _rope_tables(geom, MLA_ROPE)
    one, zero = jnp.ones((geom.R, MLA_NOPE), F32), jnp.zeros((geom.R, MLA_NOPE), F32)
    cos_mla = jnp.concatenate([one, cos_m, one[:, :64]], axis=-1)
    sin_mla = jnp.concatenate([zero, sin_m, zero[:, :64]], axis=-1)
    lane = np.arange(128)
    ones_blk = jnp.asarray((lane[:, None] // RW_HEAD_DIM) == (lane[None, :] // RW_HEAD_DIM), F32)

    xs = jnp.concatenate([x.reshape(B * T, D), ctx.reshape(B * L, D)], axis=0)
    cvec = jnp.concatenate([c, c_ctx[None, :], jnp.zeros((8 - (B + 1) % 8, D), F32)], axis=0)
    v_first = None
    for l in range(depth):
        need_ctx = l < depth - 1
        mod = ada_modulation(cvec, ada_w[l], ada_b[l])
        mod = [mod[:, k * D:(k + 1) * D].reshape(-1, 1, D) for k in range(6)]

        w = w_in[l]
        tn_in = _pick_tile(3 * D, (768, 512, 256, 128))
        w_rw = _pad_cols(_rw_in_cols(w[:, :rw_cols]), _round_up(RW_Z_COLS, tn_in)).astype(BF16)
        w_wa = _pad_cols(w[:, rw_cols:rw_cols + wa_cols], _round_up(wa_cols, tn_in)).astype(BF16)
        w_mla = _pad_cols(w[:, rw_cols + wa_cols:rw_cols + wa_cols + mla_cols],
                          _round_up(mla_cols_pad, tn_in)).astype(BF16)
        w_gate = w[:, rw_cols + wa_cols + mla_cols:].astype(BF16)
        z_rw, z_wa, z_mla, gates = in_projection(geom, xs, norm1_g[l], mod[0], mod[1], w_rw, w_wa, w_mla, w_gate,
                                                 tm=tmm, tn=tn_in)

        rw_p = dict(
            mu=_rw_in_cols(rw_mu[l][None, :]), w0=rw_w0[l], w2=_pad_rows(rw_w2[l], LORA_PAD), a0=rw_a0[l],
            a2=_pad_rows(rw_a2[l], LORA_PAD), g2=_pad_rows(rw_g2[l], LORA_PAD), kk=rw_kk[l][None, :],
            ka=rw_ka[l][None, :], rk=rw_rk[l].reshape(1, RW_WIDTH), ones=ones_blk)
        vres = None if l == 0 else (rw_v0[l - 1][None, :], rw_v1[l - 1], rw_v2[l - 1])
        r, v, kk, lwf, kf, bf, lwb, kb, bb, g, bonus = rw_prep(geom, z_rw, rw_p, v_first, vres, tm=min(tm, 256))
        if l == 0:
            v_first = v
        yf, yb = rw_fused_scan(geom, r, v, kk, lwf, kf, bf, lwb, kb, bb)
        o_a = rw_post(yf, yb, bonus, g, rw_lnx_g[l][None, :], rw_lnx_b[l][None, :], ones_blk, tm=tm)

        q_wa, k_wa, v_wa = wa_prep(z_wa, cos_wa, sin_wa, tm=tm)
        ob_l = wa_attention(geom, q_wa, k_wa, v_wa, wa_sink[l], local=True)
        parts = [ob_l]
        if need_ctx:
            parts.append(wa_attention(geom, q_wa, k_wa, v_wa, wa_sink[l], local=False))
        else:
            parts.append(jnp.zeros((B * L, WA_WIDTH), BF16))
        o_b = jnp.concatenate(parts, axis=0)

        wq = _mla_wq_cols(mla_w_uq[l]).astype(BF16)
        q_m, k_m, v_m = mla_prep(z_mla, mla_qnorm_g[l], mla_kvnorm_g[l], wq, mla_w_ukv[l].astype(BF16),
                                 cos_mla, sin_mla, tm=tm)
        tq = _pick_tile(T, (1024, 512, 256, 128))
        tk = _pick_tile(T, (4096, 2048, 1024, 512, 256, 128))
        parts = [mla_attention(geom, q_m, k_m, v_m, with_latent=True, tq=tq, tk=tk)]
        if need_ctx:
            parts.append(mla_attention(geom, q_m, k_m, v_m, with_latent=False, tq=L, tk=L))
        else:
            parts.append(jnp.zeros((B * L, MLA_HEADS * MLA_V), BF16))
        o_c = jnp.concatenate(parts, axis=0)

        tn_d = _pick_tile(D, (1024, 512, 256, 128))
        y = merge_branches(o_a, o_b, o_c, gates, w_branch[l].astype(BF16), tm=tmm, tn=tn_d)
        xs = matmul_gated_residual(geom, y, w_out[l].astype(BF16), xs, mod[2], tm=tmm, tn=tn_d)

        tf = _pick_tile(F, (512, 256, 128))
        hmid = ffn_in_conv_glu(geom, xs, norm2_g[l], mod[3], mod[4], ffn_w_in[l].astype(BF16), ffn_conv_w[l],
                               ffn_conv_b[l], tm=tmm, tf=tf)
        xs = matmul_gated_residual(geom, hmid, ffn_w_out[l].astype(BF16), xs, mod[5], tm=tmm,
                                   tn=_pick_tile(D, (512, 256, 128)))

    out = final_rmsnorm(xs, final_norm_g, B * T, tm=tm)
    return out.reshape(B, T, D)
```

```python
import functools

import jax
import jax.numpy as jnp
import numpy as np
from jax import lax
from jax.experimental import pallas as pl
from jax.experimental.pallas import tpu as pltpu

F32 = jnp.float32
BF16 = jnp.bfloat16
HIGHEST = lax.Precision.HIGHEST

NORM_EPS = 1e-6
NEG_INF = -1e30
GRID_W = 64
ROPE_BASE = 10000.0

RW_HEADS = 16
RW_HEAD_DIM = 64
RW_WIDTH = RW_HEADS * RW_HEAD_DIM
RW_DECAY_LORA = 96
RW_AAA_LORA = 96
RW_GATE_LORA = 64
RW_LNX_EPS = 64e-5
RW_CHUNK = 64
RW_PAIR = 2 * RW_HEAD_DIM
RW_PASSES_LOCAL = 1
RW_PASSES_STATE = 3
RW_PASSES_TRANSITION = 1
LORA_PAD = 128

WA_HEADS = 16
WA_KV_HEADS = 4
WA_GROUP = WA_HEADS // WA_KV_HEADS
WA_HEAD_DIM = 64
WA_WIDTH = WA_HEADS * WA_HEAD_DIM
WA_KV_WIDTH = WA_KV_HEADS * WA_HEAD_DIM
WA_V_PAD = WA_KV_HEADS * 128
WINDOW = 128
WA_SCALE = WA_HEAD_DIM ** -0.5

MLA_HEADS = 8
MLA_NOPE = 128
MLA_ROPE = 64
MLA_V = 128
MLA_Q_LORA = 512
MLA_KV_LORA = 512
MLA_QK_PAD = 256
MLA_V_PAD = 256
MLA_SCALE = (MLA_NOPE + MLA_ROPE) ** -0.5
MLA_SUB_KEYS = 2048
LOG2_E = 1.4426950408889634

CONV_W = 3
VMEM_LIMIT_BYTES = 56 * 1024 * 1024


def _cparams(*sem):
    return pltpu.CompilerParams(dimension_semantics=sem, vmem_limit_bytes=VMEM_LIMIT_BYTES)


def _dot(a, b, precision=None):
    return jnp.dot(a, b, preferred_element_type=F32, precision=precision)


def _dot_nt(a, b, precision=None):
    return lax.dot_general(a, b, (((1,), (1,)), ((), ())), preferred_element_type=F32, precision=precision)


def _dot_tn(a, b, precision=None):
    return lax.dot_general(a, b, (((0,), (0,)), ((), ())), preferred_element_type=F32, precision=precision)


def _pick_tile(n, candidates):
    for c in candidates:
        if n % c == 0:
            return c
    raise ValueError(f"no tile in {candidates} divides {n}")


def _pad_cols(w, n):
    return jnp.pad(w, [(0, 0)] * (w.ndim - 1) + [(0, n - w.shape[-1])])


def _round_up(n, m):
    return (n + m - 1) // m * m


class Geom:
    def __init__(self, B, T, L):
        assert T & (T - 1) == 0 and L & (L - 1) == 0, "sequence lengths must be powers of two"
        assert T % L == 0 and L % RW_CHUNK == 0 and T % GRID_W == 0
        self.B, self.T, self.L = B, T, L
        self.BT = B * T
        self.R = B * T + B * L


def _select_row_group(geom, row0, tm, tab_ref):
    r = row0 + lax.broadcasted_iota(jnp.int32, (tm, 1), 0)
    out = tab_ref[geom.B]
    for b in range(geom.B):
        out = jnp.where((r >= b * geom.T) & (r < (b + 1) * geom.T), tab_ref[b], out)
    return out


def _seq_edge_masks(geom, row0, tm):
    r = row0 + lax.broadcasted_iota(jnp.int32, (tm, 1), 0)
    is_lat = r < geom.BT
    pos = jnp.where(is_lat, r & (geom.T - 1), (r - geom.BT) & (geom.L - 1))
    last = jnp.where(is_lat, geom.T - 1, geom.L - 1)
    return pos == 0, pos == last


def _shifted_rows(x, prev8, next8, first, last):
    tm = x.shape[0]
    rid = lax.broadcasted_iota(jnp.int32, (tm, 1), 0)
    up = jnp.where(rid == 0, prev8[7:8, :], pltpu.roll(x, 1, axis=0))
    dn = jnp.where(rid == tm - 1, next8[0:1, :], pltpu.roll(x, tm - 1, axis=0))
    return jnp.where(first, 0.0, up), jnp.where(last, 0.0, dn)


def _halo_specs(tm, R, width, col_of):
    nb8 = tm // 8
    prev = pl.BlockSpec((8, width), lambda i, *a: (jnp.maximum(i * nb8 - 1, 0), col_of(i, *a)))
    nxt = pl.BlockSpec((8, width), lambda i, *a: (jnp.minimum((i + 1) * nb8, R // 8 - 1), col_of(i, *a)))
    return prev, nxt


def _ada_kernel(c_ref, w_ref, b_ref, o_ref):
    c = c_ref[...]
    o_ref[...] = _dot(c * jax.nn.sigmoid(c), w_ref[...], HIGHEST) + b_ref[...]


def ada_modulation(cvec, w, b):
    G, D = cvec.shape
    N = w.shape[1]
    tn = _pick_tile(N, (1024, 512, 256, 128))
    return pl.pallas_call(
        _ada_kernel,
        grid=(N // tn,),
        in_specs=[pl.BlockSpec((G, D), lambda j: (0, 0)),
                  pl.BlockSpec((D, tn), lambda j: (0, j)),
                  pl.BlockSpec((1, tn), lambda j: (0, j))],
        out_specs=pl.BlockSpec((G, tn), lambda j: (0, j)),
        out_shape=jax.ShapeDtypeStruct((G, N), F32),
        compiler_params=_cparams("arbitrary"),
    )(cvec, w, b.reshape(1, N))


def _norm_modulate(x, g, sc, sh):
    gain = g * (1.0 + sc)
    return (x * lax.rsqrt(jnp.mean(x * x, axis=-1, keepdims=True) + NORM_EPS) * gain + sh).astype(BF16)


PROLOGUE_ROWS = 256


def _norm_modulate_rows(geom, row0, x_ref, g_ref, sc_ref, sh_ref, h_ref, h_off):
    tm = x_ref.shape[0]
    step = PROLOGUE_ROWS if tm % PROLOGUE_ROWS == 0 else tm
    one_group = geom.T % step == 0 and geom.BT % step == 0
    g = g_ref[...]
    for r in range(0, tm, step):
        if one_group:
            sc, sh = sc_ref[geom.B], sh_ref[geom.B]
            for b in range(geom.B):
                in_b = (row0 + r >= b * geom.T) & (row0 + r < (b + 1) * geom.T)
                sc, sh = jnp.where(in_b, sc_ref[b], sc), jnp.where(in_b, sh_ref[b], sh)
        else:
            sc = _select_row_group(geom, row0 + r, step, sc_ref)
            sh = _select_row_group(geom, row0 + r, step, sh_ref)
        h_ref[h_off + r:h_off + r + step] = _norm_modulate(x_ref[r:r + step], g, sc, sh)


def _in_proj_kernel(x_ref, g_ref, sh_ref, sc_ref, w_ref, zrw_ref, zwa_ref, zmla_ref, gate_ref, h_ref,
                    *, geom, tm, starts):
    j = pl.program_id(1)

    @pl.when(j == 0)
    def _():
        _norm_modulate_rows(geom, pl.program_id(0) * tm, x_ref, g_ref, sc_ref, sh_ref, h_ref, 0)

    wa0, mla0, gate0 = starts

    @pl.when(j < wa0)
    def _():
        zrw_ref[...] = _dot(h_ref[...], w_ref[...])

    @pl.when((j >= wa0) & (j < mla0))
    def _():
        zwa_ref[...] = _dot(h_ref[...], w_ref[...])

    @pl.when((j >= mla0) & (j < gate0))
    def _():
        zmla_ref[...] = _dot(h_ref[...], w_ref[...])

    @pl.when(j >= gate0)
    def _():
        gate_ref[...] = jax.nn.sigmoid(_dot(h_ref[...], w_ref[...])).astype(gate_ref.dtype)


def in_projection(geom, x, g, shift, scale, w_rw, w_wa, w_mla, w_gate, *, tm, tn):
    R, K = x.shape
    widths = [w.shape[1] for w in (w_rw, w_wa, w_mla, w_gate)]
    assert R % tm == 0 and all(n % tn == 0 for n in widths)
    tiles = [n // tn for n in widths]
    starts = tuple(int(v) for v in np.cumsum(tiles)[:3])
    w_all = jnp.concatenate([w_rw, w_wa, w_mla, w_gate], axis=1)
    full = lambda a: pl.BlockSpec(a.shape, lambda i, j: (0,) * a.ndim)
    out_spec = lambda first, n: pl.BlockSpec((tm, tn), lambda i, j: (i, jnp.clip(j - first, 0, n - 1)))
    firsts = (0,) + starts
    return pl.pallas_call(
        functools.partial(_in_proj_kernel, geom=geom, tm=tm, starts=starts),
        grid=(R // tm, sum(tiles)),
        in_specs=[pl.BlockSpec((tm, K), lambda i, j: (i, 0), pipeline_mode=pl.Buffered(1)),
                  pl.BlockSpec((1, K), lambda i, j: (0, 0)),
                  full(shift), full(scale),
                  pl.BlockSpec((K, tn), lambda i, j: (0, j))],
        out_specs=[out_spec(f, n) for f, n in zip(firsts, tiles)],
        out_shape=[jax.ShapeDtypeStruct((R, widths[0]), F32), jax.ShapeDtypeStruct((R, widths[1]), F32),
                   jax.ShapeDtypeStruct((R, widths[2]), F32), jax.ShapeDtypeStruct((R, widths[3]), BF16)],
        scratch_shapes=[pltpu.VMEM((tm, K), BF16)],
        compiler_params=_cparams("parallel", "arbitrary"),
    )(x, g.reshape(1, K), shift, scale, w_all)


def _mm_resid_kernel(y_ref, w_ref, r_ref, gate_ref, o_ref, *, geom, tm):
    gate = _select_row_group(geom, pl.program_id(0) * tm, tm, gate_ref)
    o_ref[...] = r_ref[...] + gate * _dot(y_ref[...], w_ref[...])


def matmul_gated_residual(geom, y, w, resid, gate, *, tm, tn):
    R, K = y.shape
    N = w.shape[1]
    G = gate.shape[0]
    assert R % tm == 0 and N % tn == 0
    return pl.pallas_call(
        functools.partial(_mm_resid_kernel, geom=geom, tm=tm),
        grid=(R // tm, N // tn),
        in_specs=[pl.BlockSpec((tm, K), lambda i, j: (i, 0)),
                  pl.BlockSpec((K, tn), lambda i, j: (0, j)),
                  pl.BlockSpec((tm, tn), lambda i, j: (i, j)),
                  pl.BlockSpec((G, 1, tn), lambda i, j: (0, 0, j))],
        out_specs=pl.BlockSpec((tm, tn), lambda i, j: (i, j)),
        out_shape=jax.ShapeDtypeStruct((R, N), F32),
        compiler_params=_cparams("parallel", "arbitrary"),
    )(y, w, resid, gate)


FFN_HALO = 16


def _ffn_in_kernel(x_ref, xp_ref, xn_ref, g_ref, sh_ref, sc_ref, wg_ref, wu_ref, cw_ref, cb_ref, o_ref, h_ref,
                   *, geom, tm):
    row0 = pl.program_id(0) * tm
    H = FFN_HALO

    @pl.when(pl.program_id(1) == 0)
    def _():
        _norm_modulate_rows(geom, row0, xp_ref, g_ref, sc_ref, sh_ref, h_ref, 0)
        _norm_modulate_rows(geom, row0, x_ref, g_ref, sc_ref, sh_ref, h_ref, H)
        _norm_modulate_rows(geom, row0 + tm - H, xn_ref, g_ref, sc_ref, sh_ref, h_ref, H + tm)

    first, last = _seq_edge_masks(geom, row0, tm)
    gt = _dot(h_ref[...], wg_ref[...])
    u = _dot(h_ref[H:H + tm], wu_ref[...])
    up = jnp.where(first, 0.0, pltpu.roll(gt, 1, axis=0)[H:H + tm])
    dn = jnp.where(last, 0.0, pltpu.roll(gt, tm + 2 * H - 1, axis=0)[H:H + tm])
    cw = cw_ref[...]
    conv = cb_ref[...] + up * cw[0:1, :]
    conv = conv + gt[H:H + tm] * cw[1:2, :]
    conv = conv + dn * cw[2:3, :]
    o_ref[...] = (jax.nn.gelu(conv, approximate=True) * u).astype(o_ref.dtype)


def ffn_in_conv_glu(geom, x, g, shift, scale, w_in, conv_w, conv_b, *, tm, tf):
    R, K = x.shape
    F = w_in.shape[1] // 2
    nj = F // tf
    nbh = tm // FFN_HALO
    assert R % tm == 0 and F % tf == 0 and tm % FFN_HALO == 0
    full = lambda a: pl.BlockSpec(a.shape, lambda i, j: (0,) * a.ndim)
    return pl.pallas_call(
        functools.partial(_ffn_in_kernel, geom=geom, tm=tm),
        grid=(R // tm, nj),
        in_specs=[pl.BlockSpec((tm, K), lambda i, j: (i, 0), pipeline_mode=pl.Buffered(1)),
                  pl.BlockSpec((FFN_HALO, K), lambda i, j: (jnp.maximum(i * nbh - 1, 0), 0)),
                  pl.BlockSpec((FFN_HALO, K), lambda i, j: (jnp.minimum((i + 1) * nbh, R // FFN_HALO - 1), 0)),
                  pl.BlockSpec((1, K), lambda i, j: (0, 0)),
                  full(shift), full(scale),
                  pl.BlockSpec((K, tf), lambda i, j: (0, j)),
                  pl.BlockSpec((K, tf), lambda i, j: (0, nj + j)),
                  pl.BlockSpec((CONV_W, tf), lambda i, j: (0, j)),
                  pl.BlockSpec((1, tf), lambda i, j: (0, j))],
        out_specs=pl.BlockSpec((tm, tf), lambda i, j: (i, j)),
        out_shape=jax.ShapeDtypeStruct((R, F), BF16),
        scratch_shapes=[pltpu.VMEM((tm + 2 * FFN_HALO, K), BF16)],
        compiler_params=_cparams("parallel", "arbitrary"),
    )(x, x, x, g.reshape(1, K), shift, scale, w_in, w_in, conv_w, conv_b.reshape(1, F))


def _merge_kernel(oa_ref, ob_ref, oc_ref, ga_ref, gb_ref, gc_ref, w_ref, o_ref):
    y = ga_ref[...] * _dot(oa_ref[...], w_ref[0])
    y = y + gb_ref[...] * _dot(ob_ref[...], w_ref[1])
    y = y + gc_ref[...] * _dot(oc_ref[...], w_ref[2])
    o_ref[...] = y.astype(o_ref.dtype)


def merge_branches(oa, ob, oc, gates, wb, *, tm, tn):
    R, K = oa.shape
    D = wb.shape[2]
    nj = D // tn
    bspec = pl.BlockSpec((tm, K), lambda i, j: (i, 0))
    gspec = lambda k: pl.BlockSpec((tm, tn), lambda i, j: (i, k * nj + j))
    return pl.pallas_call(
        _merge_kernel,
        grid=(R // tm, nj),
        in_specs=[bspec, bspec, bspec, gspec(0), gspec(1), gspec(2),
                  pl.BlockSpec((3, K, tn), lambda i, j: (0, 0, j))],
        out_specs=pl.BlockSpec((tm, tn), lambda i, j: (i, j)),
        out_shape=jax.ShapeDtypeStruct((R, D), BF16),
        compiler_params=_cparams("parallel", "arbitrary"),
    )(oa, ob, oc, gates, gates, gates, wb)


def _rmsnorm_kernel(x_ref, g_ref, o_ref):
    x = x_ref[...]
    o_ref[...] = x * lax.rsqrt(jnp.mean(x * x, axis=-1, keepdims=True) + NORM_EPS) * g_ref[...]


def final_rmsnorm(x, g, rows, *, tm):
    D = x.shape[1]
    return pl.pallas_call(
        _rmsnorm_kernel,
        grid=(rows // tm,),
        in_specs=[pl.BlockSpec((tm, D), lambda i: (i, 0)), pl.BlockSpec((1, D), lambda i: (0, 0))],
        out_specs=pl.BlockSpec((tm, D), lambda i: (i, 0)),
        out_shape=jax.ShapeDtypeStruct((rows, D), F32),
        compiler_params=_cparams("parallel"),
    )(x, g.reshape(1, D))


def _rot_half64(z):
    n = z.shape[-1]
    lane = lax.broadcasted_iota(jnp.int32, z.shape, z.ndim - 1)
    return jnp.where((lane & 63) < 32, pltpu.roll(z, n - 32, axis=z.ndim - 1), pltpu.roll(z, 32, axis=z.ndim - 1))


def _rope_tables(geom, dim):
    nf = dim // 4
    inv = ROPE_BASE ** (-jnp.arange(nf, dtype=F32) / nf)
    rows = geom.T // GRID_W
    row = jnp.repeat(jnp.arange(rows, dtype=F32), GRID_W)
    col = jnp.tile(jnp.arange(GRID_W, dtype=F32), rows)
    ang = jnp.concatenate([row[:, None] * inv, col[:, None] * inv], axis=-1)
    cos, sin = jnp.cos(ang), jnp.sin(ang)
    cos_t = jnp.concatenate([cos, cos], axis=-1)
    sin_t = jnp.concatenate([-sin, sin], axis=-1)
    nctx = geom.B * geom.L
    cos_f = jnp.concatenate([jnp.tile(cos_t, (geom.B, 1)), jnp.ones((nctx, dim), F32)], axis=0)
    sin_f = jnp.concatenate([jnp.tile(sin_t, (geom.B, 1)), jnp.zeros((nctx, dim), F32)], axis=0)
    return cos_f, sin_f


def _wa_prep_kernel(z_ref, cos_ref, sin_ref, q_ref, k_ref, v_ref):
    cos = cos_ref[...]
    sin = sin_ref[...]
    for c in range(WA_WIDTH // 128):
        z = z_ref[:, c * 128:(c + 1) * 128]
        q_ref[:, c * 128:(c + 1) * 128] = ((z * cos + _rot_half64(z) * sin) * (WA_SCALE * LOG2_E)).astype(BF16)
    for c in range(WA_KV_WIDTH // 128):
        z = z_ref[:, WA_WIDTH + c * 128:WA_WIDTH + (c + 1) * 128]
        k_ref[:, c * 128:(c + 1) * 128] = (z * cos + _rot_half64(z) * sin).astype(BF16)
    v_ref[...] = jnp.ones(v_ref.shape, BF16)
    for g in range(WA_KV_HEADS):
        src = WA_WIDTH + WA_KV_WIDTH + g * WA_HEAD_DIM
        v_ref[:, g * 128:g * 128 + WA_HEAD_DIM] = z_ref[:, src:src + WA_HEAD_DIM].astype(BF16)


def wa_prep(z, cos, sin, *, tm):
    R = z.shape[0]
    row = lambda w: pl.BlockSpec((tm, w), lambda i: (i, 0))
    return pl.pallas_call(
        _wa_prep_kernel,
        grid=(R // tm,),
        in_specs=[row(WA_WIDTH + 2 * WA_KV_WIDTH), row(128), row(128)],
        out_specs=[row(WA_WIDTH), row(WA_KV_WIDTH), row(WA_V_PAD)],
        out_shape=[jax.ShapeDtypeStruct((R, WA_WIDTH), BF16),
                   jax.ShapeDtypeStruct((R, WA_KV_WIDTH), BF16),
                   jax.ShapeDtypeStruct((R, WA_V_PAD), BF16)],
        compiler_params=_cparams("parallel"),
    )(z, cos, sin)


def _wa_attn_kernel(*refs, local, nqb, tq):
    if local:
        sink_ref, q_ref, kp_ref, kc_ref, kn_ref, vp_ref, vc_ref, vn_ref, kx_ref, vx_ref, o_ref = refs
    else:
        sink_ref, q_ref, kx_ref, vx_ref, o_ref = refs
    i = pl.program_id(1)
    nk_ctx = kx_ref.shape[0]
    rows = WA_GROUP * tq
    qpos = lax.broadcasted_iota(jnp.int32, (rows, 1), 0) & (tq - 1)
    head_in_group = lax.broadcasted_iota(jnp.int32, (rows, 1), 0) >> int(np.log2(tq))
    if local:
        off_prev = jnp.where(i > 0, 0, tq)
        off_next = jnp.where(i < nqb - 1, 0, tq)
        j = lax.broadcasted_iota(jnp.int32, (1, 3 * tq + nk_ctx), 1)
        valid = ((j >= tq) & (j < 2 * tq)) | (j >= 3 * tq)
        valid = valid | ((j < tq) & (j >= qpos + off_prev))
        valid = valid | ((j >= 2 * tq) & (j < 3 * tq) & ((j - 2 * tq) <= qpos - off_next))
    def scores(g):
        ks = slice(g * WA_HEAD_DIM, (g + 1) * WA_HEAD_DIM)
        qg = jnp.concatenate(
            [q_ref[:, (g * WA_GROUP + a) * WA_HEAD_DIM:(g * WA_GROUP + a + 1) * WA_HEAD_DIM] for a in range(WA_GROUP)],
            axis=0)
        if local:
            kcat = jnp.concatenate([kp_ref[:, ks], kc_ref[:, ks], kn_ref[:, ks], kx_ref[:, ks]], axis=0)
        else:
            kcat = kx_ref[:, ks]
        return _dot_nt(qg, kcat)

    s_next = scores(0)
    for g in range(WA_KV_HEADS):
        s = s_next
        if g + 1 < WA_KV_HEADS:
            s_next = scores(g + 1)
        vs = slice(g * 128, (g + 1) * 128)
        sink = jnp.zeros((rows, 1), F32)
        for a in range(WA_GROUP):
            sink = jnp.where(head_in_group == a, sink_ref[g * WA_GROUP + a] * LOG2_E, sink)
        if local:
            vcat = jnp.concatenate([vp_ref[:, vs], vc_ref[:, vs], vn_ref[:, vs], vx_ref[:, vs]], axis=0)
            s = jnp.where(valid, s, NEG_INF)
        else:
            vcat = vx_ref[:, vs]
        m = jnp.maximum(jnp.max(s, axis=-1, keepdims=True), sink)
        o = _dot(jnp.exp2(s - m).astype(BF16), vcat)
        o = o[:, :WA_HEAD_DIM] / (o[:, WA_HEAD_DIM:] + jnp.exp2(sink - m))
        for a in range(WA_GROUP):
            h = g * WA_GROUP + a
            o_ref[:, h * WA_HEAD_DIM:(h + 1) * WA_HEAD_DIM] = o[a * tq:(a + 1) * tq].astype(o_ref.dtype)


def wa_attention(geom, q, k, v, sink, *, local):
    B, T, L = geom.B, geom.T, geom.L
    sink_spec = pl.BlockSpec(memory_space=pltpu.SMEM)
    ctx_spec = lambda w: pl.BlockSpec((L, w), lambda b, i: (geom.BT // L + b, 0))
    if local:
        tq = WINDOW
        nqb = T // tq
        kv = lambda f, w: pl.BlockSpec((tq, w), lambda b, i: (b * nqb + f(i), 0))
        prev = lambda i: jnp.maximum(i - 1, 0)
        cur = lambda i: i
        nxt = lambda i: jnp.minimum(i + 1, nqb - 1)
        in_specs = [sink_spec, pl.BlockSpec((tq, WA_WIDTH), lambda b, i: (b * nqb + i, 0)),
                    kv(prev, WA_KV_WIDTH), kv(cur, WA_KV_WIDTH), kv(nxt, WA_KV_WIDTH),
                    kv(prev, WA_V_PAD), kv(cur, WA_V_PAD), kv(nxt, WA_V_PAD),
                    ctx_spec(WA_KV_WIDTH), ctx_spec(WA_V_PAD)]
        args = (sink, q, k, k, k, v, v, v, k, v)
        out_rows, out_spec = geom.BT, pl.BlockSpec((tq, WA_WIDTH), lambda b, i: (b * nqb + i, 0))
    else:
        tq, nqb = L, 1
        in_specs = [sink_spec, pl.BlockSpec((tq, WA_WIDTH), lambda b, i: (geom.BT // L + b, 0)),
                    ctx_spec(WA_KV_WIDTH), ctx_spec(WA_V_PAD)]
        args = (sink, q, k, v)
        out_rows, out_spec = B * L, pl.BlockSpec((tq, WA_WIDTH), lambda b, i: (b, 0))
    return pl.pallas_call(
        functools.partial(_wa_attn_kernel, local=local, nqb=nqb, tq=tq),
        grid=(B, nqb),
        in_specs=in_specs,
        out_specs=out_spec,
        out_shape=jax.ShapeDtypeStruct((out_rows, WA_WIDTH), BF16),
        compiler_params=_cparams("parallel", "arbitrary"),
    )(*args)


def _mla_prep_kernel(z_ref, qg_ref, kvg_ref, wq_ref, wkv_ref, cos_ref, sin_ref, q_ref, k_ref, v_ref):
    def norm(x, g):
        return (x * lax.rsqrt(jnp.mean(x * x, axis=-1, keepdims=True) + NORM_EPS) * g).astype(BF16)

    cos = cos_ref[...]
    sin = sin_ref[...]
    q = _dot(norm(z_ref[:, :MLA_Q_LORA], qg_ref[...]), wq_ref[...])
    kv = _dot(norm(z_ref[:, MLA_Q_LORA:MLA_Q_LORA + MLA_KV_LORA], kvg_ref[...]), wkv_ref[...])
    kr = z_ref[:, MLA_Q_LORA + MLA_KV_LORA:MLA_Q_LORA + MLA_KV_LORA + 128]
    kr = (kr * cos[:, 128:] + _rot_half64(kr) * sin[:, 128:]).astype(BF16)
    for h in range(MLA_HEADS):
        qh = q[:, h * MLA_QK_PAD:(h + 1) * MLA_QK_PAD]
        q_ref[h] = ((qh * cos + _rot_half64(qh) * sin) * (MLA_SCALE * LOG2_E)).astype(BF16)
        k_ref[h, :, :MLA_NOPE] = kv[:, h * 256:h * 256 + MLA_NOPE].astype(BF16)
        k_ref[h, :, MLA_NOPE:] = kr
        v_ref[h, :, :MLA_V] = kv[:, h * 256 + MLA_NOPE:(h + 1) * 256].astype(BF16)
        v_ref[h, :, MLA_V:] = jnp.ones((kv.shape[0], MLA_V_PAD - MLA_V), BF16)


def mla_prep(z, qnorm_g, kvnorm_g, wq, wkv, cos, sin, *, tm):
    R, Z = z.shape
    full = lambda a: pl.BlockSpec(a.shape, lambda i: (0,) * a.ndim)
    qg, kvg = qnorm_g.reshape(1, -1), kvnorm_g.reshape(1, -1)
    hd = lambda w: pl.BlockSpec((MLA_HEADS, tm, w), lambda i: (0, i, 0))
    return pl.pallas_call(
        _mla_prep_kernel,
        grid=(R // tm,),
        in_specs=[pl.BlockSpec((tm, Z), lambda i: (i, 0)), full(qg), full(kvg), full(wq), full(wkv),
                  pl.BlockSpec((tm, MLA_QK_PAD), lambda i: (i, 0)), pl.BlockSpec((tm, MLA_QK_PAD), lambda i: (i, 0))],
        out_specs=[hd(MLA_QK_PAD), hd(MLA_QK_PAD), hd(MLA_V_PAD)],
        out_shape=[jax.ShapeDtypeStruct((MLA_HEADS, R, MLA_QK_PAD), BF16),
                   jax.ShapeDtypeStruct((MLA_HEADS, R, MLA_QK_PAD), BF16),
                   jax.ShapeDtypeStruct((MLA_HEADS, R, MLA_V_PAD), BF16)],
        compiler_params=_cparams("parallel"),
    )(z, qg, kvg, wq, wkv, cos, sin)


def _mla_flash_kernel(*refs, with_latent, sub):
    if with_latent:
        q_ref, kx_ref, vx_ref, k_ref, v_ref, o_ref, m_ref, acc_ref = refs
    else:
        q_ref, kx_ref, vx_ref, o_ref, m_ref, acc_ref = refs
    ki = pl.program_id(3)
    q = q_ref[0]

    def update(s, v, m_old, acc_old):
        cols = [s[:, c * 128:(c + 1) * 128] for c in range(s.shape[1] // 128)]
        mx = functools.reduce(jnp.maximum, cols)
        m_new = jnp.maximum(m_old, jnp.max(mx, axis=-1, keepdims=True))
        alpha = jnp.exp2(m_old - m_new)
        p = jnp.concatenate([jnp.exp2(c - m_new).astype(BF16) for c in cols], axis=-1)
        return m_new, jnp.concatenate([alpha, alpha], axis=-1) * acc_old + _dot(p, v)

    @pl.when(ki == 0)
    def _():
        tq = q.shape[0]
        m, acc = update(_dot_nt(q, kx_ref[0]), vx_ref[0], jnp.full((tq, 128), NEG_INF, F32),
                        jnp.zeros((tq, MLA_V_PAD), F32))
        m_ref[...], acc_ref[...] = m, acc

    if with_latent:
        nsub = k_ref.shape[1] // sub
        m, acc = m_ref[...], acc_ref[...]
        s_next = _dot_nt(q, k_ref[0, 0:sub, :])
        for j in range(nsub):
            s = s_next
            if j + 1 < nsub:
                s_next = _dot_nt(q, k_ref[0, (j + 1) * sub:(j + 2) * sub, :])
            m, acc = update(s, v_ref[0, j * sub:(j + 1) * sub, :], m, acc)
        m_ref[...], acc_ref[...] = m, acc

    @pl.when(ki == pl.num_programs(3) - 1)
    def _():
        acc = acc_ref[...]
        o_ref[...] = (acc[:, :MLA_V] / acc[:, MLA_V:]).astype(o_ref.dtype)


def mla_attention(geom, q, k, v, *, with_latent, tq, tk):
    B, T, L = geom.B, geom.T, geom.L
    cblk = geom.BT // L
    ctx_k = pl.BlockSpec((1, L, MLA_QK_PAD), lambda b, h, qi, ki: (h, cblk + b, 0))
    ctx_v = pl.BlockSpec((1, L, MLA_V_PAD), lambda b, h, qi, ki: (h, cblk + b, 0))
    if with_latent:
        nq, nk = T // tq, T // tk
        in_specs = [pl.BlockSpec((1, tq, MLA_QK_PAD), lambda b, h, qi, ki: (h, b * nq + qi, 0)), ctx_k, ctx_v,
                    pl.BlockSpec((1, tk, MLA_QK_PAD), lambda b, h, qi, ki: (h, b * nk + ki, 0)),
                    pl.BlockSpec((1, tk, MLA_V_PAD), lambda b, h, qi, ki: (h, b * nk + ki, 0))]
        args = (q, k, v, k, v)
        out_rows, out_spec = geom.BT, pl.BlockSpec((tq, MLA_V), lambda b, h, qi, ki: (b * nq + qi, h))
    else:
        tq, nq, nk = L, 1, 1
        in_specs = [pl.BlockSpec((1, tq, MLA_QK_PAD), lambda b, h, qi, ki: (h, cblk + b, 0)), ctx_k, ctx_v]
        args = (q, k, v)
        out_rows, out_spec = B * L, pl.BlockSpec((tq, MLA_V), lambda b, h, qi, ki: (b, h))
    return pl.pallas_call(
        functools.partial(_mla_flash_kernel, with_latent=with_latent, sub=min(MLA_SUB_KEYS, tk)),
        grid=(B, MLA_HEADS, nq, nk),
        in_specs=in_specs,
        out_specs=out_spec,
        out_shape=jax.ShapeDtypeStruct((out_rows, MLA_HEADS * MLA_V), BF16),
        scratch_shapes=[pltpu.VMEM((tq, 128), F32), pltpu.VMEM((tq, MLA_V_PAD), F32)],
        compiler_params=_cparams("parallel", "parallel", "parallel", "arbitrary"),
    )(*args)


RW_Z_R, RW_Z_K, RW_Z_V = 0, RW_WIDTH, 2 * RW_WIDTH
RW_Z_LORA = 3 * RW_WIDTH
RW_Z_COLS = 3 * RW_WIDTH + 5 * LORA_PAD


def _dot3(a, b):
    return _mmx(_pieces(a, 3), _pieces(b, 3), _dot)


def _head_sum(x, ones):
    ones = (ones.astype(BF16),)
    return jnp.concatenate(
        [_mmx(_pieces(x[:, c * 128:(c + 1) * 128], 3), ones, _dot) for c in range(x.shape[1] // 128)], axis=-1)


def _rw_prep_kernel(*refs, geom, tm, has_vres):
    if has_vres:
        (z_ref, zp_ref, zn_ref, mu_ref, w0_ref, w2_ref, a0_ref, a2_ref, g2_ref, kk_ref, ka_ref, rk_ref, ones_ref,
         vf_ref, v0_ref, v1_ref, v2_ref,
         r_o, v_o, kk_o, lwf_o, kf_o, bf_o, lwb_o, kb_o, bb_o, g_o, bonus_o) = refs
    else:
        (z_ref, zp_ref, zn_ref, mu_ref, w0_ref, w2_ref, a0_ref, a2_ref, g2_ref, kk_ref, ka_ref, rk_ref, ones_ref,
         r_o, v_o, kk_o, lwf_o, kf_o, bf_o, lwb_o, kb_o, bb_o, g_o, bonus_o) = refs
    first, last = _seq_edge_masks(geom, pl.program_id(0) * tm, tm)
    z = z_ref[...]
    up, dn = _shifted_rows(z, zp_ref[...], zn_ref[...], first, last)
    z = z + mu_ref[...] * (0.5 * (up + dn) - z)
    r = z[:, RW_Z_R:RW_Z_R + RW_WIDTH]
    k = z[:, RW_Z_K:RW_Z_K + RW_WIDTH]
    v = z[:, RW_Z_V:RW_Z_V + RW_WIDTH]
    lora = lambda n: z[:, RW_Z_LORA + n * LORA_PAD:RW_Z_LORA + (n + 1) * LORA_PAD]
    ones = ones_ref[...]
    if has_vres:
        mix = jax.nn.sigmoid(v0_ref[...] + _dot3(_dot3(v, v1_ref[...]), v2_ref[...]))
        v = v + (vf_ref[...] - v) * mix
    g_o[...] = _dot3(jax.nn.sigmoid(lora(4)), g2_ref[...])
    kk = k * kk_ref[...]
    kk = kk / jnp.maximum(jnp.sqrt(_head_sum(kk * kk, ones)), 1e-12)
    ksum = None
    for d, (lw_o, k_o, b_o) in enumerate(((lwf_o, kf_o, bf_o), (lwb_o, kb_o, bb_o))):
        x = -(w0_ref[d:d + 1, :] + _dot3(jnp.tanh(lora(d)), w2_ref[d]))
        softplus = jnp.maximum(x, 0.0) + jnp.log1p(jnp.exp(-jnp.abs(x)))
        lw_o[...] = -jnp.exp(-softplus - 0.5)
        a = jax.nn.sigmoid(a0_ref[d:d + 1, :] + _dot3(lora(2 + d), a2_ref[d]))
        kd = k * (1.0 + (a - 1.0) * ka_ref[...])
        k_o[...] = kd
        b_o[...] = kk * a
        ksum = kd if ksum is None else ksum + kd
    r_o[...] = r
    v_o[...] = v
    kk_o[...] = kk
    bonus_o[...] = _head_sum(r * ksum * rk_ref[...], ones) * v


def rw_prep(geom, z, p, v_first, vres, *, tm):
    R = z.shape[0]
    has_vres = vres is not None
    full = lambda a: pl.BlockSpec(a.shape, lambda i: (0,) * a.ndim)
    row = pl.BlockSpec((tm, RW_WIDTH), lambda i: (i, 0))
    prev, nxt = _halo_specs(tm, R, RW_Z_COLS, lambda i: 0)
    params = [p["mu"], p["w0"], p["w2"], p["a0"], p["a2"], p["g2"], p["kk"], p["ka"], p["rk"], p["ones"]]
    in_specs = [pl.BlockSpec((tm, RW_Z_COLS), lambda i: (i, 0)), prev, nxt] + [full(a) for a in params]
    args = [z, z, z] + params
    if has_vres:
        in_specs += [row] + [full(a) for a in vres]
        args += [v_first] + list(vres)
    return pl.pallas_call(
        functools.partial(_rw_prep_kernel, geom=geom, tm=tm, has_vres=has_vres),
        grid=(R // tm,),
        in_specs=in_specs,
        out_specs=[row] * 11,
        out_shape=[jax.ShapeDtypeStruct((R, RW_WIDTH), F32)] * 11,
        compiler_params=_cparams("parallel"),
    )(*args)


def _pieces(x, passes):
    hi = x.astype(BF16)
    if passes == 1:
        return (hi,)
    return hi, (x - hi.astype(F32)).astype(BF16)


def _mmx(a, b, dot):
    out = dot(a[0], b[0])
    if len(a) > 1:
        out = out + dot(a[1], b[0])
    if len(b) > 1:
        out = out + dot(a[0], b[1])
    return out


def _stack_pair(first_head, x):
    return jnp.concatenate([jnp.where(first_head, x, 0.0), jnp.where(first_head, 0.0, x)], axis=0)


def _fold_pair(x):
    half = x.shape[0] // 2
    return x[:half] + x[half:]


def _rw_fused_scan_kernel(rf_ref, vf_ref, kkf_ref, lwf_ref, kf_ref, bf_ref,
                          rb_ref, vb_ref, kkb_ref, lwb_ref, kb_ref, bb_ref, yf_o, yb_o, s_ref, *, nchunk):
    C = RW_CHUNK
    PW = RW_PAIR
    ri = lax.broadcasted_iota(jnp.int32, (PW, PW), 0)
    ci = lax.broadcasted_iota(jnp.int32, (PW, PW), 1)
    eye = ri == ci
    ri, ci = ri & (C - 1), ci & (C - 1)
    first_head = lax.broadcasted_iota(jnp.int32, (1, PW), 1) < RW_HEAD_DIM
    stack = functools.partial(_stack_pair, first_head)

    @pl.when(pl.program_id(1) == 0)
    def _():
        s_ref[...] = jnp.zeros_like(s_ref)

    dirs = ((rf_ref, vf_ref, kkf_ref, lwf_ref, kf_ref, bf_ref, yf_o, ci < ri, C - 1),
            (rb_ref, vb_ref, kkb_ref, lwb_ref, kb_ref, bb_ref, yb_o, ci > ri, 0))

    def chunk_step(step, carry):
        jobs = []
        for d, (r_ref, v_ref, kk_ref, lw_ref, k_ref, b_ref, y_o, before, tot_row) in enumerate(dirs):
            chunk = step if d == 0 else nchunk - 1 - step
            rows = pl.ds(pl.multiple_of(chunk * C, C), C)
            r, v, kk = r_ref[rows, :], v_ref[rows, :], kk_ref[rows, :]
            lw, kd, bd = lw_ref[rows, :], k_ref[rows, :], b_ref[rows, :]
            incl = (before | eye)[:C, :C].astype(BF16)
            cum, rest = None, lw
            for _ in range(3):
                piece = rest.astype(BF16)
                rest = rest - piece.astype(F32)
                part = _dot(incl, piece)
                cum = part if cum is None else cum + part
            c_tot = cum[tot_row:tot_row + 1, :]
            at = -kk * jnp.exp(cum - lw)
            rt = r * jnp.exp(cum)
            e_neg = jnp.exp(-cum)
            bt, kt = bd * e_neg, kd * e_neg
            e_rest = jnp.exp(c_tot - cum)
            bc, kc = bd * e_rest, kd * e_rest
            e_tot = jnp.exp(c_tot)
            for p in range(RW_WIDTH // PW):
                ps = slice(p * PW, (p + 1) * PW)
                jobs.append(dict(d=d, ps=ps, rows=rows, y_o=y_o, before=before, incl=before | eye, e=e_tot[:, ps],
                                 at=stack(at[:, ps]), rt=stack(rt[:, ps]), bt=stack(bt[:, ps]), kt=stack(kt[:, ps]),
                                 bc=stack(bc[:, ps]), kc=stack(kc[:, ps]), v=stack(v[:, ps])))
        for j in jobs:
            p = _mmx(_pieces(jnp.concatenate([j["at"], j["rt"]], axis=0), RW_PASSES_LOCAL),
                     _pieces(jnp.concatenate([j["bt"], j["kt"]], axis=0), RW_PASSES_LOCAL), _dot_nt)
            j["a_ab"] = jnp.where(j["before"], p[:PW, :PW], 0.0)
            j["a_ak"] = jnp.where(j["before"], p[:PW, PW:], 0.0)
            j["a_rb"] = jnp.where(j["incl"], p[PW:, :PW], 0.0)
            j["a_rk"] = jnp.where(j["incl"], p[PW:, PW:], 0.0)
            j["vp"] = _pieces(j["v"], RW_PASSES_LOCAL)
        for j in jobs:
            j["w1"] = _mmx(_pieces(j["a_ak"], RW_PASSES_LOCAL), j["vp"], _dot)
            j["tinv"] = jnp.where(eye, 1.0, j["a_ab"])
            j["pw"] = j["a_ab"]
        for _ in range(int(np.log2(C)) - 1):
            for j in jobs:
                pw = _pieces(j["pw"], RW_PASSES_LOCAL)
                j["pw"] = _mmx(pw, pw, _dot)
            for j in jobs:
                j["tinv"] = j["tinv"] + _mmx(_pieces(j["tinv"], RW_PASSES_LOCAL), _pieces(j["pw"], RW_PASSES_LOCAL),
                                             _dot)
        for j in jobs:
            tw = _mmx(_pieces(j["tinv"], RW_PASSES_LOCAL),
                      _pieces(jnp.concatenate([j["at"], j["w1"]], axis=1), RW_PASSES_LOCAL), _dot)
            j["tw"] = tw
            j["twp"] = _pieces(tw, RW_PASSES_LOCAL)
        for j in jobs:
            ry = _mmx(_pieces(j["a_rb"], RW_PASSES_LOCAL), j["twp"], _dot)
            yk = _mmx(_pieces(j["a_rk"], RW_PASSES_LOCAL), j["vp"], _dot)
            j["rr"] = j["rt"] + ry[:, :PW]
            j["yl"] = _fold_pair(ry[:, PW:] + yk)
            j["m"] = _mmx(_pieces(j["bc"], RW_PASSES_TRANSITION), _pieces(j["tw"][:, :PW], RW_PASSES_TRANSITION),
                          _dot_tn)
            uv = jnp.concatenate([j["tw"][:, PW:], j["v"]], axis=0)
            bk = jnp.concatenate([j["bc"], j["kc"]], axis=0)
            j["n"] = _fold_pair(_mmx(_pieces(uv, RW_PASSES_STATE), _pieces(bk, RW_PASSES_STATE), _dot_tn))
        for j in jobs:
            s = s_ref[j["d"], :, j["ps"]]
            sp = _pieces(stack(s), RW_PASSES_STATE)
            y = _mmx(_pieces(j["rr"], RW_PASSES_STATE), sp, _dot_nt)
            j["y_o"][j["rows"], j["ps"]] = _fold_pair(y) + j["yl"]
            sm = _mmx(sp[:RW_PASSES_TRANSITION], _pieces(j["m"], RW_PASSES_TRANSITION), _dot_nt)
            s_ref[j["d"], :, j["ps"]] = s * j["e"] + _fold_pair(sm) + j["n"]
        return carry

    lax.fori_loop(0, nchunk, chunk_step, 0)


def rw_fused_scan(geom, r, v, kk, lwf, kf, bf, lwb, kb, bb):
    B, T, L = geom.B, geom.T, geom.L
    blk = L
    nlat = T // blk
    cblk = geom.BT // blk
    fwd = pl.BlockSpec((blk, RW_WIDTH), lambda b, s: (jnp.where(s == 0, cblk + b, b * nlat + s - 1), 0))
    bwd = pl.BlockSpec((blk, RW_WIDTH), lambda b, s: (jnp.where(s == 0, cblk + b, b * nlat + nlat - s), 0))
    return pl.pallas_call(
        functools.partial(_rw_fused_scan_kernel, nchunk=blk // RW_CHUNK),
        grid=(B, nlat + 1),
        in_specs=[fwd] * 6 + [bwd] * 6,
        out_specs=[fwd, bwd],
        out_shape=[jax.ShapeDtypeStruct((geom.R, RW_WIDTH), F32)] * 2,
        scratch_shapes=[pltpu.VMEM((2, RW_HEAD_DIM, RW_WIDTH), F32)],
        compiler_params=_cparams("parallel", "arbitrary"),
    )(r, v, kk, lwf, kf, bf, r, v, kk, lwb, kb, bb)


def _rw_post_kernel(yf_ref, yb_ref, bonus_ref, g_ref, lng_ref, lnb_ref, ones_ref, o_ref):
    ones = ones_ref[...]
    y = yf_ref[...] + yb_ref[...]
    mean = _head_sum(y, ones) * (1.0 / RW_HEAD_DIM)
    yc = y - mean
    var = _head_sum(yc * yc, ones) * (1.0 / RW_HEAD_DIM)
    y = yc * lax.rsqrt(var + RW_LNX_EPS) * lng_ref[...] + lnb_ref[...]
    o_ref[...] = ((y + bonus_ref[...]) * g_ref[...]).astype(o_ref.dtype)


def rw_post(yf, yb, bonus, g, lnx_g, lnx_b, ones, *, tm):
    R = yf.shape[0]
    row = pl.BlockSpec((tm, RW_WIDTH), lambda i: (i, 0))
    full = lambda a: pl.BlockSpec(a.shape, lambda i: (0,) * a.ndim)
    return pl.pallas_call(
        _rw_post_kernel,
        grid=(R // tm,),
        in_specs=[row] * 4 + [full(lnx_g), full(lnx_b), full(ones)],
        out_specs=row,
        out_shape=jax.ShapeDtypeStruct((R, RW_WIDTH), BF16),
        compiler_params=_cparams("parallel"),
    )(yf, yb, bonus, g, lnx_g, lnx_b, ones)


def _rw_in_cols(w):
    parts = [w[..., :3 * RW_WIDTH]]
    off = 3 * RW_WIDTH
    for n in (RW_DECAY_LORA, RW_DECAY_LORA, RW_AAA_LORA, RW_AAA_LORA, RW_GATE_LORA):
        parts.append(_pad_cols(w[..., off:off + n], LORA_PAD))
        off += n
    return jnp.concatenate(parts, axis=-1)


def _pad_rows(w, n):
    return jnp.pad(w, [(0, 0)] * (w.ndim - 2) + [(0, n - w.shape[-2]), (0, 0)])


def _mla_wq_cols(w):
    w = w.reshape(w.shape[0], MLA_HEADS, MLA_NOPE + MLA_ROPE)
    return _pad_cols(w, MLA_QK_PAD).reshape(w.shape[0], MLA_HEADS * MLA_QK_PAD)


def kernel(x, c, ctx, c_ctx, ada_w, ada_b, norm1_g, w_in, rw_mu, rw_w0, rw_w2, rw_a0, rw_a2, rw_g2, rw_kk, rw_ka,
           rw_rk, rw_lnx_g, rw_lnx_b, rw_v0, rw_v1, rw_v2, wa_sink, mla_qnorm_g, mla_kvnorm_g, mla_w_uq, mla_w_ukv,
           w_branch, w_out, norm2_g, ffn_w_in, ffn_conv_w, ffn_conv_b, ffn_w_out, final_norm_g):
    B, T, D = x.shape
    L = ctx.shape[1]
    depth = w_in.shape[0]
    F = ffn_w_out.shape[1]
    geom = Geom(B, T, L)
    tm = _pick_tile(T, (512, 256, 128))
    assert (B * L) % tm == 0
    tmm = _pick_tile(geom.R, (1280, 1024, 640, 512, 256, 128))

    rw_cols = 3 * RW_WIDTH + 2 * RW_DECAY_LORA + 2 * RW_AAA_LORA + RW_GATE_LORA
    wa_cols = WA_WIDTH + 2 * WA_KV_WIDTH
    mla_cols = MLA_Q_LORA + MLA_KV_LORA + MLA_ROPE
    mla_cols_pad = MLA_Q_LORA + MLA_KV_LORA + 128

    cos_wa, sin_wa = _rope_tables(geom, WA_HEAD_DIM)
    cos_wa, sin_wa = jnp.tile(cos_wa, (1, 2)), jnp.tile(sin_wa, (1, 2))
    cos_m, sin_m = _rope_tables(geom, MLA_ROPE)
    one, zero = jnp.ones((geom.R, MLA_NOPE), F32), jnp.zeros((geom.R, MLA_NOPE), F32)
    cos_mla = jnp.concatenate([one, cos_m, one[:, :64]], axis=-1)
    sin_mla = jnp.concatenate([zero, sin_m, zero[:, :64]], axis=-1)
    lane = np.arange(128)
    ones_blk = jnp.asarray((lane[:, None] // RW_HEAD_DIM) == (lane[None, :] // RW_HEAD_DIM), F32)

    xs = jnp.concatenate([x.reshape(B * T, D), ctx.reshape(B * L, D)], axis=0)
    cvec = jnp.concatenate([c, c_ctx[None, :], jnp.zeros((8 - (B + 1) % 8, D), F32)], axis=0)
    v_first = None
    for l in range(depth):
        need_ctx = l < depth - 1
        mod = ada_modulation(cvec, ada_w[l], ada_b[l])
        mod = [mod[:, k * D:(k + 1) * D].reshape(-1, 1, D) for k in range(6)]

        w = w_in[l]
        tn_in = _pick_tile(3 * D, (768, 512, 256, 128))
        w_rw = _pad_cols(_rw_in_cols(w[:, :rw_cols]), _round_up(RW_Z_COLS, tn_in)).astype(BF16)
        w_wa = _pad_cols(w[:, rw_cols:rw_cols + wa_cols], _round_up(wa_cols, tn_in)).astype(BF16)
        w_mla = _pad_cols(w[:, rw_cols + wa_cols:rw_cols + wa_cols + mla_cols],
                          _round_up(mla_cols_pad, tn_in)).astype(BF16)
        w_gate = w[:, rw_cols + wa_cols + mla_cols:].astype(BF16)
        z_rw, z_wa, z_mla, gates = in_projection(geom, xs, norm1_g[l], mod[0], mod[1], w_rw, w_wa, w_mla, w_gate,
                                                 tm=tmm, tn=tn_in)

        rw_p = dict(
            mu=_rw_in_cols(rw_mu[l][None, :]), w0=rw_w0[l], w2=_pad_rows(rw_w2[l], LORA_PAD), a0=rw_a0[l],
            a2=_pad_rows(rw_a2[l], LORA_PAD), g2=_pad_rows(rw_g2[l], LORA_PAD), kk=rw_kk[l][None, :],
            ka=rw_ka[l][None, :], rk=rw_rk[l].reshape(1, RW_WIDTH), ones=ones_blk)
        vres = None if l == 0 else (rw_v0[l - 1][None, :], rw_v1[l - 1], rw_v2[l - 1])
        r, v, kk, lwf, kf, bf, lwb, kb, bb, g, bonus = rw_prep(geom, z_rw, rw_p, v_first, vres, tm=min(tm, 256))
        if l == 0:
            v_first = v
        yf, yb = rw_fused_scan(geom, r, v, kk, lwf, kf, bf, lwb, kb, bb)
        o_a = rw_post(yf, yb, bonus, g, rw_lnx_g[l][None, :], rw_lnx_b[l][None, :], ones_blk, tm=tm)

        q_wa, k_wa, v_wa = wa_prep(z_wa, cos_wa, sin_wa, tm=tm)
        ob_l = wa_attention(geom, q_wa, k_wa, v_wa, wa_sink[l], local=True)
        parts = [ob_l]
        if need_ctx:
            parts.append(wa_attention(geom, q_wa, k_wa, v_wa, wa_sink[l], local=False))
        else:
            parts.append(jnp.zeros((B * L, WA_WIDTH), BF16))
        o_b = jnp.concatenate(parts, axis=0)

        wq = _mla_wq_cols(mla_w_uq[l]).astype(BF16)
        q_m, k_m, v_m = mla_prep(z_mla, mla_qnorm_g[l], mla_kvnorm_g[l], wq, mla_w_ukv[l].astype(BF16),
                                 cos_mla, sin_mla, tm=tm)
        tq = _pick_tile(T, (1024, 512, 256, 128))
        tk = _pick_tile(T, (4096, 2048, 1024, 512, 256, 128))
        parts = [mla_attention(geom, q_m, k_m, v_m, with_latent=True, tq=tq, tk=tk)]
        if need_ctx:
            parts.append(mla_attention(geom, q_m, k_m, v_m, with_latent=False, tq=L, tk=L))
        else:
            parts.append(jnp.zeros((B * L, MLA_HEADS * MLA_V), BF16))
        o_c = jnp.concatenate(parts, axis=0)

        tn_d = _pick_tile(D, (1024, 512, 256, 128))
        y = merge_branches(o_a, o_b, o_c, gates, w_branch[l].astype(BF16), tm=tmm, tn=tn_d)
        xs = matmul_gated_residual(geom, y, w_out[l].astype(BF16), xs, mod[2], tm=tmm, tn=tn_d)

        tf = _pick_tile(F, (512, 256, 128))
        hmid = ffn_in_conv_glu(geom, xs, norm2_g[l], mod[3], mod[4], ffn_w_in[l].astype(BF16), ffn_conv_w[l],
                               ffn_conv_b[l], tm=tmm, tf=tf)
        xs = matmul_gated_residual(geom, hmid, ffn_w_out[l].astype(BF16), xs, mod[5], tm=tmm,
                                   tn=_pick_tile(D, (512, 256, 128)))

    out = final_rmsnorm(xs, final_norm_g, B * T, tm=tm)
    return out.reshape(B, T, D)
```

```python
import functools

import jax
import jax.numpy as jnp
import numpy as np
from jax import lax
from jax.experimental import pallas as pl
from jax.experimental.pallas import tpu as pltpu

F32 = jnp.float32
BF16 = jnp.bfloat16
HIGHEST = lax.Precision.HIGHEST

NORM_EPS = 1e-6
NEG_INF = -1e30
GRID_W = 64
ROPE_BASE = 10000.0

RW_HEADS = 16
RW_HEAD_DIM = 64
RW_WIDTH = RW_HEADS * RW_HEAD_DIM
RW_DECAY_LORA = 96
RW_AAA_LORA = 96
RW_GATE_LORA = 64
RW_LNX_EPS = 64e-5
RW_CHUNK = 64
RW_PAIR = 2 * RW_HEAD_DIM
RW_PASSES_LOCAL = 1
RW_PASSES_STATE = 3
RW_PASSES_TRANSITION = 1
LORA_PAD = 128

WA_HEADS = 16
WA_KV_HEADS = 4
WA_GROUP = WA_HEADS // WA_KV_HEADS
WA_HEAD_DIM = 64
WA_WIDTH = WA_HEADS * WA_HEAD_DIM
WA_KV_WIDTH = WA_KV_HEADS * WA_HEAD_DIM
WA_V_PAD = WA_KV_HEADS * 128
WINDOW = 128
WA_SCALE = WA_HEAD_DIM ** -0.5

MLA_HEADS = 8
MLA_NOPE = 128
MLA_ROPE = 64
MLA_V = 128
MLA_Q_LORA = 512
MLA_KV_LORA = 512
MLA_QK_PAD = 256
MLA_V_PAD = 256
MLA_SCALE = (MLA_NOPE + MLA_ROPE) ** -0.5
MLA_SUB_KEYS = 2048
LOG2_E = 1.4426950408889634

CONV_W = 3
VMEM_LIMIT_BYTES = 56 * 1024 * 1024


def _cparams(*sem):
    return pltpu.CompilerParams(dimension_semantics=sem, vmem_limit_bytes=VMEM_LIMIT_BYTES)


def _dot(a, b, precision=None):
    return jnp.dot(a, b, preferred_element_type=F32, precision=precision)


def _dot_nt(a, b, precision=None):
    return lax.dot_general(a, b, (((1,), (1,)), ((), ())), preferred_element_type=F32, precision=precision)


def _dot_tn(a, b, precision=None):
    return lax.dot_general(a, b, (((0,), (0,)), ((), ())), preferred_element_type=F32, precision=precision)


def _pick_tile(n, candidates):
    for c in candidates:
        if n % c == 0:
            return c
    raise ValueError(f"no tile in {candidates} divides {n}")


def _pad_cols(w, n):
    return jnp.pad(w, [(0, 0)] * (w.ndim - 1) + [(0, n - w.shape[-1])])


def _round_up(n, m):
    return (n + m - 1) // m * m


class Geom:
    def __init__(self, B, T, L):
        assert T & (T - 1) == 0 and L & (L - 1) == 0, "sequence lengths must be powers of two"
        assert T % L == 0 and L % RW_CHUNK == 0 and T % GRID_W == 0
        self.B, self.T, self.L = B, T, L
        self.BT = B * T
        self.R = B * T + B * L


def _select_row_group(geom, row0, tm, tab_ref):
    r = row0 + lax.broadcasted_iota(jnp.int32, (tm, 1), 0)
    out = tab_ref[geom.B]
    for b in range(geom.B):
        out = jnp.where((r >= b * geom.T) & (r < (b + 1) * geom.T), tab_ref[b], out)
    return out


def _seq_edge_masks(geom, row0, tm):
    r = row0 + lax.broadcasted_iota(jnp.int32, (tm, 1), 0)
    is_lat = r < geom.BT
    pos = jnp.where(is_lat, r & (geom.T - 1), (r - geom.BT) & (geom.L - 1))
    last = jnp.where(is_lat, geom.T - 1, geom.L - 1)
    return pos == 0, pos == last


def _shifted_rows(x, prev8, next8, first, last):
    tm = x.shape[0]
    rid = lax.broadcasted_iota(jnp.int32, (tm, 1), 0)
    up = jnp.where(rid == 0, prev8[7:8, :], pltpu.roll(x, 1, axis=0))
    dn = jnp.where(rid == tm - 1, next8[0:1, :], pltpu.roll(x, tm - 1, axis=0))
    return jnp.where(first, 0.0, up), jnp.where(last, 0.0, dn)


def _halo_specs(tm, R, width, col_of):
    nb8 = tm // 8
    prev = pl.BlockSpec((8, width), lambda i, *a: (jnp.maximum(i * nb8 - 1, 0), col_of(i, *a)))
    nxt = pl.BlockSpec((8, width), lambda i, *a: (jnp.minimum((i + 1) * nb8, R // 8 - 1), col_of(i, *a)))
    return prev, nxt


def _ada_kernel(c_ref, w_ref, b_ref, o_ref):
    c = c_ref[...]
    o_ref[...] = _dot(c * jax.nn.sigmoid(c), w_ref[...], HIGHEST) + b_ref[...]


def ada_modulation(cvec, w, b):
    G, D = cvec.shape
    N = w.shape[1]
    tn = _pick_tile(N, (1024, 512, 256, 128))
    return pl.pallas_call(
        _ada_kernel,
        grid=(N // tn,),
        in_specs=[pl.BlockSpec((G, D), lambda j: (0, 0)),
                  pl.BlockSpec((D, tn), lambda j: (0, j)),
                  pl.BlockSpec((1, tn), lambda j: (0, j))],
        out_specs=pl.BlockSpec((G, tn), lambda j: (0, j)),
        out_shape=jax.ShapeDtypeStruct((G, N), F32),
        compiler_params=_cparams("arbitrary"),
    )(cvec, w, b.reshape(1, N))


def _norm_modulate(x, g, sc, sh):
    gain = g * (1.0 + sc)
    return (x * lax.rsqrt(jnp.mean(x * x, axis=-1, keepdims=True) + NORM_EPS) * gain + sh).astype(BF16)


PROLOGUE_ROWS = 256


def _norm_modulate_rows(geom, row0, x_ref, g_ref, sc_ref, sh_ref, h_ref, h_off):
    tm = x_ref.shape[0]
    step = PROLOGUE_ROWS if tm % PROLOGUE_ROWS == 0 else tm
    one_group = geom.T % step == 0 and geom.BT % step == 0
    g = g_ref[...]
    for r in range(0, tm, step):
        if one_group:
            sc, sh = sc_ref[geom.B], sh_ref[geom.B]
            for b in range(geom.B):
                in_b = (row0 + r >= b * geom.T) & (row0 + r < (b + 1) * geom.T)
                sc, sh = jnp.where(in_b, sc_ref[b], sc), jnp.where(in_b, sh_ref[b], sh)
        else:
            sc = _select_row_group(geom, row0 + r, step, sc_ref)
            sh = _select_row_group(geom, row0 + r, step, sh_ref)
        h_ref[h_off + r:h_off + r + step] = _norm_modulate(x_ref[r:r + step], g, sc, sh)


def _in_proj_kernel(x_ref, g_ref, sh_ref, sc_ref, w_ref, zrw_ref, zwa_ref, zmla_ref, gate_ref, h_ref,
                    *, geom, tm, starts):
    j = pl.program_id(1)

    @pl.when(j == 0)
    def _():
        _norm_modulate_rows(geom, pl.program_id(0) * tm, x_ref, g_ref, sc_ref, sh_ref, h_ref, 0)

    wa0, mla0, gate0 = starts

    @pl.when(j < wa0)
    def _():
        zrw_ref[...] = _dot(h_ref[...], w_ref[...])

    @pl.when((j >= wa0) & (j < mla0))
    def _():
        zwa_ref[...] = _dot(h_ref[...], w_ref[...])

    @pl.when((j >= mla0) & (j < gate0))
    def _():
        zmla_ref[...] = _dot(h_ref[...], w_ref[...])

    @pl.when(j >= gate0)
    def _():
        gate_ref[...] = jax.nn.sigmoid(_dot(h_ref[...], w_ref[...])).astype(gate_ref.dtype)


def in_projection(geom, x, g, shift, scale, w_rw, w_wa, w_mla, w_gate, *, tm, tn):
    R, K = x.shape
    widths = [w.shape[1] for w in (w_rw, w_wa, w_mla, w_gate)]
    assert R % tm == 0 and all(n % tn == 0 for n in widths)
    tiles = [n // tn for n in widths]
    starts = tuple(int(v) for v in np.cumsum(tiles)[:3])
    w_all = jnp.concatenate([w_rw, w_wa, w_mla, w_gate], axis=1)
    full = lambda a: pl.BlockSpec(a.shape, lambda i, j: (0,) * a.ndim)
    out_spec = lambda first, n: pl.BlockSpec((tm, tn), lambda i, j: (i, jnp.clip(j - first, 0, n - 1)))
    firsts = (0,) + starts
    return pl.pallas_call(
        functools.partial(_in_proj_kernel, geom=geom, tm=tm, starts=starts),
        grid=(R // tm, sum(tiles)),
        in_specs=[pl.BlockSpec((tm, K), lambda i, j: (i, 0), pipeline_mode=pl.Buffered(1)),
                  pl.BlockSpec((1, K), lambda i, j: (0, 0)),
                  full(shift), full(scale),
                  pl.BlockSpec((K, tn), lambda i, j: (0, j))],
        out_specs=[out_spec(f, n) for f, n in zip(firsts, tiles)],
        out_shape=[jax.ShapeDtypeStruct((R, widths[0]), F32), jax.ShapeDtypeStruct((R, widths[1]), F32),
                   jax.ShapeDtypeStruct((R, widths[2]), F32), jax.ShapeDtypeStruct((R, widths[3]), BF16)],
        scratch_shapes=[pltpu.VMEM((tm, K), BF16)],
        compiler_params=_cparams("parallel", "arbitrary"),
    )(x, g.reshape(1, K), shift, scale, w_all)


def _mm_resid_kernel(y_ref, w_ref, r_ref, gate_ref, o_ref, *, geom, tm):
    gate = _select_row_group(geom, pl.program_id(0) * tm, tm, gate_ref)
    o_ref[...] = r_ref[...] + gate * _dot(y_ref[...], w_ref[...])


def matmul_gated_residual(geom, y, w, resid, gate, *, tm, tn):
    R, K = y.shape
    N = w.shape[1]
    G = gate.shape[0]
    assert R % tm == 0 and N % tn == 0
    return pl.pallas_call(
        functools.partial(_mm_resid_kernel, geom=geom, tm=tm),
        grid=(R // tm, N // tn),
        in_specs=[pl.BlockSpec((tm, K), lambda i, j: (i, 0)),
                  pl.BlockSpec((K, tn), lambda i, j: (0, j)),
                  pl.BlockSpec((tm, tn), lambda i, j: (i, j)),
                  pl.BlockSpec((G, 1, tn), lambda i, j: (0, 0, j))],
        out_specs=pl.BlockSpec((tm, tn), lambda i, j: (i, j)),
        out_shape=jax.ShapeDtypeStruct((R, N), F32),
        compiler_params=_cparams("parallel", "arbitrary"),
    )(y, w, resid, gate)


FFN_HALO = 16
FFN_COL_BLOCK = 256


def _ffn_in_kernel(x_ref, xp_ref, xn_ref, g_ref, sh_ref, sc_ref, wg_ref, wu_ref, cw_ref, cb_ref, o_ref, h_ref,
                   *, geom, tm):
    row0 = pl.program_id(0) * tm
    H = FFN_HALO

    @pl.when(pl.program_id(1) == 0)
    def _():
        _norm_modulate_rows(geom, row0, xp_ref, g_ref, sc_ref, sh_ref, h_ref, 0)
        _norm_modulate_rows(geom, row0, x_ref, g_ref, sc_ref, sh_ref, h_ref, H)
        _norm_modulate_rows(geom, row0 + tm - H, xn_ref, g_ref, sc_ref, sh_ref, h_ref, H + tm)

    first, last = _seq_edge_masks(geom, row0, tm)
    tf = o_ref.shape[1]
    width = FFN_COL_BLOCK if tf % FFN_COL_BLOCK == 0 else tf
    cols = [slice(c, c + width) for c in range(0, tf, width)]
    prods = [(_dot(h_ref[...], wg_ref[:, cs]), _dot(h_ref[H:H + tm], wu_ref[:, cs])) for cs in cols]
    for cs, (gt, u) in zip(cols, prods):
        up = jnp.where(first, 0.0, pltpu.roll(gt, 1, axis=0)[H:H + tm])
        dn = jnp.where(last, 0.0, pltpu.roll(gt, tm + 2 * H - 1, axis=0)[H:H + tm])
        conv = cb_ref[:, cs] + up * cw_ref[0:1, cs]
        conv = conv + gt[H:H + tm] * cw_ref[1:2, cs]
        conv = conv + dn * cw_ref[2:3, cs]
        o_ref[:, cs] = (jax.nn.gelu(conv, approximate=True) * u).astype(o_ref.dtype)


def ffn_in_conv_glu(geom, x, g, shift, scale, w_in, conv_w, conv_b, *, tm, tf):
    R, K = x.shape
    F = w_in.shape[1] // 2
    nj = F // tf
    nbh = tm // FFN_HALO
    assert R % tm == 0 and F % tf == 0 and tm % FFN_HALO == 0
    full = lambda a: pl.BlockSpec(a.shape, lambda i, j: (0,) * a.ndim)
    return pl.pallas_call(
        functools.partial(_ffn_in_kernel, geom=geom, tm=tm),
        grid=(R // tm, nj),
        in_specs=[pl.BlockSpec((tm, K), lambda i, j: (i, 0), pipeline_mode=pl.Buffered(1)),
                  pl.BlockSpec((FFN_HALO, K), lambda i, j: (jnp.maximum(i * nbh - 1, 0), 0)),
                  pl.BlockSpec((FFN_HALO, K), lambda i, j: (jnp.minimum((i + 1) * nbh, R // FFN_HALO - 1), 0)),
                  pl.BlockSpec((1, K), lambda i, j: (0, 0)),
                  full(shift), full(scale),
                  pl.BlockSpec((K, tf), lambda i, j: (0, j)),
                  pl.BlockSpec((K, tf), lambda i, j: (0, nj + j)),
                  pl.BlockSpec((CONV_W, tf), lambda i, j: (0, j)),
                  pl.BlockSpec((1, tf), lambda i, j: (0, j))],
        out_specs=pl.BlockSpec((tm, tf), lambda i, j: (i, j)),
        out_shape=jax.ShapeDtypeStruct((R, F), BF16),
        scratch_shapes=[pltpu.VMEM((tm + 2 * FFN_HALO, K), BF16)],
        compiler_params=_cparams("parallel", "arbitrary"),
    )(x, x, x, g.reshape(1, K), shift, scale, w_in, w_in, conv_w, conv_b.reshape(1, F))


def _merge_kernel(oa_ref, ob_ref, oc_ref, ga_ref, gb_ref, gc_ref, w_ref, o_ref):
    y = ga_ref[...] * _dot(oa_ref[...], w_ref[0])
    y = y + gb_ref[...] * _dot(ob_ref[...], w_ref[1])
    y = y + gc_ref[...] * _dot(oc_ref[...], w_ref[2])
    o_ref[...] = y.astype(o_ref.dtype)


def merge_branches(oa, ob, oc, gates, wb, *, tm, tn):
    R, K = oa.shape
    D = wb.shape[2]
    nj = D // tn
    bspec = pl.BlockSpec((tm, K), lambda i, j: (i, 0))
    gspec = lambda k: pl.BlockSpec((tm, tn), lambda i, j: (i, k * nj + j))
    return pl.pallas_call(
        _merge_kernel,
        grid=(R // tm, nj),
        in_specs=[bspec, bspec, bspec, gspec(0), gspec(1), gspec(2),
                  pl.BlockSpec((3, K, tn), lambda i, j: (0, 0, j))],
        out_specs=pl.BlockSpec((tm, tn), lambda i, j: (i, j)),
        out_shape=jax.ShapeDtypeStruct((R, D), BF16),
        compiler_params=_cparams("parallel", "arbitrary"),
    )(oa, ob, oc, gates, gates, gates, wb)


def _rmsnorm_kernel(x_ref, g_ref, o_ref):
    x = x_ref[...]
    o_ref[...] = x * lax.rsqrt(jnp.mean(x * x, axis=-1, keepdims=True) + NORM_EPS) * g_ref[...]


def final_rmsnorm(x, g, rows, *, tm):
    D = x.shape[1]
    return pl.pallas_call(
        _rmsnorm_kernel,
        grid=(rows // tm,),
        in_specs=[pl.BlockSpec((tm, D), lambda i: (i, 0)), pl.BlockSpec((1, D), lambda i: (0, 0))],
        out_specs=pl.BlockSpec((tm, D), lambda i: (i, 0)),
        out_shape=jax.ShapeDtypeStruct((rows, D), F32),
        compiler_params=_cparams("parallel"),
    )(x, g.reshape(1, D))


def _rot_half64(z):
    n = z.shape[-1]
    lane = lax.broadcasted_iota(jnp.int32, z.shape, z.ndim - 1)
    return jnp.where((lane & 63) < 32, pltpu.roll(z, n - 32, axis=z.ndim - 1), pltpu.roll(z, 32, axis=z.ndim - 1))


def _rope_tables(geom, dim):
    nf = dim // 4
    inv = ROPE_BASE ** (-jnp.arange(nf, dtype=F32) / nf)
    rows = geom.T // GRID_W
    row = jnp.repeat(jnp.arange(rows, dtype=F32), GRID_W)
    col = jnp.tile(jnp.arange(GRID_W, dtype=F32), rows)
    ang = jnp.concatenate([row[:, None] * inv, col[:, None] * inv], axis=-1)
    cos, sin = jnp.cos(ang), jnp.sin(ang)
    cos_t = jnp.concatenate([cos, cos], axis=-1)
    sin_t = jnp.concatenate([-sin, sin], axis=-1)
    nctx = geom.B * geom.L
    cos_f = jnp.concatenate([jnp.tile(cos_t, (geom.B, 1)), jnp.ones((nctx, dim), F32)], axis=0)
    sin_f = jnp.concatenate([jnp.tile(sin_t, (geom.B, 1)), jnp.zeros((nctx, dim), F32)], axis=0)
    return cos_f, sin_f


def _wa_prep_kernel(z_ref, cos_ref, sin_ref, q_ref, k_ref, v_ref):
    cos = cos_ref[...]
    sin = sin_ref[...]
    for c in range(WA_WIDTH // 128):
        z = z_ref[:, c * 128:(c + 1) * 128]
        q_ref[:, c * 128:(c + 1) * 128] = ((z * cos + _rot_half64(z) * sin) * (WA_SCALE * LOG2_E)).astype(BF16)
    for c in range(WA_KV_WIDTH // 128):
        z = z_ref[:, WA_WIDTH + c * 128:WA_WIDTH + (c + 1) * 128]
        k_ref[:, c * 128:(c + 1) * 128] = (z * cos + _rot_half64(z) * sin).astype(BF16)
    v_ref[...] = jnp.ones(v_ref.shape, BF16)
    for g in range(WA_KV_HEADS):
        src = WA_WIDTH + WA_KV_WIDTH + g * WA_HEAD_DIM
        v_ref[:, g * 128:g * 128 + WA_HEAD_DIM] = z_ref[:, src:src + WA_HEAD_DIM].astype(BF16)


def wa_prep(z, cos, sin, *, tm):
    R = z.shape[0]
    row = lambda w: pl.BlockSpec((tm, w), lambda i: (i, 0))
    return pl.pallas_call(
        _wa_prep_kernel,
        grid=(R // tm,),
        in_specs=[row(WA_WIDTH + 2 * WA_KV_WIDTH), row(128), row(128)],
        out_specs=[row(WA_WIDTH), row(WA_KV_WIDTH), row(WA_V_PAD)],
        out_shape=[jax.ShapeDtypeStruct((R, WA_WIDTH), BF16),
                   jax.ShapeDtypeStruct((R, WA_KV_WIDTH), BF16),
                   jax.ShapeDtypeStruct((R, WA_V_PAD), BF16)],
        compiler_params=_cparams("parallel"),
    )(z, cos, sin)


def _wa_attn_kernel(*refs, local, nqb, tq):
    if local:
        sink_ref, q_ref, kp_ref, kc_ref, kn_ref, vp_ref, vc_ref, vn_ref, kx_ref, vx_ref, o_ref = refs
    else:
        sink_ref, q_ref, kx_ref, vx_ref, o_ref = refs
    i = pl.program_id(1)
    nk_ctx = kx_ref.shape[0]
    rows = WA_GROUP * tq
    qpos = lax.broadcasted_iota(jnp.int32, (rows, 1), 0) & (tq - 1)
    head_in_group = lax.broadcasted_iota(jnp.int32, (rows, 1), 0) >> int(np.log2(tq))
    if local:
        off_prev = jnp.where(i > 0, 0, tq)
        off_next = jnp.where(i < nqb - 1, 0, tq)
        j = lax.broadcasted_iota(jnp.int32, (1, 3 * tq + nk_ctx), 1)
        valid = ((j >= tq) & (j < 2 * tq)) | (j >= 3 * tq)
        valid = valid | ((j < tq) & (j >= qpos + off_prev))
        valid = valid | ((j >= 2 * tq) & (j < 3 * tq) & ((j - 2 * tq) <= qpos - off_next))
    def scores(g):
        ks = slice(g * WA_HEAD_DIM, (g + 1) * WA_HEAD_DIM)
        qg = jnp.concatenate(
            [q_ref[:, (g * WA_GROUP + a) * WA_HEAD_DIM:(g * WA_GROUP + a + 1) * WA_HEAD_DIM] for a in range(WA_GROUP)],
            axis=0)
        if local:
            kcat = jnp.concatenate([kp_ref[:, ks], kc_ref[:, ks], kn_ref[:, ks], kx_ref[:, ks]], axis=0)
        else:
            kcat = kx_ref[:, ks]
        return _dot_nt(qg, kcat)

    s_next = scores(0)
    for g in range(WA_KV_HEADS):
        s = s_next
        if g + 1 < WA_KV_HEADS:
            s_next = scores(g + 1)
        vs = slice(g * 128, (g + 1) * 128)
        sink = jnp.zeros((rows, 1), F32)
        for a in range(WA_GROUP):
            sink = jnp.where(head_in_group == a, sink_ref[g * WA_GROUP + a] * LOG2_E, sink)
        if local:
            vcat = jnp.concatenate([vp_ref[:, vs], vc_ref[:, vs], vn_ref[:, vs], vx_ref[:, vs]], axis=0)
            s = jnp.where(valid, s, NEG_INF)
        else:
            vcat = vx_ref[:, vs]
        m = jnp.maximum(jnp.max(s, axis=-1, keepdims=True), sink)
        o = _dot(jnp.exp2(s - m).astype(BF16), vcat)
        o = o[:, :WA_HEAD_DIM] / (o[:, WA_HEAD_DIM:] + jnp.exp2(sink - m))
        for a in range(WA_GROUP):
            h = g * WA_GROUP + a
            o_ref[:, h * WA_HEAD_DIM:(h + 1) * WA_HEAD_DIM] = o[a * tq:(a + 1) * tq].astype(o_ref.dtype)


def wa_attention(geom, q, k, v, sink, *, local):
    B, T, L = geom.B, geom.T, geom.L
    sink_spec = pl.BlockSpec(memory_space=pltpu.SMEM)
    ctx_spec = lambda w: pl.BlockSpec((L, w), lambda b, i: (geom.BT // L + b, 0))
    if local:
        tq = WINDOW
        nqb = T // tq
        kv = lambda f, w: pl.BlockSpec((tq, w), lambda b, i: (b * nqb + f(i), 0))
        prev = lambda i: jnp.maximum(i - 1, 0)
        cur = lambda i: i
        nxt = lambda i: jnp.minimum(i + 1, nqb - 1)
        in_specs = [sink_spec, pl.BlockSpec((tq, WA_WIDTH), lambda b, i: (b * nqb + i, 0)),
                    kv(prev, WA_KV_WIDTH), kv(cur, WA_KV_WIDTH), kv(nxt, WA_KV_WIDTH),
                    kv(prev, WA_V_PAD), kv(cur, WA_V_PAD), kv(nxt, WA_V_PAD),
                    ctx_spec(WA_KV_WIDTH), ctx_spec(WA_V_PAD)]
        args = (sink, q, k, k, k, v, v, v, k, v)
        out_rows, out_spec = geom.BT, pl.BlockSpec((tq, WA_WIDTH), lambda b, i: (b * nqb + i, 0))
    else:
        tq, nqb = L, 1
        in_specs = [sink_spec, pl.BlockSpec((tq, WA_WIDTH), lambda b, i: (geom.BT // L + b, 0)),
                    ctx_spec(WA_KV_WIDTH), ctx_spec(WA_V_PAD)]
        args = (sink, q, k, v)
        out_rows, out_spec = B * L, pl.BlockSpec((tq, WA_WIDTH), lambda b, i: (b, 0))
    return pl.pallas_call(
        functools.partial(_wa_attn_kernel, local=local, nqb=nqb, tq=tq),
        grid=(B, nqb),
        in_specs=in_specs,
        out_specs=out_spec,
        out_shape=jax.ShapeDtypeStruct((out_rows, WA_WIDTH), BF16),
        compiler_params=_cparams("parallel", "arbitrary"),
    )(*args)


def _mla_prep_kernel(z_ref, qg_ref, kvg_ref, wq_ref, wkv_ref, cos_ref, sin_ref, q_ref, k_ref, v_ref):
    def norm(x, g):
        return (x * lax.rsqrt(jnp.mean(x * x, axis=-1, keepdims=True) + NORM_EPS) * g).astype(BF16)

    cos = cos_ref[...]
    sin = sin_ref[...]
    q = _dot(norm(z_ref[:, :MLA_Q_LORA], qg_ref[...]), wq_ref[...])
    kv = _dot(norm(z_ref[:, MLA_Q_LORA:MLA_Q_LORA + MLA_KV_LORA], kvg_ref[...]), wkv_ref[...])
    kr = z_ref[:, MLA_Q_LORA + MLA_KV_LORA:MLA_Q_LORA + MLA_KV_LORA + 128]
    kr = (kr * cos[:, 128:] + _rot_half64(kr) * sin[:, 128:]).astype(BF16)
    for h in range(MLA_HEADS):
        qh = q[:, h * MLA_QK_PAD:(h + 1) * MLA_QK_PAD]
        q_ref[h] = ((qh * cos + _rot_half64(qh) * sin) * (MLA_SCALE * LOG2_E)).astype(BF16)
        k_ref[h, :, :MLA_NOPE] = kv[:, h * 256:h * 256 + MLA_NOPE].astype(BF16)
        k_ref[h, :, MLA_NOPE:] = kr
        v_ref[h, :, :MLA_V] = kv[:, h * 256 + MLA_NOPE:(h + 1) * 256].astype(BF16)
        v_ref[h, :, MLA_V:] = jnp.ones((kv.shape[0], MLA_V_PAD - MLA_V), BF16)


def mla_prep(z, qnorm_g, kvnorm_g, wq, wkv, cos, sin, *, tm):
    R, Z = z.shape
    full = lambda a: pl.BlockSpec(a.shape, lambda i: (0,) * a.ndim)
    qg, kvg = qnorm_g.reshape(1, -1), kvnorm_g.reshape(1, -1)
    hd = lambda w: pl.BlockSpec((MLA_HEADS, tm, w), lambda i: (0, i, 0))
    return pl.pallas_call(
        _mla_prep_kernel,
        grid=(R // tm,),
        in_specs=[pl.BlockSpec((tm, Z), lambda i: (i, 0)), full(qg), full(kvg), full(wq), full(wkv),
                  pl.BlockSpec((tm, MLA_QK_PAD), lambda i: (i, 0)), pl.BlockSpec((tm, MLA_QK_PAD), lambda i: (i, 0))],
        out_specs=[hd(MLA_QK_PAD), hd(MLA_QK_PAD), hd(MLA_V_PAD)],
        out_shape=[jax.ShapeDtypeStruct((MLA_HEADS, R, MLA_QK_PAD), BF16),
                   jax.ShapeDtypeStruct((MLA_HEADS, R, MLA_QK_PAD), BF16),
                   jax.ShapeDtypeStruct((MLA_HEADS, R, MLA_V_PAD), BF16)],
        compiler_params=_cparams("parallel"),
    )(z, qg, kvg, wq, wkv, cos, sin)


def _mla_flash_kernel(*refs, with_latent, sub):
    if with_latent:
        q_ref, kx_ref, vx_ref, k_ref, v_ref, o_ref, m_ref, acc_ref = refs
    else:
        q_ref, kx_ref, vx_ref, o_ref, m_ref, acc_ref = refs
    ki = pl.program_id(3)
    q = q_ref[0]

    def update(s, v, m_old, acc_old):
        cols = [s[:, c * 128:(c + 1) * 128] for c in range(s.shape[1] // 128)]
        mx = functools.reduce(jnp.maximum, cols)
        m_new = jnp.maximum(m_old, jnp.max(mx, axis=-1, keepdims=True))
        alpha = jnp.exp2(m_old - m_new)
        p = jnp.concatenate([jnp.exp2(c - m_new).astype(BF16) for c in cols], axis=-1)
        return m_new, jnp.concatenate([alpha, alpha], axis=-1) * acc_old + _dot(p, v)

    @pl.when(ki == 0)
    def _():
        tq = q.shape[0]
        m, acc = update(_dot_nt(q, kx_ref[0]), vx_ref[0], jnp.full((tq, 128), NEG_INF, F32),
                        jnp.zeros((tq, MLA_V_PAD), F32))
        m_ref[...], acc_ref[...] = m, acc

    if with_latent:
        nsub = k_ref.shape[1] // sub
        m, acc = m_ref[...], acc_ref[...]
        s_next = _dot_nt(q, k_ref[0, 0:sub, :])
        for j in range(nsub):
            s = s_next
            if j + 1 < nsub:
                s_next = _dot_nt(q, k_ref[0, (j + 1) * sub:(j + 2) * sub, :])
            m, acc = update(s, v_ref[0, j * sub:(j + 1) * sub, :], m, acc)
        m_ref[...], acc_ref[...] = m, acc

    @pl.when(ki == pl.num_programs(3) - 1)
    def _():
        acc = acc_ref[...]
        o_ref[...] = (acc[:, :MLA_V] / acc[:, MLA_V:]).astype(o_ref.dtype)


def mla_attention(geom, q, k, v, *, with_latent, tq, tk):
    B, T, L = geom.B, geom.T, geom.L
    cblk = geom.BT // L
    ctx_k = pl.BlockSpec((1, L, MLA_QK_PAD), lambda b, h, qi, ki: (h, cblk + b, 0))
    ctx_v = pl.BlockSpec((1, L, MLA_V_PAD), lambda b, h, qi, ki: (h, cblk + b, 0))
    if with_latent:
        nq, nk = T // tq, T // tk
        in_specs = [pl.BlockSpec((1, tq, MLA_QK_PAD), lambda b, h, qi, ki: (h, b * nq + qi, 0)), ctx_k, ctx_v,
                    pl.BlockSpec((1, tk, MLA_QK_PAD), lambda b, h, qi, ki: (h, b * nk + ki, 0)),
                    pl.BlockSpec((1, tk, MLA_V_PAD), lambda b, h, qi, ki: (h, b * nk + ki, 0))]
        args = (q, k, v, k, v)
        out_rows, out_spec = geom.BT, pl.BlockSpec((tq, MLA_V), lambda b, h, qi, ki: (b * nq + qi, h))
    else:
        tq, nq, nk = L, 1, 1
        in_specs = [pl.BlockSpec((1, tq, MLA_QK_PAD), lambda b, h, qi, ki: (h, cblk + b, 0)), ctx_k, ctx_v]
        args = (q, k, v)
        out_rows, out_spec = B * L, pl.BlockSpec((tq, MLA_V), lambda b, h, qi, ki: (b, h))
    return pl.pallas_call(
        functools.partial(_mla_flash_kernel, with_latent=with_latent, sub=min(MLA_SUB_KEYS, tk)),
        grid=(B, MLA_HEADS, nq, nk),
        in_specs=in_specs,
        out_specs=out_spec,
        out_shape=jax.ShapeDtypeStruct((out_rows, MLA_HEADS * MLA_V), BF16),
        scratch_shapes=[pltpu.VMEM((tq, 128), F32), pltpu.VMEM((tq, MLA_V_PAD), F32)],
        compiler_params=_cparams("parallel", "parallel", "parallel", "arbitrary"),
    )(*args)


RW_Z_R, RW_Z_K, RW_Z_V = 0, RW_WIDTH, 2 * RW_WIDTH
RW_Z_LORA = 3 * RW_WIDTH
RW_Z_COLS = 3 * RW_WIDTH + 5 * LORA_PAD


def _dot3(a, b):
    return _mmx(_pieces(a, 3), _pieces(b, 3), _dot)


def _head_sum(x, ones):
    ones = (ones.astype(BF16),)
    return jnp.concatenate(
        [_mmx(_pieces(x[:, c * 128:(c + 1) * 128], 3), ones, _dot) for c in range(x.shape[1] // 128)], axis=-1)


def _rw_prep_kernel(*refs, geom, tm, has_vres):
    if has_vres:
        (z_ref, zp_ref, zn_ref, mu_ref, w0_ref, w2_ref, a0_ref, a2_ref, g2_ref, kk_ref, ka_ref, rk_ref, ones_ref,
         vf_ref, v0_ref, v1_ref, v2_ref,
         r_o, v_o, kk_o, lwf_o, kf_o, bf_o, lwb_o, kb_o, bb_o, g_o, bonus_o) = refs
    else:
        (z_ref, zp_ref, zn_ref, mu_ref, w0_ref, w2_ref, a0_ref, a2_ref, g2_ref, kk_ref, ka_ref, rk_ref, ones_ref,
         r_o, v_o, kk_o, lwf_o, kf_o, bf_o, lwb_o, kb_o, bb_o, g_o, bonus_o) = refs
    first, last = _seq_edge_masks(geom, pl.program_id(0) * tm, tm)
    z = z_ref[...]
    up, dn = _shifted_rows(z, zp_ref[...], zn_ref[...], first, last)
    z = z + mu_ref[...] * (0.5 * (up + dn) - z)
    r = z[:, RW_Z_R:RW_Z_R + RW_WIDTH]
    k = z[:, RW_Z_K:RW_Z_K + RW_WIDTH]
    v = z[:, RW_Z_V:RW_Z_V + RW_WIDTH]
    lora = lambda n: z[:, RW_Z_LORA + n * LORA_PAD:RW_Z_LORA + (n + 1) * LORA_PAD]
    ones = ones_ref[...]
    if has_vres:
        mix = jax.nn.sigmoid(v0_ref[...] + _dot3(_dot3(v, v1_ref[...]), v2_ref[...]))
        v = v + (vf_ref[...] - v) * mix
    g_o[...] = _dot3(jax.nn.sigmoid(lora(4)), g2_ref[...])
    kk = k * kk_ref[...]
    kk = kk / jnp.maximum(jnp.sqrt(_head_sum(kk * kk, ones)), 1e-12)
    ksum = None
    for d, (lw_o, k_o, b_o) in enumerate(((lwf_o, kf_o, bf_o), (lwb_o, kb_o, bb_o))):
        x = -(w0_ref[d:d + 1, :] + _dot3(jnp.tanh(lora(d)), w2_ref[d]))
        softplus = jnp.maximum(x, 0.0) + jnp.log1p(jnp.exp(-jnp.abs(x)))
        lw_o[...] = -jnp.exp(-softplus - 0.5)
        a = jax.nn.sigmoid(a0_ref[d:d + 1, :] + _dot3(lora(2 + d), a2_ref[d]))
        kd = k * (1.0 + (a - 1.0) * ka_ref[...])
        k_o[...] = kd
        b_o[...] = kk * a
        ksum = kd if ksum is None else ksum + kd
    r_o[...] = r
    v_o[...] = v
    kk_o[...] = kk
    bonus_o[...] = _head_sum(r * ksum * rk_ref[...], ones) * v


def rw_prep(geom, z, p, v_first, vres, *, tm):
    R = z.shape[0]
    has_vres = vres is not None
    full = lambda a: pl.BlockSpec(a.shape, lambda i: (0,) * a.ndim)
    row = pl.BlockSpec((tm, RW_WIDTH), lambda i: (i, 0))
    prev, nxt = _halo_specs(tm, R, RW_Z_COLS, lambda i: 0)
    params = [p["mu"], p["w0"], p["w2"], p["a0"], p["a2"], p["g2"], p["kk"], p["ka"], p["rk"], p["ones"]]
    in_specs = [pl.BlockSpec((tm, RW_Z_COLS), lambda i: (i, 0)), prev, nxt] + [full(a) for a in params]
    args = [z, z, z] + params
    if has_vres:
        in_specs += [row] + [full(a) for a in vres]
        args += [v_first] + list(vres)
    return pl.pallas_call(
        functools.partial(_rw_prep_kernel, geom=geom, tm=tm, has_vres=has_vres),
        grid=(R // tm,),
        in_specs=in_specs,
        out_specs=[row] * 11,
        out_shape=[jax.ShapeDtypeStruct((R, RW_WIDTH), F32)] * 11,
        compiler_params=_cparams("parallel"),
    )(*args)


def _pieces(x, passes):
    hi = x.astype(BF16)
    if passes == 1:
        return (hi,)
    return hi, (x - hi.astype(F32)).astype(BF16)


def _mmx(a, b, dot):
    out = dot(a[0], b[0])
    if len(a) > 1:
        out = out + dot(a[1], b[0])
    if len(b) > 1:
        out = out + dot(a[0], b[1])
    return out


def _stack_pair(first_head, x):
    return jnp.concatenate([jnp.where(first_head, x, 0.0), jnp.where(first_head, 0.0, x)], axis=0)


def _fold_pair(x):
    half = x.shape[0] // 2
    return x[:half] + x[half:]


def _rw_fused_scan_kernel(rf_ref, vf_ref, kkf_ref, lwf_ref, kf_ref, bf_ref,
                          rb_ref, vb_ref, kkb_ref, lwb_ref, kb_ref, bb_ref, yf_o, yb_o, s_ref, *, nchunk):
    C = RW_CHUNK
    PW = RW_PAIR
    ri = lax.broadcasted_iota(jnp.int32, (PW, PW), 0)
    ci = lax.broadcasted_iota(jnp.int32, (PW, PW), 1)
    eye = ri == ci
    ri, ci = ri & (C - 1), ci & (C - 1)
    first_head = lax.broadcasted_iota(jnp.int32, (1, PW), 1) < RW_HEAD_DIM
    stack = functools.partial(_stack_pair, first_head)

    @pl.when(pl.program_id(1) == 0)
    def _():
        s_ref[...] = jnp.zeros_like(s_ref)

    dirs = ((rf_ref, vf_ref, kkf_ref, lwf_ref, kf_ref, bf_ref, yf_o, ci < ri, C - 1),
            (rb_ref, vb_ref, kkb_ref, lwb_ref, kb_ref, bb_ref, yb_o, ci > ri, 0))

    def chunk_step(step, carry):
        jobs = []
        for d, (r_ref, v_ref, kk_ref, lw_ref, k_ref, b_ref, y_o, before, tot_row) in enumerate(dirs):
            chunk = step if d == 0 else nchunk - 1 - step
            rows = pl.ds(pl.multiple_of(chunk * C, C), C)
            r, v, kk = r_ref[rows, :], v_ref[rows, :], kk_ref[rows, :]
            lw, kd, bd = lw_ref[rows, :], k_ref[rows, :], b_ref[rows, :]
            incl = (before | eye)[:C, :C].astype(BF16)
            cum, rest = None, lw
            for _ in range(3):
                piece = rest.astype(BF16)
                rest = rest - piece.astype(F32)
                part = _dot(incl, piece)
                cum = part if cum is None else cum + part
            c_tot = cum[tot_row:tot_row + 1, :]
            at = -kk * jnp.exp(cum - lw)
            rt = r * jnp.exp(cum)
            e_neg = jnp.exp(-cum)
            bt, kt = bd * e_neg, kd * e_neg
            e_rest = jnp.exp(c_tot - cum)
            bc, kc = bd * e_rest, kd * e_rest
            e_tot = jnp.exp(c_tot)
            for p in range(RW_WIDTH // PW):
                ps = slice(p * PW, (p + 1) * PW)
                jobs.append(dict(d=d, ps=ps, rows=rows, y_o=y_o, before=before, incl=before | eye, e=e_tot[:, ps],
                                 at=stack(at[:, ps]), rt=stack(rt[:, ps]), bt=stack(bt[:, ps]), kt=stack(kt[:, ps]),
                                 bc=stack(bc[:, ps]), kc=stack(kc[:, ps]), v=stack(v[:, ps])))
        for j in jobs:
            p = _mmx(_pieces(jnp.concatenate([j["at"], j["rt"]], axis=0), RW_PASSES_LOCAL),
                     _pieces(jnp.concatenate([j["bt"], j["kt"]], axis=0), RW_PASSES_LOCAL), _dot_nt)
            j["a_ab"] = jnp.where(j["before"], p[:PW, :PW], 0.0)
            j["a_ak"] = jnp.where(j["before"], p[:PW, PW:], 0.0)
            j["a_rb"] = jnp.where(j["incl"], p[PW:, :PW], 0.0)
            j["a_rk"] = jnp.where(j["incl"], p[PW:, PW:], 0.0)
            j["vp"] = _pieces(j["v"], RW_PASSES_LOCAL)
        for j in jobs:
            j["w1"] = _mmx(_pieces(j["a_ak"], RW_PASSES_LOCAL), j["vp"], _dot)
            j["tinv"] = jnp.where(eye, 1.0, j["a_ab"])
            j["pw"] = j["a_ab"]
        for _ in range(int(np.log2(C)) - 1):
            for j in jobs:
                pw = _pieces(j["pw"], RW_PASSES_LOCAL)
                j["pw"] = _mmx(pw, pw, _dot)
            for j in jobs:
                j["tinv"] = j["tinv"] + _mmx(_pieces(j["tinv"], RW_PASSES_LOCAL), _pieces(j["pw"], RW_PASSES_LOCAL),
                                             _dot)
        for j in jobs:
            tw = _mmx(_pieces(j["tinv"], RW_PASSES_LOCAL),
                      _pieces(jnp.concatenate([j["at"], j["w1"]], axis=1), RW_PASSES_LOCAL), _dot)
            j["tw"] = tw
            j["twp"] = _pieces(tw, RW_PASSES_LOCAL)
        for j in jobs:
            ry = _mmx(_pieces(j["a_rb"], RW_PASSES_LOCAL), j["twp"], _dot)
            yk = _mmx(_pieces(j["a_rk"], RW_PASSES_LOCAL), j["vp"], _dot)
            j["rr"] = j["rt"] + ry[:, :PW]
            j["yl"] = _fold_pair(ry[:, PW:] + yk)
            j["m"] = _mmx(_pieces(j["bc"], RW_PASSES_TRANSITION), _pieces(j["tw"][:, :PW], RW_PASSES_TRANSITION),
                          _dot_tn)
            uv = jnp.concatenate([j["tw"][:, PW:], j["v"]], axis=0)
            bk = jnp.concatenate([j["bc"], j["kc"]], axis=0)
            j["n"] = _fold_pair(_mmx(_pieces(uv, RW_PASSES_STATE), _pieces(bk, RW_PASSES_STATE), _dot_tn))
        for j in jobs:
            s = s_ref[j["d"], :, j["ps"]]
            sp = _pieces(stack(s), RW_PASSES_STATE)
            y = _mmx(_pieces(j["rr"], RW_PASSES_STATE), sp, _dot_nt)
            j["y_o"][j["rows"], j["ps"]] = _fold_pair(y) + j["yl"]
            sm = _mmx(sp[:RW_PASSES_TRANSITION], _pieces(j["m"], RW_PASSES_TRANSITION), _dot_nt)
            s_ref[j["d"], :, j["ps"]] = s * j["e"] + _fold_pair(sm) + j["n"]
        return carry

    lax.fori_loop(0, nchunk, chunk_step, 0)


def rw_fused_scan(geom, r, v, kk, lwf, kf, bf, lwb, kb, bb):
    B, T, L = geom.B, geom.T, geom.L
    blk = L
    nlat = T // blk
    cblk = geom.BT // blk
    fwd = pl.BlockSpec((blk, RW_WIDTH), lambda b, s: (jnp.where(s == 0, cblk + b, b * nlat + s - 1), 0))
    bwd = pl.BlockSpec((blk, RW_WIDTH), lambda b, s: (jnp.where(s == 0, cblk + b, b * nlat + nlat - s), 0))
    return pl.pallas_call(
        functools.partial(_rw_fused_scan_kernel, nchunk=blk // RW_CHUNK),
        grid=(B, nlat + 1),
        in_specs=[fwd] * 6 + [bwd] * 6,
        out_specs=[fwd, bwd],
        out_shape=[jax.ShapeDtypeStruct((geom.R, RW_WIDTH), F32)] * 2,
        scratch_shapes=[pltpu.VMEM((2, RW_HEAD_DIM, RW_WIDTH), F32)],
        compiler_params=_cparams("parallel", "arbitrary"),
    )(r, v, kk, lwf, kf, bf, r, v, kk, lwb, kb, bb)


def _rw_post_kernel(yf_ref, yb_ref, bonus_ref, g_ref, lng_ref, lnb_ref, ones_ref, o_ref):
    ones = ones_ref[...]
    y = yf_ref[...] + yb_ref[...]
    mean = _head_sum(y, ones) * (1.0 / RW_HEAD_DIM)
    yc = y - mean
    var = _head_sum(yc * yc, ones) * (1.0 / RW_HEAD_DIM)
    y = yc * lax.rsqrt(var + RW_LNX_EPS) * lng_ref[...] + lnb_ref[...]
    o_ref[...] = ((y + bonus_ref[...]) * g_ref[...]).astype(o_ref.dtype)


def rw_post(yf, yb, bonus, g, lnx_g, lnx_b, ones, *, tm):
    R = yf.shape[0]
    row = pl.BlockSpec((tm, RW_WIDTH), lambda i: (i, 0))
    full = lambda a: pl.BlockSpec(a.shape, lambda i: (0,) * a.ndim)
    return pl.pallas_call(
        _rw_post_kernel,
        grid=(R // tm,),
        in_specs=[row] * 4 + [full(lnx_g), full(lnx_b), full(ones)],
        out_specs=row,
        out_shape=jax.ShapeDtypeStruct((R, RW_WIDTH), BF16),
        compiler_params=_cparams("parallel"),
    )(yf, yb, bonus, g, lnx_g, lnx_b, ones)


def _rw_in_cols(w):
    parts = [w[..., :3 * RW_WIDTH]]
    off = 3 * RW_WIDTH
    for n in (RW_DECAY_LORA, RW_DECAY_LORA, RW_AAA_LORA, RW_AAA_LORA, RW_GATE_LORA):
        parts.append(_pad_cols(w[..., off:off + n], LORA_PAD))
        off += n
    return jnp.concatenate(parts, axis=-1)


def _pad_rows(w, n):
    return jnp.pad(w, [(0, 0)] * (w.ndim - 2) + [(0, n - w.shape[-2]), (0, 0)])


def _mla_wq_cols(w):
    w = w.reshape(w.shape[0], MLA_HEADS, MLA_NOPE + MLA_ROPE)
    return _pad_cols(w, MLA_QK_PAD).reshape(w.shape[0], MLA_HEADS * MLA_QK_PAD)


def kernel(x, c, ctx, c_ctx, ada_w, ada_b, norm1_g, w_in, rw_mu, rw_w0, rw_w2, rw_a0, rw_a2, rw_g2, rw_kk, rw_ka,
           rw_rk, rw_lnx_g, rw_lnx_b, rw_v0, rw_v1, rw_v2, wa_sink, mla_qnorm_g, mla_kvnorm_g, mla_w_uq, mla_w_ukv,
           w_branch, w_out, norm2_g, ffn_w_in, ffn_conv_w, ffn_conv_b, ffn_w_out, final_norm_g):
    B, T, D = x.shape
    L = ctx.shape[1]
    depth = w_in.shape[0]
    F = ffn_w_out.shape[1]
    geom = Geom(B, T, L)
    tm = _pick_tile(T, (512, 256, 128))
    assert (B * L) % tm == 0
    tmm = _pick_tile(geom.R, (1280, 1024, 640, 512, 256, 128))

    rw_cols = 3 * RW_WIDTH + 2 * RW_DECAY_LORA + 2 * RW_AAA_LORA + RW_GATE_LORA
    wa_cols = WA_WIDTH + 2 * WA_KV_WIDTH
    mla_cols = MLA_Q_LORA + MLA_KV_LORA + MLA_ROPE
    mla_cols_pad = MLA_Q_LORA + MLA_KV_LORA + 128

    cos_wa, sin_wa = _rope_tables(geom, WA_HEAD_DIM)
    cos_wa, sin_wa = jnp.tile(cos_wa, (1, 2)), jnp.tile(sin_wa, (1, 2))
    cos_m, sin_m = _rope_tables(geom, MLA_ROPE)
    one, zero = jnp.ones((geom.R, MLA_NOPE), F32), jnp.zeros((geom.R, MLA_NOPE), F32)
    cos_mla = jnp.concatenate([one, cos_m, one[:, :64]], axis=-1)
    sin_mla = jnp.concatenate([zero, sin_m, zero[:, :64]], axis=-1)
    lane = np.arange(128)
    ones_blk = jnp.asarray((lane[:, None] // RW_HEAD_DIM) == (lane[None, :] // RW_HEAD_DIM), F32)

    xs = jnp.concatenate([x.reshape(B * T, D), ctx.reshape(B * L, D)], axis=0)
    cvec = jnp.concatenate([c, c_ctx[None, :], jnp.zeros((8 - (B + 1) % 8, D), F32)], axis=0)
    v_first = None
    for l in range(depth):
        need_ctx = l < depth - 1
        mod = ada_modulation(cvec, ada_w[l], ada_b[l])
        mod = [mod[:, k * D:(k + 1) * D].reshape(-1, 1, D) for k in range(6)]

        w = w_in[l]
        tn_in = _pick_tile(3 * D, (768, 512, 256, 128))
        w_rw = _pad_cols(_rw_in_cols(w[:, :rw_cols]), _round_up(RW_Z_COLS, tn_in)).astype(BF16)
        w_wa = _pad_cols(w[:, rw_cols:rw_cols + wa_cols], _round_up(wa_cols, tn_in)).astype(BF16)
        w_mla = _pad_cols(w[:, rw_cols + wa_cols:rw_cols + wa_cols + mla_cols],
                          _round_up(mla_cols_pad, tn_in)).astype(BF16)
        w_gate = w[:, rw_cols + wa_cols + mla_cols:].astype(BF16)
        z_rw, z_wa, z_mla, gates = in_projection(geom, xs, norm1_g[l], mod[0], mod[1], w_rw, w_wa, w_mla, w_gate,
                                                 tm=tmm, tn=tn_in)

        rw_p = dict(
            mu=_rw_in_cols(rw_mu[l][None, :]), w0=rw_w0[l], w2=_pad_rows(rw_w2[l], LORA_PAD), a0=rw_a0[l],
            a2=_pad_rows(rw_a2[l], LORA_PAD), g2=_pad_rows(rw_g2[l], LORA_PAD), kk=rw_kk[l][None, :],
            ka=rw_ka[l][None, :], rk=rw_rk[l].reshape(1, RW_WIDTH), ones=ones_blk)
        vres = None if l == 0 else (rw_v0[l - 1][None, :], rw_v1[l - 1], rw_v2[l - 1])
        r, v, kk, lwf, kf, bf, lwb, kb, bb, g, bonus = rw_prep(geom, z_rw, rw_p, v_first, vres, tm=min(tm, 256))
        if l == 0:
            v_first = v
        yf, yb = rw_fused_scan(geom, r, v, kk, lwf, kf, bf, lwb, kb, bb)
        o_a = rw_post(yf, yb, bonus, g, rw_lnx_g[l][None, :], rw_lnx_b[l][None, :], ones_blk, tm=tm)

        q_wa, k_wa, v_wa = wa_prep(z_wa, cos_wa, sin_wa, tm=tm)
        ob_l = wa_attention(geom, q_wa, k_wa, v_wa, wa_sink[l], local=True)
        parts = [ob_l]
        if need_ctx:
            parts.append(wa_attention(geom, q_wa, k_wa, v_wa, wa_sink[l], local=False))
        else:
            parts.append(jnp.zeros((B * L, WA_WIDTH), BF16))
        o_b = jnp.concatenate(parts, axis=0)

        wq = _mla_wq_cols(mla_w_uq[l]).astype(BF16)
        q_m, k_m, v_m = mla_prep(z_mla, mla_qnorm_g[l], mla_kvnorm_g[l], wq, mla_w_ukv[l].astype(BF16),
                                 cos_mla, sin_mla, tm=tm)
        tq = _pick_tile(T, (1024, 512, 256, 128))
        tk = _pick_tile(T, (4096, 2048, 1024, 512, 256, 128))
        parts = [mla_attention(geom, q_m, k_m, v_m, with_latent=True, tq=tq, tk=tk)]
        if need_ctx:
            parts.append(mla_attention(geom, q_m, k_m, v_m, with_latent=False, tq=L, tk=L))
        else:
            parts.append(jnp.zeros((B * L, MLA_HEADS * MLA_V), BF16))
        o_c = jnp.concatenate(parts, axis=0)

        tn_d = _pick_tile(D, (1024, 512, 256, 128))
        y = merge_branches(o_a, o_b, o_c, gates, w_branch[l].astype(BF16), tm=tmm, tn=tn_d)
        xs = matmul_gated_residual(geom, y, w_out[l].astype(BF16), xs, mod[2], tm=tmm, tn=tn_d)

        tf = _pick_tile(F, (512, 256, 128))
        hmid = ffn_in_conv_glu(geom, xs, norm2_g[l], mod[3], mod[4], ffn_w_in[l].astype(BF16), ffn_conv_w[l],
                               ffn_conv_b[l], tm=tmm, tf=tf)
        xs = matmul_gated_residual(geom, hmid, ffn_w_out[l].astype(BF16), xs, mod[5], tm=tmm,
                                   tn=_pick_tile(D, (512, 256, 128)))

    out = final_rmsnorm(xs, final_norm_g, B * T, tm=tm)
    return out.reshape(B, T, D)
```

```python
import functools

import jax
import jax.numpy as jnp
import numpy as np
from jax import lax
from jax.experimental import pallas as pl
from jax.experimental.pallas import tpu as pltpu

F32 = jnp.float32
BF16 = jnp.bfloat16
HIGHEST = lax.Precision.HIGHEST

NORM_EPS = 1e-6
NEG_INF = -1e30
GRID_W = 64
ROPE_BASE = 10000.0

RW_HEADS = 16
RW_HEAD_DIM = 64
RW_WIDTH = RW_HEADS * RW_HEAD_DIM
RW_DECAY_LORA = 96
RW_AAA_LORA = 96
RW_GATE_LORA = 64
RW_LNX_EPS = 64e-5
RW_CHUNK = 64
RW_PAIR = 2 * RW_HEAD_DIM
RW_PASSES_LOCAL = 1
RW_PASSES_STATE = 3
RW_PASSES_TRANSITION = 1
LORA_PAD = 128

WA_HEADS = 16
WA_KV_HEADS = 4
WA_GROUP = WA_HEADS // WA_KV_HEADS
WA_HEAD_DIM = 64
WA_WIDTH = WA_HEADS * WA_HEAD_DIM
WA_KV_WIDTH = WA_KV_HEADS * WA_HEAD_DIM
WA_V_PAD = WA_KV_HEADS * 128
WINDOW = 128
WA_SCALE = WA_HEAD_DIM ** -0.5

MLA_HEADS = 8
MLA_NOPE = 128
MLA_ROPE = 64
MLA_V = 128
MLA_Q_LORA = 512
MLA_KV_LORA = 512
MLA_QK_PAD = 256
MLA_V_PAD = 256
MLA_SCALE = (MLA_NOPE + MLA_ROPE) ** -0.5
MLA_SUB_KEYS = 2048
LOG2_E = 1.4426950408889634

CONV_W = 3
VMEM_LIMIT_BYTES = 56 * 1024 * 1024


def _cparams(*sem):
    return pltpu.CompilerParams(dimension_semantics=sem, vmem_limit_bytes=VMEM_LIMIT_BYTES)


def _dot(a, b, precision=None):
    return jnp.dot(a, b, preferred_element_type=F32, precision=precision)


def _dot_nt(a, b, precision=None):
    return lax.dot_general(a, b, (((1,), (1,)), ((), ())), preferred_element_type=F32, precision=precision)


def _dot_tn(a, b, precision=None):
    return lax.dot_general(a, b, (((0,), (0,)), ((), ())), preferred_element_type=F32, precision=precision)


def _pick_tile(n, candidates):
    for c in candidates:
        if n % c == 0:
            return c
    raise ValueError(f"no tile in {candidates} divides {n}")


def _pad_cols(w, n):
    return jnp.pad(w, [(0, 0)] * (w.ndim - 1) + [(0, n - w.shape[-1])])


def _round_up(n, m):
    return (n + m - 1) // m * m


class Geom:
    def __init__(self, B, T, L):
        assert T & (T - 1) == 0 and L & (L - 1) == 0, "sequence lengths must be powers of two"
        assert T % L == 0 and L % RW_CHUNK == 0 and T % GRID_W == 0
        self.B, self.T, self.L = B, T, L
        self.BT = B * T
        self.R = B * T + B * L


def _select_row_group(geom, row0, tm, tab_ref):
    r = row0 + lax.broadcasted_iota(jnp.int32, (tm, 1), 0)
    out = tab_ref[geom.B]
    for b in range(geom.B):
        out = jnp.where((r >= b * geom.T) & (r < (b + 1) * geom.T), tab_ref[b], out)
    return out


def _seq_edge_masks(geom, row0, tm):
    r = row0 + lax.broadcasted_iota(jnp.int32, (tm, 1), 0)
    is_lat = r < geom.BT
    pos = jnp.where(is_lat, r & (geom.T - 1), (r - geom.BT) & (geom.L - 1))
    last = jnp.where(is_lat, geom.T - 1, geom.L - 1)
    return pos == 0, pos == last


def _shifted_rows(x, prev8, next8, first, last):
    tm = x.shape[0]
    rid = lax.broadcasted_iota(jnp.int32, (tm, 1), 0)
    up = jnp.where(rid == 0, prev8[7:8, :], pltpu.roll(x, 1, axis=0))
    dn = jnp.where(rid == tm - 1, next8[0:1, :], pltpu.roll(x, tm - 1, axis=0))
    return jnp.where(first, 0.0, up), jnp.where(last, 0.0, dn)


def _halo_specs(tm, R, width, col_of):
    nb8 = tm // 8
    prev = pl.BlockSpec((8, width), lambda i, *a: (jnp.maximum(i * nb8 - 1, 0), col_of(i, *a)))
    nxt = pl.BlockSpec((8, width), lambda i, *a: (jnp.minimum((i + 1) * nb8, R // 8 - 1), col_of(i, *a)))
    return prev, nxt


def _ada_kernel(c_ref, w_ref, b_ref, o_ref):
    c = c_ref[...]
    o_ref[...] = _dot(c * jax.nn.sigmoid(c), w_ref[...], HIGHEST) + b_ref[...]


def ada_modulation(cvec, w, b):
    G, D = cvec.shape
    N = w.shape[1]
    tn = _pick_tile(N, (1024, 512, 256, 128))
    return pl.pallas_call(
        _ada_kernel,
        grid=(N // tn,),
        in_specs=[pl.BlockSpec((G, D), lambda j: (0, 0)),
                  pl.BlockSpec((D, tn), lambda j: (0, j)),
                  pl.BlockSpec((1, tn), lambda j: (0, j))],
        out_specs=pl.BlockSpec((G, tn), lambda j: (0, j)),
        out_shape=jax.ShapeDtypeStruct((G, N), F32),
        compiler_params=_cparams("arbitrary"),
    )(cvec, w, b.reshape(1, N))


def _norm_modulate(x, g, sc, sh):
    gain = g * (1.0 + sc)
    return (x * lax.rsqrt(jnp.mean(x * x, axis=-1, keepdims=True) + NORM_EPS) * gain + sh).astype(BF16)


PROLOGUE_ROWS = 256


def _norm_modulate_rows(geom, row0, x_ref, g_ref, sc_ref, sh_ref, h_ref, h_off):
    tm = x_ref.shape[0]
    step = PROLOGUE_ROWS if tm % PROLOGUE_ROWS == 0 else tm
    one_group = geom.T % step == 0 and geom.BT % step == 0
    g = g_ref[...]
    for r in range(0, tm, step):
        if one_group:
            sc, sh = sc_ref[geom.B], sh_ref[geom.B]
            for b in range(geom.B):
                in_b = (row0 + r >= b * geom.T) & (row0 + r < (b + 1) * geom.T)
                sc, sh = jnp.where(in_b, sc_ref[b], sc), jnp.where(in_b, sh_ref[b], sh)
        else:
            sc = _select_row_group(geom, row0 + r, step, sc_ref)
            sh = _select_row_group(geom, row0 + r, step, sh_ref)
        h_ref[h_off + r:h_off + r + step] = _norm_modulate(x_ref[r:r + step], g, sc, sh)


def _in_proj_kernel(x_ref, g_ref, sh_ref, sc_ref, w_ref, zrw_ref, zwa_ref, zmla_ref, gate_ref, h_ref,
                    *, geom, tm, starts):
    j = pl.program_id(1)

    @pl.when(j == 0)
    def _():
        _norm_modulate_rows(geom, pl.program_id(0) * tm, x_ref, g_ref, sc_ref, sh_ref, h_ref, 0)

    wa0, mla0, gate0 = starts

    @pl.when(j < wa0)
    def _():
        zrw_ref[...] = _dot(h_ref[...], w_ref[...])

    @pl.when((j >= wa0) & (j < mla0))
    def _():
        zwa_ref[...] = _dot(h_ref[...], w_ref[...])

    @pl.when((j >= mla0) & (j < gate0))
    def _():
        zmla_ref[...] = _dot(h_ref[...], w_ref[...])

    @pl.when(j >= gate0)
    def _():
        gate_ref[...] = jax.nn.sigmoid(_dot(h_ref[...], w_ref[...])).astype(gate_ref.dtype)


def in_projection(geom, x, g, shift, scale, w_rw, w_wa, w_mla, w_gate, *, tm, tn):
    R, K = x.shape
    widths = [w.shape[1] for w in (w_rw, w_wa, w_mla, w_gate)]
    assert R % tm == 0 and all(n % tn == 0 for n in widths)
    tiles = [n // tn for n in widths]
    starts = tuple(int(v) for v in np.cumsum(tiles)[:3])
    w_all = jnp.concatenate([w_rw, w_wa, w_mla, w_gate], axis=1)
    full = lambda a: pl.BlockSpec(a.shape, lambda i, j: (0,) * a.ndim)
    out_spec = lambda first, n: pl.BlockSpec((tm, tn), lambda i, j: (i, jnp.clip(j - first, 0, n - 1)))
    firsts = (0,) + starts
    return pl.pallas_call(
        functools.partial(_in_proj_kernel, geom=geom, tm=tm, starts=starts),
        grid=(R // tm, sum(tiles)),
        in_specs=[pl.BlockSpec((tm, K), lambda i, j: (i, 0), pipeline_mode=pl.Buffered(1)),
                  pl.BlockSpec((1, K), lambda i, j: (0, 0)),
                  full(shift), full(scale),
                  pl.BlockSpec((K, tn), lambda i, j: (0, j))],
        out_specs=[out_spec(f, n) for f, n in zip(firsts, tiles)],
        out_shape=[jax.ShapeDtypeStruct((R, widths[0]), F32), jax.ShapeDtypeStruct((R, widths[1]), F32),
                   jax.ShapeDtypeStruct((R, widths[2]), F32), jax.ShapeDtypeStruct((R, widths[3]), BF16)],
        scratch_shapes=[pltpu.VMEM((tm, K), BF16)],
        compiler_params=_cparams("parallel", "arbitrary"),
    )(x, g.reshape(1, K), shift, scale, w_all)


def _mm_resid_kernel(y_ref, w_ref, r_ref, gate_ref, o_ref, *, geom, tm):
    gate = _select_row_group(geom, pl.program_id(0) * tm, tm, gate_ref)
    o_ref[...] = r_ref[...] + gate * _dot(y_ref[...], w_ref[...])


def matmul_gated_residual(geom, y, w, resid, gate, *, tm, tn):
    R, K = y.shape
    N = w.shape[1]
    G = gate.shape[0]
    assert R % tm == 0 and N % tn == 0
    return pl.pallas_call(
        functools.partial(_mm_resid_kernel, geom=geom, tm=tm),
        grid=(R // tm, N // tn),
        in_specs=[pl.BlockSpec((tm, K), lambda i, j: (i, 0)),
                  pl.BlockSpec((K, tn), lambda i, j: (0, j)),
                  pl.BlockSpec((tm, tn), lambda i, j: (i, j)),
                  pl.BlockSpec((G, 1, tn), lambda i, j: (0, 0, j))],
        out_specs=pl.BlockSpec((tm, tn), lambda i, j: (i, j)),
        out_shape=jax.ShapeDtypeStruct((R, N), F32),
        compiler_params=_cparams("parallel", "arbitrary"),
    )(y, w, resid, gate)


FFN_HALO = 16


def _ffn_in_kernel(x_ref, xp_ref, xn_ref, g_ref, sh_ref, sc_ref, wg_ref, wu_ref, cw_ref, cb_ref, o_ref, h_ref,
                   *, geom, tm):
    row0 = pl.program_id(0) * tm
    H = FFN_HALO

    @pl.when(pl.program_id(1) == 0)
    def _():
        _norm_modulate_rows(geom, row0, xp_ref, g_ref, sc_ref, sh_ref, h_ref, 0)
        _norm_modulate_rows(geom, row0, x_ref, g_ref, sc_ref, sh_ref, h_ref, H)
        _norm_modulate_rows(geom, row0 + tm - H, xn_ref, g_ref, sc_ref, sh_ref, h_ref, H + tm)

    first, last = _seq_edge_masks(geom, row0, tm)
    gt = _dot(h_ref[...], wg_ref[...])
    u = _dot(h_ref[H:H + tm], wu_ref[...])
    up = jnp.where(first, 0.0, pltpu.roll(gt, 1, axis=0)[H:H + tm])
    dn = jnp.where(last, 0.0, pltpu.roll(gt, tm + 2 * H - 1, axis=0)[H:H + tm])
    cw = cw_ref[...]
    conv = cb_ref[...] + up * cw[0:1, :]
    conv = conv + gt[H:H + tm] * cw[1:2, :]
    conv = conv + dn * cw[2:3, :]
    o_ref[...] = (jax.nn.gelu(conv, approximate=True) * u).astype(o_ref.dtype)


def ffn_in_conv_glu(geom, x, g, shift, scale, w_in, conv_w, conv_b, *, tm, tf):
    R, K = x.shape
    F = w_in.shape[1] // 2
    nj = F // tf
    nbh = tm // FFN_HALO
    assert R % tm == 0 and F % tf == 0 and tm % FFN_HALO == 0
    full = lambda a: pl.BlockSpec(a.shape, lambda i, j: (0,) * a.ndim)
    return pl.pallas_call(
        functools.partial(_ffn_in_kernel, geom=geom, tm=tm),
        grid=(R // tm, nj),
        in_specs=[pl.BlockSpec((tm, K), lambda i, j: (i, 0), pipeline_mode=pl.Buffered(1)),
                  pl.BlockSpec((FFN_HALO, K), lambda i, j: (jnp.maximum(i * nbh - 1, 0), 0)),
                  pl.BlockSpec((FFN_HALO, K), lambda i, j: (jnp.minimum((i + 1) * nbh, R // FFN_HALO - 1), 0)),
                  pl.BlockSpec((1, K), lambda i, j: (0, 0)),
                  full(shift), full(scale),
                  pl.BlockSpec((K, tf), lambda i, j: (0, j)),
                  pl.BlockSpec((K, tf), lambda i, j: (0, nj + j)),
                  pl.BlockSpec((CONV_W, tf), lambda i, j: (0, j)),
                  pl.BlockSpec((1, tf), lambda i, j: (0, j))],
        out_specs=pl.BlockSpec((tm, tf), lambda i, j: (i, j)),
        out_shape=jax.ShapeDtypeStruct((R, F), BF16),
        scratch_shapes=[pltpu.VMEM((tm + 2 * FFN_HALO, K), BF16)],
        compiler_params=_cparams("parallel", "arbitrary"),
    )(x, x, x, g.reshape(1, K), shift, scale, w_in, w_in, conv_w, conv_b.reshape(1, F))


def _merge_kernel(oa_ref, ob_ref, oc_ref, ga_ref, gb_ref, gc_ref, w_ref, o_ref):
    y = ga_ref[...] * _dot(oa_ref[...], w_ref[0])
    y = y + gb_ref[...] * _dot(ob_ref[...], w_ref[1])
    y = y + gc_ref[...] * _dot(oc_ref[...], w_ref[2])
    o_ref[...] = y.astype(o_ref.dtype)


def merge_branches(oa, ob, oc, gates, wb, *, tm, tn):
    R, K = oa.shape
    D = wb.shape[2]
    nj = D // tn
    bspec = pl.BlockSpec((tm, K), lambda i, j: (i, 0))
    gspec = lambda k: pl.BlockSpec((tm, tn), lambda i, j: (i, k * nj + j))
    return pl.pallas_call(
        _merge_kernel,
        grid=(R // tm, nj),
        in_specs=[bspec, bspec, bspec, gspec(0), gspec(1), gspec(2),
                  pl.BlockSpec((3, K, tn), lambda i, j: (0, 0, j))],
        out_specs=pl.BlockSpec((tm, tn), lambda i, j: (i, j)),
        out_shape=jax.ShapeDtypeStruct((R, D), BF16),
        compiler_params=_cparams("parallel", "arbitrary"),
    )(oa, ob, oc, gates, gates, gates, wb)


def _rmsnorm_kernel(x_ref, g_ref, o_ref):
    x = x_ref[...]
    o_ref[...] = x * lax.rsqrt(jnp.mean(x * x, axis=-1, keepdims=True) + NORM_EPS) * g_ref[...]


def final_rmsnorm(x, g, rows, *, tm):
    D = x.shape[1]
    return pl.pallas_call(
        _rmsnorm_kernel,
        grid=(rows // tm,),
        in_specs=[pl.BlockSpec((tm, D), lambda i: (i, 0)), pl.BlockSpec((1, D), lambda i: (0, 0))],
        out_specs=pl.BlockSpec((tm, D), lambda i: (i, 0)),
        out_shape=jax.ShapeDtypeStruct((rows, D), F32),
        compiler_params=_cparams("parallel"),
    )(x, g.reshape(1, D))


def _rot_half64(z):
    n = z.shape[-1]
    lane = lax.broadcasted_iota(jnp.int32, z.shape, z.ndim - 1)
    return jnp.where((lane & 63) < 32, pltpu.roll(z, n - 32, axis=z.ndim - 1), pltpu.roll(z, 32, axis=z.ndim - 1))


def _rope_tables(geom, dim):
    nf = dim // 4
    inv = ROPE_BASE ** (-jnp.arange(nf, dtype=F32) / nf)
    rows = geom.T // GRID_W
    row = jnp.repeat(jnp.arange(rows, dtype=F32), GRID_W)
    col = jnp.tile(jnp.arange(GRID_W, dtype=F32), rows)
    ang = jnp.concatenate([row[:, None] * inv, col[:, None] * inv], axis=-1)
    cos, sin = jnp.cos(ang), jnp.sin(ang)
    cos_t = jnp.concatenate([cos, cos], axis=-1)
    sin_t = jnp.concatenate([-sin, sin], axis=-1)
    nctx = geom.B * geom.L
    cos_f = jnp.concatenate([jnp.tile(cos_t, (geom.B, 1)), jnp.ones((nctx, dim), F32)], axis=0)
    sin_f = jnp.concatenate([jnp.tile(sin_t, (geom.B, 1)), jnp.zeros((nctx, dim), F32)], axis=0)
    return cos_f, sin_f


def _wa_prep_kernel(z_ref, cos_ref, sin_ref, q_ref, k_ref, v_ref):
    cos = cos_ref[...]
    sin = sin_ref[...]
    for c in range(WA_WIDTH // 128):
        z = z_ref[:, c * 128:(c + 1) * 128]
        q_ref[:, c * 128:(c + 1) * 128] = ((z * cos + _rot_half64(z) * sin) * (WA_SCALE * LOG2_E)).astype(BF16)
    for c in range(WA_KV_WIDTH // 128):
        z = z_ref[:, WA_WIDTH + c * 128:WA_WIDTH + (c + 1) * 128]
        k_ref[:, c * 128:(c + 1) * 128] = (z * cos + _rot_half64(z) * sin).astype(BF16)
    v_ref[...] = jnp.ones(v_ref.shape, BF16)
    for g in range(WA_KV_HEADS):
        src = WA_WIDTH + WA_KV_WIDTH + g * WA_HEAD_DIM
        v_ref[:, g * 128:g * 128 + WA_HEAD_DIM] = z_ref[:, src:src + WA_HEAD_DIM].astype(BF16)


def wa_prep(z, cos, sin, *, tm):
    R = z.shape[0]
    row = lambda w: pl.BlockSpec((tm, w), lambda i: (i, 0))
    return pl.pallas_call(
        _wa_prep_kernel,
        grid=(R // tm,),
        in_specs=[row(WA_WIDTH + 2 * WA_KV_WIDTH), row(128), row(128)],
        out_specs=[row(WA_WIDTH), row(WA_KV_WIDTH), row(WA_V_PAD)],
        out_shape=[jax.ShapeDtypeStruct((R, WA_WIDTH), BF16),
                   jax.ShapeDtypeStruct((R, WA_KV_WIDTH), BF16),
                   jax.ShapeDtypeStruct((R, WA_V_PAD), BF16)],
        compiler_params=_cparams("parallel"),
    )(z, cos, sin)


def _wa_attn_kernel(*refs, local, nqb, tq):
    if local:
        sink_ref, q_ref, kp_ref, kc_ref, kn_ref, vp_ref, vc_ref, vn_ref, kx_ref, vx_ref, o_ref = refs
    else:
        sink_ref, q_ref, kx_ref, vx_ref, o_ref = refs
    i = pl.program_id(1)
    nk_ctx = kx_ref.shape[0]
    rows = WA_GROUP * tq
    qpos = lax.broadcasted_iota(jnp.int32, (rows, 1), 0) & (tq - 1)
    head_in_group = lax.broadcasted_iota(jnp.int32, (rows, 1), 0) >> int(np.log2(tq))
    if local:
        off_prev = jnp.where(i > 0, 0, tq)
        off_next = jnp.where(i < nqb - 1, 0, tq)
        j = lax.broadcasted_iota(jnp.int32, (1, 3 * tq + nk_ctx), 1)
        valid = ((j >= tq) & (j < 2 * tq)) | (j >= 3 * tq)
        valid = valid | ((j < tq) & (j >= qpos + off_prev))
        valid = valid | ((j >= 2 * tq) & (j < 3 * tq) & ((j - 2 * tq) <= qpos - off_next))
    def scores(g):
        ks = slice(g * WA_HEAD_DIM, (g + 1) * WA_HEAD_DIM)
        qg = jnp.concatenate(
            [q_ref[:, (g * WA_GROUP + a) * WA_HEAD_DIM:(g * WA_GROUP + a + 1) * WA_HEAD_DIM] for a in range(WA_GROUP)],
            axis=0)
        if local:
            kcat = jnp.concatenate([kp_ref[:, ks], kc_ref[:, ks], kn_ref[:, ks], kx_ref[:, ks]], axis=0)
        else:
            kcat = kx_ref[:, ks]
        return _dot_nt(qg, kcat)

    s_next = scores(0)
    for g in range(WA_KV_HEADS):
        s = s_next
        if g + 1 < WA_KV_HEADS:
            s_next = scores(g + 1)
        vs = slice(g * 128, (g + 1) * 128)
        sink = jnp.zeros((rows, 1), F32)
        for a in range(WA_GROUP):
            sink = jnp.where(head_in_group == a, sink_ref[g * WA_GROUP + a] * LOG2_E, sink)
        if local:
            vcat = jnp.concatenate([vp_ref[:, vs], vc_ref[:, vs], vn_ref[:, vs], vx_ref[:, vs]], axis=0)
            s = jnp.where(valid, s, NEG_INF)
        else:
            vcat = vx_ref[:, vs]
        m = jnp.maximum(jnp.max(s, axis=-1, keepdims=True), sink)
        o = _dot(jnp.exp2(s - m).astype(BF16), vcat)
        o = o[:, :WA_HEAD_DIM] / (o[:, WA_HEAD_DIM:] + jnp.exp2(sink - m))
        for a in range(WA_GROUP):
            h = g * WA_GROUP + a
            o_ref[:, h * WA_HEAD_DIM:(h + 1) * WA_HEAD_DIM] = o[a * tq:(a + 1) * tq].astype(o_ref.dtype)


def wa_attention(geom, q, k, v, sink, *, local):
    B, T, L = geom.B, geom.T, geom.L
    sink_spec = pl.BlockSpec(memory_space=pltpu.SMEM)
    ctx_spec = lambda w: pl.BlockSpec((L, w), lambda b, i: (geom.BT // L + b, 0))
    if local:
        tq = WINDOW
        nqb = T // tq
        kv = lambda f, w: pl.BlockSpec((tq, w), lambda b, i: (b * nqb + f(i), 0))
        prev = lambda i: jnp.maximum(i - 1, 0)
        cur = lambda i: i
        nxt = lambda i: jnp.minimum(i + 1, nqb - 1)
        in_specs = [sink_spec, pl.BlockSpec((tq, WA_WIDTH), lambda b, i: (b * nqb + i, 0)),
                    kv(prev, WA_KV_WIDTH), kv(cur, WA_KV_WIDTH), kv(nxt, WA_KV_WIDTH),
                    kv(prev, WA_V_PAD), kv(cur, WA_V_PAD), kv(nxt, WA_V_PAD),
                    ctx_spec(WA_KV_WIDTH), ctx_spec(WA_V_PAD)]
        args = (sink, q, k, k, k, v, v, v, k, v)
        out_rows, out_spec = geom.BT, pl.BlockSpec((tq, WA_WIDTH), lambda b, i: (b * nqb + i, 0))
    else:
        tq, nqb = L, 1
        in_specs = [sink_spec, pl.BlockSpec((tq, WA_WIDTH), lambda b, i: (geom.BT // L + b, 0)),
                    ctx_spec(WA_KV_WIDTH), ctx_spec(WA_V_PAD)]
        args = (sink, q, k, v)
        out_rows, out_spec = B * L, pl.BlockSpec((tq, WA_WIDTH), lambda b, i: (b, 0))
    return pl.pallas_call(
        functools.partial(_wa_attn_kernel, local=local, nqb=nqb, tq=tq),
        grid=(B, nqb),
        in_specs=in_specs,
        out_specs=out_spec,
        out_shape=jax.ShapeDtypeStruct((out_rows, WA_WIDTH), BF16),
        compiler_params=_cparams("parallel", "arbitrary"),
    )(*args)


def _mla_prep_kernel(z_ref, qg_ref, kvg_ref, wq_ref, wkv_ref, cos_ref, sin_ref, q_ref, k_ref, v_ref):
    def norm(x, g):
        return (x * lax.rsqrt(jnp.mean(x * x, axis=-1, keepdims=True) + NORM_EPS) * g).astype(BF16)

    cos = cos_ref[...]
    sin = sin_ref[...]
    q = _dot(norm(z_ref[:, :MLA_Q_LORA], qg_ref[...]), wq_ref[...])
    kv = _dot(norm(z_ref[:, MLA_Q_LORA:MLA_Q_LORA + MLA_KV_LORA], kvg_ref[...]), wkv_ref[...])
    kr = z_ref[:, MLA_Q_LORA + MLA_KV_LORA:MLA_Q_LORA + MLA_KV_LORA + 128]
    kr = (kr * cos[:, 128:] + _rot_half64(kr) * sin[:, 128:]).astype(BF16)
    for h in range(MLA_HEADS):
        qh = q[:, h * MLA_QK_PAD:(h + 1) * MLA_QK_PAD]
        q_ref[h] = ((qh * cos + _rot_half64(qh) * sin) * (MLA_SCALE * LOG2_E)).astype(BF16)
        k_ref[h, :, :MLA_NOPE] = kv[:, h * 256:h * 256 + MLA_NOPE].astype(BF16)
        k_ref[h, :, MLA_NOPE:] = kr
        v_ref[h, :, :MLA_V] = kv[:, h * 256 + MLA_NOPE:(h + 1) * 256].astype(BF16)
        v_ref[h, :, MLA_V:] = jnp.ones((kv.shape[0], MLA_V_PAD - MLA_V), BF16)


def mla_prep(z, qnorm_g, kvnorm_g, wq, wkv, cos, sin, *, tm):
    R, Z = z.shape
    full = lambda a: pl.BlockSpec(a.shape, lambda i: (0,) * a.ndim)
    qg, kvg = qnorm_g.reshape(1, -1), kvnorm_g.reshape(1, -1)
    hd = lambda w: pl.BlockSpec((MLA_HEADS, tm, w), lambda i: (0, i, 0))
    return pl.pallas_call(
        _mla_prep_kernel,
        grid=(R // tm,),
        in_specs=[pl.BlockSpec((tm, Z), lambda i: (i, 0)), full(qg), full(kvg), full(wq), full(wkv),
                  pl.BlockSpec((tm, MLA_QK_PAD), lambda i: (i, 0)), pl.BlockSpec((tm, MLA_QK_PAD), lambda i: (i, 0))],
        out_specs=[hd(MLA_QK_PAD), hd(MLA_QK_PAD), hd(MLA_V_PAD)],
        out_shape=[jax.ShapeDtypeStruct((MLA_HEADS, R, MLA_QK_PAD), BF16),
                   jax.ShapeDtypeStruct((MLA_HEADS, R, MLA_QK_PAD), BF16),
                   jax.ShapeDtypeStruct((MLA_HEADS, R, MLA_V_PAD), BF16)],
        compiler_params=_cparams("parallel"),
    )(z, qg, kvg, wq, wkv, cos, sin)


def _mla_flash_kernel(*refs, with_latent, sub):
    if with_latent:
        q_ref, kx_ref, vx_ref, k_ref, v_ref, o_ref, m_ref, acc_ref = refs
    else:
        q_ref, kx_ref, vx_ref, o_ref, m_ref, acc_ref = refs
    ki = pl.program_id(3)
    q = q_ref[0]

    def update(s, v, m_old, acc_old):
        cols = [s[:, c * 128:(c + 1) * 128] for c in range(s.shape[1] // 128)]
        mx = functools.reduce(jnp.maximum, cols)
        m_new = jnp.maximum(m_old, jnp.max(mx, axis=-1, keepdims=True))
        alpha = jnp.exp2(m_old - m_new)
        p = jnp.concatenate([jnp.exp2(c - m_new).astype(BF16) for c in cols], axis=-1)
        return m_new, jnp.concatenate([alpha, alpha], axis=-1) * acc_old + _dot(p, v)

    @pl.when(ki == 0)
    def _():
        tq = q.shape[0]
        m, acc = update(_dot_nt(q, kx_ref[0]), vx_ref[0], jnp.full((tq, 128), NEG_INF, F32),
                        jnp.zeros((tq, MLA_V_PAD), F32))
        m_ref[...], acc_ref[...] = m, acc

    if with_latent:
        nsub = k_ref.shape[1] // sub
        m, acc = m_ref[...], acc_ref[...]
        s_next = _dot_nt(q, k_ref[0, 0:sub, :])
        for j in range(nsub):
            s = s_next
            if j + 1 < nsub:
                s_next = _dot_nt(q, k_ref[0, (j + 1) * sub:(j + 2) * sub, :])
            m, acc = update(s, v_ref[0, j * sub:(j + 1) * sub, :], m, acc)
        m_ref[...], acc_ref[...] = m, acc

    @pl.when(ki == pl.num_programs(3) - 1)
    def _():
        acc = acc_ref[...]
        o_ref[...] = (acc[:, :MLA_V] / acc[:, MLA_V:]).astype(o_ref.dtype)


def mla_attention(geom, q, k, v, *, with_latent, tq, tk):
    B, T, L = geom.B, geom.T, geom.L
    cblk = geom.BT // L
    ctx_k = pl.BlockSpec((1, L, MLA_QK_PAD), lambda b, h, qi, ki: (h, cblk + b, 0))
    ctx_v = pl.BlockSpec((1, L, MLA_V_PAD), lambda b, h, qi, ki: (h, cblk + b, 0))
    if with_latent:
        nq, nk = T // tq, T // tk
        in_specs = [pl.BlockSpec((1, tq, MLA_QK_PAD), lambda b, h, qi, ki: (h, b * nq + qi, 0)), ctx_k, ctx_v,
                    pl.BlockSpec((1, tk, MLA_QK_PAD), lambda b, h, qi, ki: (h, b * nk + ki, 0)),
                    pl.BlockSpec((1, tk, MLA_V_PAD), lambda b, h, qi, ki: (h, b * nk + ki, 0))]
        args = (q, k, v, k, v)
        out_rows, out_spec = geom.BT, pl.BlockSpec((tq, MLA_V), lambda b, h, qi, ki: (b * nq + qi, h))
    else:
        tq, nq, nk = L, 1, 1
        in_specs = [pl.BlockSpec((1, tq, MLA_QK_PAD), lambda b, h, qi, ki: (h, cblk + b, 0)), ctx_k, ctx_v]
        args = (q, k, v)
        out_rows, out_spec = B * L, pl.BlockSpec((tq, MLA_V), lambda b, h, qi, ki: (b, h))
    return pl.pallas_call(
        functools.partial(_mla_flash_kernel, with_latent=with_latent, sub=min(MLA_SUB_KEYS, tk)),
        grid=(B, MLA_HEADS, nq, nk),
        in_specs=in_specs,
        out_specs=out_spec,
        out_shape=jax.ShapeDtypeStruct((out_rows, MLA_HEADS * MLA_V), BF16),
        scratch_shapes=[pltpu.VMEM((tq, 128), F32), pltpu.VMEM((tq, MLA_V_PAD), F32)],
        compiler_params=_cparams("parallel", "parallel", "parallel", "arbitrary"),
    )(*args)


RW_Z_R, RW_Z_K, RW_Z_V = 0, RW_WIDTH, 2 * RW_WIDTH
RW_Z_LORA = 3 * RW_WIDTH
RW_Z_COLS = 3 * RW_WIDTH + 5 * LORA_PAD


def _dot3(a, b):
    return _mmx(_pieces(a, 3), _pieces(b, 3), _dot)


def _head_sum(x, ones):
    ones = (ones.astype(BF16),)
    return jnp.concatenate(
        [_mmx(_pieces(x[:, c * 128:(c + 1) * 128], 3), ones, _dot) for c in range(x.shape[1] // 128)], axis=-1)


def _rw_prep_kernel(*refs, geom, tm, has_vres):
    if has_vres:
        (z_ref, zp_ref, zn_ref, mu_ref, w0_ref, w2_ref, a0_ref, a2_ref, g2_ref, kk_ref, ka_ref, rk_ref, ones_ref,
         vf_ref, v0_ref, v1_ref, v2_ref,
         r_o, v_o, kk_o, lwf_o, kf_o, bf_o, lwb_o, kb_o, bb_o, g_o, bonus_o) = refs
    else:
        (z_ref, zp_ref, zn_ref, mu_ref, w0_ref, w2_ref, a0_ref, a2_ref, g2_ref, kk_ref, ka_ref, rk_ref, ones_ref,
         r_o, v_o, kk_o, lwf_o, kf_o, bf_o, lwb_o, kb_o, bb_o, g_o, bonus_o) = refs
    first, last = _seq_edge_masks(geom, pl.program_id(0) * tm, tm)
    z = z_ref[...]
    up, dn = _shifted_rows(z, zp_ref[...], zn_ref[...], first, last)
    z = z + mu_ref[...] * (0.5 * (up + dn) - z)
    r = z[:, RW_Z_R:RW_Z_R + RW_WIDTH]
    k = z[:, RW_Z_K:RW_Z_K + RW_WIDTH]
    v = z[:, RW_Z_V:RW_Z_V + RW_WIDTH]
    lora = lambda n: z[:, RW_Z_LORA + n * LORA_PAD:RW_Z_LORA + (n + 1) * LORA_PAD]
    ones = ones_ref[...]
    if has_vres:
        mix = jax.nn.sigmoid(v0_ref[...] + _dot3(_dot3(v, v1_ref[...]), v2_ref[...]))
        v = v + (vf_ref[...] - v) * mix
    g_o[...] = _dot3(jax.nn.sigmoid(lora(4)), g2_ref[...])
    kk = k * kk_ref[...]
    kk = kk / jnp.maximum(jnp.sqrt(_head_sum(kk * kk, ones)), 1e-12)
    ksum = None
    for d, (lw_o, k_o, b_o) in enumerate(((lwf_o, kf_o, bf_o), (lwb_o, kb_o, bb_o))):
        x = -(w0_ref[d:d + 1, :] + _dot3(jnp.tanh(lora(d)), w2_ref[d]))
        softplus = jnp.maximum(x, 0.0) + jnp.log1p(jnp.exp(-jnp.abs(x)))
        lw_o[...] = -jnp.exp(-softplus - 0.5)
        a = jax.nn.sigmoid(a0_ref[d:d + 1, :] + _dot3(lora(2 + d), a2_ref[d]))
        kd = k * (1.0 + (a - 1.0) * ka_ref[...])
        k_o[...] = kd
        b_o[...] = kk * a
        ksum = kd if ksum is None else ksum + kd
    r_o[...] = r
    v_o[...] = v
    kk_o[...] = kk
    bonus_o[...] = _head_sum(r * ksum * rk_ref[...], ones) * v


def rw_prep(geom, z, p, v_first, vres, *, tm):
    R = z.shape[0]
    has_vres = vres is not None
    full = lambda a: pl.BlockSpec(a.shape, lambda i: (0,) * a.ndim)
    row = pl.BlockSpec((tm, RW_WIDTH), lambda i: (i, 0))
    prev, nxt = _halo_specs(tm, R, RW_Z_COLS, lambda i: 0)
    params = [p["mu"], p["w0"], p["w2"], p["a0"], p["a2"], p["g2"], p["kk"], p["ka"], p["rk"], p["ones"]]
    in_specs = [pl.BlockSpec((tm, RW_Z_COLS), lambda i: (i, 0)), prev, nxt] + [full(a) for a in params]
    args = [z, z, z] + params
    if has_vres:
        in_specs += [row] + [full(a) for a in vres]
        args += [v_first] + list(vres)
    return pl.pallas_call(
        functools.partial(_rw_prep_kernel, geom=geom, tm=tm, has_vres=has_vres),
        grid=(R // tm,),
        in_specs=in_specs,
        out_specs=[row] * 11,
        out_shape=[jax.ShapeDtypeStruct((R, RW_WIDTH), F32)] * 11,
        compiler_params=_cparams("parallel"),
    )(*args)


def _pieces(x, passes):
    hi = x.astype(BF16)
    if passes == 1:
        return (hi,)
    return hi, (x - hi.astype(F32)).astype(BF16)


def _mmx(a, b, dot):
    out = dot(a[0], b[0])
    if len(a) > 1:
        out = out + dot(a[1], b[0])
    if len(b) > 1:
        out = out + dot(a[0], b[1])
    return out


def _stack_pair(first_head, x):
    return jnp.concatenate([jnp.where(first_head, x, 0.0), jnp.where(first_head, 0.0, x)], axis=0)


def _fold_pair(x):
    half = x.shape[0] // 2
    return x[:half] + x[half:]


def _rw_fused_scan_kernel(rf_ref, vf_ref, kkf_ref, lwf_ref, kf_ref, bf_ref,
                          rb_ref, vb_ref, kkb_ref, lwb_ref, kb_ref, bb_ref, yf_o, yb_o, s_ref, *, nchunk):
    C = RW_CHUNK
    PW = RW_PAIR
    ri = lax.broadcasted_iota(jnp.int32, (PW, PW), 0)
    ci = lax.broadcasted_iota(jnp.int32, (PW, PW), 1)
    eye = ri == ci
    ri, ci = ri & (C - 1), ci & (C - 1)
    first_head = lax.broadcasted_iota(jnp.int32, (1, PW), 1) < RW_HEAD_DIM
    stack = functools.partial(_stack_pair, first_head)

    @pl.when(pl.program_id(1) == 0)
    def _():
        s_ref[...] = jnp.zeros_like(s_ref)

    dirs = ((rf_ref, vf_ref, kkf_ref, lwf_ref, kf_ref, bf_ref, yf_o, ci < ri, C - 1),
            (rb_ref, vb_ref, kkb_ref, lwb_ref, kb_ref, bb_ref, yb_o, ci > ri, 0))

    def chunk_step(step, carry):
        jobs = []
        for d, (r_ref, v_ref, kk_ref, lw_ref, k_ref, b_ref, y_o, before, tot_row) in enumerate(dirs):
            chunk = step if d == 0 else nchunk - 1 - step
            rows = pl.ds(pl.multiple_of(chunk * C, C), C)
            r, v, kk = r_ref[rows, :], v_ref[rows, :], kk_ref[rows, :]
            lw, kd, bd = lw_ref[rows, :], k_ref[rows, :], b_ref[rows, :]
            incl = (before | eye)[:C, :C].astype(BF16)
            cum, rest = None, lw
            for _ in range(3):
                piece = rest.astype(BF16)
                rest = rest - piece.astype(F32)
                part = _dot(incl, piece)
                cum = part if cum is None else cum + part
            c_tot = cum[tot_row:tot_row + 1, :]
            at = -kk * jnp.exp(cum - lw)
            rt = r * jnp.exp(cum)
            e_neg = jnp.exp(-cum)
            bt, kt = bd * e_neg, kd * e_neg
            e_rest = jnp.exp(c_tot - cum)
            bc, kc = bd * e_rest, kd * e_rest
            e_tot = jnp.exp(c_tot)
            for p in range(RW_WIDTH // PW):
                ps = slice(p * PW, (p + 1) * PW)
                jobs.append(dict(d=d, ps=ps, rows=rows, y_o=y_o, before=before, incl=before | eye, e=e_tot[:, ps],
                                 at=stack(at[:, ps]), rt=stack(rt[:, ps]), bt=stack(bt[:, ps]), kt=stack(kt[:, ps]),
                                 bc=stack(bc[:, ps]), kc=stack(kc[:, ps]), v=stack(v[:, ps])))
        for j in jobs:
            p = _mmx(_pieces(jnp.concatenate([j["at"], j["rt"]], axis=0), RW_PASSES_LOCAL),
                     _pieces(jnp.concatenate([j["bt"], j["kt"]], axis=0), RW_PASSES_LOCAL), _dot_nt)
            j["a_ab"] = jnp.where(j["before"], p[:PW, :PW], 0.0)
            j["a_ak"] = jnp.where(j["before"], p[:PW, PW:], 0.0)
            j["a_rb"] = jnp.where(j["incl"], p[PW:, :PW], 0.0)
            j["a_rk"] = jnp.where(j["incl"], p[PW:, PW:], 0.0)
            j["vp"] = _pieces(j["v"], RW_PASSES_LOCAL)
        for j in jobs:
            j["w1"] = _mmx(_pieces(j["a_ak"], RW_PASSES_LOCAL), j["vp"], _dot)
            j["tinv"] = jnp.where(eye, 1.0, j["a_ab"])
            j["pw"] = j["a_ab"]
        for _ in range(int(np.log2(C)) - 1):
            for j in jobs:
                pw = _pieces(j["pw"], RW_PASSES_LOCAL)
                j["pw"] = _mmx(pw, pw, _dot)
            for j in jobs:
                j["tinv"] = j["tinv"] + _mmx(_pieces(j["tinv"], RW_PASSES_LOCAL), _pieces(j["pw"], RW_PASSES_LOCAL),
                                             _dot)
        for j in jobs:
            tw = _mmx(_pieces(j["tinv"], RW_PASSES_LOCAL),
                      _pieces(jnp.concatenate([j["at"], j["w1"]], axis=1), RW_PASSES_LOCAL), _dot)
            j["tw"] = tw
            j["twp"] = _pieces(tw, RW_PASSES_LOCAL)
        for j in jobs:
            ry = _mmx(_pieces(j["a_rb"], RW_PASSES_LOCAL), j["twp"], _dot)
            yk = _mmx(_pieces(j["a_rk"], RW_PASSES_LOCAL), j["vp"], _dot)
            j["rr"] = j["rt"] + ry[:, :PW]
            j["yl"] = _fold_pair(ry[:, PW:] + yk)
            j["m"] = _mmx(_pieces(j["bc"], RW_PASSES_TRANSITION), _pieces(j["tw"][:, :PW], RW_PASSES_TRANSITION),
                          _dot_tn)
            uv = jnp.concatenate([j["tw"][:, PW:], j["v"]], axis=0)
            bk = jnp.concatenate([j["bc"], j["kc"]], axis=0)
            j["n"] = _fold_pair(_mmx(_pieces(uv, RW_PASSES_STATE), _pieces(bk, RW_PASSES_STATE), _dot_tn))
        for j in jobs:
            s = s_ref[j["d"], :, j["ps"]]
            sp = _pieces(stack(s), RW_PASSES_STATE)
            y = _mmx(_pieces(j["rr"], RW_PASSES_STATE), sp, _dot_nt)
            j["y_o"][j["rows"], j["ps"]] = _fold_pair(y) + j["yl"]
            sm = _mmx(sp[:RW_PASSES_TRANSITION], _pieces(j["m"], RW_PASSES_TRANSITION), _dot_nt)
            s_ref[j["d"], :, j["ps"]] = s * j["e"] + _fold_pair(sm) + j["n"]
        return carry

    lax.fori_loop(0, nchunk, chunk_step, 0)


def rw_fused_scan(geom, r, v, kk, lwf, kf, bf, lwb, kb, bb):
    B, T, L = geom.B, geom.T, geom.L
    blk = L
    nlat = T // blk
    cblk = geom.BT // blk
    fwd = pl.BlockSpec((blk, RW_WIDTH), lambda b, s: (jnp.where(s == 0, cblk + b, b * nlat + s - 1), 0))
    bwd = pl.BlockSpec((blk, RW_WIDTH), lambda b, s: (jnp.where(s == 0, cblk + b, b * nlat + nlat - s), 0))
    return pl.pallas_call(
        functools.partial(_rw_fused_scan_kernel, nchunk=blk // RW_CHUNK),
        grid=(B, nlat + 1),
        in_specs=[fwd] * 6 + [bwd] * 6,
        out_specs=[fwd, bwd],
        out_shape=[jax.ShapeDtypeStruct((geom.R, RW_WIDTH), F32)] * 2,
        scratch_shapes=[pltpu.VMEM((2, RW_HEAD_DIM, RW_WIDTH), F32)],
        compiler_params=_cparams("parallel", "arbitrary"),
    )(r, v, kk, lwf, kf, bf, r, v, kk, lwb, kb, bb)


def _rw_post_kernel(yf_ref, yb_ref, bonus_ref, g_ref, lng_ref, lnb_ref, ones_ref, o_ref):
    ones = ones_ref[...]
    y = yf_ref[...] + yb_ref[...]
    mean = _head_sum(y, ones) * (1.0 / RW_HEAD_DIM)
    yc = y - mean
    var = _head_sum(yc * yc, ones) * (1.0 / RW_HEAD_DIM)
    y = yc * lax.rsqrt(var + RW_LNX_EPS) * lng_ref[...] + lnb_ref[...]
    o_ref[...] = ((y + bonus_ref[...]) * g_ref[...]).astype(o_ref.dtype)


def rw_post(yf, yb, bonus, g, lnx_g, lnx_b, ones, *, tm):
    R = yf.shape[0]
    row = pl.BlockSpec((tm, RW_WIDTH), lambda i: (i, 0))
    full = lambda a: pl.BlockSpec(a.shape, lambda i: (0,) * a.ndim)
    return pl.pallas_call(
        _rw_post_kernel,
        grid=(R // tm,),
        in_specs=[row] * 4 + [full(lnx_g), full(lnx_b), full(ones)],
        out_specs=row,
        out_shape=jax.ShapeDtypeStruct((R, RW_WIDTH), BF16),
        compiler_params=_cparams("parallel"),
    )(yf, yb, bonus, g, lnx_g, lnx_b, ones)


def _rw_in_cols(w):
    parts = [w[..., :3 * RW_WIDTH]]
    off = 3 * RW_WIDTH
    for n in (RW_DECAY_LORA, RW_DECAY_LORA, RW_AAA_LORA, RW_AAA_LORA, RW_GATE_LORA):
        parts.append(_pad_cols(w[..., off:off + n], LORA_PAD))
        off += n
    return jnp.concatenate(parts, axis=-1)


def _pad_rows(w, n):
    return jnp.pad(w, [(0, 0)] * (w.ndim - 2) + [(0, n - w.shape[-2]), (0, 0)])


def _mla_wq_cols(w):
    w = w.reshape(w.shape[0], MLA_HEADS, MLA_NOPE + MLA_ROPE)
    return _pad_cols(w, MLA_QK_PAD).reshape(w.shape[0], MLA_HEADS * MLA_QK_PAD)


def kernel(x, c, ctx, c_ctx, ada_w, ada_b, norm1_g, w_in, rw_mu, rw_w0, rw_w2, rw_a0, rw_a2, rw_g2, rw_kk, rw_ka,
           rw_rk, rw_lnx_g, rw_lnx_b, rw_v0, rw_v1, rw_v2, wa_sink, mla_qnorm_g, mla_kvnorm_g, mla_w_uq, mla_w_ukv,
           w_branch, w_out, norm2_g, ffn_w_in, ffn_conv_w, ffn_conv_b, ffn_w_out, final_norm_g):
    B, T, D = x.shape
    L = ctx.shape[1]
    depth = w_in.shape[0]
    F = ffn_w_out.shape[1]
    geom = Geom(B, T, L)
    tm = _pick_tile(T, (512, 256, 128))
    assert (B * L) % tm == 0
    tmm = _pick_tile(geom.R, (1280, 1024, 640, 512, 256, 128))

    rw_cols = 3 * RW_WIDTH + 2 * RW_DECAY_LORA + 2 * RW_AAA_LORA + RW_GATE_LORA
    wa_cols = WA_WIDTH + 2 * WA_KV_WIDTH
    mla_cols = MLA_Q_LORA + MLA_KV_LORA + MLA_ROPE
    mla_cols_pad = MLA_Q_LORA + MLA_KV_LORA + 128

    cos_wa, sin_wa = _rope_tables(geom, WA_HEAD_DIM)
    cos_wa, sin_wa = jnp.tile(cos_wa, (1, 2)), jnp.tile(sin_wa, (1, 2))
    cos_m, sin_m = _rope_tables(geom, MLA_ROPE)
    one, zero = jnp.ones((geom.R, MLA_NOPE), F32), jnp.zeros((geom.R, MLA_NOPE), F32)
    cos_mla = jnp.concatenate([one, cos_m, one[:, :64]], axis=-1)
    sin_mla = jnp.concatenate([zero, sin_m, zero[:, :64]], axis=-1)
    lane = np.arange(128)
    ones_blk = jnp.asarray((lane[:, None] // RW_HEAD_DIM) == (lane[None, :] // RW_HEAD_DIM), F32)

    xs = jnp.concatenate([x.reshape(B * T, D), ctx.reshape(B * L, D)], axis=0)
    cvec = jnp.concatenate([c, c_ctx[None, :], jnp.zeros((8 - (B + 1) % 8, D), F32)], axis=0)
    v_first = None
    for l in range(depth):
        need_ctx = l < depth - 1
        mod = ada_modulation(cvec, ada_w[l], ada_b[l])
        mod = [mod[:, k * D:(k + 1) * D].reshape(-1, 1, D) for k in range(6)]

        w = w_in[l]
        tn_in = _pick_tile(3 * D, (768, 512, 256, 128))
        w_rw = _pad_cols(_rw_in_cols(w[:, :rw_cols]), _round_up(RW_Z_COLS, tn_in)).astype(BF16)
        w_wa = _pad_cols(w[:, rw_cols:rw_cols + wa_cols], _round_up(wa_cols, tn_in)).astype(BF16)
        w_mla = _pad_cols(w[:, rw_cols + wa_cols:rw_cols + wa_cols + mla_cols],
                          _round_up(mla_cols_pad, tn_in)).astype(BF16)
        w_gate = w[:, rw_cols + wa_cols + mla_cols:].astype(BF16)
        z_rw, z_wa, z_mla, gates = in_projection(geom, xs, norm1_g[l], mod[0], mod[1], w_rw, w_wa, w_mla, w_gate,
                                                 tm=tmm, tn=tn_in)

        rw_p = dict(
            mu=_rw_in_cols(rw_mu[l][None, :]), w0=rw_w0[l], w2=_pad_rows(rw_w2[l], LORA_PAD), a0=rw_a0[l],
            a2=_pad_rows(rw_a2[l], LORA_PAD), g2=_pad_rows(rw_g2[l], LORA_PAD), kk=rw_kk[l][None, :],
            ka=rw_ka[l][None, :], rk=rw_rk[l].reshape(1, RW_WIDTH), ones=ones_blk)
        vres = None if l == 0 else (rw_v0[l - 1][None, :], rw_v1[l - 1], rw_v2[l - 1])
        r, v, kk, lwf, kf, bf, lwb, kb, bb, g, bonus = rw_prep(geom, z_rw, rw_p, v_first, vres, tm=min(tm, 256))
        if l == 0:
            v_first = v
        yf, yb = rw_fused_scan(geom, r, v, kk, lwf, kf, bf, lwb, kb, bb)
        o_a = rw_post(yf, yb, bonus, g, rw_lnx_g[l][None, :], rw_lnx_b[l][None, :], ones_blk, tm=tm)

        q_wa, k_wa, v_wa = wa_prep(z_wa, cos_wa, sin_wa, tm=tm)
        ob_l = wa_attention(geom, q_wa, k_wa, v_wa, wa_sink[l], local=True)
        parts = [ob_l]
        if need_ctx:
            parts.append(wa_attention(geom, q_wa, k_wa, v_wa, wa_sink[l], local=False))
        else:
            parts.append(jnp.zeros((B * L, WA_WIDTH), BF16))
        o_b = jnp.concatenate(parts, axis=0)

        wq = _mla_wq_cols(mla_w_uq[l]).astype(BF16)
        q_m, k_m, v_m = mla_prep(z_mla, mla_qnorm_g[l], mla_kvnorm_g[l], wq, mla_w_ukv[l].astype(BF16),
                                 cos_mla, sin_mla, tm=tm)
        tq = _pick_tile(T, (1024, 512, 256, 128))
        tk = _pick_tile(T, (8192, 4096, 2048, 1024, 512, 256, 128))
        parts = [mla_attention(geom, q_m, k_m, v_m, with_latent=True, tq=tq, tk=tk)]
        if need_ctx:
            parts.append(mla_attention(geom, q_m, k_m, v_m, with_latent=False, tq=L, tk=L))
        else:
            parts.append(jnp.zeros((B * L, MLA_HEADS * MLA_V), BF16))
        o_c = jnp.concatenate(parts, axis=0)

        tn_d = _pick_tile(D, (1024, 512, 256, 128))
        y = merge_branches(o_a, o_b, o_c, gates, w_branch[l].astype(BF16), tm=tmm, tn=tn_d)
        xs = matmul_gated_residual(geom, y, w_out[l].astype(BF16), xs, mod[2], tm=tmm, tn=tn_d)

        tf = _pick_tile(F, (512, 256, 128))
        hmid = ffn_in_conv_glu(geom, xs, norm2_g[l], mod[3], mod[4], ffn_w_in[l].astype(BF16), ffn_conv_w[l],
                               ffn_conv_b[l], tm=tmm, tf=tf)
        xs = matmul_gated_residual(geom, hmid, ffn_w_out[l].astype(BF16), xs, mod[5], tm=tmm,
                                   tn=_pick_tile(D, (512, 256, 128)))

    out = final_rmsnorm(xs, final_norm_g, B * T, tm=tm)
    return out.reshape(B, T, D)
```

```python
import functools

import jax
import jax.numpy as jnp
import numpy as np
from jax import lax
from jax.experimental import pallas as pl
from jax.experimental.pallas import tpu as pltpu

F32 = jnp.float32
BF16 = jnp.bfloat16
HIGHEST = lax.Precision.HIGHEST

NORM_EPS = 1e-6
NEG_INF = -1e30
GRID_W = 64
ROPE_BASE = 10000.0

RW_HEADS = 16
RW_HEAD_DIM = 64
RW_WIDTH = RW_HEADS * RW_HEAD_DIM
RW_DECAY_LORA = 96
RW_AAA_LORA = 96
RW_GATE_LORA = 64
RW_LNX_EPS = 64e-5
RW_CHUNK = 64
RW_PAIR = 2 * RW_HEAD_DIM
RW_PASSES_LOCAL = 1
RW_PASSES_STATE = 3
RW_PASSES_TRANSITION = 1
LORA_PAD = 128

WA_HEADS = 16
WA_KV_HEADS = 4
WA_GROUP = WA_HEADS // WA_KV_HEADS
WA_HEAD_DIM = 64
WA_WIDTH = WA_HEADS * WA_HEAD_DIM
WA_KV_WIDTH = WA_KV_HEADS * WA_HEAD_DIM
WA_V_PAD = WA_KV_HEADS * 128
WINDOW = 128
WA_SCALE = WA_HEAD_DIM ** -0.5

MLA_HEADS = 8
MLA_NOPE = 128
MLA_ROPE = 64
MLA_V = 128
MLA_Q_LORA = 512
MLA_KV_LORA = 512
MLA_QK_PAD = 256
MLA_V_PAD = 256
MLA_SCALE = (MLA_NOPE + MLA_ROPE) ** -0.5
MLA_SUB_KEYS = 2048
LOG2_E = 1.4426950408889634

CONV_W = 3
VMEM_LIMIT_BYTES = 56 * 1024 * 1024


def _cparams(*sem):
    return pltpu.CompilerParams(dimension_semantics=sem, vmem_limit_bytes=VMEM_LIMIT_BYTES)


def _dot(a, b, precision=None):
    return jnp.dot(a, b, preferred_element_type=F32, precision=precision)


def _dot_nt(a, b, precision=None):
    return lax.dot_general(a, b, (((1,), (1,)), ((), ())), preferred_element_type=F32, precision=precision)


def _dot_tn(a, b, precision=None):
    return lax.dot_general(a, b, (((0,), (0,)), ((), ())), preferred_element_type=F32, precision=precision)


def _pick_tile(n, candidates):
    for c in candidates:
        if n % c == 0:
            return c
    raise ValueError(f"no tile in {candidates} divides {n}")


def _pad_cols(w, n):
    return jnp.pad(w, [(0, 0)] * (w.ndim - 1) + [(0, n - w.shape[-1])])


def _round_up(n, m):
    return (n + m - 1) // m * m


class Geom:
    def __init__(self, B, T, L):
        assert T & (T - 1) == 0 and L & (L - 1) == 0, "sequence lengths must be powers of two"
        assert T % L == 0 and L % RW_CHUNK == 0 and T % GRID_W == 0
        self.B, self.T, self.L = B, T, L
        self.BT = B * T
        self.R = B * T + B * L


def _select_row_group(geom, row0, tm, tab_ref):
    r = row0 + lax.broadcasted_iota(jnp.int32, (tm, 1), 0)
    out = tab_ref[geom.B]
    for b in range(geom.B):
        out = jnp.where((r >= b * geom.T) & (r < (b + 1) * geom.T), tab_ref[b], out)
    return out


def _seq_edge_masks(geom, row0, tm):
    r = row0 + lax.broadcasted_iota(jnp.int32, (tm, 1), 0)
    is_lat = r < geom.BT
    pos = jnp.where(is_lat, r & (geom.T - 1), (r - geom.BT) & (geom.L - 1))
    last = jnp.where(is_lat, geom.T - 1, geom.L - 1)
    return pos == 0, pos == last


def _shifted_rows(x, prev8, next8, first, last):
    tm = x.shape[0]
    rid = lax.broadcasted_iota(jnp.int32, (tm, 1), 0)
    up = jnp.where(rid == 0, prev8[7:8, :], pltpu.roll(x, 1, axis=0))
    dn = jnp.where(rid == tm - 1, next8[0:1, :], pltpu.roll(x, tm - 1, axis=0))
    return jnp.where(first, 0.0, up), jnp.where(last, 0.0, dn)


def _halo_specs(tm, R, width, col_of):
    nb8 = tm // 8
    prev = pl.BlockSpec((8, width), lambda i, *a: (jnp.maximum(i * nb8 - 1, 0), col_of(i, *a)))
    nxt = pl.BlockSpec((8, width), lambda i, *a: (jnp.minimum((i + 1) * nb8, R // 8 - 1), col_of(i, *a)))
    return prev, nxt


def _ada_kernel(c_ref, w_ref, b_ref, o_ref):
    c = c_ref[...]
    o_ref[...] = _dot(c * jax.nn.sigmoid(c), w_ref[...], HIGHEST) + b_ref[...]


def ada_modulation(cvec, w, b):
    G, D = cvec.shape
    N = w.shape[1]
    tn = _pick_tile(N, (1024, 512, 256, 128))
    return pl.pallas_call(
        _ada_kernel,
        grid=(N // tn,),
        in_specs=[pl.BlockSpec((G, D), lambda j: (0, 0)),
                  pl.BlockSpec((D, tn), lambda j: (0, j)),
                  pl.BlockSpec((1, tn), lambda j: (0, j))],
        out_specs=pl.BlockSpec((G, tn), lambda j: (0, j)),
        out_shape=jax.ShapeDtypeStruct((G, N), F32),
        compiler_params=_cparams("arbitrary"),
    )(cvec, w, b.reshape(1, N))


def _norm_modulate(x, g, sc, sh):
    gain = g * (1.0 + sc)
    return (x * lax.rsqrt(jnp.mean(x * x, axis=-1, keepdims=True) + NORM_EPS) * gain + sh).astype(BF16)


PROLOGUE_ROWS = 256


def _norm_modulate_rows(geom, row0, x_ref, g_ref, sc_ref, sh_ref, h_ref, h_off):
    tm = x_ref.shape[0]
    step = PROLOGUE_ROWS if tm % PROLOGUE_ROWS == 0 else tm
    one_group = geom.T % step == 0 and geom.BT % step == 0
    g = g_ref[...]
    for r in range(0, tm, step):
        if one_group:
            sc, sh = sc_ref[geom.B], sh_ref[geom.B]
            for b in range(geom.B):
                in_b = (row0 + r >= b * geom.T) & (row0 + r < (b + 1) * geom.T)
                sc, sh = jnp.where(in_b, sc_ref[b], sc), jnp.where(in_b, sh_ref[b], sh)
        else:
            sc = _select_row_group(geom, row0 + r, step, sc_ref)
            sh = _select_row_group(geom, row0 + r, step, sh_ref)
        h_ref[h_off + r:h_off + r + step] = _norm_modulate(x_ref[r:r + step], g, sc, sh)


def _in_proj_kernel(x_ref, g_ref, sh_ref, sc_ref, w_ref, zrw_ref, zwa_ref, zmla_ref, gate_ref, h_ref,
                    *, geom, tm, starts):
    j = pl.program_id(1)

    @pl.when(j == 0)
    def _():
        _norm_modulate_rows(geom, pl.program_id(0) * tm, x_ref, g_ref, sc_ref, sh_ref, h_ref, 0)

    wa0, mla0, gate0 = starts

    @pl.when(j < wa0)
    def _():
        zrw_ref[...] = _dot(h_ref[...], w_ref[...])

    @pl.when((j >= wa0) & (j < mla0))
    def _():
        zwa_ref[...] = _dot(h_ref[...], w_ref[...])

    @pl.when((j >= mla0) & (j < gate0))
    def _():
        zmla_ref[...] = _dot(h_ref[...], w_ref[...])

    @pl.when(j >= gate0)
    def _():
        gate_ref[...] = jax.nn.sigmoid(_dot(h_ref[...], w_ref[...])).astype(gate_ref.dtype)


def in_projection(geom, x, g, shift, scale, w_rw, w_wa, w_mla, w_gate, *, tm, tn):
    R, K = x.shape
    widths = [w.shape[1] for w in (w_rw, w_wa, w_mla, w_gate)]
    assert R % tm == 0 and all(n % tn == 0 for n in widths)
    tiles = [n // tn for n in widths]
    starts = tuple(int(v) for v in np.cumsum(tiles)[:3])
    w_all = jnp.concatenate([w_rw, w_wa, w_mla, w_gate], axis=1)
    full = lambda a: pl.BlockSpec(a.shape, lambda i, j: (0,) * a.ndim)
    out_spec = lambda first, n: pl.BlockSpec((tm, tn), lambda i, j: (i, jnp.clip(j - first, 0, n - 1)))
    firsts = (0,) + starts
    return pl.pallas_call(
        functools.partial(_in_proj_kernel, geom=geom, tm=tm, starts=starts),
        grid=(R // tm, sum(tiles)),
        in_specs=[pl.BlockSpec((tm, K), lambda i, j: (i, 0), pipeline_mode=pl.Buffered(1)),
                  pl.BlockSpec((1, K), lambda i, j: (0, 0)),
                  full(shift), full(scale),
                  pl.BlockSpec((K, tn), lambda i, j: (0, j))],
        out_specs=[out_spec(f, n) for f, n in zip(firsts, tiles)],
        out_shape=[jax.ShapeDtypeStruct((R, widths[0]), F32), jax.ShapeDtypeStruct((R, widths[1]), F32),
                   jax.ShapeDtypeStruct((R, widths[2]), F32), jax.ShapeDtypeStruct((R, widths[3]), BF16)],
        scratch_shapes=[pltpu.VMEM((tm, K), BF16)],
        compiler_params=_cparams("parallel", "arbitrary"),
    )(x, g.reshape(1, K), shift, scale, w_all)


def _mm_resid_kernel(y_ref, w_ref, r_ref, gate_ref, o_ref, *, geom, tm):
    gate = _select_row_group(geom, pl.program_id(0) * tm, tm, gate_ref)
    o_ref[...] = r_ref[...] + gate * _dot(y_ref[...], w_ref[...])


def matmul_gated_residual(geom, y, w, resid, gate, *, tm, tn):
    R, K = y.shape
    N = w.shape[1]
    G = gate.shape[0]
    assert R % tm == 0 and N % tn == 0
    return pl.pallas_call(
        functools.partial(_mm_resid_kernel, geom=geom, tm=tm),
        grid=(R // tm, N // tn),
        in_specs=[pl.BlockSpec((tm, K), lambda i, j: (i, 0)),
                  pl.BlockSpec((K, tn), lambda i, j: (0, j)),
                  pl.BlockSpec((tm, tn), lambda i, j: (i, j)),
                  pl.BlockSpec((G, 1, tn), lambda i, j: (0, 0, j))],
        out_specs=pl.BlockSpec((tm, tn), lambda i, j: (i, j)),
        out_shape=jax.ShapeDtypeStruct((R, N), F32),
        compiler_params=_cparams("parallel", "arbitrary"),
    )(y, w, resid, gate)


FFN_HALO = 16


def _ffn_in_kernel(x_ref, xp_ref, xn_ref, g_ref, sh_ref, sc_ref, wg_ref, wu_ref, cw_ref, cb_ref, o_ref, h_ref,
                   *, geom, tm):
    row0 = pl.program_id(0) * tm
    H = FFN_HALO

    @pl.when(pl.program_id(1) == 0)
    def _():
        _norm_modulate_rows(geom, row0, xp_ref, g_ref, sc_ref, sh_ref, h_ref, 0)
        _norm_modulate_rows(geom, row0, x_ref, g_ref, sc_ref, sh_ref, h_ref, H)
        _norm_modulate_rows(geom, row0 + tm - H, xn_ref, g_ref, sc_ref, sh_ref, h_ref, H + tm)

    first, last = _seq_edge_masks(geom, row0, tm)
    gt = _dot(h_ref[...], wg_ref[...])
    u = _dot(h_ref[H:H + tm], wu_ref[...])
    up = jnp.where(first, 0.0, pltpu.roll(gt, 1, axis=0)[H:H + tm])
    dn = jnp.where(last, 0.0, pltpu.roll(gt, tm + 2 * H - 1, axis=0)[H:H + tm])
    cw = cw_ref[...]
    conv = cb_ref[...] + up * cw[0:1, :]
    conv = conv + gt[H:H + tm] * cw[1:2, :]
    conv = conv + dn * cw[2:3, :]
    o_ref[...] = (jax.nn.gelu(conv, approximate=True) * u).astype(o_ref.dtype)


def ffn_in_conv_glu(geom, x, g, shift, scale, w_in, conv_w, conv_b, *, tm, tf):
    R, K = x.shape
    F = w_in.shape[1] // 2
    nj = F // tf
    nbh = tm // FFN_HALO
    assert R % tm == 0 and F % tf == 0 and tm % FFN_HALO == 0
    full = lambda a: pl.BlockSpec(a.shape, lambda i, j: (0,) * a.ndim)
    return pl.pallas_call(
        functools.partial(_ffn_in_kernel, geom=geom, tm=tm),
        grid=(R // tm, nj),
        in_specs=[pl.BlockSpec((tm, K), lambda i, j: (i, 0), pipeline_mode=pl.Buffered(1)),
                  pl.BlockSpec((FFN_HALO, K), lambda i, j: (jnp.maximum(i * nbh - 1, 0), 0)),
                  pl.BlockSpec((FFN_HALO, K), lambda i, j: (jnp.minimum((i + 1) * nbh, R // FFN_HALO - 1), 0)),
                  pl.BlockSpec((1, K), lambda i, j: (0, 0)),
                  full(shift), full(scale),
                  pl.BlockSpec((K, tf), lambda i, j: (0, j)),
                  pl.BlockSpec((K, tf), lambda i, j: (0, nj + j)),
                  pl.BlockSpec((CONV_W, tf), lambda i, j: (0, j)),
                  pl.BlockSpec((1, tf), lambda i, j: (0, j))],
        out_specs=pl.BlockSpec((tm, tf), lambda i, j: (i, j)),
        out_shape=jax.ShapeDtypeStruct((R, F), BF16),
        scratch_shapes=[pltpu.VMEM((tm + 2 * FFN_HALO, K), BF16)],
        compiler_params=_cparams("parallel", "arbitrary"),
    )(x, x, x, g.reshape(1, K), shift, scale, w_in, w_in, conv_w, conv_b.reshape(1, F))


def _merge_kernel(oa_ref, ob_ref, oc_ref, ga_ref, gb_ref, gc_ref, w_ref, o_ref):
    y = ga_ref[...] * _dot(oa_ref[...], w_ref[0])
    y = y + gb_ref[...] * _dot(ob_ref[...], w_ref[1])
    y = y + gc_ref[...] * _dot(oc_ref[...], w_ref[2])
    o_ref[...] = y.astype(o_ref.dtype)


def merge_branches(oa, ob, oc, gates, wb, *, tm, tn):
    R, K = oa.shape
    D = wb.shape[2]
    nj = D // tn
    bspec = pl.BlockSpec((tm, K), lambda i, j: (i, 0))
    gspec = lambda k: pl.BlockSpec((tm, tn), lambda i, j: (i, k * nj + j))
    return pl.pallas_call(
        _merge_kernel,
        grid=(R // tm, nj),
        in_specs=[bspec, bspec, bspec, gspec(0), gspec(1), gspec(2),
                  pl.BlockSpec((3, K, tn), lambda i, j: (0, 0, j))],
        out_specs=pl.BlockSpec((tm, tn), lambda i, j: (i, j)),
        out_shape=jax.ShapeDtypeStruct((R, D), BF16),
        compiler_params=_cparams("parallel", "arbitrary"),
    )(oa, ob, oc, gates, gates, gates, wb)


def _rmsnorm_kernel(x_ref, g_ref, o_ref):
    x = x_ref[...]
    o_ref[...] = x * lax.rsqrt(jnp.mean(x * x, axis=-1, keepdims=True) + NORM_EPS) * g_ref[...]


def final_rmsnorm(x, g, rows, *, tm):
    D = x.shape[1]
    return pl.pallas_call(
        _rmsnorm_kernel,
        grid=(rows // tm,),
        in_specs=[pl.BlockSpec((tm, D), lambda i: (i, 0)), pl.BlockSpec((1, D), lambda i: (0, 0))],
        out_specs=pl.BlockSpec((tm, D), lambda i: (i, 0)),
        out_shape=jax.ShapeDtypeStruct((rows, D), F32),
        compiler_params=_cparams("parallel"),
    )(x, g.reshape(1, D))


def _rot_half64(z):
    n = z.shape[-1]
    lane = lax.broadcasted_iota(jnp.int32, z.shape, z.ndim - 1)
    return jnp.where((lane & 63) < 32, pltpu.roll(z, n - 32, axis=z.ndim - 1), pltpu.roll(z, 32, axis=z.ndim - 1))


def _rope_tables(geom, dim):
    nf = dim // 4
    inv = ROPE_BASE ** (-jnp.arange(nf, dtype=F32) / nf)
    rows = geom.T // GRID_W
    row = jnp.repeat(jnp.arange(rows, dtype=F32), GRID_W)
    col = jnp.tile(jnp.arange(GRID_W, dtype=F32), rows)
    ang = jnp.concatenate([row[:, None] * inv, col[:, None] * inv], axis=-1)
    cos, sin = jnp.cos(ang), jnp.sin(ang)
    cos_t = jnp.concatenate([cos, cos], axis=-1)
    sin_t = jnp.concatenate([-sin, sin], axis=-1)
    nctx = geom.B * geom.L
    cos_f = jnp.concatenate([jnp.tile(cos_t, (geom.B, 1)), jnp.ones((nctx, dim), F32)], axis=0)
    sin_f = jnp.concatenate([jnp.tile(sin_t, (geom.B, 1)), jnp.zeros((nctx, dim), F32)], axis=0)
    return cos_f, sin_f


def _wa_prep_kernel(z_ref, cos_ref, sin_ref, q_ref, k_ref, v_ref):
    cos = cos_ref[...]
    sin = sin_ref[...]
    for c in range(WA_WIDTH // 128):
        z = z_ref[:, c * 128:(c + 1) * 128]
        q_ref[:, c * 128:(c + 1) * 128] = ((z * cos + _rot_half64(z) * sin) * (WA_SCALE * LOG2_E)).astype(BF16)
    for c in range(WA_KV_WIDTH // 128):
        z = z_ref[:, WA_WIDTH + c * 128:WA_WIDTH + (c + 1) * 128]
        k_ref[:, c * 128:(c + 1) * 128] = (z * cos + _rot_half64(z) * sin).astype(BF16)
    v_ref[...] = jnp.ones(v_ref.shape, BF16)
    for g in range(WA_KV_HEADS):
        src = WA_WIDTH + WA_KV_WIDTH + g * WA_HEAD_DIM
        v_ref[:, g * 128:g * 128 + WA_HEAD_DIM] = z_ref[:, src:src + WA_HEAD_DIM].astype(BF16)


def wa_prep(z, cos, sin, *, tm):
    R = z.shape[0]
    row = lambda w: pl.BlockSpec((tm, w), lambda i: (i, 0))
    return pl.pallas_call(
        _wa_prep_kernel,
        grid=(R // tm,),
        in_specs=[row(WA_WIDTH + 2 * WA_KV_WIDTH), row(128), row(128)],
        out_specs=[row(WA_WIDTH), row(WA_KV_WIDTH), row(WA_V_PAD)],
        out_shape=[jax.ShapeDtypeStruct((R, WA_WIDTH), BF16),
                   jax.ShapeDtypeStruct((R, WA_KV_WIDTH), BF16),
                   jax.ShapeDtypeStruct((R, WA_V_PAD), BF16)],
        compiler_params=_cparams("parallel"),
    )(z, cos, sin)


def _wa_attn_kernel(*refs, local, nqb, tq):
    if local:
        sink_ref, q_ref, kp_ref, kc_ref, kn_ref, vp_ref, vc_ref, vn_ref, kx_ref, vx_ref, o_ref = refs
    else:
        sink_ref, q_ref, kx_ref, vx_ref, o_ref = refs
    i = pl.program_id(1)
    nk_ctx = kx_ref.shape[0]
    rows = WA_GROUP * tq
    qpos = lax.broadcasted_iota(jnp.int32, (rows, 1), 0) & (tq - 1)
    head_in_group = lax.broadcasted_iota(jnp.int32, (rows, 1), 0) >> int(np.log2(tq))
    if local:
        off_prev = jnp.where(i > 0, 0, tq)
        off_next = jnp.where(i < nqb - 1, 0, tq)
        j = lax.broadcasted_iota(jnp.int32, (1, 3 * tq + nk_ctx), 1)
        valid = ((j >= tq) & (j < 2 * tq)) | (j >= 3 * tq)
        valid = valid | ((j < tq) & (j >= qpos + off_prev))
        valid = valid | ((j >= 2 * tq) & (j < 3 * tq) & ((j - 2 * tq) <= qpos - off_next))
    def scores(g):
        ks = slice(g * WA_HEAD_DIM, (g + 1) * WA_HEAD_DIM)
        qg = jnp.concatenate(
            [q_ref[:, (g * WA_GROUP + a) * WA_HEAD_DIM:(g * WA_GROUP + a + 1) * WA_HEAD_DIM] for a in range(WA_GROUP)],
            axis=0)
        if local:
            kcat = jnp.concatenate([kp_ref[:, ks], kc_ref[:, ks], kn_ref[:, ks], kx_ref[:, ks]], axis=0)
        else:
            kcat = kx_ref[:, ks]
        return _dot_nt(qg, kcat)

    s_next = scores(0)
    for g in range(WA_KV_HEADS):
        s = s_next
        if g + 1 < WA_KV_HEADS:
            s_next = scores(g + 1)
        vs = slice(g * 128, (g + 1) * 128)
        sink = jnp.zeros((rows, 1), F32)
        for a in range(WA_GROUP):
            sink = jnp.where(head_in_group == a, sink_ref[g * WA_GROUP + a] * LOG2_E, sink)
        if local:
            vcat = jnp.concatenate([vp_ref[:, vs], vc_ref[:, vs], vn_ref[:, vs], vx_ref[:, vs]], axis=0)
            s = jnp.where(valid, s, NEG_INF)
        else:
            vcat = vx_ref[:, vs]
        m = jnp.maximum(jnp.max(s, axis=-1, keepdims=True), sink)
        o = _dot(jnp.exp2(s - m).astype(BF16), vcat)
        o = o[:, :WA_HEAD_DIM] / (o[:, WA_HEAD_DIM:] + jnp.exp2(sink - m))
        for a in range(WA_GROUP):
            h = g * WA_GROUP + a
            o_ref[:, h * WA_HEAD_DIM:(h + 1) * WA_HEAD_DIM] = o[a * tq:(a + 1) * tq].astype(o_ref.dtype)


def wa_attention(geom, q, k, v, sink, *, local):
    B, T, L = geom.B, geom.T, geom.L
    sink_spec = pl.BlockSpec(memory_space=pltpu.SMEM)
    ctx_spec = lambda w: pl.BlockSpec((L, w), lambda b, i: (geom.BT // L + b, 0))
    if local:
        tq = WINDOW
        nqb = T // tq
        kv = lambda f, w: pl.BlockSpec((tq, w), lambda b, i: (b * nqb + f(i), 0))
        prev = lambda i: jnp.maximum(i - 1, 0)
        cur = lambda i: i
        nxt = lambda i: jnp.minimum(i + 1, nqb - 1)
        in_specs = [sink_spec, pl.BlockSpec((tq, WA_WIDTH), lambda b, i: (b * nqb + i, 0)),
                    kv(prev, WA_KV_WIDTH), kv(cur, WA_KV_WIDTH), kv(nxt, WA_KV_WIDTH),
                    kv(prev, WA_V_PAD), kv(cur, WA_V_PAD), kv(nxt, WA_V_PAD),
                    ctx_spec(WA_KV_WIDTH), ctx_spec(WA_V_PAD)]
        args = (sink, q, k, k, k, v, v, v, k, v)
        out_rows, out_spec = geom.BT, pl.BlockSpec((tq, WA_WIDTH), lambda b, i: (b * nqb + i, 0))
    else:
        tq, nqb = L, 1
        in_specs = [sink_spec, pl.BlockSpec((tq, WA_WIDTH), lambda b, i: (geom.BT // L + b, 0)),
                    ctx_spec(WA_KV_WIDTH), ctx_spec(WA_V_PAD)]
        args = (sink, q, k, v)
        out_rows, out_spec = B * L, pl.BlockSpec((tq, WA_WIDTH), lambda b, i: (b, 0))
    return pl.pallas_call(
        functools.partial(_wa_attn_kernel, local=local, nqb=nqb, tq=tq),
        grid=(B, nqb),
        in_specs=in_specs,
        out_specs=out_spec,
        out_shape=jax.ShapeDtypeStruct((out_rows, WA_WIDTH), BF16),
        compiler_params=_cparams("parallel", "arbitrary"),
    )(*args)


def _mla_prep_kernel(z_ref, qg_ref, kvg_ref, wq_ref, wkv_ref, cos_ref, sin_ref, q_ref, k_ref, v_ref):
    def norm(x, g):
        return (x * lax.rsqrt(jnp.mean(x * x, axis=-1, keepdims=True) + NORM_EPS) * g).astype(BF16)

    cos = cos_ref[...]
    sin = sin_ref[...]
    q = _dot(norm(z_ref[:, :MLA_Q_LORA], qg_ref[...]), wq_ref[...])
    kv = _dot(norm(z_ref[:, MLA_Q_LORA:MLA_Q_LORA + MLA_KV_LORA], kvg_ref[...]), wkv_ref[...])
    kr = z_ref[:, MLA_Q_LORA + MLA_KV_LORA:MLA_Q_LORA + MLA_KV_LORA + 128]
    kr = (kr * cos[:, 128:] + _rot_half64(kr) * sin[:, 128:]).astype(BF16)
    for h in range(MLA_HEADS):
        qh = q[:, h * MLA_QK_PAD:(h + 1) * MLA_QK_PAD]
        q_ref[h] = ((qh * cos + _rot_half64(qh) * sin) * (MLA_SCALE * LOG2_E)).astype(BF16)
        k_ref[h, :, :MLA_NOPE] = kv[:, h * 256:h * 256 + MLA_NOPE].astype(BF16)
        k_ref[h, :, MLA_NOPE:] = kr
        v_ref[h, :, :MLA_V] = kv[:, h * 256 + MLA_NOPE:(h + 1) * 256].astype(BF16)
        v_ref[h, :, MLA_V:] = jnp.ones((kv.shape[0], MLA_V_PAD - MLA_V), BF16)


def mla_prep(z, qnorm_g, kvnorm_g, wq, wkv, cos, sin, *, tm):
    R, Z = z.shape
    full = lambda a: pl.BlockSpec(a.shape, lambda i: (0,) * a.ndim)
    qg, kvg = qnorm_g.reshape(1, -1), kvnorm_g.reshape(1, -1)
    hd = lambda w: pl.BlockSpec((MLA_HEADS, tm, w), lambda i: (0, i, 0))
    return pl.pallas_call(
        _mla_prep_kernel,
        grid=(R // tm,),
        in_specs=[pl.BlockSpec((tm, Z), lambda i: (i, 0)), full(qg), full(kvg), full(wq), full(wkv),
                  pl.BlockSpec((tm, MLA_QK_PAD), lambda i: (i, 0)), pl.BlockSpec((tm, MLA_QK_PAD), lambda i: (i, 0))],
        out_specs=[hd(MLA_QK_PAD), hd(MLA_QK_PAD), hd(MLA_V_PAD)],
        out_shape=[jax.ShapeDtypeStruct((MLA_HEADS, R, MLA_QK_PAD), BF16),
                   jax.ShapeDtypeStruct((MLA_HEADS, R, MLA_QK_PAD), BF16),
                   jax.ShapeDtypeStruct((MLA_HEADS, R, MLA_V_PAD), BF16)],
        compiler_params=_cparams("parallel"),
    )(z, qg, kvg, wq, wkv, cos, sin)


def _mla_flash_kernel(*refs, with_latent, sub):
    if with_latent:
        q_ref, kx_ref, vx_ref, k_ref, v_ref, o_ref, m_ref, acc_ref = refs
    else:
        q_ref, kx_ref, vx_ref, o_ref, m_ref, acc_ref = refs
    ki = pl.program_id(3)
    q = q_ref[0]

    def update(s, v, m_old, acc_old):
        cols = [s[:, c * 128:(c + 1) * 128] for c in range(s.shape[1] // 128)]
        mx = functools.reduce(jnp.maximum, cols)
        m_new = jnp.maximum(m_old, jnp.max(mx, axis=-1, keepdims=True))
        alpha = jnp.exp2(m_old - m_new)
        p = jnp.concatenate([jnp.exp2(c - m_new).astype(BF16) for c in cols], axis=-1)
        return m_new, jnp.concatenate([alpha, alpha], axis=-1) * acc_old + _dot(p, v)

    @pl.when(ki == 0)
    def _():
        tq = q.shape[0]
        m, acc = update(_dot_nt(q, kx_ref[0]), vx_ref[0], jnp.full((tq, 128), NEG_INF, F32),
                        jnp.zeros((tq, MLA_V_PAD), F32))
        m_ref[...], acc_ref[...] = m, acc

    if with_latent:
        nsub = k_ref.shape[1] // sub
        m, acc = m_ref[...], acc_ref[...]
        s_next = _dot_nt(q, k_ref[0, 0:sub, :])
        for j in range(nsub):
            s = s_next
            if j + 1 < nsub:
                s_next = _dot_nt(q, k_ref[0, (j + 1) * sub:(j + 2) * sub, :])
            m, acc = update(s, v_ref[0, j * sub:(j + 1) * sub, :], m, acc)
        m_ref[...], acc_ref[...] = m, acc

    @pl.when(ki == pl.num_programs(3) - 1)
    def _():
        acc = acc_ref[...]
        o_ref[...] = (acc[:, :MLA_V] / acc[:, MLA_V:]).astype(o_ref.dtype)


def mla_attention(geom, q, k, v, *, with_latent, tq, tk):
    B, T, L = geom.B, geom.T, geom.L
    cblk = geom.BT // L
    ctx_k = pl.BlockSpec((1, L, MLA_QK_PAD), lambda b, h, qi, ki: (h, cblk + b, 0))
    ctx_v = pl.BlockSpec((1, L, MLA_V_PAD), lambda b, h, qi, ki: (h, cblk + b, 0))
    if with_latent:
        nq, nk = T // tq, T // tk
        in_specs = [pl.BlockSpec((1, tq, MLA_QK_PAD), lambda b, h, qi, ki: (h, b * nq + qi, 0)), ctx_k, ctx_v,
                    pl.BlockSpec((1, tk, MLA_QK_PAD), lambda b, h, qi, ki: (h, b * nk + ki, 0)),
                    pl.BlockSpec((1, tk, MLA_V_PAD), lambda b, h, qi, ki: (h, b * nk + ki, 0))]
        args = (q, k, v, k, v)
        out_rows, out_spec = geom.BT, pl.BlockSpec((tq, MLA_V), lambda b, h, qi, ki: (b * nq + qi, h))
    else:
        tq, nq, nk = L, 1, 1
        in_specs = [pl.BlockSpec((1, tq, MLA_QK_PAD), lambda b, h, qi, ki: (h, cblk + b, 0)), ctx_k, ctx_v]
        args = (q, k, v)
        out_rows, out_spec = B * L, pl.BlockSpec((tq, MLA_V), lambda b, h, qi, ki: (b, h))
    return pl.pallas_call(
        functools.partial(_mla_flash_kernel, with_latent=with_latent, sub=min(MLA_SUB_KEYS, tk)),
        grid=(B, MLA_HEADS, nq, nk),
        in_specs=in_specs,
        out_specs=out_spec,
        out_shape=jax.ShapeDtypeStruct((out_rows, MLA_HEADS * MLA_V), BF16),
        scratch_shapes=[pltpu.VMEM((tq, 128), F32), pltpu.VMEM((tq, MLA_V_PAD), F32)],
        compiler_params=_cparams("parallel", "parallel", "parallel", "arbitrary"),
    )(*args)


RW_Z_R, RW_Z_K, RW_Z_V = 0, RW_WIDTH, 2 * RW_WIDTH
RW_Z_LORA = 3 * RW_WIDTH
RW_Z_COLS = 3 * RW_WIDTH + 5 * LORA_PAD


def _dot3(a, b):
    return _mmx(_pieces(a, 3), _pieces(b, 3), _dot)


def _head_sum(x, ones):
    ones = (ones.astype(BF16),)
    return jnp.concatenate(
        [_mmx(_pieces(x[:, c * 128:(c + 1) * 128], 3), ones, _dot) for c in range(x.shape[1] // 128)], axis=-1)


def _rw_prep_kernel(*refs, geom, tm, has_vres):
    if has_vres:
        (z_ref, zp_ref, zn_ref, mu_ref, w0_ref, w2_ref, a0_ref, a2_ref, g2_ref, kk_ref, ka_ref, rk_ref, ones_ref,
         vf_ref, v0_ref, v1_ref, v2_ref,
         r_o, v_o, kk_o, lwf_o, kf_o, bf_o, lwb_o, kb_o, bb_o, g_o, bonus_o) = refs
    else:
        (z_ref, zp_ref, zn_ref, mu_ref, w0_ref, w2_ref, a0_ref, a2_ref, g2_ref, kk_ref, ka_ref, rk_ref, ones_ref,
         r_o, v_o, kk_o, lwf_o, kf_o, bf_o, lwb_o, kb_o, bb_o, g_o, bonus_o) = refs
    first, last = _seq_edge_masks(geom, pl.program_id(0) * tm, tm)
    z = z_ref[...]
    up, dn = _shifted_rows(z, zp_ref[...], zn_ref[...], first, last)
    z = z + mu_ref[...] * (0.5 * (up + dn) - z)
    r = z[:, RW_Z_R:RW_Z_R + RW_WIDTH]
    k = z[:, RW_Z_K:RW_Z_K + RW_WIDTH]
    v = z[:, RW_Z_V:RW_Z_V + RW_WIDTH]
    lora = lambda n: z[:, RW_Z_LORA + n * LORA_PAD:RW_Z_LORA + (n + 1) * LORA_PAD]
    ones = ones_ref[...]
    if has_vres:
        mix = jax.nn.sigmoid(v0_ref[...] + _dot3(_dot3(v, v1_ref[...]), v2_ref[...]))
        v = v + (vf_ref[...] - v) * mix
    g_o[...] = _dot3(jax.nn.sigmoid(lora(4)), g2_ref[...])
    kk = k * kk_ref[...]
    kk = kk / jnp.maximum(jnp.sqrt(_head_sum(kk * kk, ones)), 1e-12)
    ksum = None
    for d, (lw_o, k_o, b_o) in enumerate(((lwf_o, kf_o, bf_o), (lwb_o, kb_o, bb_o))):
        x = -(w0_ref[d:d + 1, :] + _dot3(jnp.tanh(lora(d)), w2_ref[d]))
        softplus = jnp.maximum(x, 0.0) + jnp.log1p(jnp.exp(-jnp.abs(x)))
        lw_o[...] = -jnp.exp(-softplus - 0.5)
        a = jax.nn.sigmoid(a0_ref[d:d + 1, :] + _dot3(lora(2 + d), a2_ref[d]))
        kd = k * (1.0 + (a - 1.0) * ka_ref[...])
        k_o[...] = kd
        b_o[...] = kk * a
        ksum = kd if ksum is None else ksum + kd
    r_o[...] = r
    v_o[...] = v
    kk_o[...] = kk
    bonus_o[...] = _head_sum(r * ksum * rk_ref[...], ones) * v


def rw_prep(geom, z, p, v_first, vres, *, tm):
    R = z.shape[0]
    has_vres = vres is not None
    full = lambda a: pl.BlockSpec(a.shape, lambda i: (0,) * a.ndim)
    row = pl.BlockSpec((tm, RW_WIDTH), lambda i: (i, 0))
    prev, nxt = _halo_specs(tm, R, RW_Z_COLS, lambda i: 0)
    params = [p["mu"], p["w0"], p["w2"], p["a0"], p["a2"], p["g2"], p["kk"], p["ka"], p["rk"], p["ones"]]
    in_specs = [pl.BlockSpec((tm, RW_Z_COLS), lambda i: (i, 0)), prev, nxt] + [full(a) for a in params]
    args = [z, z, z] + params
    if has_vres:
        in_specs += [row] + [full(a) for a in vres]
        args += [v_first] + list(vres)
    return pl.pallas_call(
        functools.partial(_rw_prep_kernel, geom=geom, tm=tm, has_vres=has_vres),
        grid=(R // tm,),
        in_specs=in_specs,
        out_specs=[row] * 11,
        out_shape=[jax.ShapeDtypeStruct((R, RW_WIDTH), F32)] * 11,
        compiler_params=_cparams("parallel"),
    )(*args)


def _pieces(x, passes):
    hi = x.astype(BF16)
    if passes == 1:
        return (hi,)
    return hi, (x - hi.astype(F32)).astype(BF16)


def _mmx(a, b, dot):
    out = dot(a[0], b[0])
    if len(a) > 1:
        out = out + dot(a[1], b[0])
    if len(b) > 1:
        out = out + dot(a[0], b[1])
    return out


def _stack_pair(first_head, x):
    return jnp.concatenate([jnp.where(first_head, x, 0.0), jnp.where(first_head, 0.0, x)], axis=0)


def _fold_pair(x):
    half = x.shape[0] // 2
    return x[:half] + x[half:]


def _rw_fused_scan_kernel(rf_ref, vf_ref, kkf_ref, lwf_ref, kf_ref, bf_ref,
                          rb_ref, vb_ref, kkb_ref, lwb_ref, kb_ref, bb_ref, yf_o, yb_o, s_ref, *, nchunk):
    C = RW_CHUNK
    PW = RW_PAIR
    ri = lax.broadcasted_iota(jnp.int32, (PW, PW), 0)
    ci = lax.broadcasted_iota(jnp.int32, (PW, PW), 1)
    eye = ri == ci
    ri, ci = ri & (C - 1), ci & (C - 1)
    first_head = lax.broadcasted_iota(jnp.int32, (1, PW), 1) < RW_HEAD_DIM
    stack = functools.partial(_stack_pair, first_head)

    @pl.when(pl.program_id(1) == 0)
    def _():
        s_ref[...] = jnp.zeros_like(s_ref)

    dirs = ((rf_ref, vf_ref, kkf_ref, lwf_ref, kf_ref, bf_ref, yf_o, ci < ri, C - 1),
            (rb_ref, vb_ref, kkb_ref, lwb_ref, kb_ref, bb_ref, yb_o, ci > ri, 0))

    def chunk_step(step, carry):
        jobs = []
        for d, (r_ref, v_ref, kk_ref, lw_ref, k_ref, b_ref, y_o, before, tot_row) in enumerate(dirs):
            chunk = step if d == 0 else nchunk - 1 - step
            rows = pl.ds(pl.multiple_of(chunk * C, C), C)
            r, v, kk = r_ref[rows, :], v_ref[rows, :], kk_ref[rows, :]
            lw, kd, bd = lw_ref[rows, :], k_ref[rows, :], b_ref[rows, :]
            incl = (before | eye)[:C, :C].astype(BF16)
            cum, rest = None, lw
            for _ in range(3):
                piece = rest.astype(BF16)
                rest = rest - piece.astype(F32)
                part = _dot(incl, piece)
                cum = part if cum is None else cum + part
            c_tot = cum[tot_row:tot_row + 1, :]
            at = -kk * jnp.exp(cum - lw)
            rt = r * jnp.exp(cum)
            e_neg = jnp.exp(-cum)
            bt, kt = bd * e_neg, kd * e_neg
            e_rest = jnp.exp(c_tot - cum)
            bc, kc = bd * e_rest, kd * e_rest
            e_tot = jnp.exp(c_tot)
            for p in range(RW_WIDTH // PW):
                ps = slice(p * PW, (p + 1) * PW)
                jobs.append(dict(d=d, ps=ps, rows=rows, y_o=y_o, before=before, incl=before | eye, e=e_tot[:, ps],
                                 at=stack(at[:, ps]), rt=stack(rt[:, ps]), bt=stack(bt[:, ps]), kt=stack(kt[:, ps]),
                                 bc=stack(bc[:, ps]), kc=stack(kc[:, ps]), v=stack(v[:, ps])))
        for j in jobs:
            p = _mmx(_pieces(jnp.concatenate([j["at"], j["rt"]], axis=0), RW_PASSES_LOCAL),
                     _pieces(jnp.concatenate([j["bt"], j["kt"]], axis=0), RW_PASSES_LOCAL), _dot_nt)
            j["a_ab"] = jnp.where(j["before"], p[:PW, :PW], 0.0)
            j["a_ak"] = jnp.where(j["before"], p[:PW, PW:], 0.0)
            j["a_rb"] = jnp.where(j["incl"], p[PW:, :PW], 0.0)
            j["a_rk"] = jnp.where(j["incl"], p[PW:, PW:], 0.0)
            j["vp"] = _pieces(j["v"], RW_PASSES_LOCAL)
        for j in jobs:
            j["w1"] = _mmx(_pieces(j["a_ak"], RW_PASSES_LOCAL), j["vp"], _dot)
            j["tinv"] = jnp.where(eye, 1.0, j["a_ab"])
            j["pw"] = j["a_ab"]
        for _ in range(int(np.log2(C)) - 1):
            for j in jobs:
                pw = _pieces(j["pw"], RW_PASSES_LOCAL)
                j["pw"] = _mmx(pw, pw, _dot)
            for j in jobs:
                j["tinv"] = j["tinv"] + _mmx(_pieces(j["tinv"], RW_PASSES_LOCAL), _pieces(j["pw"], RW_PASSES_LOCAL),
                                             _dot)
        for j in jobs:
            tw = _mmx(_pieces(j["tinv"], RW_PASSES_LOCAL),
                      _pieces(jnp.concatenate([j["at"], j["w1"]], axis=1), RW_PASSES_LOCAL), _dot)
            j["tw"] = tw
            j["twp"] = _pieces(tw, RW_PASSES_LOCAL)
        for j in jobs:
            ry = _mmx(_pieces(j["a_rb"], RW_PASSES_LOCAL), j["twp"], _dot)
            yk = _mmx(_pieces(j["a_rk"], RW_PASSES_LOCAL), j["vp"], _dot)
            j["rr"] = j["rt"] + ry[:, :PW]
            j["yl"] = _fold_pair(ry[:, PW:] + yk)
            j["m"] = _mmx(_pieces(j["bc"], RW_PASSES_TRANSITION), _pieces(j["tw"][:, :PW], RW_PASSES_TRANSITION),
                          _dot_tn)
            uv = jnp.concatenate([j["tw"][:, PW:], j["v"]], axis=0)
            bk = jnp.concatenate([j["bc"], j["kc"]], axis=0)
            j["n"] = _fold_pair(_mmx(_pieces(uv, RW_PASSES_STATE), _pieces(bk, RW_PASSES_STATE), _dot_tn))
        for j in jobs:
            s = s_ref[j["d"], :, j["ps"]]
            sp = _pieces(stack(s), RW_PASSES_STATE)
            y = _mmx(_pieces(j["rr"], RW_PASSES_STATE), sp, _dot_nt)
            j["y_o"][j["rows"], j["ps"]] = _fold_pair(y) + j["yl"]
            sm = _mmx(sp[:RW_PASSES_TRANSITION], _pieces(j["m"], RW_PASSES_TRANSITION), _dot_nt)
            s_ref[j["d"], :, j["ps"]] = s * j["e"] + _fold_pair(sm) + j["n"]
        return carry

    lax.fori_loop(0, nchunk, chunk_step, 0, unroll=2)


def rw_fused_scan(geom, r, v, kk, lwf, kf, bf, lwb, kb, bb):
    B, T, L = geom.B, geom.T, geom.L
    blk = L
    nlat = T // blk
    cblk = geom.BT // blk
    fwd = pl.BlockSpec((blk, RW_WIDTH), lambda b, s: (jnp.where(s == 0, cblk + b, b * nlat + s - 1), 0))
    bwd = pl.BlockSpec((blk, RW_WIDTH), lambda b, s: (jnp.where(s == 0, cblk + b, b * nlat + nlat - s), 0))
    return pl.pallas_call(
        functools.partial(_rw_fused_scan_kernel, nchunk=blk // RW_CHUNK),
        grid=(B, nlat + 1),
        in_specs=[fwd] * 6 + [bwd] * 6,
        out_specs=[fwd, bwd],
        out_shape=[jax.ShapeDtypeStruct((geom.R, RW_WIDTH), F32)] * 2,
        scratch_shapes=[pltpu.VMEM((2, RW_HEAD_DIM, RW_WIDTH), F32)],
        compiler_params=_cparams("parallel", "arbitrary"),
    )(r, v, kk, lwf, kf, bf, r, v, kk, lwb, kb, bb)


def _rw_post_kernel(yf_ref, yb_ref, bonus_ref, g_ref, lng_ref, lnb_ref, ones_ref, o_ref):
    ones = ones_ref[...]
    y = yf_ref[...] + yb_ref[...]
    mean = _head_sum(y, ones) * (1.0 / RW_HEAD_DIM)
    yc = y - mean
    var = _head_sum(yc * yc, ones) * (1.0 / RW_HEAD_DIM)
    y = yc * lax.rsqrt(var + RW_LNX_EPS) * lng_ref[...] + lnb_ref[...]
    o_ref[...] = ((y + bonus_ref[...]) * g_ref[...]).astype(o_ref.dtype)


def rw_post(yf, yb, bonus, g, lnx_g, lnx_b, ones, *, tm):
    R = yf.shape[0]
    row = pl.BlockSpec((tm, RW_WIDTH), lambda i: (i, 0))
    full = lambda a: pl.BlockSpec(a.shape, lambda i: (0,) * a.ndim)
    return pl.pallas_call(
        _rw_post_kernel,
        grid=(R // tm,),
        in_specs=[row] * 4 + [full(lnx_g), full(lnx_b), full(ones)],
        out_specs=row,
        out_shape=jax.ShapeDtypeStruct((R, RW_WIDTH), BF16),
        compiler_params=_cparams("parallel"),
    )(yf, yb, bonus, g, lnx_g, lnx_b, ones)


def _rw_in_cols(w):
    parts = [w[..., :3 * RW_WIDTH]]
    off = 3 * RW_WIDTH
    for n in (RW_DECAY_LORA, RW_DECAY_LORA, RW_AAA_LORA, RW_AAA_LORA, RW_GATE_LORA):
        parts.append(_pad_cols(w[..., off:off + n], LORA_PAD))
        off += n
    return jnp.concatenate(parts, axis=-1)


def _pad_rows(w, n):
    return jnp.pad(w, [(0, 0)] * (w.ndim - 2) + [(0, n - w.shape[-2]), (0, 0)])


def _mla_wq_cols(w):
    w = w.reshape(w.shape[0], MLA_HEADS, MLA_NOPE + MLA_ROPE)
    return _pad_cols(w, MLA_QK_PAD).reshape(w.shape[0], MLA_HEADS * MLA_QK_PAD)


def kernel(x, c, ctx, c_ctx, ada_w, ada_b, norm1_g, w_in, rw_mu, rw_w0, rw_w2, rw_a0, rw_a2, rw_g2, rw_kk, rw_ka,
           rw_rk, rw_lnx_g, rw_lnx_b, rw_v0, rw_v1, rw_v2, wa_sink, mla_qnorm_g, mla_kvnorm_g, mla_w_uq, mla_w_ukv,
           w_branch, w_out, norm2_g, ffn_w_in, ffn_conv_w, ffn_conv_b, ffn_w_out, final_norm_g):
    B, T, D = x.shape
    L = ctx.shape[1]
    depth = w_in.shape[0]
    F = ffn_w_out.shape[1]
    geom = Geom(B, T, L)
    tm = _pick_tile(T, (512, 256, 128))
    assert (B * L) % tm == 0
    tmm = _pick_tile(geom.R, (1280, 1024, 640, 512, 256, 128))

    rw_cols = 3 * RW_WIDTH + 2 * RW_DECAY_LORA + 2 * RW_AAA_LORA + RW_GATE_LORA
    wa_cols = WA_WIDTH + 2 * WA_KV_WIDTH
    mla_cols = MLA_Q_LORA + MLA_KV_LORA + MLA_ROPE
    mla_cols_pad = MLA_Q_LORA + MLA_KV_LORA + 128

    cos_wa, sin_wa = _rope_tables(geom, WA_HEAD_DIM)
    cos_wa, sin_wa = jnp.tile(cos_wa, (1, 2)), jnp.tile(sin_wa, (1, 2))
    cos_m, sin_m = _rope_tables(geom, MLA_ROPE)
    one, zero = jnp.ones((geom.R, MLA_NOPE), F32), jnp.zeros((geom.R, MLA_NOPE), F32)
    cos_mla = jnp.concatenate([one, cos_m, one[:, :64]], axis=-1)
    sin_mla = jnp.concatenate([zero, sin_m, zero[:, :64]], axis=-1)
    lane = np.arange(128)
    ones_blk = jnp.asarray((lane[:, None] // RW_HEAD_DIM) == (lane[None, :] // RW_HEAD_DIM), F32)

    xs = jnp.concatenate([x.reshape(B * T, D), ctx.reshape(B * L, D)], axis=0)
    cvec = jnp.concatenate([c, c_ctx[None, :], jnp.zeros((8 - (B + 1) % 8, D), F32)], axis=0)
    v_first = None
    for l in range(depth):
        need_ctx = l < depth - 1
        mod = ada_modulation(cvec, ada_w[l], ada_b[l])
        mod = [mod[:, k * D:(k + 1) * D].reshape(-1, 1, D) for k in range(6)]

        w = w_in[l]
        tn_in = _pick_tile(3 * D, (768, 512, 256, 128))
        w_rw = _pad_cols(_rw_in_cols(w[:, :rw_cols]), _round_up(RW_Z_COLS, tn_in)).astype(BF16)
        w_wa = _pad_cols(w[:, rw_cols:rw_cols + wa_cols], _round_up(wa_cols, tn_in)).astype(BF16)
        w_mla = _pad_cols(w[:, rw_cols + wa_cols:rw_cols + wa_cols + mla_cols],
                          _round_up(mla_cols_pad, tn_in)).astype(BF16)
        w_gate = w[:, rw_cols + wa_cols + mla_cols:].astype(BF16)
        z_rw, z_wa, z_mla, gates = in_projection(geom, xs, norm1_g[l], mod[0], mod[1], w_rw, w_wa, w_mla, w_gate,
                                                 tm=tmm, tn=tn_in)

        rw_p = dict(
            mu=_rw_in_cols(rw_mu[l][None, :]), w0=rw_w0[l], w2=_pad_rows(rw_w2[l], LORA_PAD), a0=rw_a0[l],
            a2=_pad_rows(rw_a2[l], LORA_PAD), g2=_pad_rows(rw_g2[l], LORA_PAD), kk=rw_kk[l][None, :],
            ka=rw_ka[l][None, :], rk=rw_rk[l].reshape(1, RW_WIDTH), ones=ones_blk)
        vres = None if l == 0 else (rw_v0[l - 1][None, :], rw_v1[l - 1], rw_v2[l - 1])
        r, v, kk, lwf, kf, bf, lwb, kb, bb, g, bonus = rw_prep(geom, z_rw, rw_p, v_first, vres, tm=min(tm, 256))
        if l == 0:
            v_first = v
        yf, yb = rw_fused_scan(geom, r, v, kk, lwf, kf, bf, lwb, kb, bb)
        o_a = rw_post(yf, yb, bonus, g, rw_lnx_g[l][None, :], rw_lnx_b[l][None, :], ones_blk, tm=tm)

        q_wa, k_wa, v_wa = wa_prep(z_wa, cos_wa, sin_wa, tm=tm)
        ob_l = wa_attention(geom, q_wa, k_wa, v_wa, wa_sink[l], local=True)
        parts = [ob_l]
        if need_ctx:
            parts.append(wa_attention(geom, q_wa, k_wa, v_wa, wa_sink[l], local=False))
        else:
            parts.append(jnp.zeros((B * L, WA_WIDTH), BF16))
        o_b = jnp.concatenate(parts, axis=0)

        wq = _mla_wq_cols(mla_w_uq[l]).astype(BF16)
        q_m, k_m, v_m = mla_prep(z_mla, mla_qnorm_g[l], mla_kvnorm_g[l], wq, mla_w_ukv[l].astype(BF16),
                                 cos_mla, sin_mla, tm=tm)
        tq = _pick_tile(T, (1024, 512, 256, 128))
        tk = _pick_tile(T, (4096, 2048, 1024, 512, 256, 128))
        parts = [mla_attention(geom, q_m, k_m, v_m, with_latent=True, tq=tq, tk=tk)]
        if need_ctx:
            parts.append(mla_attention(geom, q_m, k_m, v_m, with_latent=False, tq=L, tk=L))
        else:
            parts.append(jnp.zeros((B * L, MLA_HEADS * MLA_V), BF16))
        o_c = jnp.concatenate(parts, axis=0)

        tn_d = _pick_tile(D, (1024, 512, 256, 128))
        y = merge_branches(o_a, o_b, o_c, gates, w_branch[l].astype(BF16), tm=tmm, tn=tn_d)
        xs = matmul_gated_residual(geom, y, w_out[l].astype(BF16), xs, mod[2], tm=tmm, tn=tn_d)

        tf = _pick_tile(F, (512, 256, 128))
        hmid = ffn_in_conv_glu(geom, xs, norm2_g[l], mod[3], mod[4], ffn_w_in[l].astype(BF16), ffn_conv_w[l],
                               ffn_conv_b[l], tm=tmm, tf=tf)
        xs = matmul_gated_residual(geom, hmid, ffn_w_out[l].astype(BF16), xs, mod[5], tm=tmm,
                                   tn=_pick_tile(D, (512, 256, 128)))

    out = final_rmsnorm(xs, final_norm_g, B * T, tm=tm)
    return out.reshape(B, T, D)
```
